```python
import math
import jax
import jax.numpy as jnp
from jax import lax
import numpy as np

D_MODEL = 1024
BATCH = 8
SEQ = 4096
DEPTH = 2

GRID_W = 64
CTX_LEN = 256
EPS = 1e-6
CHUNK = 64

GLA_HEADS = 4
GLA_DK = 64
GLA_DV = 128
GLA_RANK = 16
GLA_GATE_NORM = 16.0
GLA_QK = GLA_HEADS * GLA_DK
GLA_V = GLA_HEADS * GLA_DV
S5_WIDTH = D_MODEL - GLA_V
S5_GROUP = 16
S5_GROUPS = S5_WIDTH // S5_GROUP
S5_STATE = 64
S5_DT_MIN = 1e-3
S5_DT_MAX = 1e-1
AB_SPLITS = (GLA_QK, 2 * GLA_QK, 2 * GLA_QK + GLA_V, 2 * GLA_QK + 2 * GLA_V,
             2 * GLA_QK + 2 * GLA_V + GLA_RANK, 2 * GLA_QK + 2 * GLA_V + 2 * GLA_RANK)
AB_IN = AB_SPLITS[-1] + S5_WIDTH
HG_EXPAND = 128
HG_HEADS = D_MODEL // HG_EXPAND
HG_DV = D_MODEL // HG_HEADS
N_EXPERTS = 32
N_GROUPS = 8
EXPERTS_PER_GROUP = N_EXPERTS // N_GROUPS
TOP_K = 2
D_EXPERT = 256

N_EVEN = (DEPTH + 1) // 2
N_ODD = DEPTH // 2

kernel_name = 'hybrid_gla_s5_hgrn2_groupmoe_diffusion'

F32 = jnp.float32


def rmsnorm(x, g):
    xf = x.astype(F32)
    y = xf * lax.rsqrt(jnp.mean(xf * xf, axis=-1, keepdims=True) + EPS)
    return y.astype(x.dtype) * g


def to_heads(t, n_heads):
    b, n, w = t.shape
    return t.reshape(b, n, n_heads, w // n_heads).transpose(0, 2, 1, 3)


def from_heads(t):
    b, h, n, d = t.shape
    return t.transpose(0, 2, 1, 3).reshape(b, n, h * d)


def grid_sincos(n_tokens, dim):
    rows = n_tokens // GRID_W
    r, col = jnp.meshgrid(jnp.arange(rows, dtype=F32), jnp.arange(GRID_W, dtype=F32), indexing='ij')
    quarter = dim // 4
    omega = 1.0 / (10000.0 ** (jnp.arange(quarter, dtype=F32) / quarter))
    def emb(pos):
        ang = pos.reshape(-1, 1) * omega
        return jnp.concatenate([jnp.sin(ang), jnp.cos(ang)], axis=-1)
    return jnp.concatenate([emb(r), emb(col)], axis=-1)


def chunked_gated_recurrence(q, k, v, log_a, h0):
    out_dtype = v.dtype
    q, k, v, log_a = (t.astype(F32) for t in (q, k, v, log_a))
    bsz, nh, n, dk = q.shape
    dv = v.shape[-1]
    nc = n // CHUNK
    rs = lambda t: t.reshape(bsz, nh, nc, CHUNK, t.shape[-1])
    q, k, v, log_a = rs(q), rs(k), rs(v), rs(log_a)
    b = jnp.cumsum(log_a, axis=3)
    b_mid = b[:, :, :, CHUNK // 2 - 1:CHUNK // 2, :]
    b_end = b[:, :, :, -1:, :]
    attn = jnp.einsum('bhclk,bhcsk->bhcls', q * jnp.exp(b - b_mid), k * jnp.exp(b_mid - b))
    attn = jnp.where(jnp.tril(jnp.ones((CHUNK, CHUNK), dtype=bool)), attn, 0.0)
    o_intra = jnp.einsum('bhcls,bhcsv->bhclv', attn, v)
    q_in = jnp.moveaxis(q * jnp.exp(b), 2, 0)
    d_state = jnp.moveaxis(jnp.einsum('bhcsk,bhcsv->bhckv', k * jnp.exp(b_end - b), v), 2, 0)
    decay = jnp.moveaxis(jnp.exp(b_end[:, :, :, 0, :]), 2, 0)
    s0 = jnp.zeros((bsz, nh, dk, dv), F32) if h0 is None else h0.astype(F32)

    def step(s, xs):
        qc, dsc, dc = xs
        oc = jnp.einsum('bhlk,bhkv->bhlv', qc, s)
        return dc[..., None] * s + dsc, oc

    s_fin, o_inter = lax.scan(step, s0, (q_in, d_state, decay))
    o = o_intra + jnp.moveaxis(o_inter, 0, 2)
    return o.reshape(bsz, nh, n, dv).astype(out_dtype), s_fin


def gated_final_state(k, v, log_a):
    b = jnp.cumsum(log_a.astype(F32), axis=2)
    w = jnp.exp(b[:, :, -1:, :] - b)
    return jnp.einsum('bhnk,bhnv->bhkv', k.astype(F32) * w, v.astype(F32))


def bidir_gated_recurrence(lat, ctx, need_ctx):
    q, k_f, k_b, v, la_f, la_b = lat
    cq, ck_f, ck_b, cv, cla_f, cla_b = ctx
    fl = lambda t: jnp.flip(t, axis=2)
    if need_ctx:
        co_f, s_f = chunked_gated_recurrence(cq, ck_f, cv, cla_f, None)
        co_b, s_b = chunked_gated_recurrence(fl(cq), fl(ck_b), fl(cv), fl(cla_b), None)
        co = co_f + fl(co_b)
    else:
        s_f = gated_final_state(ck_f, cv, cla_f)
        s_b = gated_final_state(fl(ck_b), fl(cv), fl(cla_b))
        co = None
    o_f, _ = chunked_gated_recurrence(q, k_f, v, la_f, s_f)
    o_b, _ = chunked_gated_recurrence(fl(q), fl(k_b), fl(v), fl(la_b), s_b)
    return o_f + fl(o_b), co


def s5_discretize(lam_re, lam_im, log_dt, b_re, b_im):
    dt = jnp.exp(log_dt)[:, None]
    mag = jnp.exp(lam_re * dt)
    a_re, a_im = mag * jnp.cos(lam_im * dt), mag * jnp.sin(lam_im * dt)
    den = lam_re * lam_re + lam_im * lam_im
    f_re = ((a_re - 1.0) * lam_re + a_im * lam_im) / den
    f_im = (a_im * lam_re - (a_re - 1.0) * lam_im) / den
    bb_re = f_re[..., None] * b_re - f_im[..., None] * b_im
    bb_im = f_re[..., None] * b_im + f_im[..., None] * b_re
    return a_re, a_im, bb_re, bb_im


def s5_scan(u, a_re, a_im, bb_re, bb_im, h0, reverse):
    x_re = jnp.einsum('nbgh,gph->nbgp', u, bb_re)
    x_im = jnp.einsum('nbgh,gph->nbgp', u, bb_im)
    if h0 is not None:
        h_re, h_im = h0
        edge = -1 if reverse else 0
        x_re = x_re.at[edge].add(a_re * h_re - a_im * h_im)
        x_im = x_im.at[edge].add(a_re * h_im + a_im * h_re)
    n = u.shape[0]
    ar = jnp.broadcast_to(a_re[None, None], (n, 1) + a_re.shape)
    ai = jnp.broadcast_to(a_im[None, None], (n, 1) + a_im.shape)

    def combine(e1, e2):
        a1r, a1i, b1r, b1i = e1
        a2r, a2i, b2r, b2i = e2
        return (a2r * a1r - a2i * a1i, a2r * a1i + a2i * a1r,
                a2r * b1r - a2i * b1i + b2r, a2r * b1i + a2i * b1r + b2i)

    _, _, s_re, s_im = lax.associative_scan(combine, (ar, ai, x_re, x_im), reverse=reverse, axis=0)
    return s_re, s_im


def s5_bidir(u, uc, lam_re, lam_im, log_dt, b_re, b_im, c_re, c_im, d, need_ctx):
    def to_scan(t):
        bsz, n, _ = t.shape
        return t.astype(F32).reshape(bsz, n, S5_GROUPS, S5_GROUP).transpose(1, 0, 2, 3)

    def readout(s_re, s_im, cr, ci):
        y = jnp.einsum('nbgp,ghp->bngh', s_re, cr) - jnp.einsum('nbgp,ghp->bngh', s_im, ci)
        return y.reshape(y.shape[0], y.shape[1], S5_WIDTH)

    us, ucs = to_scan(u), to_scan(uc)
    y = d.astype(F32) * u.astype(F32)
    yc = d.astype(F32) * uc.astype(F32) if need_ctx else None
    for di, rev in enumerate((False, True)):
        a_re, a_im, bb_re, bb_im = s5_discretize(lam_re[di].astype(F32), lam_im[di].astype(F32),
                                                 log_dt[di].astype(F32), b_re[di].astype(F32),
                                                 b_im[di].astype(F32))
        sc_re, sc_im = s5_scan(ucs, a_re, a_im, bb_re, bb_im, None, rev)
        edge = 0 if rev else -1
        s_re, s_im = s5_scan(us, a_re, a_im, bb_re, bb_im, (sc_re[edge], sc_im[edge]), rev)
        cr, ci = c_re[di].astype(F32), c_im[di].astype(F32)
        y = y + readout(s_re, s_im, cr, ci)
        if need_ctx:
            yc = yc + readout(sc_re, sc_im, cr, ci)
    return y.astype(u.dtype), (yc.astype(uc.dtype) if need_ctx else None)


def mixer_gla_s5(h, hc, w_in, w_out, gla_a2, gla_ab, gla_norm, lam_re, lam_im, log_dt,
                 b_re, b_im, c_re, c_im, s5_d, glu_w, glu_b, need_ctx):
    def project(t):
        q, k, v, g, a_f, a_b, u = jnp.split(t @ w_in, AB_SPLITS, axis=-1)
        la_f = jax.nn.log_sigmoid((a_f @ gla_a2[0] + gla_ab[0]).astype(F32)) / GLA_GATE_NORM
        la_b = jax.nn.log_sigmoid((a_b @ gla_a2[1] + gla_ab[1]).astype(F32)) / GLA_GATE_NORM
        kh = to_heads(k, GLA_HEADS)
        heads = (to_heads(q * GLA_DK ** -0.5, GLA_HEADS), kh, kh, to_heads(v, GLA_HEADS),
                 to_heads(la_f, GLA_HEADS), to_heads(la_b, GLA_HEADS))
        return heads, g, u

    lat, g, u = project(h)
    ctx_heads, gc, uc = project(hc)
    o, oc = bidir_gated_recurrence(lat, ctx_heads, need_ctx)
    s, sc = s5_bidir(u, uc, lam_re, lam_im, log_dt, b_re, b_im, c_re, c_im, s5_d, need_ctx)
    gla_out = lambda o_, g_: from_heads(rmsnorm(o_, gla_norm)) * jax.nn.silu(g_)

    def glu(s_):
        a = jax.nn.gelu(s_)
        return a * jax.nn.sigmoid(a @ glu_w + glu_b)

    y = jnp.concatenate([gla_out(o, g), glu(s)], axis=-1) @ w_out
    yc = jnp.concatenate([gla_out(oc, gc), glu(sc)], axis=-1) @ w_out if need_ctx else None
    return y, yc


def mixer_hgrn2(h, hc, w_in, w_out, lb, hg_norm, need_ctx):
    def project(t):
        q, f_f, f_b, i, g = jnp.split(t @ w_in, 5, axis=-1)

        def gate(f_pre):
            f = lb + (1.0 - lb) * jax.nn.sigmoid(f_pre.astype(F32))
            return to_heads(1.0 - f, HG_HEADS), to_heads(jnp.log(f), HG_HEADS)

        k_f, la_f = gate(f_f)
        k_b, la_b = gate(f_b)
        heads = (to_heads(jax.nn.silu(q), HG_HEADS), k_f, k_b, to_heads(i, HG_HEADS), la_f, la_b)
        return heads, g

    lat, g = project(h)
    ctx_heads, gc = project(hc)
    o, oc = bidir_gated_recurrence(lat, ctx_heads, need_ctx)
    out = lambda o_, g_: (from_heads(rmsnorm(o_, hg_norm)) * jax.nn.silu(g_)) @ w_out
    return out(o, g), (out(oc, gc) if need_ctx else None)


def grouped_moe(t, router_w, router_bias, w_gate, w_up, w_down):
    aff = jax.nn.sigmoid((t @ router_w).astype(F32))
    sel = aff + router_bias.astype(F32)
    sel_g = sel.reshape(-1, N_GROUPS, EXPERTS_PER_GROUP)
    group_score = jnp.sum(lax.top_k(sel_g, TOP_K)[0], axis=-1)
    grp = jnp.argmax(group_score, axis=-1)
    in_grp = jnp.take_along_axis(sel_g, grp[:, None, None], axis=1)[:, 0]
    _, local = lax.top_k(in_grp, TOP_K)
    idx = grp[:, None] * EXPERTS_PER_GROUP + local
    w = jnp.take_along_axis(aff, idx, axis=1)
    w = w / jnp.sum(w, axis=-1, keepdims=True)
    gates = jnp.sum(jax.nn.one_hot(idx, N_EXPERTS, dtype=F32) * w[..., None], axis=1).astype(t.dtype)

    def expert(acc, p):
        wg, wu, wd, ge = p
        y = (jax.nn.silu(t @ wg) * (t @ wu)) @ wd
        return acc + ge[:, None] * y, None

    out, _ = lax.scan(expert, jnp.zeros_like(t), (w_gate, w_up, w_down, gates.T))
    return out


def setup_inputs(seed: int = 0) -> dict:
    key = jax.random.key(seed)
    ks = iter(jax.random.split(key, 40))
    D = D_MODEL
    nrm = lambda shape, scale: scale * jax.random.normal(next(ks), shape, F32)
    gain = lambda shape: 1.0 + nrm(shape, 0.02)
    s5_shape = (N_EVEN, 2, S5_GROUPS, S5_STATE)
    n_idx = jnp.arange(S5_STATE, dtype=F32)
    return {
        'x': nrm((BATCH, SEQ, D), 1.0),
        'c': nrm((BATCH, D), 1.0),
        'ctx': nrm((BATCH, CTX_LEN, D), 1.0),
        'c_ctx': nrm((D,), 1.0),
        'ada_w': nrm((DEPTH, D, 6 * D), 0.5 * D ** -0.5),
        'ada_b': nrm((DEPTH, 6 * D), 0.02),
        'norm_mix': gain((DEPTH, D)),
        'norm_ffn': gain((DEPTH, D)),
        'ab_w_in': nrm((N_EVEN, D, AB_IN), D ** -0.5),
        'ab_w_out': nrm((N_EVEN, GLA_V + S5_WIDTH, D), (GLA_V + S5_WIDTH) ** -0.5),
        'gla_a2': nrm((N_EVEN, 2, GLA_RANK, GLA_QK), GLA_RANK ** -0.5),
        'gla_ab': nrm((N_EVEN, 2, GLA_QK), 0.1),
        'gla_norm': gain((N_EVEN, GLA_DV)),
        's5_lam_re': -0.5 + nrm(s5_shape, 0.01),
        's5_lam_im': math.pi * n_idx + nrm(s5_shape, 0.01),
        's5_log_dt': jax.random.uniform(next(ks), (N_EVEN, 2, S5_GROUPS), F32,
                                        math.log(S5_DT_MIN), math.log(S5_DT_MAX)),
        's5_b_re': nrm((N_EVEN, 2, S5_GROUPS, S5_STATE, S5_GROUP), (2 * S5_GROUP) ** -0.5),
        's5_b_im': nrm((N_EVEN, 2, S5_GROUPS, S5_STATE, S5_GROUP), (2 * S5_GROUP) ** -0.5),
        's5_c_re': nrm((N_EVEN, 2, S5_GROUPS, S5_GROUP, S5_STATE), 0.5),
        's5_c_im': nrm((N_EVEN, 2, S5_GROUPS, S5_GROUP, S5_STATE), 0.5),
        's5_d': nrm((N_EVEN, S5_WIDTH), 0.5),
        's5_glu_w': nrm((N_EVEN, S5_WIDTH, S5_WIDTH), S5_WIDTH ** -0.5),
        's5_glu_b': nrm((N_EVEN, S5_WIDTH), 0.01),
        'hg_w_in': nrm((N_ODD, D, 5 * D), D ** -0.5),
        'hg_w_out': nrm((N_ODD, D, D), D ** -0.5),
        'hg_lb_logits': nrm((DEPTH, D), 0.1),
        'hg_norm': gain((N_ODD, HG_DV)),
        'router_w': nrm((D, N_EXPERTS), D ** -0.5),
        'router_bias': nrm((N_EXPERTS,), 0.01),
        'moe_w_gate': nrm((DEPTH, N_EXPERTS, D, D_EXPERT), D ** -0.5),
        'moe_w_up': nrm((DEPTH, N_EXPERTS, D, D_EXPERT), D ** -0.5),
        'moe_w_down': nrm((DEPTH, N_EXPERTS, D_EXPERT, D), D_EXPERT ** -0.5),
        'final_norm': gain((D,)),
    }


def reference(x, c, ctx, c_ctx, ada_w, ada_b, norm_mix, norm_ffn, ab_w_in, ab_w_out,
              gla_a2, gla_ab, gla_norm, s5_lam_re, s5_lam_im, s5_log_dt, s5_b_re, s5_b_im,
              s5_c_re, s5_c_im, s5_d, s5_glu_w, s5_glu_b, hg_w_in, hg_w_out, hg_lb_logits,
              hg_norm, router_w, router_bias, moe_w_gate, moe_w_up, moe_w_down, final_norm):
    n_lat_seq, d = x.shape[1], x.shape[2]
    x = x + grid_sincos(n_lat_seq, d).astype(x.dtype)[None]
    cx = ctx
    lb_all = jax.nn.softmax(hg_lb_logits.astype(F32), axis=0)
    lb_all = jnp.cumsum(lb_all, axis=0) - lb_all[0]
    silu_c = jax.nn.silu(c)
    silu_cc = jax.nn.silu(c_ctx)
    for l in range(DEPTH):
        last = l == DEPTH - 1
        j = l // 2
        mod = silu_c @ ada_w[l] + ada_b[l]
        mod_c = silu_cc @ ada_w[l] + ada_b[l]
        sh1, sc1, g1, sh2, sc2, g2 = jnp.split(mod[:, None, :], 6, axis=-1)
        csh1, csc1, cg1, csh2, csc2, cg2 = jnp.split(mod_c, 6)
        h = rmsnorm(x, norm_mix[l]) * (1.0 + sc1) + sh1
        hc = rmsnorm(cx, norm_mix[l]) * (1.0 + csc1) + csh1
        if l % 2 == 0:
            y, yc = mixer_gla_s5(h, hc, ab_w_in[j], ab_w_out[j], gla_a2[j], gla_ab[j], gla_norm[j],
                                 s5_lam_re[j], s5_lam_im[j], s5_log_dt[j], s5_b_re[j], s5_b_im[j],
                                 s5_c_re[j], s5_c_im[j], s5_d[j], s5_glu_w[j], s5_glu_b[j],
                                 need_ctx=not last)
        else:
            y, yc = mixer_hgrn2(h, hc, hg_w_in[j], hg_w_out[j], lb_all[l], hg_norm[j],
                                need_ctx=not last)
        x = x + g1 * y
        h = rmsnorm(x, norm_ffn[l]) * (1.0 + sc2) + sh2
        if last:
            ffn = grouped_moe(h.reshape(-1, d), router_w, router_bias,
                              moe_w_gate[l], moe_w_up[l], moe_w_down[l])
            x = x + g2 * ffn.reshape(x.shape)
        else:
            cx = cx + cg1 * yc
            hc = rmsnorm(cx, norm_ffn[l]) * (1.0 + csc2) + csh2
            tokens = jnp.concatenate([h.reshape(-1, d), hc.reshape(-1, d)], axis=0)
            ffn = grouped_moe(tokens, router_w, router_bias,
                              moe_w_gate[l], moe_w_up[l], moe_w_down[l])
            n_lat = x.shape[0] * x.shape[1]
            x = x + g2 * ffn[:n_lat].reshape(x.shape)
            cx = cx + cg2 * ffn[n_lat:].reshape(cx.shape)
    return rmsnorm(x, final_norm)
```

```python
import functools
import math

import numpy as np
import jax
import jax.numpy as jnp
from jax import lax
from jax.experimental import pallas as pl
from jax.experimental.pallas import tpu as pltpu

F32, BF16, I32 = jnp.float32, jnp.bfloat16, jnp.int32

EPS = 1e-6
CHUNK = 64
GRID_W = 64
GLA_HEADS = 4
GLA_GATE_NORM = 16.0
HG_HEADS = 8
N_GROUPS = 8
EXPERTS_PER_GROUP = 4
TOP_K = 2

LANES = 128
SUBLANES = 8
TM = 256
CPT = TM // CHUNK
HEAD_W = 128
S5_L = 16
MOD_ROWS = 16
VMEM_LIMIT = 56 * 1024 * 1024


def _dot(a, b):
    return jnp.dot(a, b, preferred_element_type=F32)


def _dot_nt(a, b):
    return lax.dot_general(a, b, (((1,), (1,)), ((), ())), preferred_element_type=F32)


def _split(x):
    hi = x.astype(BF16)
    lo = (x - hi.astype(F32)).astype(BF16)
    return hi, lo


def _sigmoid(x):
    return 1.0 / (1.0 + jnp.exp(-x))


def _silu(x):
    return x * _sigmoid(x)


def _gelu_tanh(x):
    return 0.5 * x * (1.0 + jnp.tanh(math.sqrt(2.0 / math.pi) * (x + 0.044715 * (x * x * x))))


def _rms(x):
    return x * lax.rsqrt(jnp.mean(x * x, axis=-1, keepdims=True) + EPS)


def _head_rms(o, gain, n_heads):
    outs = []
    for h in range(n_heads):
        oh = o[:, h * HEAD_W:(h + 1) * HEAD_W]
        outs.append(_rms(oh) * gain)
    return jnp.concatenate(outs, axis=1)


def _pack_bf16_pair(a, b):
    ai = pltpu.bitcast(a, I32)
    bi = pltpu.bitcast(b, I32)
    return (ai & jnp.int32(-65536)) | lax.shift_right_logical(bi, jnp.int32(16))


def _unpack_bf16_pair(w):
    a = pltpu.bitcast(w & jnp.int32(-65536), F32)
    b = pltpu.bitcast(lax.shift_left(w, jnp.int32(16)), F32)
    return a, b


def _params(n_grid_dims, vmem=VMEM_LIMIT):
    return pltpu.CompilerParams(dimension_semantics=("arbitrary",) * n_grid_dims, vmem_limit_bytes=vmem)


def _const_spec(shape):
    nd = len(shape)
    return pl.BlockSpec(shape, lambda *_: (0,) * nd)


def _mod_kernel(c_ref, w_ref, b_ref, o_ref):
    s = _silu(c_ref[...])
    shi, slo = _split(s)
    whi, wlo = _split(w_ref[0])
    o_ref[0] = _dot(shi, whi) + _dot(shi, wlo) + _dot(slo, whi) + b_ref[0]


def _modulation(cond, ada_w, ada_b):
    depth, d, six_d = ada_w.shape
    nchunk = six_d // d
    return pl.pallas_call(
        _mod_kernel,
        grid=(depth, nchunk),
        in_specs=[
            _const_spec((MOD_ROWS, d)),
            pl.BlockSpec((1, d, d), lambda l, n: (l, 0, n)),
            pl.BlockSpec((1, 1, d), lambda l, n: (l, 0, n)),
        ],
        out_specs=pl.BlockSpec((1, MOD_ROWS, d), lambda l, n: (l, 0, n)),
        out_shape=jax.ShapeDtypeStruct((depth, MOD_ROWS, six_d), F32),
        compiler_params=_params(2),
        name="modulation",
    )(cond, ada_w, ada_b.reshape(depth, 1, six_d))


def _chunk_matrices():
    t = np.arange(TM)
    c, l = t // CHUNK, t % CHUNK
    same = (c[:, None] == c[None, :]).astype(np.float32)
    li, ls = l[:, None], l[None, :]
    mid_f = CHUNK // 2 - 1
    mid_b = CHUNK // 2
    d1f = same * ((ls <= li).astype(np.float32) - (ls <= mid_f).astype(np.float32))
    d1b = same * ((ls >= li).astype(np.float32) - (ls >= mid_b).astype(np.float32))
    inchunk = (np.arange(SUBLANES)[:, None] == c[None, :]).astype(np.float32)
    mf = np.concatenate([inchunk * (l <= mid_f), inchunk * (l > mid_f), inchunk])
    mb = np.concatenate([inchunk * (l >= mid_b), inchunk * (l < mid_b), inchunk])
    as_bf16 = lambda a: jnp.asarray(a, dtype=BF16)
    return as_bf16(d1f), as_bf16(d1b), as_bf16(mf), as_bf16(mb)


def _gate_outputs(q, k, la, d1_ref, m_ref, q_out, k_out, cv_out):
    hi, lo = _split(la)
    d1m = d1_ref[...]
    d1 = _dot(d1m, hi) + _dot(d1m, lo)
    q_out[0] = (q * jnp.exp(d1)).astype(BF16)
    k_out[0] = (k * jnp.exp(-d1)).astype(BF16)
    mm = m_ref[...]
    cv_out[0, 0] = jnp.exp(_dot(mm, hi) + _dot(mm, lo))


def _inproj0_kernel(nctx_blk, nb, gla_w, q_scale, x_ref, ctx_ref, pos_ref, mod_ref, nw_ref, wm_ref, wa_ref, a2_ref,
                    ab_ref, d1f_ref, d1b_ref, mf_ref, mb_ref,
                    xs_ref, qf_ref, kf_ref, qb_ref, kb_ref, v_ref, vt_ref, g_ref, u_ref, cvf_ref, cvb_ref):
    b, j = pl.program_id(0), pl.program_id(1)
    is_ctx = j < nctx_blk
    xin = jnp.where(is_ctx, ctx_ref[0], x_ref[0] + pos_ref[...])
    xs_ref[0] = xin
    d = xin.shape[-1]
    mod = mod_ref[pl.ds(jnp.where(is_ctx, nb, b), 1), :]
    h = _rms(xin) * nw_ref[...] * (1.0 + mod[:, d:2 * d]) + mod[:, 0:d]
    hb = h.astype(BF16)
    w = gla_w
    y = _dot(hb, wm_ref[...])
    q = y[:, 0:w] * q_scale
    k = y[:, w:2 * w]
    v = y[:, 2 * w:3 * w]
    g_ref[0] = y[:, 3 * w:4 * w].astype(BF16)
    u_ref[0] = y[:, 4 * w:5 * w].astype(BF16)
    v_ref[0] = v.astype(BF16)
    vt_ref[0] = v.T.astype(BF16)
    a = _dot(hb, wa_ref[...])
    z = _dot(a.astype(BF16), a2_ref[...]) + ab_ref[...]
    la = (jnp.minimum(z, 0.0) - jnp.log(1.0 + jnp.exp(-jnp.abs(z)))) * (1.0 / GLA_GATE_NORM)
    _gate_outputs(q, k, la[:, 0:w], d1f_ref, mf_ref, qf_ref, kf_ref, cvf_ref)
    _gate_outputs(q, k, la[:, w:2 * w], d1b_ref, mb_ref, qb_ref, kb_ref, cvb_ref)


def _inproj0(x, ctx, pos, mod, nw, wm, wa, a2, ab, consts, q_scale):
    bsz, seq, d = x.shape
    nctx = ctx.shape[1]
    nctx_blk, nlat_blk = nctx // TM, seq // TM
    nblk = nctx_blk + nlat_blk
    nt = nblk * TM
    w = wm.shape[1] // 5
    d1f, d1b, mf, mb = consts
    lat = lambda b, j: (b, jnp.maximum(j - nctx_blk, 0), 0)
    tok = lambda b, j: (b, j, 0)
    in_specs = [
        pl.BlockSpec((1, TM, d), lat),
        pl.BlockSpec((1, TM, d), lambda b, j: (b, jnp.minimum(j, nctx_blk - 1), 0)),
        pl.BlockSpec((TM, d), lambda b, j: (jnp.maximum(j - nctx_blk, 0), 0)),
        _const_spec(mod.shape), _const_spec(nw.shape), _const_spec(wm.shape), _const_spec(wa.shape),
        _const_spec(a2.shape), _const_spec(ab.shape), _const_spec(d1f.shape), _const_spec(d1b.shape),
        _const_spec(mf.shape), _const_spec(mb.shape),
    ]
    tokspec = lambda width: pl.BlockSpec((1, TM, width), tok)
    cvspec = pl.BlockSpec((1, 1, 3 * SUBLANES, w), lambda b, j: (b, j, 0, 0))
    out_specs = [tokspec(d), tokspec(w), tokspec(w), tokspec(w), tokspec(w), tokspec(w),
                 pl.BlockSpec((1, w, TM), lambda b, j: (b, 0, j)), tokspec(w), tokspec(w), cvspec, cvspec]
    sds = jax.ShapeDtypeStruct
    out_shape = [sds((bsz, nt, d), F32)] + [sds((bsz, nt, w), BF16)] * 5 + [sds((bsz, w, nt), BF16)] + \
                [sds((bsz, nt, w), BF16)] * 2 + [sds((bsz, nblk, 3 * SUBLANES, w), F32)] * 2
    return pl.pallas_call(
        functools.partial(_inproj0_kernel, nctx_blk, bsz, w, q_scale),
        grid=(bsz, nblk), in_specs=in_specs, out_specs=out_specs, out_shape=out_shape,
        compiler_params=_params(2), name="inproj_gla_s5",
    )(x, ctx, pos, mod, nw, wm, wa, a2, ab, d1f, d1b, mf, mb)


def _recur_kernel(n_heads, qf_ref, kf_ref, vf_ref, vtf_ref, cvf_ref, qb_ref, kb_ref, vb_ref, vtb_ref, cvb_ref,
                  of_ref, ob_ref, s_ref):
    @pl.when(pl.program_id(1) == 0)
    def _():
        s_ref[...] = jnp.zeros_like(s_ref)

    ri = lax.broadcasted_iota(I32, (CHUNK, CHUNK), 0)
    ci = lax.broadcasted_iota(I32, (CHUNK, CHUNK), 1)
    causal = (ci <= ri, ci >= ri)
    prow = lax.broadcasted_iota(I32, (2 * CHUNK, HEAD_W), 0)
    dirs = ((qf_ref, kf_ref, vf_ref, vtf_ref, cvf_ref, of_ref), (qb_ref, kb_ref, vb_ref, vtb_ref, cvb_ref, ob_ref))
    for d, (q_ref, k_ref, v_ref, vt_ref, cv_ref, o_ref) in enumerate(dirs):
        for cc in range(CPT):
            c = cc if d == 0 else CPT - 1 - cc
            r0 = c * CHUNK
            p0 = (c // 2) * 2 * CHUNK
            in_chunk = (prow >= r0 - p0) & (prow < r0 - p0 + CHUNK)
            e_mid = cv_ref[0, 0, c:c + 1, :]
            e_rest = cv_ref[0, 0, SUBLANES + c:SUBLANES + c + 1, :]
            e_all = cv_ref[0, 0, 2 * SUBLANES + c:2 * SUBLANES + c + 1, :]
            for h in range(n_heads):
                hs = slice(h * HEAD_W, (h + 1) * HEAD_W)
                qi = q_ref[0, r0:r0 + CHUNK, hs]
                ki = k_ref[0, r0:r0 + CHUNK, hs]
                vv = v_ref[0, r0:r0 + CHUNK, hs]
                kp = k_ref[0, p0:p0 + 2 * CHUNK, hs]
                kp = jnp.where(in_chunk, kp, jnp.zeros_like(kp))
                vtp = vt_ref[0, hs, p0:p0 + 2 * CHUNK]
                st = s_ref[d, h]
                attn = jnp.where(causal[d], _dot_nt(qi, ki), 0.0).astype(BF16)
                o = _dot(attn, vv) + _dot_nt(qi, (st * e_mid[:, hs]).astype(BF16))
                o_ref[0, r0:r0 + CHUNK, hs] = o.astype(o_ref.dtype)
                s_ref[d, h] = st * e_all[:, hs] + _dot(vtp, kp) * e_rest[:, hs]


def _recurrence(qf, kf, qb, kb, v, vt, cvf, cvb, n_heads, nctx_blk):
    bsz, nt, w = qf.shape
    nblk = nt // TM
    fwd = lambda j: j
    bwd = lambda j: jnp.where(j < nctx_blk, nctx_blk - 1 - j, nblk - 1 - (j - nctx_blk))
    specs = []
    for order in (fwd, bwd):
        tok = pl.BlockSpec((1, TM, w), lambda b, j, o=order: (b, o(j), 0))
        specs += [tok, tok, tok,
                  pl.BlockSpec((1, w, TM), lambda b, j, o=order: (b, 0, o(j))),
                  pl.BlockSpec((1, 1, 3 * SUBLANES, w), lambda b, j, o=order: (b, o(j), 0, 0))]
    out_specs = [pl.BlockSpec((1, TM, w), lambda b, j: (b, j, 0)),
                 pl.BlockSpec((1, TM, w), lambda b, j: (b, bwd(j), 0))]
    return pl.pallas_call(
        functools.partial(_recur_kernel, n_heads),
        grid=(bsz, nblk), in_specs=specs, out_specs=out_specs,
        out_shape=[jax.ShapeDtypeStruct((bsz, nt, w), BF16)] * 2,
        scratch_shapes=[pltpu.VMEM((2, n_heads, HEAD_W, HEAD_W), F32)],
        compiler_params=_params(2), name=f"recurrence_h{n_heads}",
    )(qf, kf, v, vt, cvf, qb, kb, v, vt, cvb)


def _s5_operators(lam_re, lam_im, log_dt, b_re, b_im, c_re, c_im, d_skip):
    L = S5_L
    ndir, ng, p = lam_re.shape
    hs = b_re.shape[-1]
    dt = jnp.exp(log_dt)[..., None]
    lr, li = lam_re * dt, lam_im * dt
    mag = jnp.exp(lr)
    a_re, a_im = mag * jnp.cos(li), mag * jnp.sin(li)
    den = lam_re * lam_re + lam_im * lam_im
    f_re = ((a_re - 1.0) * lam_re + a_im * lam_im) / den
    f_im = (a_im * lam_re - (a_re - 1.0) * lam_im) / den
    bb_re = f_re[..., None] * b_re - f_im[..., None] * b_im
    bb_im = f_re[..., None] * b_im + f_im[..., None] * b_re
    m = jnp.arange(L + 1, dtype=F32)[:, None, None, None]
    pw_mag = jnp.exp(m * lr[None])
    pw_re, pw_im = pw_mag * jnp.cos(m * li[None]), pw_mag * jnp.sin(m * li[None])
    ab_re = pw_re[..., None] * bb_re[None] - pw_im[..., None] * bb_im[None]
    ab_im = pw_re[..., None] * bb_im[None] + pw_im[..., None] * bb_re[None]
    kern = jnp.einsum('dghp,mdgpk->mdghk', c_re, ab_re) - jnp.einsum('dghp,mdgpk->mdghk', c_im, ab_im)
    ca_re = c_re[None] * pw_re[:, :, :, None, :] - c_im[None] * pw_im[:, :, :, None, :]
    ca_im = c_re[None] * pw_im[:, :, :, None, :] + c_im[None] * pw_re[:, :, :, None, :]

    jj, ii = np.arange(L)[:, None], np.arange(L)[None, :]
    toep = jnp.zeros((ng, L, hs, L, hs), F32)
    wz, vo = [], []
    for di in range(ndir):
        lag = (ii - jj) if di == 0 else (jj - ii)
        valid = jnp.asarray(lag >= 0, F32)
        kd = kern[np.clip(lag, 0, L), di]
        toep = toep + jnp.transpose(kd * valid[:, :, None, None, None], (2, 0, 4, 1, 3))
        e_in = (L - 1 - np.arange(L)) if di == 0 else np.arange(L)
        e_out = (np.arange(L) + 1) if di == 0 else (L - np.arange(L))
        w_re = jnp.transpose(ab_re[e_in, di], (1, 0, 3, 2))
        w_im = jnp.transpose(ab_im[e_in, di], (1, 0, 3, 2))
        wz.append((w_re.reshape(ng, L * hs, p), w_im.reshape(ng, L * hs, p)))
        v_re = jnp.transpose(ca_re[e_out, di], (1, 3, 0, 2))
        v_im = -jnp.transpose(ca_im[e_out, di], (1, 3, 0, 2))
        vo.append((v_re.reshape(ng, p, L * hs), v_im.reshape(ng, p, L * hs)))
    toep = toep.reshape(ng, L * hs, L * hs)

    npair = ng // 2
    cw = L * hs
    wz_p = jnp.zeros((npair, ndir, 2 * cw, 4 * p), F32)
    vo_p = jnp.zeros((npair, ndir, 4 * p, 2 * cw), F32)
    for di in range(ndir):
        for s in range(2):
            w_re, w_im = wz[di][0][s::2], wz[di][1][s::2]
            wz_p = wz_p.at[:, di, s * cw:(s + 1) * cw, s * p:(s + 1) * p].set(w_re)
            wz_p = wz_p.at[:, di, s * cw:(s + 1) * cw, 2 * p + s * p:2 * p + (s + 1) * p].set(w_im)
            v_re, v_im = vo[di][0][s::2], vo[di][1][s::2]
            vo_p = vo_p.at[:, di, s * p:(s + 1) * p, s * cw:(s + 1) * cw].set(v_re)
            vo_p = vo_p.at[:, di, 2 * p + s * p:2 * p + (s + 1) * p, s * cw:(s + 1) * cw].set(v_im)
    toep_p = toep.reshape(npair, 2, cw, cw)
    al = jnp.stack([pw_re[L], pw_im[L]], axis=1)
    al_p = jnp.transpose(al.reshape(ndir, 2, npair, 2 * p), (2, 0, 1, 3))
    d_p = jnp.tile(d_skip.reshape(ng, 1, hs), (1, L, 1)).reshape(npair, 1, 2 * cw)
    return wz_p.astype(BF16), vo_p.astype(BF16), toep_p.astype(BF16), al_p, d_p


def _s5_kernel(nc, ncc, bsz, u_ref, wz_ref, vo_ref, tp_ref, al_ref, d_ref, y_ref, zf, zb, xf, xb):
    rows_total = nc * bsz
    rb = LANES
    half = zf.shape[1] // 2
    cw = tp_ref.shape[-1]

    def z_body(i, carry):
        rows = pl.ds(pl.multiple_of(i * rb, rb), rb)
        ub = u_ref[0, rows, :]
        zf[rows, :] = _dot(ub, wz_ref[0, 0])
        zb[rows, :] = _dot(ub, wz_ref[0, 1])
        return carry

    lax.fori_loop(0, rows_total // rb, z_body, 0)

    def scan(z, xs, di, order):
        a_re = jnp.broadcast_to(al_ref[0, di, 0:1, :], (bsz, half))
        a_im = jnp.broadcast_to(al_ref[0, di, 1:2, :], (bsz, half))

        def body(i, carry):
            x_re, x_im = carry
            rows = pl.ds(pl.multiple_of(order(i) * bsz, bsz), bsz)
            xs[rows, 0:half] = x_re
            xs[rows, half:2 * half] = x_im
            zc = z[rows, :]
            return (a_re * x_re - a_im * x_im + zc[:, 0:half], a_re * x_im + a_im * x_re + zc[:, half:2 * half])

        zero = jnp.zeros((bsz, half), F32)
        lax.fori_loop(0, nc, body, (zero, zero))

    scan(zf, xf, 0, lambda i: i)
    scan(zb, xb, 1, lambda i: jnp.where(i < ncc, ncc - 1 - i, nc - 1 - (i - ncc)))

    def y_body(i, carry):
        rows = pl.ds(pl.multiple_of(i * rb, rb), rb)
        ub = u_ref[0, rows, :]
        y = _dot(xf[rows, :].astype(BF16), vo_ref[0, 0]) + _dot(xb[rows, :].astype(BF16), vo_ref[0, 1])
        yt = jnp.concatenate([_dot(ub[:, 0:cw], tp_ref[0, 0]), _dot(ub[:, cw:2 * cw], tp_ref[0, 1])], axis=1)
        y_ref[0, rows, :] = (y + yt + ub.astype(F32) * d_ref[0]).astype(y_ref.dtype)
        return carry

    lax.fori_loop(0, rows_total // rb, y_body, 0)


def _s5(u_cr, ops, nc, ncc, bsz):
    wz_p, vo_p, toep_p, al_p, d_p = ops
    npair, rows, width = u_cr.shape
    per_pair = lambda a: pl.BlockSpec((1,) + a.shape[1:], lambda p: (p,) + (0,) * (a.ndim - 1))
    return pl.pallas_call(
        functools.partial(_s5_kernel, nc, ncc, bsz),
        grid=(npair,),
        in_specs=[per_pair(u_cr), per_pair(wz_p), per_pair(vo_p), per_pair(toep_p), per_pair(al_p), per_pair(d_p)],
        out_specs=per_pair(u_cr),
        out_shape=jax.ShapeDtypeStruct(u_cr.shape, BF16),
        scratch_shapes=[pltpu.VMEM((rows, wz_p.shape[-1]), F32)] * 4,
        compiler_params=_params(1), name="s5",
    )(u_cr, wz_p, vo_p, toep_p, al_p, d_p)


def _route(h2, rwh_ref, rwl_ref, rb_ref, utri_ref, ones_ref, carry_ref):
    hi, lo = _split(h2)
    logits = _dot_nt(rwh_ref[...], hi) + _dot_nt(rwl_ref[...], hi) + _dot_nt(rwh_ref[...], lo)
    aff = _sigmoid(logits)
    sel = aff + rb_ref[...]
    epg = EXPERTS_PER_GROUP
    s = [sel[N_GROUPS * p:N_GROUPS * (p + 1), :] for p in range(epg)]
    a = [aff[N_GROUPS * p:N_GROUPS * (p + 1), :] for p in range(epg)]
    m1, n1 = jnp.maximum(s[0], s[1]), jnp.minimum(s[0], s[1])
    m2, n2 = jnp.maximum(s[2], s[3]), jnp.minimum(s[2], s[3])
    score = jnp.maximum(m1, m2) + jnp.maximum(jnp.minimum(m1, m2), jnp.maximum(n1, n2))
    gi = lax.broadcasted_iota(I32, score.shape, 0)
    best = jnp.max(score, axis=0, keepdims=True)
    gidx = jnp.min(jnp.where(score == best, gi, N_GROUPS), axis=0, keepdims=True)
    onehot = gi == gidx
    gates = []
    for p in range(epg):
        ahead = jnp.zeros_like(score)
        for q in range(epg):
            if q != p:
                beats = (s[q] >= s[p]) if q < p else (s[q] > s[p])
                ahead = ahead + jnp.where(beats, 1.0, 0.0)
        picked = jnp.where(onehot, jnp.where(ahead < float(TOP_K) - 0.5, a[p], 0.0), 0.0)
        gates.append(jnp.sum(picked, axis=0, keepdims=True))
    den = gates[0] + gates[1] + gates[2] + gates[3]
    gates = [g / den for g in gates]
    oh = jnp.where(onehot, 1.0, 0.0)
    before = _dot(oh, utri_ref[...])
    carry = carry_ref[...]
    rank = jnp.sum(jnp.where(onehot, before + carry, 0.0), axis=0, keepdims=True)
    carry_ref[...] = carry + _dot(oh, ones_ref[...])
    return gidx, rank.astype(I32), gates


def _mixout_kernel(has_s5, n_heads, nctx_blk, tok_off, nb, *refs):
    if has_s5:
        (x_ref, of_ref, ob_ref, g_ref, y5_ref, mod_ref, gn_ref, gluw_ref, glub_ref, wo_ref, nf_ref,
         rwh_ref, rwl_ref, rb_ref, utri_ref, ones_ref, x1_ref, pay_ref, meta_ref, cnt_ref, carry_ref) = refs
    else:
        (x_ref, of_ref, ob_ref, g_ref, mod_ref, gn_ref, wo_ref, nf_ref,
         rwh_ref, rwl_ref, rb_ref, utri_ref, ones_ref, x1_ref, pay_ref, meta_ref, cnt_ref, carry_ref) = refs
    b, j = pl.program_id(0), pl.program_id(1)

    @pl.when((b == 0) & (j == 0))
    def _():
        carry_ref[...] = jnp.zeros_like(carry_ref)

    x = x_ref[0]
    d = x.shape[-1]
    mod = mod_ref[pl.ds(jnp.where(j + tok_off < nctx_blk, nb, b), 1), :]
    o = of_ref[0].astype(F32) + ob_ref[0].astype(F32)
    mixed = _head_rms(o, gn_ref[...], n_heads) * _silu(g_ref[0].astype(F32))
    if has_s5:
        act = _gelu_tanh(y5_ref[0].astype(F32))
        glu = act * _sigmoid(_dot(act.astype(BF16), gluw_ref[...]) + glub_ref[...])
        mixed = jnp.concatenate([mixed, glu], axis=1)
    x1 = x + mod[:, 2 * d:3 * d] * _dot(mixed.astype(BF16), wo_ref[...])
    x1_ref[0] = x1
    h2 = _rms(x1) * nf_ref[...] * (1.0 + mod[:, 4 * d:5 * d]) + mod[:, 3 * d:4 * d]

    gidx, rank, gates = _route(h2, rwh_ref, rwl_ref, rb_ref, utri_ref, ones_ref, carry_ref)
    gi = lax.broadcasted_iota(I32, (SUBLANES, TM), 0)
    meta_ref[0] = jnp.where(gi == 0, gidx, jnp.where(gi == 1, rank, 0))
    cnt_ref[...] = carry_ref[...]

    li = lax.broadcasted_iota(I32, (LANES, TM), 0)
    gt = jnp.zeros((LANES, TM), F32)
    for p, g in enumerate(gates):
        gt = gt + jnp.where(li == p, g, 0.0)
    hb = h2.astype(BF16).astype(F32)
    half = d // 2
    pay_ref[:, 0:half] = _pack_bf16_pair(hb[:, 0:half], hb[:, half:d])
    pay_ref[:, half:half + LANES] = pltpu.bitcast(gt.T, I32)


def _mixout(has_s5, n_heads, nctx_blk, tok_off, x, of, ob, g, y5, mod, gn, gluw, glub, wo, nf, rwh, rwl, rb, utri, ones):
    bsz, _, d = x.shape
    w = of.shape[-1]
    nblk = of.shape[1] // TM - tok_off
    tok = lambda width: pl.BlockSpec((1, TM, width), lambda b, j: (b, j + tok_off, 0))
    ins, specs = [x, of, ob, g], [tok(d), tok(w), tok(w), tok(w)]
    if has_s5:
        ins.append(y5)
        specs.append(tok(w))
    consts = [mod, gn] + ([gluw, glub] if has_s5 else []) + [wo, nf, rwh, rwl, rb, utri, ones]
    ins += consts
    specs += [_const_spec(c.shape) for c in consts]
    ntile = bsz * nblk
    pay_w = d // 2 + LANES
    lin = lambda b, j: b * nblk + j
    out_specs = [pl.BlockSpec((1, TM, d), lambda b, j: (b, j, 0)),
                 pl.BlockSpec((TM, pay_w), lambda b, j: (lin(b, j), 0)),
                 pl.BlockSpec((1, SUBLANES, TM), lambda b, j: (lin(b, j), 0, 0)),
                 _const_spec((N_GROUPS, TM))]
    sds = jax.ShapeDtypeStruct
    out_shape = [sds((bsz, nblk * TM, d), F32), sds((ntile * TM, pay_w), I32), sds((ntile, SUBLANES, TM), I32),
                 sds((N_GROUPS, TM), F32)]
    return pl.pallas_call(
        functools.partial(_mixout_kernel, has_s5, n_heads, nctx_blk, tok_off, bsz),
        grid=(bsz, nblk), in_specs=specs, out_specs=out_specs, out_shape=out_shape,
        scratch_shapes=[pltpu.VMEM((N_GROUPS, TM), F32)],
        compiler_params=_params(2), name="mixout_s5" if has_s5 else "mixout",
    )(*ins)


def _dispatch_kernel(seg_ref, meta_ref, pay_hbm, init_hbm, hs_hbm, sem):
    del init_hbm
    i = pl.program_id(0)

    def issue(r, carry):
        dst = seg_ref[meta_ref[0, 0, r]] + meta_ref[0, 1, r]
        pltpu.make_async_copy(pay_hbm.at[pl.ds(i * TM + r, 1)], hs_hbm.at[pl.ds(dst, 1)], sem).start()
        return carry

    lax.fori_loop(0, TM, issue, 0)
    pltpu.make_async_copy(pay_hbm.at[pl.ds(0, TM)], hs_hbm.at[pl.ds(0, TM)], sem).wait()


def _dispatch(seg, meta, pay, n_sorted):
    ntile = meta.shape[0]
    init = jnp.zeros((n_sorted, pay.shape[1]), pay.dtype)
    grid_spec = pltpu.PrefetchScalarGridSpec(
        num_scalar_prefetch=1, grid=(ntile,),
        in_specs=[pl.BlockSpec((1, SUBLANES, TM), lambda i, seg: (i, 0, 0), memory_space=pltpu.SMEM),
                  pl.BlockSpec(memory_space=pl.ANY), pl.BlockSpec(memory_space=pl.ANY)],
        out_specs=pl.BlockSpec(memory_space=pl.ANY),
        scratch_shapes=[pltpu.SemaphoreType.DMA(())])
    return pl.pallas_call(
        _dispatch_kernel, grid_spec=grid_spec,
        out_shape=jax.ShapeDtypeStruct(init.shape, init.dtype),
        input_output_aliases={3: 0},
        compiler_params=_params(1), name="moe_dispatch",
    )(seg, meta, pay, init)


def _moe_kernel(tg_ref, nv_ref, hs_ref, wg_ref, wu_ref, wd_ref, ys_ref):
    del tg_ref

    @pl.when(pl.program_id(0) < nv_ref[0])
    def _():
        words = hs_ref[...]
        half = wg_ref.shape[1] // 2
        ha, hb = _unpack_bf16_pair(words[:, 0:half])
        ha, hb = ha.astype(BF16), hb.astype(BF16)
        gate = pltpu.bitcast(words[:, half:half + LANES], F32)
        epg = EXPERTS_PER_GROUP
        fe = wg_ref.shape[2] // epg
        acc = jnp.zeros((TM, wd_ref.shape[2]), F32)
        for p in range(epg):
            cs = slice(p * fe, (p + 1) * fe)
            gp = _dot(ha, wg_ref[0, 0:half, cs]) + _dot(hb, wg_ref[0, half:2 * half, cs])
            up = _dot(ha, wu_ref[0, 0:half, cs]) + _dot(hb, wu_ref[0, half:2 * half, cs])
            act = _silu(gp) * up * gate[:, p:p + 1]
            acc = acc + _dot(act.astype(BF16), wd_ref[0, cs, :])
        yb = acc.astype(BF16).astype(F32)
        ys_ref[...] = _pack_bf16_pair(yb[:, 0:half], yb[:, half:2 * half])

    @pl.when(pl.program_id(0) >= nv_ref[0])
    def _():
        ys_ref[...] = jnp.zeros_like(ys_ref)


def _moe(tile_group, n_valid, hs, wg, wu, wd):
    n_sorted, pay_w = hs.shape
    ntile = n_sorted // TM
    d = wg.shape[1]
    clamp = lambda i, tg, nv: jnp.minimum(i, nv[0] - 1)
    wspec = lambda a: pl.BlockSpec((1,) + a.shape[1:], lambda i, tg, nv: (tg[i], 0, 0))
    grid_spec = pltpu.PrefetchScalarGridSpec(
        num_scalar_prefetch=2, grid=(ntile,),
        in_specs=[pl.BlockSpec((TM, pay_w), lambda i, tg, nv: (clamp(i, tg, nv), 0)), wspec(wg), wspec(wu), wspec(wd)],
        out_specs=pl.BlockSpec((TM, d // 2), lambda i, tg, nv: (i, 0)))
    return pl.pallas_call(
        _moe_kernel, grid_spec=grid_spec,
        out_shape=jax.ShapeDtypeStruct((n_sorted, d // 2), I32),
        compiler_params=_params(1), name="moe_experts",
    )(tile_group, n_valid, hs, wg, wu, wd)


def _gather_ffn(seg_ref, meta_ref, ys_hbm, buf, sem):
    def issue(r, carry):
        src = seg_ref[meta_ref[0, 0, r]] + meta_ref[0, 1, r]
        pltpu.make_async_copy(ys_hbm.at[pl.ds(src, 1)], buf.at[pl.ds(r, 1)], sem).start()
        return carry

    lax.fori_loop(0, TM, issue, 0)
    pltpu.make_async_copy(ys_hbm.at[pl.ds(0, TM)], buf, sem).wait()
    a, b = _unpack_bf16_pair(buf[...])
    return jnp.concatenate([a, b], axis=1)


def _inproj1_kernel(nctx_blk, nb, seg_ref, meta_ref, ys_hbm, x_ref, mod0_ref, mod1_ref, nw_ref, w_ref, lb_ref,
                    d1f_ref, d1b_ref, mf_ref, mb_ref,
                    x2_ref, qf_ref, kf_ref, qb_ref, kb_ref, v_ref, vt_ref, g_ref, cvf_ref, cvb_ref, buf, sem):
    b, j = pl.program_id(0), pl.program_id(1)
    ffn = _gather_ffn(seg_ref, meta_ref, ys_hbm, buf, sem)
    x = x_ref[0]
    d = x.shape[-1]
    row = jnp.where(j < nctx_blk, nb, b)
    mod0 = mod0_ref[pl.ds(row, 1), :]
    mod1 = mod1_ref[pl.ds(row, 1), :]
    x2 = x + mod0[:, 5 * d:6 * d] * ffn
    x2_ref[0] = x2
    hb = (_rms(x2) * nw_ref[...] * (1.0 + mod1[:, d:2 * d]) + mod1[:, 0:d]).astype(BF16)
    proj = lambda n: _dot(hb, w_ref[:, n * d:(n + 1) * d])
    q = _silu(proj(0))
    lb = lb_ref[...]
    v = proj(3)
    v_ref[0] = v.astype(BF16)
    vt_ref[0] = v.T.astype(BF16)
    g_ref[0] = proj(4).astype(BF16)
    f_f = lb + (1.0 - lb) * _sigmoid(proj(1))
    _gate_outputs(q, 1.0 - f_f, jnp.log(f_f), d1f_ref, mf_ref, qf_ref, kf_ref, cvf_ref)
    f_b = lb + (1.0 - lb) * _sigmoid(proj(2))
    _gate_outputs(q, 1.0 - f_b, jnp.log(f_b), d1b_ref, mb_ref, qb_ref, kb_ref, cvb_ref)


def _inproj1(seg, meta, ys, x1, mod0, mod1, nw, w_in, lb, consts, nctx_blk):
    bsz, nt, d = x1.shape
    nblk = nt // TM
    d1f, d1b, mf, mb = consts
    cs = lambda a: pl.BlockSpec(a.shape, lambda b, j, seg: (0,) * a.ndim)
    tok = lambda width: pl.BlockSpec((1, TM, width), lambda b, j, seg: (b, j, 0))
    cvspec = pl.BlockSpec((1, 1, 3 * SUBLANES, d), lambda b, j, seg: (b, j, 0, 0))
    grid_spec = pltpu.PrefetchScalarGridSpec(
        num_scalar_prefetch=1, grid=(bsz, nblk),
        in_specs=[pl.BlockSpec((1, SUBLANES, TM), lambda b, j, seg: (b * nblk + j, 0, 0), memory_space=pltpu.SMEM),
                  pl.BlockSpec(memory_space=pl.ANY), tok(d), cs(mod0), cs(mod1), cs(nw), cs(w_in), cs(lb),
                  cs(d1f), cs(d1b), cs(mf), cs(mb)],
        out_specs=[tok(d)] * 6 + [pl.BlockSpec((1, d, TM), lambda b, j, seg: (b, 0, j)), tok(d), cvspec, cvspec],
        scratch_shapes=[pltpu.VMEM((TM, d // 2), I32), pltpu.SemaphoreType.DMA(())])
    sds = jax.ShapeDtypeStruct
    out_shape = [sds((bsz, nt, d), F32)] + [sds((bsz, nt, d), BF16)] * 5 + [sds((bsz, d, nt), BF16)] + \
                [sds((bsz, nt, d), BF16)] + [sds((bsz, nblk, 3 * SUBLANES, d), F32)] * 2
    return pl.pallas_call(
        functools.partial(_inproj1_kernel, nctx_blk, bsz), grid_spec=grid_spec, out_shape=out_shape,
        compiler_params=_params(2), name="combine_inproj_hgrn",
    )(seg, meta, ys, x1, mod0, mod1, nw, w_in, lb, d1f, d1b, mf, mb)


def _final_kernel(seg_ref, meta_ref, ys_hbm, x_ref, mod_ref, fw_ref, o_ref, buf, sem):
    ffn = _gather_ffn(seg_ref, meta_ref, ys_hbm, buf, sem)
    x = x_ref[0]
    d = x.shape[-1]
    mod = mod_ref[pl.ds(pl.program_id(0), 1), :]
    o_ref[0] = _rms(x + mod[:, 5 * d:6 * d] * ffn) * fw_ref[...]


def _final(seg, meta, ys, x3, mod, fw):
    bsz, seq, d = x3.shape
    nblk = seq // TM
    cs = lambda a: pl.BlockSpec(a.shape, lambda b, j, seg: (0,) * a.ndim)
    tok = pl.BlockSpec((1, TM, d), lambda b, j, seg: (b, j, 0))
    grid_spec = pltpu.PrefetchScalarGridSpec(
        num_scalar_prefetch=1, grid=(bsz, nblk),
        in_specs=[pl.BlockSpec((1, SUBLANES, TM), lambda b, j, seg: (b * nblk + j, 0, 0), memory_space=pltpu.SMEM),
                  pl.BlockSpec(memory_space=pl.ANY), tok, cs(mod), cs(fw)],
        out_specs=tok,
        scratch_shapes=[pltpu.VMEM((TM, d // 2), I32), pltpu.SemaphoreType.DMA(())])
    return pl.pallas_call(
        _final_kernel, grid_spec=grid_spec, out_shape=jax.ShapeDtypeStruct((bsz, seq, d), F32),
        compiler_params=_params(2), name="combine_final_norm",
    )(seg, meta, ys, x3, mod, fw)


def _grid_sincos(n_tokens, dim):
    rows = n_tokens // GRID_W
    r, col = jnp.meshgrid(jnp.arange(rows, dtype=F32), jnp.arange(GRID_W, dtype=F32), indexing='ij')
    quarter = dim // 4
    omega = 1.0 / (10000.0 ** (jnp.arange(quarter, dtype=F32) / quarter))

    def emb(p):
        ang = p.reshape(-1, 1) * omega
        return jnp.concatenate([jnp.sin(ang), jnp.cos(ang)], axis=-1)

    return jnp.concatenate([emb(r), emb(col)], axis=-1)


def _pad_heads(w, n_heads):
    dk = w.shape[-1] // n_heads
    w = w.reshape(w.shape[:-1] + (n_heads, dk))
    w = jnp.pad(w, [(0, 0)] * (w.ndim - 1) + [(0, HEAD_W - dk)])
    return w.reshape(w.shape[:-2] + (n_heads * HEAD_W,))


def _router_tables(router_w, router_bias):
    n_exp = router_w.shape[1]
    epg = n_exp // N_GROUPS
    assert epg == EXPERTS_PER_GROUP
    perm = np.array([epg * g + p for p in range(epg) for g in range(N_GROUPS)])
    wt = jnp.pad(router_w.T[perm], ((0, LANES - n_exp), (0, 0)))
    hi = wt.astype(BF16)
    lo = (wt - hi.astype(F32)).astype(BF16)
    bias = jnp.pad(router_bias[perm], (0, LANES - n_exp))
    return hi, lo, jnp.broadcast_to(bias[:, None], (LANES, TM)).astype(F32)


def _segments(counts, n_tokens):
    cnt = counts[:, 0].astype(I32)
    tiles = (cnt + TM - 1) // TM
    ends = jnp.cumsum(tiles)
    seg = (ends - tiles) * TM
    ntile = n_tokens // TM + N_GROUPS
    n_valid = ends[-1]
    tidx = jnp.minimum(jnp.arange(ntile, dtype=I32), n_valid - 1)
    tile_group = jnp.sum((tidx[:, None] >= ends[None, :]).astype(I32), axis=1)
    return seg.astype(I32), tile_group.astype(I32), n_valid.reshape(1).astype(I32), ntile * TM


def _moe_weights(w_gate, w_up, w_down):
    n_exp, d, fe = w_gate.shape
    epg = n_exp // N_GROUPS
    grp = lambda w: jnp.transpose(w.reshape(N_GROUPS, epg, d, fe), (0, 2, 1, 3)).reshape(N_GROUPS, d, epg * fe)
    return grp(w_gate).astype(BF16), grp(w_up).astype(BF16), w_down.reshape(N_GROUPS, epg * fe, d).astype(BF16)


def _moe_layer(meta, counts, pay, weights):
    seg, tile_group, n_valid, n_sorted = _segments(counts, pay.shape[0])
    hs = _dispatch(seg, meta, pay, n_sorted)
    ys = _moe(tile_group, n_valid, hs, *weights)
    return seg, ys


def kernel(x, c, ctx, c_ctx, ada_w, ada_b, norm_mix, norm_ffn, ab_w_in, ab_w_out, gla_a2, gla_ab, gla_norm, s5_lam_re, s5_lam_im, s5_log_dt, s5_b_re, s5_b_im, s5_c_re, s5_c_im, s5_d, s5_glu_w, s5_glu_b, hg_w_in, hg_w_out, hg_lb_logits, hg_norm, router_w, router_bias, moe_w_gate, moe_w_up, moe_w_down, final_norm):
    bsz, seq, d = x.shape
    nctx = ctx.shape[1]
    depth = ada_w.shape[0]
    assert depth == 2 and seq % TM == 0 and nctx % TM == 0 and bsz % SUBLANES == 0 and bsz < MOD_ROWS
    nctx_blk = nctx // TM
    nt = nctx + seq
    nc = nt // S5_L
    assert (nc * bsz) % LANES == 0

    cond = jnp.zeros((MOD_ROWS, d), F32).at[:bsz].set(c).at[bsz].set(c_ctx)
    mod = _modulation(cond, ada_w, ada_b)
    consts = _chunk_matrices()
    row = lambda v: v.reshape(1, -1)

    qk = gla_a2.shape[-1]
    gv = ab_w_out.shape[1] // 2
    rank = gla_a2.shape[2]
    w_in = ab_w_in[0]
    o_v, o_g, o_a, o_u = 2 * qk, 2 * qk + gv, 2 * qk + 2 * gv, 2 * qk + 2 * gv + 2 * rank
    wm = jnp.concatenate([_pad_heads(w_in[:, 0:qk], GLA_HEADS), _pad_heads(w_in[:, qk:o_v], GLA_HEADS),
                          w_in[:, o_v:o_g], w_in[:, o_g:o_a], w_in[:, o_u:]], axis=1).astype(BF16)
    wa = jnp.pad(w_in[:, o_a:o_u], ((0, 0), (0, LANES - 2 * rank))).astype(BF16)
    a2p = _pad_heads(gla_a2[0], GLA_HEADS)
    a2 = jnp.zeros((LANES, 2 * gv), F32).at[0:rank, 0:gv].set(a2p[0]).at[rank:2 * rank, gv:].set(a2p[1]).astype(BF16)
    ab = _pad_heads(gla_ab[0], GLA_HEADS).reshape(1, 2 * gv)
    pos = _grid_sincos(seq, d)
    (xs, qf, kf, qb, kb, v, vt, g, u, cvf, cvb) = _inproj0(x, ctx, pos, mod[0], row(norm_mix[0]), wm, wa, a2, ab, consts,
                                                               float(qk // GLA_HEADS) ** -0.5)
    o_f, o_b = _recurrence(qf, kf, qb, kb, v, vt, cvf, cvb, GLA_HEADS, nctx_blk)

    ng, hs5 = s5_log_dt.shape[2], s5_b_re.shape[-1]
    ops = _s5_operators(s5_lam_re[0], s5_lam_im[0], s5_log_dt[0], s5_b_re[0], s5_b_im[0], s5_c_re[0], s5_c_im[0], s5_d[0])
    u_cr = jnp.transpose(u.reshape(bsz, nc, S5_L, ng // 2, 2, hs5), (3, 1, 0, 4, 2, 5)).reshape(ng // 2, nc * bsz, 2 * S5_L * hs5)
    y_cr = _s5(u_cr, ops, nc, nctx // S5_L, bsz)
    y5 = jnp.transpose(y_cr.reshape(ng // 2, nc, bsz, 2, S5_L, hs5), (2, 1, 4, 0, 3, 5)).reshape(bsz, nt, gv)

    rwh, rwl, rb = _router_tables(router_w, router_bias)
    utri = jnp.asarray(np.triu(np.ones((TM, TM), np.float32), 1))
    ones = jnp.ones((TM, TM), F32)
    x1, pay, meta, counts = _mixout(True, GLA_HEADS, nctx_blk, 0, xs, o_f, o_b, g, y5, mod[0], row(gla_norm[0]),
                                    s5_glu_w[0].astype(BF16), row(s5_glu_b[0]), ab_w_out[0].astype(BF16),
                                    row(norm_ffn[0]), rwh, rwl, rb, utri, ones)
    seg, ys = _moe_layer(meta, counts, pay, _moe_weights(moe_w_gate[0], moe_w_up[0], moe_w_down[0]))

    lb_all = jax.nn.softmax(hg_lb_logits.astype(F32), axis=0)
    lb_all = jnp.cumsum(lb_all, axis=0) - lb_all[0]
    (x2, qf, kf, qb, kb, v, vt, g, cvf, cvb) = _inproj1(seg, meta, ys, x1, mod[0], mod[1], row(norm_mix[1]),
                                                          hg_w_in[0].astype(BF16), row(lb_all[1]), consts, nctx_blk)
    o_f, o_b = _recurrence(qf, kf, qb, kb, v, vt, cvf, cvb, HG_HEADS, nctx_blk)
    x3, pay, meta, counts = _mixout(False, HG_HEADS, nctx_blk, nctx_blk, x2, o_f, o_b, g, None, mod[1], row(hg_norm[0]),
                                    None, None, hg_w_out[0].astype(BF16), row(norm_ffn[1]), rwh, rwl, rb, utri, ones)
    seg, ys = _moe_layer(meta, counts, pay, _moe_weights(moe_w_gate[1], moe_w_up[1], moe_w_down[1]))
    return _final(seg, meta, ys, x3, mod[1], row(final_norm))
```

```python
import functools
import math

import numpy as np
import jax
import jax.numpy as jnp
from jax import lax
from jax.experimental import pallas as pl
from jax.experimental.pallas import tpu as pltpu

F32, BF16, I32 = jnp.float32, jnp.bfloat16, jnp.int32

EPS = 1e-6
CHUNK = 64
GRID_W = 64
GLA_HEADS = 4
GLA_GATE_NORM = 16.0
HG_HEADS = 8
N_GROUPS = 8
EXPERTS_PER_GROUP = 4
TOP_K = 2

LANES = 128
SUBLANES = 8
TM = 256
CPT = TM // CHUNK
HEAD_W = 128
S5_L = 16
MOD_ROWS = 16
DMA_UNROLL = 8
VMEM_LIMIT = 56 * 1024 * 1024


def _dot(a, b):
    return jnp.dot(a, b, preferred_element_type=F32)


def _dot_nt(a, b):
    return lax.dot_general(a, b, (((1,), (1,)), ((), ())), preferred_element_type=F32)


def _split(x):
    hi = x.astype(BF16)
    lo = (x - hi.astype(F32)).astype(BF16)
    return hi, lo


def _sigmoid(x):
    return 1.0 / (1.0 + jnp.exp(-x))


def _silu(x):
    return x * _sigmoid(x)


def _gelu_tanh(x):
    return 0.5 * x * (1.0 + jnp.tanh(math.sqrt(2.0 / math.pi) * (x + 0.044715 * (x * x * x))))


def _rms(x):
    return x * lax.rsqrt(jnp.mean(x * x, axis=-1, keepdims=True) + EPS)


def _head_rms(o, gain, n_heads):
    outs = []
    for h in range(n_heads):
        oh = o[:, h * HEAD_W:(h + 1) * HEAD_W]
        outs.append(_rms(oh) * gain)
    return jnp.concatenate(outs, axis=1)


def _pack_bf16_pair(a, b):
    ai = pltpu.bitcast(a, I32)
    bi = pltpu.bitcast(b, I32)
    return (ai & jnp.int32(-65536)) | lax.shift_right_logical(bi, jnp.int32(16))


def _unpack_bf16_pair(w):
    a = pltpu.bitcast(w & jnp.int32(-65536), F32)
    b = pltpu.bitcast(lax.shift_left(w, jnp.int32(16)), F32)
    return a, b


def _params(n_grid_dims, vmem=VMEM_LIMIT):
    return pltpu.CompilerParams(dimension_semantics=("arbitrary",) * n_grid_dims, vmem_limit_bytes=vmem)


def _const_spec(shape):
    nd = len(shape)
    return pl.BlockSpec(shape, lambda *_: (0,) * nd)


def _mod_kernel(c_ref, w_ref, b_ref, o_ref):
    s = _silu(c_ref[...])
    shi, slo = _split(s)
    whi, wlo = _split(w_ref[0])
    o_ref[0] = _dot(shi, whi) + _dot(shi, wlo) + _dot(slo, whi) + b_ref[0]


def _modulation(cond, ada_w, ada_b):
    depth, d, six_d = ada_w.shape
    nchunk = six_d // d
    return pl.pallas_call(
        _mod_kernel,
        grid=(depth, nchunk),
        in_specs=[
            _const_spec((MOD_ROWS, d)),
            pl.BlockSpec((1, d, d), lambda l, n: (l, 0, n)),
            pl.BlockSpec((1, 1, d), lambda l, n: (l, 0, n)),
        ],
        out_specs=pl.BlockSpec((1, MOD_ROWS, d), lambda l, n: (l, 0, n)),
        out_shape=jax.ShapeDtypeStruct((depth, MOD_ROWS, six_d), F32),
        compiler_params=_params(2),
        name="modulation",
    )(cond, ada_w, ada_b.reshape(depth, 1, six_d))


def _chunk_matrices():
    t = np.arange(TM)
    c, l = t // CHUNK, t % CHUNK
    same = (c[:, None] == c[None, :]).astype(np.float32)
    li, ls = l[:, None], l[None, :]
    mid_f = CHUNK // 2 - 1
    mid_b = CHUNK // 2
    d1f = same * ((ls <= li).astype(np.float32) - (ls <= mid_f).astype(np.float32))
    d1b = same * ((ls >= li).astype(np.float32) - (ls >= mid_b).astype(np.float32))
    inchunk = (np.arange(SUBLANES)[:, None] == c[None, :]).astype(np.float32)
    mf = np.concatenate([inchunk * (l <= mid_f), inchunk * (l > mid_f), inchunk])
    mb = np.concatenate([inchunk * (l >= mid_b), inchunk * (l < mid_b), inchunk])
    as_bf16 = lambda a: jnp.asarray(a, dtype=BF16)
    return as_bf16(d1f), as_bf16(d1b), as_bf16(mf), as_bf16(mb)


def _gate_outputs(q, k, la, d1_ref, m_ref, q_out, k_out, cv_out):
    hi, lo = _split(la)
    d1m = d1_ref[...]
    d1 = _dot(d1m, hi) + _dot(d1m, lo)
    q_out[0] = (q * jnp.exp(d1)).astype(BF16)
    k_out[0] = (k * jnp.exp(-d1)).astype(BF16)
    mm = m_ref[...]
    cv_out[0, 0] = jnp.exp(_dot(mm, hi) + _dot(mm, lo))


PAIR_W = 32
PAIRS_PER_TILE = LANES // PAIR_W
CROWS = TM // S5_L


def _to_chunk_rows(u, us_ref, ucr_ref):
    for q in range(us_ref.shape[0]):
        us_ref[q] = u[:, q * LANES:(q + 1) * LANES]
    lane = lax.broadcasted_iota(I32, (CROWS, LANES), 1)
    for p in range(ucr_ref.shape[0]):
        q, pp = divmod(p, PAIRS_PER_TILE)
        tiles = []
        for t in range(S5_L // PAIRS_PER_TILE):
            acc = None
            for jj in range(PAIRS_PER_TILE):
                piece = us_ref[q, pl.ds(t * PAIRS_PER_TILE + jj, CROWS, stride=S5_L), :]
                shift = (PAIR_W * (jj - pp)) % LANES
                if shift:
                    piece = pltpu.roll(piece, shift, axis=1)
                here = (lane >= PAIR_W * jj) & (lane < PAIR_W * (jj + 1))
                acc = piece if acc is None else jnp.where(here, piece, acc)
            tiles.append(acc)
        ucr_ref[p] = jnp.concatenate(tiles, axis=1).astype(BF16)


def _from_chunk_rows(ycr_ref, ys_ref):
    lane = lax.broadcasted_iota(I32, (CROWS, LANES), 1)
    for q in range(ycr_ref.shape[0] // PAIRS_PER_TILE):
        for i in range(S5_L):
            t, ii = divmod(i, PAIRS_PER_TILE)
            acc = None
            for pp in range(PAIRS_PER_TILE):
                piece = ycr_ref[q * PAIRS_PER_TILE + pp, :, t * LANES:(t + 1) * LANES].astype(F32)
                shift = (PAIR_W * (pp - ii)) % LANES
                if shift:
                    piece = pltpu.roll(piece, shift, axis=1)
                here = (lane >= PAIR_W * pp) & (lane < PAIR_W * (pp + 1))
                acc = piece if acc is None else jnp.where(here, piece, acc)
            ys_ref[q, pl.ds(i, CROWS, stride=S5_L), :] = acc
    return jnp.concatenate([ys_ref[q] for q in range(ys_ref.shape[0])], axis=1)


def _inproj0_kernel(nctx_blk, nb, gla_w, q_scale, x_ref, ctx_ref, pos_ref, mod_ref, nw_ref, wm_ref, wa_ref, a2_ref,
                    ab_ref, d1f_ref, d1b_ref, mf_ref, mb_ref,
                    xs_ref, qf_ref, kf_ref, qb_ref, kb_ref, v_ref, vt_ref, g_ref, ucr_ref, cvf_ref, cvb_ref, us_ref):
    b, j = pl.program_id(0), pl.program_id(1)
    is_ctx = j < nctx_blk
    xin = jnp.where(is_ctx, ctx_ref[0], x_ref[0] + pos_ref[...])
    xs_ref[0] = xin
    d = xin.shape[-1]
    mod = mod_ref[pl.ds(jnp.where(is_ctx, nb, b), 1), :]
    h = _rms(xin) * nw_ref[...] * (1.0 + mod[:, d:2 * d]) + mod[:, 0:d]
    hb = h.astype(BF16)
    w = gla_w
    y = _dot(hb, wm_ref[...])
    q = y[:, 0:w] * q_scale
    k = y[:, w:2 * w]
    v = y[:, 2 * w:3 * w]
    g_ref[0] = y[:, 3 * w:4 * w].astype(BF16)
    _to_chunk_rows(y[:, 4 * w:5 * w], us_ref, ucr_ref)
    v_ref[0] = v.astype(BF16)
    vt_ref[0] = v.T.astype(BF16)
    a = _dot(hb, wa_ref[...])
    z = _dot(a.astype(BF16), a2_ref[...]) + ab_ref[...]
    la = (jnp.minimum(z, 0.0) - jnp.log(1.0 + jnp.exp(-jnp.abs(z)))) * (1.0 / GLA_GATE_NORM)
    _gate_outputs(q, k, la[:, 0:w], d1f_ref, mf_ref, qf_ref, kf_ref, cvf_ref)
    _gate_outputs(q, k, la[:, w:2 * w], d1b_ref, mb_ref, qb_ref, kb_ref, cvb_ref)


def _inproj0(x, ctx, pos, mod, nw, wm, wa, a2, ab, consts, q_scale):
    bsz, seq, d = x.shape
    nctx = ctx.shape[1]
    nctx_blk, nlat_blk = nctx // TM, seq // TM
    nblk = nctx_blk + nlat_blk
    nt = nblk * TM
    w = wm.shape[1] // 5
    d1f, d1b, mf, mb = consts
    lat = lambda b, j: (b, jnp.maximum(j - nctx_blk, 0), 0)
    tok = lambda b, j: (b, j, 0)
    in_specs = [
        pl.BlockSpec((1, TM, d), lat),
        pl.BlockSpec((1, TM, d), lambda b, j: (b, jnp.minimum(j, nctx_blk - 1), 0)),
        pl.BlockSpec((TM, d), lambda b, j: (jnp.maximum(j - nctx_blk, 0), 0)),
        _const_spec(mod.shape), _const_spec(nw.shape), _const_spec(wm.shape), _const_spec(wa.shape),
        _const_spec(a2.shape), _const_spec(ab.shape), _const_spec(d1f.shape), _const_spec(d1b.shape),
        _const_spec(mf.shape), _const_spec(mb.shape),
    ]
    tokspec = lambda width: pl.BlockSpec((1, TM, width), tok)
    cvspec = pl.BlockSpec((1, 1, 3 * SUBLANES, w), lambda b, j: (b, j, 0, 0))
    npair, crow_w = w // PAIR_W, S5_L * PAIR_W
    out_specs = [tokspec(d), tokspec(w), tokspec(w), tokspec(w), tokspec(w), tokspec(w),
                 pl.BlockSpec((1, w, TM), lambda b, j: (b, 0, j)), tokspec(w),
                 pl.BlockSpec((npair, CROWS, crow_w), lambda b, j: (0, b * nblk + j, 0)), cvspec, cvspec]
    sds = jax.ShapeDtypeStruct
    out_shape = [sds((bsz, nt, d), F32)] + [sds((bsz, nt, w), BF16)] * 5 + [sds((bsz, w, nt), BF16)] + \
                [sds((bsz, nt, w), BF16), sds((npair, bsz * nblk * CROWS, crow_w), BF16)] + \
                [sds((bsz, nblk, 3 * SUBLANES, w), F32)] * 2
    return pl.pallas_call(
        functools.partial(_inproj0_kernel, nctx_blk, bsz, w, q_scale),
        grid=(bsz, nblk), in_specs=in_specs, out_specs=out_specs, out_shape=out_shape,
        scratch_shapes=[pltpu.VMEM((w // LANES, TM, LANES), F32)],
        compiler_params=_params(2), name="inproj_gla_s5",
    )(x, ctx, pos, mod, nw, wm, wa, a2, ab, d1f, d1b, mf, mb)


def _recur_kernel(n_heads, qf_ref, kf_ref, vf_ref, vtf_ref, cvf_ref, qb_ref, kb_ref, vb_ref, vtb_ref, cvb_ref,
                  of_ref, ob_ref, s_ref):
    @pl.when(pl.program_id(1) == 0)
    def _():
        s_ref[...] = jnp.zeros_like(s_ref)

    ri = lax.broadcasted_iota(I32, (CHUNK, CHUNK), 0)
    ci = lax.broadcasted_iota(I32, (CHUNK, CHUNK), 1)
    causal = (ci <= ri, ci >= ri)
    prow = lax.broadcasted_iota(I32, (2 * CHUNK, HEAD_W), 0)
    dirs = ((qf_ref, kf_ref, vf_ref, vtf_ref, cvf_ref, of_ref), (qb_ref, kb_ref, vb_ref, vtb_ref, cvb_ref, ob_ref))
    for d, (q_ref, k_ref, v_ref, vt_ref, cv_ref, o_ref) in enumerate(dirs):
        for cc in range(CPT):
            c = cc if d == 0 else CPT - 1 - cc
            r0 = c * CHUNK
            p0 = (c // 2) * 2 * CHUNK
            in_chunk = (prow >= r0 - p0) & (prow < r0 - p0 + CHUNK)
            e_mid = cv_ref[0, 0, c:c + 1, :]
            e_rest = cv_ref[0, 0, SUBLANES + c:SUBLANES + c + 1, :]
            e_all = cv_ref[0, 0, 2 * SUBLANES + c:2 * SUBLANES + c + 1, :]
            for h in range(n_heads):
                hs = slice(h * HEAD_W, (h + 1) * HEAD_W)
                qi = q_ref[0, r0:r0 + CHUNK, hs]
                ki = k_ref[0, r0:r0 + CHUNK, hs]
                vv = v_ref[0, r0:r0 + CHUNK, hs]
                kp = k_ref[0, p0:p0 + 2 * CHUNK, hs]
                kp = jnp.where(in_chunk, kp, jnp.zeros_like(kp))
                vtp = vt_ref[0, hs, p0:p0 + 2 * CHUNK]
                st = s_ref[d, h]
                attn = jnp.where(causal[d], _dot_nt(qi, ki), 0.0).astype(BF16)
                o = _dot(attn, vv) + _dot_nt(qi, (st * e_mid[:, hs]).astype(BF16))
                o_ref[0, r0:r0 + CHUNK, hs] = o.astype(o_ref.dtype)
                s_ref[d, h] = st * e_all[:, hs] + _dot(vtp, kp) * e_rest[:, hs]


def _recurrence(qf, kf, qb, kb, v, vt, cvf, cvb, n_heads, nctx_blk):
    bsz, nt, w = qf.shape
    nblk = nt // TM
    fwd = lambda j: j
    bwd = lambda j: jnp.where(j < nctx_blk, nctx_blk - 1 - j, nblk - 1 - (j - nctx_blk))
    specs = []
    for order in (fwd, bwd):
        tok = pl.BlockSpec((1, TM, w), lambda b, j, o=order: (b, o(j), 0))
        specs += [tok, tok, tok,
                  pl.BlockSpec((1, w, TM), lambda b, j, o=order: (b, 0, o(j))),
                  pl.BlockSpec((1, 1, 3 * SUBLANES, w), lambda b, j, o=order: (b, o(j), 0, 0))]
    out_specs = [pl.BlockSpec((1, TM, w), lambda b, j: (b, j, 0)),
                 pl.BlockSpec((1, TM, w), lambda b, j: (b, bwd(j), 0))]
    return pl.pallas_call(
        functools.partial(_recur_kernel, n_heads),
        grid=(bsz, nblk), in_specs=specs, out_specs=out_specs,
        out_shape=[jax.ShapeDtypeStruct((bsz, nt, w), BF16)] * 2,
        scratch_shapes=[pltpu.VMEM((2, n_heads, HEAD_W, HEAD_W), F32)],
        compiler_params=_params(2), name=f"recurrence_h{n_heads}",
    )(qf, kf, v, vt, cvf, qb, kb, v, vt, cvb)


def _s5_operators(lam_re, lam_im, log_dt, b_re, b_im, c_re, c_im, d_skip):
    L = S5_L
    ndir, ng, p = lam_re.shape
    hs = b_re.shape[-1]
    dt = jnp.exp(log_dt)[..., None]
    lr, li = lam_re * dt, lam_im * dt
    mag = jnp.exp(lr)
    a_re, a_im = mag * jnp.cos(li), mag * jnp.sin(li)
    den = lam_re * lam_re + lam_im * lam_im
    f_re = ((a_re - 1.0) * lam_re + a_im * lam_im) / den
    f_im = (a_im * lam_re - (a_re - 1.0) * lam_im) / den
    bb_re = f_re[..., None] * b_re - f_im[..., None] * b_im
    bb_im = f_re[..., None] * b_im + f_im[..., None] * b_re
    m = jnp.arange(L + 1, dtype=F32)[:, None, None, None]
    pw_mag = jnp.exp(m * lr[None])
    pw_re, pw_im = pw_mag * jnp.cos(m * li[None]), pw_mag * jnp.sin(m * li[None])
    ab_re = pw_re[..., None] * bb_re[None] - pw_im[..., None] * bb_im[None]
    ab_im = pw_re[..., None] * bb_im[None] + pw_im[..., None] * bb_re[None]
    kern = jnp.einsum('dghp,mdgpk->mdghk', c_re, ab_re) - jnp.einsum('dghp,mdgpk->mdghk', c_im, ab_im)
    ca_re = c_re[None] * pw_re[:, :, :, None, :] - c_im[None] * pw_im[:, :, :, None, :]
    ca_im = c_re[None] * pw_im[:, :, :, None, :] + c_im[None] * pw_re[:, :, :, None, :]

    jj, ii = np.arange(L)[:, None], np.arange(L)[None, :]
    toep = jnp.zeros((ng, L, hs, L, hs), F32)
    wz, vo = [], []
    for di in range(ndir):
        lag = (ii - jj) if di == 0 else (jj - ii)
        valid = jnp.asarray(lag >= 0, F32)
        kd = kern[np.clip(lag, 0, L), di]
        toep = toep + jnp.transpose(kd * valid[:, :, None, None, None], (2, 0, 4, 1, 3))
        e_in = (L - 1 - np.arange(L)) if di == 0 else np.arange(L)
        e_out = (np.arange(L) + 1) if di == 0 else (L - np.arange(L))
        w_re = jnp.transpose(ab_re[e_in, di], (1, 0, 3, 2))
        w_im = jnp.transpose(ab_im[e_in, di], (1, 0, 3, 2))
        wz.append((w_re.reshape(ng, L * hs, p), w_im.reshape(ng, L * hs, p)))
        v_re = jnp.transpose(ca_re[e_out, di], (1, 3, 0, 2))
        v_im = -jnp.transpose(ca_im[e_out, di], (1, 3, 0, 2))
        vo.append((v_re.reshape(ng, p, L * hs), v_im.reshape(ng, p, L * hs)))
    npair = ng // 2
    cw = 2 * L * hs
    eye = jnp.eye(2, dtype=F32)
    toep_p = jnp.einsum('psjkih,st->pjskith', toep.reshape(npair, 2, L, hs, L, hs), eye).reshape(npair, cw, cw)
    wz_p, vo_p = [], []
    for di in range(ndir):
        pack_w = lambda w: jnp.einsum('psjkn,st->pjsktn', w.reshape(npair, 2, L, hs, p), eye).reshape(npair, cw, 2 * p)
        wz_p.append(jnp.concatenate([pack_w(wz[di][0]), pack_w(wz[di][1])], axis=2))
        pack_v = lambda v: jnp.einsum('psnih,st->psnith', v.reshape(npair, 2, p, L, hs), eye).reshape(npair, 2 * p, cw)
        vo_p.append(jnp.concatenate([pack_v(vo[di][0]), pack_v(vo[di][1])], axis=1))
    wz_p, vo_p = jnp.stack(wz_p, axis=1), jnp.stack(vo_p, axis=1)
    al = jnp.stack([pw_re[L], pw_im[L]], axis=1)
    al_p = jnp.transpose(al.reshape(ndir, 2, npair, 2 * p), (2, 0, 1, 3))
    d_p = jnp.tile(d_skip.reshape(npair, 1, 2 * hs), (1, L, 1)).reshape(npair, 1, cw)
    return wz_p.astype(BF16), vo_p.astype(BF16), toep_p.astype(BF16), al_p, d_p


def _s5_kernel(nc, ncc, bsz, u_ref, wz_ref, vo_ref, tp_ref, al_ref, d_ref, y_ref, zf, zb, xf, xb):
    rows_total = nc * bsz
    rb = LANES
    half = zf.shape[2]

    def z_body(i, carry):
        rows = pl.ds(pl.multiple_of(i * rb, rb), rb)
        ub = u_ref[0, rows, :]
        for z, di in ((zf, 0), (zb, 1)):
            zz = _dot(ub, wz_ref[0, di])
            z[0, rows, :] = zz[:, 0:half]
            z[1, rows, :] = zz[:, half:2 * half]
        return carry

    lax.fori_loop(0, rows_total // rb, z_body, 0)

    def scan(z, xs, di, order):
        a_re = jnp.broadcast_to(al_ref[0, di, 0:1, :], (bsz, half))
        a_im = jnp.broadcast_to(al_ref[0, di, 1:2, :], (bsz, half))

        def body(i, carry):
            x_re, x_im = carry
            rows = pl.ds(order(i), bsz, stride=nc)
            xs[0, rows, :] = x_re
            xs[1, rows, :] = x_im
            return (a_re * x_re - a_im * x_im + z[0, rows, :], a_re * x_im + a_im * x_re + z[1, rows, :])

        zero = jnp.zeros((bsz, half), F32)
        lax.fori_loop(0, nc, body, (zero, zero))

    scan(zf, xf, 0, lambda i: i)
    scan(zb, xb, 1, lambda i: jnp.where(i < ncc, ncc - 1 - i, nc - 1 - (i - ncc)))

    def y_body(i, carry):
        rows = pl.ds(pl.multiple_of(i * rb, rb), rb)
        ub = u_ref[0, rows, :]
        state = lambda xs: jnp.concatenate([xs[0, rows, :], xs[1, rows, :]], axis=1).astype(BF16)
        y = _dot(state(xf), vo_ref[0, 0]) + _dot(state(xb), vo_ref[0, 1])
        y_ref[0, rows, :] = (y + _dot(ub, tp_ref[0]) + ub.astype(F32) * d_ref[0]).astype(y_ref.dtype)
        return carry

    lax.fori_loop(0, rows_total // rb, y_body, 0)


def _s5(u_cr, ops, nc, ncc, bsz):
    wz_p, vo_p, toep_p, al_p, d_p = ops
    npair, rows, width = u_cr.shape
    per_pair = lambda a: pl.BlockSpec((1,) + a.shape[1:], lambda p: (p,) + (0,) * (a.ndim - 1))
    return pl.pallas_call(
        functools.partial(_s5_kernel, nc, ncc, bsz),
        grid=(npair,),
        in_specs=[per_pair(u_cr), per_pair(wz_p), per_pair(vo_p), per_pair(toep_p), per_pair(al_p), per_pair(d_p)],
        out_specs=per_pair(u_cr),
        out_shape=jax.ShapeDtypeStruct(u_cr.shape, BF16),
        scratch_shapes=[pltpu.VMEM((2, rows, wz_p.shape[-1] // 2), F32)] * 4,
        compiler_params=_params(1), name="s5",
    )(u_cr, wz_p, vo_p, toep_p, al_p, d_p)


def _route(h2, rwh_ref, rwl_ref, rb_ref, utri_ref, ones_ref, carry_ref):
    hi, lo = _split(h2)
    logits = _dot_nt(rwh_ref[...], hi) + _dot_nt(rwl_ref[...], hi) + _dot_nt(rwh_ref[...], lo)
    aff = _sigmoid(logits)
    sel = aff + rb_ref[...]
    epg = EXPERTS_PER_GROUP
    s = [sel[N_GROUPS * p:N_GROUPS * (p + 1), :] for p in range(epg)]
    a = [aff[N_GROUPS * p:N_GROUPS * (p + 1), :] for p in range(epg)]
    m1, n1 = jnp.maximum(s[0], s[1]), jnp.minimum(s[0], s[1])
    m2, n2 = jnp.maximum(s[2], s[3]), jnp.minimum(s[2], s[3])
    score = jnp.maximum(m1, m2) + jnp.maximum(jnp.minimum(m1, m2), jnp.maximum(n1, n2))
    gi = lax.broadcasted_iota(I32, score.shape, 0)
    best = jnp.max(score, axis=0, keepdims=True)
    gidx = jnp.min(jnp.where(score == best, gi, N_GROUPS), axis=0, keepdims=True)
    onehot = gi == gidx
    gates = []
    for p in range(epg):
        ahead = jnp.zeros_like(score)
        for q in range(epg):
            if q != p:
                beats = (s[q] >= s[p]) if q < p else (s[q] > s[p])
                ahead = ahead + jnp.where(beats, 1.0, 0.0)
        picked = jnp.where(onehot, jnp.where(ahead < float(TOP_K) - 0.5, a[p], 0.0), 0.0)
        gates.append(jnp.sum(picked, axis=0, keepdims=True))
    den = gates[0] + gates[1] + gates[2] + gates[3]
    gates = [g / den for g in gates]
    oh = jnp.where(onehot, 1.0, 0.0)
    before = _dot(oh, utri_ref[...])
    carry = carry_ref[...]
    rank = jnp.sum(jnp.where(onehot, before + carry, 0.0), axis=0, keepdims=True)
    carry_ref[...] = carry + _dot(oh, ones_ref[...])
    return gidx, rank.astype(I32), gates


def _mixout_kernel(has_s5, n_heads, nctx_blk, tok_off, nb, *refs):
    if has_s5:
        (x_ref, of_ref, ob_ref, g_ref, ycr_ref, mod_ref, gn_ref, gluw_ref, glub_ref, wo_ref, nf_ref,
         rwh_ref, rwl_ref, rb_ref, utri_ref, ones_ref, x1_ref, pay_ref, meta_ref, cnt_ref, carry_ref, ys_ref) = refs
    else:
        (x_ref, of_ref, ob_ref, g_ref, mod_ref, gn_ref, wo_ref, nf_ref,
         rwh_ref, rwl_ref, rb_ref, utri_ref, ones_ref, x1_ref, pay_ref, meta_ref, cnt_ref, carry_ref) = refs
    b, j = pl.program_id(0), pl.program_id(1)

    @pl.when((b == 0) & (j == 0))
    def _():
        carry_ref[...] = jnp.zeros_like(carry_ref)

    x = x_ref[0]
    d = x.shape[-1]
    mod = mod_ref[pl.ds(jnp.where(j + tok_off < nctx_blk, nb, b), 1), :]
    o = of_ref[0].astype(F32) + ob_ref[0].astype(F32)
    mixed = _head_rms(o, gn_ref[...], n_heads) * _silu(g_ref[0].astype(F32))
    if has_s5:
        act = _gelu_tanh(_from_chunk_rows(ycr_ref, ys_ref))
        glu = act * _sigmoid(_dot(act.astype(BF16), gluw_ref[...]) + glub_ref[...])
        mixed = jnp.concatenate([mixed, glu], axis=1)
    x1 = x + mod[:, 2 * d:3 * d] * _dot(mixed.astype(BF16), wo_ref[...])
    x1_ref[0] = x1
    h2 = _rms(x1) * nf_ref[...] * (1.0 + mod[:, 4 * d:5 * d]) + mod[:, 3 * d:4 * d]

    gidx, rank, gates = _route(h2, rwh_ref, rwl_ref, rb_ref, utri_ref, ones_ref, carry_ref)
    gi = lax.broadcasted_iota(I32, (SUBLANES, TM), 0)
    meta_ref[0] = jnp.where(gi == 0, gidx, jnp.where(gi == 1, rank, 0))
    cnt_ref[...] = carry_ref[...]

    li = lax.broadcasted_iota(I32, (LANES, TM), 0)
    gt = jnp.zeros((LANES, TM), F32)
    for p, g in enumerate(gates):
        gt = gt + jnp.where(li == p, g, 0.0)
    hb = h2.astype(BF16).astype(F32)
    half = d // 2
    pay_ref[:, 0:half] = _pack_bf16_pair(hb[:, 0:half], hb[:, half:d])
    pay_ref[:, half:half + LANES] = pltpu.bitcast(gt.T, I32)


def _mixout(has_s5, n_heads, nctx_blk, tok_off, x, of, ob, g, y5, mod, gn, gluw, glub, wo, nf, rwh, rwl, rb, utri, ones):
    bsz, _, d = x.shape
    w = of.shape[-1]
    nblk = of.shape[1] // TM - tok_off
    tok = lambda width: pl.BlockSpec((1, TM, width), lambda b, j: (b, j + tok_off, 0))
    ins, specs = [x, of, ob, g], [tok(d), tok(w), tok(w), tok(w)]
    scratch = [pltpu.VMEM((N_GROUPS, TM), F32)]
    if has_s5:
        ins.append(y5)
        specs.append(pl.BlockSpec((y5.shape[0], CROWS, y5.shape[2]), lambda b, j: (0, b * nblk + j, 0)))
        scratch.append(pltpu.VMEM((w // LANES, TM, LANES), F32))
    consts = [mod, gn] + ([gluw, glub] if has_s5 else []) + [wo, nf, rwh, rwl, rb, utri, ones]
    ins += consts
    specs += [_const_spec(c.shape) for c in consts]
    ntile = bsz * nblk
    pay_w = d // 2 + LANES
    lin = lambda b, j: b * nblk + j
    out_specs = [pl.BlockSpec((1, TM, d), lambda b, j: (b, j, 0)),
                 pl.BlockSpec((TM, pay_w), lambda b, j: (lin(b, j), 0)),
                 pl.BlockSpec((1, SUBLANES, TM), lambda b, j: (lin(b, j), 0, 0)),
                 _const_spec((N_GROUPS, TM))]
    sds = jax.ShapeDtypeStruct
    out_shape = [sds((bsz, nblk * TM, d), F32), sds((ntile * TM, pay_w), I32), sds((ntile, SUBLANES, TM), I32),
                 sds((N_GROUPS, TM), F32)]
    return pl.pallas_call(
        functools.partial(_mixout_kernel, has_s5, n_heads, nctx_blk, tok_off, bsz),
        grid=(bsz, nblk), in_specs=specs, out_specs=out_specs, out_shape=out_shape,
        scratch_shapes=scratch,
        compiler_params=_params(2), name="mixout_s5" if has_s5 else "mixout",
    )(*ins)


def _dispatch_kernel(seg_ref, meta_ref, pay_ref, init_hbm, hs_hbm, sem):
    del init_hbm

    def issue(r, carry):
        dst = seg_ref[meta_ref[0, 0, r]] + meta_ref[0, 1, r]
        pltpu.make_async_copy(pay_ref.at[pl.ds(r, 1)], hs_hbm.at[pl.ds(dst, 1)], sem).start()
        return carry

    lax.fori_loop(0, TM, issue, 0, unroll=DMA_UNROLL)
    pltpu.make_async_copy(pay_ref, hs_hbm.at[pl.ds(0, TM)], sem).wait()


def _dispatch(seg, meta, pay, n_sorted):
    ntile = meta.shape[0]
    init = jnp.zeros((n_sorted, pay.shape[1]), pay.dtype)
    grid_spec = pltpu.PrefetchScalarGridSpec(
        num_scalar_prefetch=1, grid=(ntile,),
        in_specs=[pl.BlockSpec((1, SUBLANES, TM), lambda i, seg: (i, 0, 0), memory_space=pltpu.SMEM),
                  pl.BlockSpec((TM, pay.shape[1]), lambda i, seg: (i, 0)), pl.BlockSpec(memory_space=pl.ANY)],
        out_specs=pl.BlockSpec(memory_space=pl.ANY),
        scratch_shapes=[pltpu.SemaphoreType.DMA(())])
    return pl.pallas_call(
        _dispatch_kernel, grid_spec=grid_spec,
        out_shape=jax.ShapeDtypeStruct(init.shape, init.dtype),
        input_output_aliases={3: 0},
        compiler_params=_params(1), name="moe_dispatch",
    )(seg, meta, pay, init)


def _moe_kernel(tg_ref, nv_ref, hs_ref, wg_ref, wu_ref, wd_ref, ys_ref):
    del tg_ref

    @pl.when(pl.program_id(0) < nv_ref[0])
    def _():
        words = hs_ref[...]
        half = wg_ref.shape[1] // 2
        ha, hb = _unpack_bf16_pair(words[:, 0:half])
        ha, hb = ha.astype(BF16), hb.astype(BF16)
        gate = pltpu.bitcast(words[:, half:half + LANES], F32)
        epg = EXPERTS_PER_GROUP
        fe = wg_ref.shape[2] // epg
        acc = jnp.zeros((TM, wd_ref.shape[2]), F32)
        for p in range(epg):
            cs = slice(p * fe, (p + 1) * fe)
            gp = _dot(ha, wg_ref[0, 0:half, cs]) + _dot(hb, wg_ref[0, half:2 * half, cs])
            up = _dot(ha, wu_ref[0, 0:half, cs]) + _dot(hb, wu_ref[0, half:2 * half, cs])
            act = _silu(gp) * up * gate[:, p:p + 1]
            acc = acc + _dot(act.astype(BF16), wd_ref[0, cs, :])
        yb = acc.astype(BF16).astype(F32)
        ys_ref[...] = _pack_bf16_pair(yb[:, 0:half], yb[:, half:2 * half])

    @pl.when(pl.program_id(0) >= nv_ref[0])
    def _():
        ys_ref[...] = jnp.zeros_like(ys_ref)


def _moe(tile_group, n_valid, hs, wg, wu, wd):
    n_sorted, pay_w = hs.shape
    ntile = n_sorted // TM
    d = wg.shape[1]
    clamp = lambda i, tg, nv: jnp.minimum(i, nv[0] - 1)
    wspec = lambda a: pl.BlockSpec((1,) + a.shape[1:], lambda i, tg, nv: (tg[i], 0, 0))
    grid_spec = pltpu.PrefetchScalarGridSpec(
        num_scalar_prefetch=2, grid=(ntile,),
        in_specs=[pl.BlockSpec((TM, pay_w), lambda i, tg, nv: (clamp(i, tg, nv), 0)), wspec(wg), wspec(wu), wspec(wd)],
        out_specs=pl.BlockSpec((TM, d // 2), lambda i, tg, nv: (i, 0)))
    return pl.pallas_call(
        _moe_kernel, grid_spec=grid_spec,
        out_shape=jax.ShapeDtypeStruct((n_sorted, d // 2), I32),
        compiler_params=_params(1), name="moe_experts",
    )(tile_group, n_valid, hs, wg, wu, wd)


def _meta_specs(bsz, nblk):
    last = bsz * nblk - 1
    cur = pl.BlockSpec((1, SUBLANES, TM), lambda b, j, seg: (b * nblk + j, 0, 0), memory_space=pltpu.SMEM)
    nxt = pl.BlockSpec((1, SUBLANES, TM), lambda b, j, seg: (jnp.minimum(b * nblk + j + 1, last), 0, 0),
                       memory_space=pltpu.SMEM)
    return [cur, nxt]


def _gather_ffn(step, n_steps, seg_ref, meta_ref, meta_next_ref, ys_hbm, buf, sem):
    def issue_from(m_ref, slot):
        def issue(r, carry):
            src = seg_ref[m_ref[0, 0, r]] + m_ref[0, 1, r]
            pltpu.make_async_copy(ys_hbm.at[pl.ds(src, 1)], buf.at[slot, pl.ds(r, 1)], sem.at[slot]).start()
            return carry

        lax.fori_loop(0, TM, issue, 0, unroll=DMA_UNROLL)

    slot = lax.rem(step, 2)

    @pl.when(step == 0)
    def _():
        issue_from(meta_ref, 0)

    @pl.when(step + 1 < n_steps)
    def _():
        issue_from(meta_next_ref, 1 - slot)

    pltpu.make_async_copy(ys_hbm.at[pl.ds(0, TM)], buf.at[slot], sem.at[slot]).wait()
    a, b = _unpack_bf16_pair(buf[slot])
    return jnp.concatenate([a, b], axis=1)


def _inproj1_kernel(nctx_blk, nb, seg_ref, meta_ref, meta_next_ref, ys_hbm, x_ref, mod0_ref, mod1_ref, nw_ref, w_ref,
                    lb_ref, d1f_ref, d1b_ref, mf_ref, mb_ref,
                    x2_ref, qf_ref, kf_ref, qb_ref, kb_ref, v_ref, vt_ref, g_ref, cvf_ref, cvb_ref, buf, sem):
    b, j = pl.program_id(0), pl.program_id(1)
    nblk = pl.num_programs(1)
    ffn = _gather_ffn(b * nblk + j, nb * nblk, seg_ref, meta_ref, meta_next_ref, ys_hbm, buf, sem)
    x = x_ref[0]
    d = x.shape[-1]
    row = jnp.where(j < nctx_blk, nb, b)
    mod0 = mod0_ref[pl.ds(row, 1), :]
    mod1 = mod1_ref[pl.ds(row, 1), :]
    x2 = x + mod0[:, 5 * d:6 * d] * ffn
    x2_ref[0] = x2
    hb = (_rms(x2) * nw_ref[...] * (1.0 + mod1[:, d:2 * d]) + mod1[:, 0:d]).astype(BF16)
    proj = lambda n: _dot(hb, w_ref[:, n * d:(n + 1) * d])
    q = _silu(proj(0))
    lb = lb_ref[...]
    v = proj(3)
    v_ref[0] = v.astype(BF16)
    vt_ref[0] = v.T.astype(BF16)
    g_ref[0] = proj(4).astype(BF16)
    f_f = lb + (1.0 - lb) * _sigmoid(proj(1))
    _gate_outputs(q, 1.0 - f_f, jnp.log(f_f), d1f_ref, mf_ref, qf_ref, kf_ref, cvf_ref)
    f_b = lb + (1.0 - lb) * _sigmoid(proj(2))
    _gate_outputs(q, 1.0 - f_b, jnp.log(f_b), d1b_ref, mb_ref, qb_ref, kb_ref, cvb_ref)


def _inproj1(seg, meta, ys, x1, mod0, mod1, nw, w_in, lb, consts, nctx_blk):
    bsz, nt, d = x1.shape
    nblk = nt // TM
    d1f, d1b, mf, mb = consts
    cs = lambda a: pl.BlockSpec(a.shape, lambda b, j, seg: (0,) * a.ndim)
    tok = lambda width: pl.BlockSpec((1, TM, width), lambda b, j, seg: (b, j, 0))
    cvspec = pl.BlockSpec((1, 1, 3 * SUBLANES, d), lambda b, j, seg: (b, j, 0, 0))
    grid_spec = pltpu.PrefetchScalarGridSpec(
        num_scalar_prefetch=1, grid=(bsz, nblk),
        in_specs=_meta_specs(bsz, nblk) + [
                  pl.BlockSpec(memory_space=pl.ANY), tok(d), cs(mod0), cs(mod1), cs(nw), cs(w_in), cs(lb),
                  cs(d1f), cs(d1b), cs(mf), cs(mb)],
        out_specs=[tok(d)] * 6 + [pl.BlockSpec((1, d, TM), lambda b, j, seg: (b, 0, j)), tok(d), cvspec, cvspec],
        scratch_shapes=[pltpu.VMEM((2, TM, d // 2), I32), pltpu.SemaphoreType.DMA((2,))])
    sds = jax.ShapeDtypeStruct
    out_shape = [sds((bsz, nt, d), F32)] + [sds((bsz, nt, d), BF16)] * 5 + [sds((bsz, d, nt), BF16)] + \
                [sds((bsz, nt, d), BF16)] + [sds((bsz, nblk, 3 * SUBLANES, d), F32)] * 2
    return pl.pallas_call(
        functools.partial(_inproj1_kernel, nctx_blk, bsz), grid_spec=grid_spec, out_shape=out_shape,
        compiler_params=_params(2), name="combine_inproj_hgrn",
    )(seg, meta, meta, ys, x1, mod0, mod1, nw, w_in, lb, d1f, d1b, mf, mb)


def _final_kernel(seg_ref, meta_ref, meta_next_ref, ys_hbm, x_ref, mod_ref, fw_ref, o_ref, buf, sem):
    nblk = pl.num_programs(1)
    step = pl.program_id(0) * nblk + pl.program_id(1)
    ffn = _gather_ffn(step, pl.num_programs(0) * nblk, seg_ref, meta_ref, meta_next_ref, ys_hbm, buf, sem)
    x = x_ref[0]
    d = x.shape[-1]
    mod = mod_ref[pl.ds(pl.program_id(0), 1), :]
    o_ref[0] = _rms(x + mod[:, 5 * d:6 * d] * ffn) * fw_ref[...]


def _final(seg, meta, ys, x3, mod, fw):
    bsz, seq, d = x3.shape
    nblk = seq // TM
    cs = lambda a: pl.BlockSpec(a.shape, lambda b, j, seg: (0,) * a.ndim)
    tok = pl.BlockSpec((1, TM, d), lambda b, j, seg: (b, j, 0))
    grid_spec = pltpu.PrefetchScalarGridSpec(
        num_scalar_prefetch=1, grid=(bsz, nblk),
        in_specs=_meta_specs(bsz, nblk) + [pl.BlockSpec(memory_space=pl.ANY), tok, cs(mod), cs(fw)],
        out_specs=tok,
        scratch_shapes=[pltpu.VMEM((2, TM, d // 2), I32), pltpu.SemaphoreType.DMA((2,))])
    return pl.pallas_call(
        _final_kernel, grid_spec=grid_spec, out_shape=jax.ShapeDtypeStruct((bsz, seq, d), F32),
        compiler_params=_params(2), name="combine_final_norm",
    )(seg, meta, meta, ys, x3, mod, fw)


def _grid_sincos(n_tokens, dim):
    rows = n_tokens // GRID_W
    r, col = jnp.meshgrid(jnp.arange(rows, dtype=F32), jnp.arange(GRID_W, dtype=F32), indexing='ij')
    quarter = dim // 4
    omega = 1.0 / (10000.0 ** (jnp.arange(quarter, dtype=F32) / quarter))

    def emb(p):
        ang = p.reshape(-1, 1) * omega
        return jnp.concatenate([jnp.sin(ang), jnp.cos(ang)], axis=-1)

    return jnp.concatenate([emb(r), emb(col)], axis=-1)


def _pad_heads(w, n_heads):
    dk = w.shape[-1] // n_heads
    w = w.reshape(w.shape[:-1] + (n_heads, dk))
    w = jnp.pad(w, [(0, 0)] * (w.ndim - 1) + [(0, HEAD_W - dk)])
    return w.reshape(w.shape[:-2] + (n_heads * HEAD_W,))


def _router_tables(router_w, router_bias):
    n_exp = router_w.shape[1]
    epg = n_exp // N_GROUPS
    assert epg == EXPERTS_PER_GROUP
    perm = np.array([epg * g + p for p in range(epg) for g in range(N_GROUPS)])
    wt = jnp.pad(router_w.T[perm], ((0, LANES - n_exp), (0, 0)))
    hi = wt.astype(BF16)
    lo = (wt - hi.astype(F32)).astype(BF16)
    bias = jnp.pad(router_bias[perm], (0, LANES - n_exp))
    return hi, lo, jnp.broadcast_to(bias[:, None], (LANES, TM)).astype(F32)


def _segments(counts, n_tokens):
    cnt = counts[:, 0].astype(I32)
    tiles = (cnt + TM - 1) // TM
    ends = jnp.cumsum(tiles)
    seg = (ends - tiles) * TM
    ntile = n_tokens // TM + N_GROUPS
    n_valid = ends[-1]
    tidx = jnp.minimum(jnp.arange(ntile, dtype=I32), n_valid - 1)
    tile_group = jnp.sum((tidx[:, None] >= ends[None, :]).astype(I32), axis=1)
    return seg.astype(I32), tile_group.astype(I32), n_valid.reshape(1).astype(I32), ntile * TM


def _moe_weights(w_gate, w_up, w_down):
    n_exp, d, fe = w_gate.shape
    epg = n_exp // N_GROUPS
    grp = lambda w: jnp.transpose(w.reshape(N_GROUPS, epg, d, fe), (0, 2, 1, 3)).reshape(N_GROUPS, d, epg * fe)
    return grp(w_gate).astype(BF16), grp(w_up).astype(BF16), w_down.reshape(N_GROUPS, epg * fe, d).astype(BF16)


def _moe_layer(meta, counts, pay, weights):
    seg, tile_group, n_valid, n_sorted = _segments(counts, pay.shape[0])
    hs = _dispatch(seg, meta, pay, n_sorted)
    ys = _moe(tile_group, n_valid, hs, *weights)
    return seg, ys


def kernel(x, c, ctx, c_ctx, ada_w, ada_b, norm_mix, norm_ffn, ab_w_in, ab_w_out, gla_a2, gla_ab, gla_norm, s5_lam_re, s5_lam_im, s5_log_dt, s5_b_re, s5_b_im, s5_c_re, s5_c_im, s5_d, s5_glu_w, s5_glu_b, hg_w_in, hg_w_out, hg_lb_logits, hg_norm, router_w, router_bias, moe_w_gate, moe_w_up, moe_w_down, final_norm):
    bsz, seq, d = x.shape
    nctx = ctx.shape[1]
    depth = ada_w.shape[0]
    assert depth == 2 and seq % TM == 0 and nctx % TM == 0 and bsz % SUBLANES == 0 and bsz < MOD_ROWS
    nctx_blk = nctx // TM
    nt = nctx + seq
    nc = nt // S5_L
    assert (nc * bsz) % LANES == 0

    cond = jnp.zeros((MOD_ROWS, d), F32).at[:bsz].set(c).at[bsz].set(c_ctx)
    mod = _modulation(cond, ada_w, ada_b)
    consts = _chunk_matrices()
    row = lambda v: v.reshape(1, -1)

    qk = gla_a2.shape[-1]
    gv = ab_w_out.shape[1] // 2
    rank = gla_a2.shape[2]
    w_in = ab_w_in[0]
    o_v, o_g, o_a, o_u = 2 * qk, 2 * qk + gv, 2 * qk + 2 * gv, 2 * qk + 2 * gv + 2 * rank
    wm = jnp.concatenate([_pad_heads(w_in[:, 0:qk], GLA_HEADS), _pad_heads(w_in[:, qk:o_v], GLA_HEADS),
                          w_in[:, o_v:o_g], w_in[:, o_g:o_a], w_in[:, o_u:]], axis=1).astype(BF16)
    wa = jnp.pad(w_in[:, o_a:o_u], ((0, 0), (0, LANES - 2 * rank))).astype(BF16)
    a2p = _pad_heads(gla_a2[0], GLA_HEADS)
    a2 = jnp.zeros((LANES, 2 * gv), F32).at[0:rank, 0:gv].set(a2p[0]).at[rank:2 * rank, gv:].set(a2p[1]).astype(BF16)
    ab = _pad_heads(gla_ab[0], GLA_HEADS).reshape(1, 2 * gv)
    pos = _grid_sincos(seq, d)
    (xs, qf, kf, qb, kb, v, vt, g, u_cr, cvf, cvb) = _inproj0(x, ctx, pos, mod[0], row(norm_mix[0]), wm, wa, a2, ab, consts,
                                                               float(qk // GLA_HEADS) ** -0.5)
    o_f, o_b = _recurrence(qf, kf, qb, kb, v, vt, cvf, cvb, GLA_HEADS, nctx_blk)

    assert 2 * s5_b_re.shape[-1] == PAIR_W
    ops = _s5_operators(s5_lam_re[0], s5_lam_im[0], s5_log_dt[0], s5_b_re[0], s5_b_im[0], s5_c_re[0], s5_c_im[0], s5_d[0])
    y5 = _s5(u_cr, ops, nc, nctx // S5_L, bsz)

    rwh, rwl, rb = _router_tables(router_w, router_bias)
    utri = jnp.asarray(np.triu(np.ones((TM, TM), np.float32), 1))
    ones = jnp.ones((TM, TM), F32)
    x1, pay, meta, counts = _mixout(True, GLA_HEADS, nctx_blk, 0, xs, o_f, o_b, g, y5, mod[0], row(gla_norm[0]),
                                    s5_glu_w[0].astype(BF16), row(s5_glu_b[0]), ab_w_out[0].astype(BF16),
                                    row(norm_ffn[0]), rwh, rwl, rb, utri, ones)
    seg, ys = _moe_layer(meta, counts, pay, _moe_weights(moe_w_gate[0], moe_w_up[0], moe_w_down[0]))

    lb_all = jax.nn.softmax(hg_lb_logits.astype(F32), axis=0)
    lb_all = jnp.cumsum(lb_all, axis=0) - lb_all[0]
    (x2, qf, kf, qb, kb, v, vt, g, cvf, cvb) = _inproj1(seg, meta, ys, x1, mod[0], mod[1], row(norm_mix[1]),
                                                          hg_w_in[0].astype(BF16), row(lb_all[1]), consts, nctx_blk)
    o_f, o_b = _recurrence(qf, kf, qb, kb, v, vt, cvf, cvb, HG_HEADS, nctx_blk)
    x3, pay, meta, counts = _mixout(False, HG_HEADS, nctx_blk, nctx_blk, x2, o_f, o_b, g, None, mod[1], row(hg_norm[0]),
                                    None, None, hg_w_out[0].astype(BF16), row(norm_ffn[1]), rwh, rwl, rb, utri, ones)
    seg, ys = _moe_layer(meta, counts, pay, _moe_weights(moe_w_gate[1], moe_w_up[1], moe_w_down[1]))
    return _final(seg, meta, ys, x3, mod[1], row(final_norm))
```

```python
import functools
import math

import numpy as np
import jax
import jax.numpy as jnp
from jax import lax
from jax.experimental import pallas as pl
from jax.experimental.pallas import tpu as pltpu

F32, BF16, I32 = jnp.float32, jnp.bfloat16, jnp.int32

EPS = 1e-6
CHUNK = 64
GRID_W = 64
GLA_HEADS = 4
GLA_GATE_NORM = 16.0
HG_HEADS = 8
N_GROUPS = 8
EXPERTS_PER_GROUP = 4
TOP_K = 2

LANES = 128
SUBLANES = 8
TM = 256
CPT = TM // CHUNK
HEAD_W = 128
S5_L = 16
MOD_ROWS = 16
DMA_UNROLL = 8
VMEM_LIMIT = 56 * 1024 * 1024


def _dot(a, b):
    return jnp.dot(a, b, preferred_element_type=F32)


def _dot_nt(a, b):
    return lax.dot_general(a, b, (((1,), (1,)), ((), ())), preferred_element_type=F32)


def _split(x):
    hi = x.astype(BF16)
    lo = (x - hi.astype(F32)).astype(BF16)
    return hi, lo


def _sigmoid(x):
    return 1.0 / (1.0 + jnp.exp(-x))


def _silu(x):
    return x * _sigmoid(x)


def _gelu_tanh(x):
    return 0.5 * x * (1.0 + jnp.tanh(math.sqrt(2.0 / math.pi) * (x + 0.044715 * (x * x * x))))


def _rms(x):
    return x * lax.rsqrt(jnp.mean(x * x, axis=-1, keepdims=True) + EPS)


def _head_rms(o, gain, n_heads):
    outs = []
    for h in range(n_heads):
        oh = o[:, h * HEAD_W:(h + 1) * HEAD_W]
        outs.append(_rms(oh) * gain)
    return jnp.concatenate(outs, axis=1)


def _params(n_grid_dims, vmem=VMEM_LIMIT):
    return pltpu.CompilerParams(dimension_semantics=("arbitrary",) * n_grid_dims, vmem_limit_bytes=vmem)


def _const_spec(shape):
    nd = len(shape)
    return pl.BlockSpec(shape, lambda *_: (0,) * nd)


def _mod_kernel(c_ref, w_ref, b_ref, o_ref):
    s = _silu(c_ref[...])
    shi, slo = _split(s)
    whi, wlo = _split(w_ref[0])
    o_ref[0] = _dot(shi, whi) + _dot(shi, wlo) + _dot(slo, whi) + b_ref[0]


def _modulation(cond, ada_w, ada_b):
    depth, d, six_d = ada_w.shape
    nchunk = six_d // d
    return pl.pallas_call(
        _mod_kernel,
        grid=(depth, nchunk),
        in_specs=[
            _const_spec((MOD_ROWS, d)),
            pl.BlockSpec((1, d, d), lambda l, n: (l, 0, n)),
            pl.BlockSpec((1, 1, d), lambda l, n: (l, 0, n)),
        ],
        out_specs=pl.BlockSpec((1, MOD_ROWS, d), lambda l, n: (l, 0, n)),
        out_shape=jax.ShapeDtypeStruct((depth, MOD_ROWS, six_d), F32),
        compiler_params=_params(2),
        name="modulation",
    )(cond, ada_w, ada_b.reshape(depth, 1, six_d))


def _chunk_matrices():
    t = np.arange(TM)
    c, l = t // CHUNK, t % CHUNK
    same = (c[:, None] == c[None, :]).astype(np.float32)
    li, ls = l[:, None], l[None, :]
    mid_f = CHUNK // 2 - 1
    mid_b = CHUNK // 2
    d1f = same * ((ls <= li).astype(np.float32) - (ls <= mid_f).astype(np.float32))
    d1b = same * ((ls >= li).astype(np.float32) - (ls >= mid_b).astype(np.float32))
    inchunk = (np.arange(SUBLANES)[:, None] == c[None, :]).astype(np.float32)
    mf = np.concatenate([inchunk * (l <= mid_f), inchunk * (l > mid_f), inchunk])
    mb = np.concatenate([inchunk * (l >= mid_b), inchunk * (l < mid_b), inchunk])
    as_bf16 = lambda a: jnp.asarray(a, dtype=BF16)
    return as_bf16(d1f), as_bf16(d1b), as_bf16(mf), as_bf16(mb)


def _gate_outputs(q, k, la, d1_ref, m_ref, q_out, k_out, cv_out):
    hi, lo = _split(la)
    d1m = d1_ref[...]
    d1 = _dot(d1m, hi) + _dot(d1m, lo)
    q_out[0] = (q * jnp.exp(d1)).astype(BF16)
    k_out[0] = (k * jnp.exp(-d1)).astype(BF16)
    mm = m_ref[...]
    cv_out[0, 0] = jnp.exp(_dot(mm, hi) + _dot(mm, lo))


PAIR_W = 32
PAIRS_PER_TILE = LANES // PAIR_W
CROWS = TM // S5_L


def _to_chunk_rows(u, us_ref, ucr_ref):
    for q in range(us_ref.shape[0]):
        us_ref[q] = u[:, q * LANES:(q + 1) * LANES]
    lane = lax.broadcasted_iota(I32, (CROWS, LANES), 1)
    for p in range(ucr_ref.shape[0]):
        q, pp = divmod(p, PAIRS_PER_TILE)
        tiles = []
        for t in range(S5_L // PAIRS_PER_TILE):
            acc = None
            for jj in range(PAIRS_PER_TILE):
                piece = us_ref[q, pl.ds(t * PAIRS_PER_TILE + jj, CROWS, stride=S5_L), :]
                shift = (PAIR_W * (jj - pp)) % LANES
                if shift:
                    piece = pltpu.roll(piece, shift, axis=1)
                here = (lane >= PAIR_W * jj) & (lane < PAIR_W * (jj + 1))
                acc = piece if acc is None else jnp.where(here, piece, acc)
            tiles.append(acc)
        ucr_ref[p] = jnp.concatenate(tiles, axis=1).astype(BF16)


def _from_chunk_rows(ycr_ref, ys_ref):
    lane = lax.broadcasted_iota(I32, (CROWS, LANES), 1)
    for q in range(ycr_ref.shape[0] // PAIRS_PER_TILE):
        for i in range(S5_L):
            t, ii = divmod(i, PAIRS_PER_TILE)
            acc = None
            for pp in range(PAIRS_PER_TILE):
                piece = ycr_ref[q * PAIRS_PER_TILE + pp, :, t * LANES:(t + 1) * LANES].astype(F32)
                shift = (PAIR_W * (pp - ii)) % LANES
                if shift:
                    piece = pltpu.roll(piece, shift, axis=1)
                here = (lane >= PAIR_W * pp) & (lane < PAIR_W * (pp + 1))
                acc = piece if acc is None else jnp.where(here, piece, acc)
            ys_ref[q, pl.ds(i, CROWS, stride=S5_L), :] = acc
    return jnp.concatenate([ys_ref[q] for q in range(ys_ref.shape[0])], axis=1)


def _inproj0_kernel(nctx_blk, nb, gla_w, q_scale, x_ref, ctx_ref, pos_ref, mod_ref, nw_ref, wm_ref, wa_ref, a2_ref,
                    ab_ref, d1f_ref, d1b_ref, mf_ref, mb_ref,
                    xs_ref, qf_ref, kf_ref, qb_ref, kb_ref, v_ref, vt_ref, g_ref, ucr_ref, cvf_ref, cvb_ref, us_ref):
    b, j = pl.program_id(0), pl.program_id(1)
    is_ctx = j < nctx_blk
    xin = jnp.where(is_ctx, ctx_ref[0], x_ref[0] + pos_ref[...])
    xs_ref[0] = xin
    d = xin.shape[-1]
    mod = mod_ref[pl.ds(jnp.where(is_ctx, nb, b), 1), :]
    h = _rms(xin) * nw_ref[...] * (1.0 + mod[:, d:2 * d]) + mod[:, 0:d]
    hb = h.astype(BF16)
    w = gla_w
    y = _dot(hb, wm_ref[...])
    q = y[:, 0:w] * q_scale
    k = y[:, w:2 * w]
    v = y[:, 2 * w:3 * w]
    g_ref[0] = y[:, 3 * w:4 * w].astype(BF16)
    _to_chunk_rows(y[:, 4 * w:5 * w], us_ref, ucr_ref)
    v_ref[0] = v.astype(BF16)
    vt_ref[0] = v.T.astype(BF16)
    a = _dot(hb, wa_ref[...])
    z = _dot(a.astype(BF16), a2_ref[...]) + ab_ref[...]
    la = (jnp.minimum(z, 0.0) - jnp.log(1.0 + jnp.exp(-jnp.abs(z)))) * (1.0 / GLA_GATE_NORM)
    _gate_outputs(q, k, la[:, 0:w], d1f_ref, mf_ref, qf_ref, kf_ref, cvf_ref)
    _gate_outputs(q, k, la[:, w:2 * w], d1b_ref, mb_ref, qb_ref, kb_ref, cvb_ref)


def _inproj0(x, ctx, pos, mod, nw, wm, wa, a2, ab, consts, q_scale):
    bsz, seq, d = x.shape
    nctx = ctx.shape[1]
    nctx_blk, nlat_blk = nctx // TM, seq // TM
    nblk = nctx_blk + nlat_blk
    nt = nblk * TM
    w = wm.shape[1] // 5
    d1f, d1b, mf, mb = consts
    lat = lambda b, j: (b, jnp.maximum(j - nctx_blk, 0), 0)
    tok = lambda b, j: (b, j, 0)
    in_specs = [
        pl.BlockSpec((1, TM, d), lat),
        pl.BlockSpec((1, TM, d), lambda b, j: (b, jnp.minimum(j, nctx_blk - 1), 0)),
        pl.BlockSpec((TM, d), lambda b, j: (jnp.maximum(j - nctx_blk, 0), 0)),
        _const_spec(mod.shape), _const_spec(nw.shape), _const_spec(wm.shape), _const_spec(wa.shape),
        _const_spec(a2.shape), _const_spec(ab.shape), _const_spec(d1f.shape), _const_spec(d1b.shape),
        _const_spec(mf.shape), _const_spec(mb.shape),
    ]
    tokspec = lambda width: pl.BlockSpec((1, TM, width), tok)
    cvspec = pl.BlockSpec((1, 1, 3 * SUBLANES, w), lambda b, j: (b, j, 0, 0))
    npair, crow_w = w // PAIR_W, S5_L * PAIR_W
    out_specs = [tokspec(d), tokspec(w), tokspec(w), tokspec(w), tokspec(w), tokspec(w),
                 pl.BlockSpec((1, w, TM), lambda b, j: (b, 0, j)), tokspec(w),
                 pl.BlockSpec((npair, CROWS, crow_w), lambda b, j: (0, b * nblk + j, 0)), cvspec, cvspec]
    sds = jax.ShapeDtypeStruct
    out_shape = [sds((bsz, nt, d), F32)] + [sds((bsz, nt, w), BF16)] * 5 + [sds((bsz, w, nt), BF16)] + \
                [sds((bsz, nt, w), BF16), sds((npair, bsz * nblk * CROWS, crow_w), BF16)] + \
                [sds((bsz, nblk, 3 * SUBLANES, w), F32)] * 2
    return pl.pallas_call(
        functools.partial(_inproj0_kernel, nctx_blk, bsz, w, q_scale),
        grid=(bsz, nblk), in_specs=in_specs, out_specs=out_specs, out_shape=out_shape,
        scratch_shapes=[pltpu.VMEM((w // LANES, TM, LANES), F32)],
        compiler_params=_params(2), name="inproj_gla_s5",
    )(x, ctx, pos, mod, nw, wm, wa, a2, ab, d1f, d1b, mf, mb)


def _recur_kernel(n_heads, qf_ref, kf_ref, vf_ref, vtf_ref, cvf_ref, qb_ref, kb_ref, vb_ref, vtb_ref, cvb_ref,
                  of_ref, ob_ref, s_ref):
    @pl.when(pl.program_id(1) == 0)
    def _():
        s_ref[...] = jnp.zeros_like(s_ref)

    ri = lax.broadcasted_iota(I32, (CHUNK, CHUNK), 0)
    ci = lax.broadcasted_iota(I32, (CHUNK, CHUNK), 1)
    causal = (ci <= ri, ci >= ri)
    prow = lax.broadcasted_iota(I32, (2 * CHUNK, HEAD_W), 0)
    dirs = ((qf_ref, kf_ref, vf_ref, vtf_ref, cvf_ref, of_ref), (qb_ref, kb_ref, vb_ref, vtb_ref, cvb_ref, ob_ref))
    for d, (q_ref, k_ref, v_ref, vt_ref, cv_ref, o_ref) in enumerate(dirs):
        for cc in range(CPT):
            c = cc if d == 0 else CPT - 1 - cc
            r0 = c * CHUNK
            p0 = (c // 2) * 2 * CHUNK
            in_chunk = (prow >= r0 - p0) & (prow < r0 - p0 + CHUNK)
            e_mid = cv_ref[0, 0, c:c + 1, :]
            e_rest = cv_ref[0, 0, SUBLANES + c:SUBLANES + c + 1, :]
            e_all = cv_ref[0, 0, 2 * SUBLANES + c:2 * SUBLANES + c + 1, :]
            for h in range(n_heads):
                hs = slice(h * HEAD_W, (h + 1) * HEAD_W)
                qi = q_ref[0, r0:r0 + CHUNK, hs]
                ki = k_ref[0, r0:r0 + CHUNK, hs]
                vv = v_ref[0, r0:r0 + CHUNK, hs]
                kp = k_ref[0, p0:p0 + 2 * CHUNK, hs]
                kp = jnp.where(in_chunk, kp, jnp.zeros_like(kp))
                vtp = vt_ref[0, hs, p0:p0 + 2 * CHUNK]
                st = s_ref[d, h]
                attn = jnp.where(causal[d], _dot_nt(qi, ki), 0.0).astype(BF16)
                o = _dot(attn, vv) + _dot_nt(qi, (st * e_mid[:, hs]).astype(BF16))
                o_ref[0, r0:r0 + CHUNK, hs] = o.astype(o_ref.dtype)
                s_ref[d, h] = st * e_all[:, hs] + _dot(vtp, kp) * e_rest[:, hs]


def _recurrence(qf, kf, qb, kb, v, vt, cvf, cvb, n_heads, nctx_blk):
    bsz, nt, w = qf.shape
    nblk = nt // TM
    fwd = lambda j: j
    bwd = lambda j: jnp.where(j < nctx_blk, nctx_blk - 1 - j, nblk - 1 - (j - nctx_blk))
    specs = []
    for order in (fwd, bwd):
        tok = pl.BlockSpec((1, TM, w), lambda b, j, o=order: (b, o(j), 0))
        specs += [tok, tok, tok,
                  pl.BlockSpec((1, w, TM), lambda b, j, o=order: (b, 0, o(j))),
                  pl.BlockSpec((1, 1, 3 * SUBLANES, w), lambda b, j, o=order: (b, o(j), 0, 0))]
    out_specs = [pl.BlockSpec((1, TM, w), lambda b, j: (b, j, 0)),
                 pl.BlockSpec((1, TM, w), lambda b, j: (b, bwd(j), 0))]
    return pl.pallas_call(
        functools.partial(_recur_kernel, n_heads),
        grid=(bsz, nblk), in_specs=specs, out_specs=out_specs,
        out_shape=[jax.ShapeDtypeStruct((bsz, nt, w), BF16)] * 2,
        scratch_shapes=[pltpu.VMEM((2, n_heads, HEAD_W, HEAD_W), F32)],
        compiler_params=_params(2), name=f"recurrence_h{n_heads}",
    )(qf, kf, v, vt, cvf, qb, kb, v, vt, cvb)


def _s5_operators(lam_re, lam_im, log_dt, b_re, b_im, c_re, c_im, d_skip):
    L = S5_L
    ndir, ng, p = lam_re.shape
    hs = b_re.shape[-1]
    dt = jnp.exp(log_dt)[..., None]
    lr, li = lam_re * dt, lam_im * dt
    mag = jnp.exp(lr)
    a_re, a_im = mag * jnp.cos(li), mag * jnp.sin(li)
    den = lam_re * lam_re + lam_im * lam_im
    f_re = ((a_re - 1.0) * lam_re + a_im * lam_im) / den
    f_im = (a_im * lam_re - (a_re - 1.0) * lam_im) / den
    bb_re = f_re[..., None] * b_re - f_im[..., None] * b_im
    bb_im = f_re[..., None] * b_im + f_im[..., None] * b_re
    m = jnp.arange(L + 1, dtype=F32)[:, None, None, None]
    pw_mag = jnp.exp(m * lr[None])
    pw_re, pw_im = pw_mag * jnp.cos(m * li[None]), pw_mag * jnp.sin(m * li[None])
    eye = jnp.eye(2, dtype=F32)
    npair, pw, cw = ng // 2, 2 * hs, 2 * L * hs

    toep = jnp.zeros((npair, cw, cw), F32)
    wz_p, vo_p = [], []
    for di in range(ndir):
        pr, pi = pw_re[:L, di, :, :, None], pw_im[:L, di, :, :, None]
        ab_re = pr * bb_re[di][None] - pi * bb_im[di][None]
        ab_im = pr * bb_im[di][None] + pi * bb_re[di][None]
        kf = jnp.einsum('mgnk,ghn->gkmh', ab_re, c_re[di]) - jnp.einsum('mgnk,ghn->gkmh', ab_im, c_im[di])
        if di == 1:
            kf = kf[:, :, ::-1, :]
        kp = jnp.einsum('pskmh,st->pskmth', kf.reshape(npair, 2, hs, L, hs), eye).reshape(npair, pw, cw)
        if di == 0:
            kpad = jnp.pad(kp, ((0, 0), (0, 0), (cw, 0)))
            blocks = [kpad[:, :, cw - pw * j:2 * cw - pw * j] for j in range(L)]
        else:
            kpad = jnp.pad(kp, ((0, 0), (0, 0), (0, cw)))
            blocks = [kpad[:, :, pw * (L - 1 - j):pw * (L - 1 - j) + cw] for j in range(L)]
        toep = toep + jnp.stack(blocks, axis=1).reshape(npair, cw, cw)
        sel_i = (lambda a: a[::-1]) if di == 0 else (lambda a: a)
        pack_w = lambda a: jnp.einsum('jpsnk,st->pjsktn', sel_i(a).reshape(L, npair, 2, p, hs), eye).reshape(npair, cw, 2 * p)
        wz_p.append(jnp.concatenate([pack_w(ab_re), pack_w(ab_im)], axis=2))
        qr, qi = pw_re[1:, di, :, None, :], pw_im[1:, di, :, None, :]
        ca_re = c_re[di][None] * qr - c_im[di][None] * qi
        ca_im = c_re[di][None] * qi + c_im[di][None] * qr
        sel_o = (lambda a: a) if di == 0 else (lambda a: a[::-1])
        pack_v = lambda a: jnp.einsum('ipshn,st->psnith', sel_o(a).reshape(L, npair, 2, hs, p), eye).reshape(npair, 2 * p, cw)
        vo_p.append(jnp.concatenate([pack_v(ca_re), -pack_v(ca_im)], axis=1))
    wz_p, vo_p = jnp.stack(wz_p, axis=1), jnp.stack(vo_p, axis=1)
    al = jnp.stack([pw_re[L], pw_im[L]], axis=1)
    al_p = jnp.transpose(al.reshape(ndir, 2, npair, 2 * p), (2, 0, 1, 3))
    d_p = jnp.tile(d_skip.reshape(npair, 1, pw), (1, L, 1)).reshape(npair, 1, cw)
    return wz_p.astype(BF16), vo_p.astype(BF16), toep.astype(BF16), al_p, d_p


def _s5_kernel(nc, ncc, bsz, u_ref, wz_ref, vo_ref, tp_ref, al_ref, d_ref, y_ref, zf, zb, xf, xb):
    rows_total = nc * bsz
    rb = LANES
    half = zf.shape[2]

    def z_body(i, carry):
        rows = pl.ds(pl.multiple_of(i * rb, rb), rb)
        ub = u_ref[0, rows, :]
        for z, di in ((zf, 0), (zb, 1)):
            zz = _dot(ub, wz_ref[0, di])
            z[0, rows, :] = zz[:, 0:half]
            z[1, rows, :] = zz[:, half:2 * half]
        return carry

    lax.fori_loop(0, rows_total // rb, z_body, 0)

    def scan(z, xs, di, order):
        a_re = jnp.broadcast_to(al_ref[0, di, 0:1, :], (bsz, half))
        a_im = jnp.broadcast_to(al_ref[0, di, 1:2, :], (bsz, half))

        def body(i, carry):
            x_re, x_im = carry
            rows = pl.ds(order(i), bsz, stride=nc)
            xs[0, rows, :] = x_re
            xs[1, rows, :] = x_im
            return (a_re * x_re - a_im * x_im + z[0, rows, :], a_re * x_im + a_im * x_re + z[1, rows, :])

        zero = jnp.zeros((bsz, half), F32)
        lax.fori_loop(0, nc, body, (zero, zero))

    scan(zf, xf, 0, lambda i: i)
    scan(zb, xb, 1, lambda i: jnp.where(i < ncc, ncc - 1 - i, nc - 1 - (i - ncc)))

    def y_body(i, carry):
        rows = pl.ds(pl.multiple_of(i * rb, rb), rb)
        ub = u_ref[0, rows, :]
        state = lambda xs: jnp.concatenate([xs[0, rows, :], xs[1, rows, :]], axis=1).astype(BF16)
        y = _dot(state(xf), vo_ref[0, 0]) + _dot(state(xb), vo_ref[0, 1])
        y_ref[0, rows, :] = (y + _dot(ub, tp_ref[0]) + ub.astype(F32) * d_ref[0]).astype(y_ref.dtype)
        return carry

    lax.fori_loop(0, rows_total // rb, y_body, 0)


def _s5(u_cr, ops, nc, ncc, bsz):
    wz_p, vo_p, toep_p, al_p, d_p = ops
    npair, rows, width = u_cr.shape
    per_pair = lambda a: pl.BlockSpec((1,) + a.shape[1:], lambda p: (p,) + (0,) * (a.ndim - 1))
    return pl.pallas_call(
        functools.partial(_s5_kernel, nc, ncc, bsz),
        grid=(npair,),
        in_specs=[per_pair(u_cr), per_pair(wz_p), per_pair(vo_p), per_pair(toep_p), per_pair(al_p), per_pair(d_p)],
        out_specs=per_pair(u_cr),
        out_shape=jax.ShapeDtypeStruct(u_cr.shape, BF16),
        scratch_shapes=[pltpu.VMEM((2, rows, wz_p.shape[-1] // 2), F32)] * 4,
        compiler_params=_params(1), name="s5",
    )(u_cr, wz_p, vo_p, toep_p, al_p, d_p)


def _route(h2, rwh_ref, rwl_ref, rb_ref, utri_ref, ones_ref, carry_ref):
    hi, lo = _split(h2)
    logits = _dot_nt(rwh_ref[...], hi) + _dot_nt(rwl_ref[...], hi) + _dot_nt(rwh_ref[...], lo)
    aff = _sigmoid(logits)
    sel = aff + rb_ref[...]
    epg = EXPERTS_PER_GROUP
    s = [sel[N_GROUPS * p:N_GROUPS * (p + 1), :] for p in range(epg)]
    a = [aff[N_GROUPS * p:N_GROUPS * (p + 1), :] for p in range(epg)]
    m1, n1 = jnp.maximum(s[0], s[1]), jnp.minimum(s[0], s[1])
    m2, n2 = jnp.maximum(s[2], s[3]), jnp.minimum(s[2], s[3])
    score = jnp.maximum(m1, m2) + jnp.maximum(jnp.minimum(m1, m2), jnp.maximum(n1, n2))
    gi = lax.broadcasted_iota(I32, score.shape, 0)
    best = jnp.max(score, axis=0, keepdims=True)
    gidx = jnp.min(jnp.where(score == best, gi, N_GROUPS), axis=0, keepdims=True)
    onehot = gi == gidx
    gates = []
    for p in range(epg):
        ahead = jnp.zeros_like(score)
        for q in range(epg):
            if q != p:
                beats = (s[q] >= s[p]) if q < p else (s[q] > s[p])
                ahead = ahead + jnp.where(beats, 1.0, 0.0)
        picked = jnp.where(onehot, jnp.where(ahead < float(TOP_K) - 0.5, a[p], 0.0), 0.0)
        gates.append(jnp.sum(picked, axis=0, keepdims=True))
    den = gates[0] + gates[1] + gates[2] + gates[3]
    gates = [g / den for g in gates]
    oh = jnp.where(onehot, 1.0, 0.0)
    before = _dot(oh, utri_ref[...])
    carry = carry_ref[...]
    rank = jnp.sum(jnp.where(onehot, before + carry, 0.0), axis=0, keepdims=True)
    carry_ref[...] = carry + _dot(oh, ones_ref[...])
    return gidx, rank.astype(I32), gates


def _mixout_kernel(has_s5, n_heads, nctx_blk, tok_off, nb, *refs):
    if has_s5:
        (x_ref, of_ref, ob_ref, g_ref, ycr_ref, mod_ref, gn_ref, gluw_ref, glub_ref, wo_ref, nf_ref,
         rwh_ref, rwl_ref, rb_ref, utri_ref, ones_ref, x1_ref, pay_ref, meta_ref, cnt_ref, carry_ref, ys_ref) = refs
    else:
        (x_ref, of_ref, ob_ref, g_ref, mod_ref, gn_ref, wo_ref, nf_ref,
         rwh_ref, rwl_ref, rb_ref, utri_ref, ones_ref, x1_ref, pay_ref, meta_ref, cnt_ref, carry_ref) = refs
    b, j = pl.program_id(0), pl.program_id(1)

    @pl.when((b == 0) & (j == 0))
    def _():
        carry_ref[...] = jnp.zeros_like(carry_ref)

    x = x_ref[0]
    d = x.shape[-1]
    mod = mod_ref[pl.ds(jnp.where(j + tok_off < nctx_blk, nb, b), 1), :]
    o = of_ref[0].astype(F32) + ob_ref[0].astype(F32)
    mixed = _head_rms(o, gn_ref[...], n_heads) * _silu(g_ref[0].astype(F32))
    if has_s5:
        act = _gelu_tanh(_from_chunk_rows(ycr_ref, ys_ref))
        glu = act * _sigmoid(_dot(act.astype(BF16), gluw_ref[...]) + glub_ref[...])
        mixed = jnp.concatenate([mixed, glu], axis=1)
    x1 = x + mod[:, 2 * d:3 * d] * _dot(mixed.astype(BF16), wo_ref[...])
    x1_ref[0] = x1
    h2 = _rms(x1) * nf_ref[...] * (1.0 + mod[:, 4 * d:5 * d]) + mod[:, 3 * d:4 * d]

    gidx, rank, gates = _route(h2, rwh_ref, rwl_ref, rb_ref, utri_ref, ones_ref, carry_ref)
    gi = lax.broadcasted_iota(I32, (SUBLANES, TM), 0)
    meta_ref[0] = jnp.where(gi == 0, gidx, jnp.where(gi == 1, rank, 0))
    cnt_ref[...] = carry_ref[...]

    li = lax.broadcasted_iota(I32, (LANES, TM), 0)
    gt = jnp.zeros((LANES, TM), F32)
    for p, g in enumerate(gates):
        gt = gt + jnp.where(li == p, g, 0.0)
    pay_ref[:, 0:d] = h2
    pay_ref[:, d:d + LANES] = gt.T


def _mixout(has_s5, n_heads, nctx_blk, tok_off, x, of, ob, g, y5, mod, gn, gluw, glub, wo, nf, rwh, rwl, rb, utri, ones):
    bsz, _, d = x.shape
    w = of.shape[-1]
    nblk = of.shape[1] // TM - tok_off
    tok = lambda width: pl.BlockSpec((1, TM, width), lambda b, j: (b, j + tok_off, 0))
    ins, specs = [x, of, ob, g], [tok(d), tok(w), tok(w), tok(w)]
    scratch = [pltpu.VMEM((N_GROUPS, TM), F32)]
    if has_s5:
        ins.append(y5)
        specs.append(pl.BlockSpec((y5.shape[0], CROWS, y5.shape[2]), lambda b, j: (0, b * nblk + j, 0)))
        scratch.append(pltpu.VMEM((w // LANES, TM, LANES), F32))
    consts = [mod, gn] + ([gluw, glub] if has_s5 else []) + [wo, nf, rwh, rwl, rb, utri, ones]
    ins += consts
    specs += [_const_spec(c.shape) for c in consts]
    ntile = bsz * nblk
    pay_w = d + LANES
    lin = lambda b, j: b * nblk + j
    out_specs = [pl.BlockSpec((1, TM, d), lambda b, j: (b, j, 0)),
                 pl.BlockSpec((TM, pay_w), lambda b, j: (lin(b, j), 0)),
                 pl.BlockSpec((1, SUBLANES, TM), lambda b, j: (lin(b, j), 0, 0)),
                 _const_spec((N_GROUPS, TM))]
    sds = jax.ShapeDtypeStruct
    out_shape = [sds((bsz, nblk * TM, d), F32), sds((ntile * TM, pay_w), F32), sds((ntile, SUBLANES, TM), I32),
                 sds((N_GROUPS, TM), F32)]
    return pl.pallas_call(
        functools.partial(_mixout_kernel, has_s5, n_heads, nctx_blk, tok_off, bsz),
        grid=(bsz, nblk), in_specs=specs, out_specs=out_specs, out_shape=out_shape,
        scratch_shapes=scratch,
        compiler_params=_params(2), name="mixout_s5" if has_s5 else "mixout",
    )(*ins)


def _dispatch_kernel(seg_ref, meta_ref, pay_ref, init_hbm, hs_hbm, sem):
    del init_hbm

    def issue(r, carry):
        dst = seg_ref[meta_ref[0, 0, r]] + meta_ref[0, 1, r]
        pltpu.make_async_copy(pay_ref.at[pl.ds(r, 1)], hs_hbm.at[pl.ds(dst, 1)], sem).start()
        return carry

    lax.fori_loop(0, TM, issue, 0, unroll=DMA_UNROLL)
    pltpu.make_async_copy(pay_ref, hs_hbm.at[pl.ds(0, TM)], sem).wait()


def _dispatch(seg, meta, pay, n_sorted):
    ntile = meta.shape[0]
    init = jnp.zeros((n_sorted, pay.shape[1]), pay.dtype)
    grid_spec = pltpu.PrefetchScalarGridSpec(
        num_scalar_prefetch=1, grid=(ntile,),
        in_specs=[pl.BlockSpec((1, SUBLANES, TM), lambda i, seg: (i, 0, 0), memory_space=pltpu.SMEM),
                  pl.BlockSpec((TM, pay.shape[1]), lambda i, seg: (i, 0)), pl.BlockSpec(memory_space=pl.ANY)],
        out_specs=pl.BlockSpec(memory_space=pl.ANY),
        scratch_shapes=[pltpu.SemaphoreType.DMA(())])
    return pl.pallas_call(
        _dispatch_kernel, grid_spec=grid_spec,
        out_shape=jax.ShapeDtypeStruct(init.shape, init.dtype),
        input_output_aliases={3: 0},
        compiler_params=_params(1), name="moe_dispatch",
    )(seg, meta, pay, init)


def _moe_kernel(tg_ref, nv_ref, hs_ref, wg_ref, wu_ref, wd_ref, ys_ref, wgb, wub, wdb):
    i = pl.program_id(0)
    valid = i < nv_ref[0]

    @pl.when(valid & ((i == 0) | (tg_ref[i] != tg_ref[jnp.maximum(i - 1, 0)])))
    def _():
        wgb[...] = wg_ref[...].astype(BF16)
        wub[...] = wu_ref[...].astype(BF16)
        wdb[...] = wd_ref[...].astype(BF16)

    @pl.when(valid)
    def _():
        d = wg_ref.shape[1]
        h = hs_ref[:, 0:d].astype(BF16)
        gate = hs_ref[:, d:d + LANES]
        acc = jnp.zeros((TM, d), F32)
        for p in range(EXPERTS_PER_GROUP):
            act = _silu(_dot(h, wgb[p])) * _dot(h, wub[p]) * gate[:, p:p + 1]
            acc = acc + _dot(act.astype(BF16), wdb[p])
        ys_ref[...] = acc

    @pl.when(jnp.logical_not(valid))
    def _():
        ys_ref[...] = jnp.zeros_like(ys_ref)


def _moe(tile_group, n_valid, hs, layer, wg, wu, wd):
    n_sorted, pay_w = hs.shape
    ntile = n_sorted // TM
    d = wg.shape[2]
    epg = EXPERTS_PER_GROUP
    clamp = lambda i, tg, nv: jnp.minimum(i, nv[0] - 1)
    wspec = lambda a: pl.BlockSpec((None, epg) + a.shape[2:], lambda i, tg, nv: (layer, tg[i], 0, 0))
    grid_spec = pltpu.PrefetchScalarGridSpec(
        num_scalar_prefetch=2, grid=(ntile,),
        in_specs=[pl.BlockSpec((TM, pay_w), lambda i, tg, nv: (clamp(i, tg, nv), 0)), wspec(wg), wspec(wu), wspec(wd)],
        out_specs=pl.BlockSpec((TM, d), lambda i, tg, nv: (i, 0)),
        scratch_shapes=[pltpu.VMEM((epg,) + wg.shape[2:], BF16), pltpu.VMEM((epg,) + wu.shape[2:], BF16),
                        pltpu.VMEM((epg,) + wd.shape[2:], BF16)])
    return pl.pallas_call(
        _moe_kernel, grid_spec=grid_spec,
        out_shape=jax.ShapeDtypeStruct((n_sorted, d), F32),
        compiler_params=_params(1), name="moe_experts",
    )(tile_group, n_valid, hs, wg, wu, wd)


def _meta_specs(bsz, nblk):
    last = bsz * nblk - 1
    cur = pl.BlockSpec((1, SUBLANES, TM), lambda b, j, seg: (b * nblk + j, 0, 0), memory_space=pltpu.SMEM)
    nxt = pl.BlockSpec((1, SUBLANES, TM), lambda b, j, seg: (jnp.minimum(b * nblk + j + 1, last), 0, 0),
                       memory_space=pltpu.SMEM)
    return [cur, nxt]


def _gather_ffn(step, n_steps, seg_ref, meta_ref, meta_next_ref, ys_hbm, buf, sem):
    def issue_from(m_ref, slot):
        def issue(r, carry):
            src = seg_ref[m_ref[0, 0, r]] + m_ref[0, 1, r]
            pltpu.make_async_copy(ys_hbm.at[pl.ds(src, 1)], buf.at[slot, pl.ds(r, 1)], sem.at[slot]).start()
            return carry

        lax.fori_loop(0, TM, issue, 0, unroll=DMA_UNROLL)

    slot = lax.rem(step, 2)

    @pl.when(step == 0)
    def _():
        issue_from(meta_ref, 0)

    @pl.when(step + 1 < n_steps)
    def _():
        issue_from(meta_next_ref, 1 - slot)

    pltpu.make_async_copy(ys_hbm.at[pl.ds(0, TM)], buf.at[slot], sem.at[slot]).wait()
    return buf[slot]


def _inproj1_kernel(nctx_blk, nb, seg_ref, meta_ref, meta_next_ref, ys_hbm, x_ref, mod0_ref, mod1_ref, nw_ref, w_ref,
                    lb_ref, d1f_ref, d1b_ref, mf_ref, mb_ref,
                    x2_ref, qf_ref, kf_ref, qb_ref, kb_ref, v_ref, vt_ref, g_ref, cvf_ref, cvb_ref, buf, sem):
    b, j = pl.program_id(0), pl.program_id(1)
    nblk = pl.num_programs(1)
    ffn = _gather_ffn(b * nblk + j, nb * nblk, seg_ref, meta_ref, meta_next_ref, ys_hbm, buf, sem)
    x = x_ref[0]
    d = x.shape[-1]
    row = jnp.where(j < nctx_blk, nb, b)
    mod0 = mod0_ref[pl.ds(row, 1), :]
    mod1 = mod1_ref[pl.ds(row, 1), :]
    x2 = x + mod0[:, 5 * d:6 * d] * ffn
    x2_ref[0] = x2
    hb = (_rms(x2) * nw_ref[...] * (1.0 + mod1[:, d:2 * d]) + mod1[:, 0:d]).astype(BF16)
    proj = lambda n: _dot(hb, w_ref[:, n * d:(n + 1) * d])
    q = _silu(proj(0))
    lb = lb_ref[...]
    v = proj(3)
    v_ref[0] = v.astype(BF16)
    vt_ref[0] = v.T.astype(BF16)
    g_ref[0] = proj(4).astype(BF16)
    f_f = lb + (1.0 - lb) * _sigmoid(proj(1))
    _gate_outputs(q, 1.0 - f_f, jnp.log(f_f), d1f_ref, mf_ref, qf_ref, kf_ref, cvf_ref)
    f_b = lb + (1.0 - lb) * _sigmoid(proj(2))
    _gate_outputs(q, 1.0 - f_b, jnp.log(f_b), d1b_ref, mb_ref, qb_ref, kb_ref, cvb_ref)


def _inproj1(seg, meta, ys, x1, mod0, mod1, nw, w_in, lb, consts, nctx_blk):
    bsz, nt, d = x1.shape
    nblk = nt // TM
    d1f, d1b, mf, mb = consts
    cs = lambda a: pl.BlockSpec(a.shape, lambda b, j, seg: (0,) * a.ndim)
    tok = lambda width: pl.BlockSpec((1, TM, width), lambda b, j, seg: (b, j, 0))
    cvspec = pl.BlockSpec((1, 1, 3 * SUBLANES, d), lambda b, j, seg: (b, j, 0, 0))
    grid_spec = pltpu.PrefetchScalarGridSpec(
        num_scalar_prefetch=1, grid=(bsz, nblk),
        in_specs=_meta_specs(bsz, nblk) + [
                  pl.BlockSpec(memory_space=pl.ANY), tok(d), cs(mod0), cs(mod1), cs(nw), cs(w_in), cs(lb),
                  cs(d1f), cs(d1b), cs(mf), cs(mb)],
        out_specs=[tok(d)] * 6 + [pl.BlockSpec((1, d, TM), lambda b, j, seg: (b, 0, j)), tok(d), cvspec, cvspec],
        scratch_shapes=[pltpu.VMEM((2, TM, d), F32), pltpu.SemaphoreType.DMA((2,))])
    sds = jax.ShapeDtypeStruct
    out_shape = [sds((bsz, nt, d), F32)] + [sds((bsz, nt, d), BF16)] * 5 + [sds((bsz, d, nt), BF16)] + \
                [sds((bsz, nt, d), BF16)] + [sds((bsz, nblk, 3 * SUBLANES, d), F32)] * 2
    return pl.pallas_call(
        functools.partial(_inproj1_kernel, nctx_blk, bsz), grid_spec=grid_spec, out_shape=out_shape,
        compiler_params=_params(2), name="combine_inproj_hgrn",
    )(seg, meta, meta, ys, x1, mod0, mod1, nw, w_in, lb, d1f, d1b, mf, mb)


def _final_kernel(seg_ref, meta_ref, meta_next_ref, ys_hbm, x_ref, mod_ref, fw_ref, o_ref, buf, sem):
    nblk = pl.num_programs(1)
    step = pl.program_id(0) * nblk + pl.program_id(1)
    ffn = _gather_ffn(step, pl.num_programs(0) * nblk, seg_ref, meta_ref, meta_next_ref, ys_hbm, buf, sem)
    x = x_ref[0]
    d = x.shape[-1]
    mod = mod_ref[pl.ds(pl.program_id(0), 1), :]
    o_ref[0] = _rms(x + mod[:, 5 * d:6 * d] * ffn) * fw_ref[...]


def _final(seg, meta, ys, x3, mod, fw):
    bsz, seq, d = x3.shape
    nblk = seq // TM
    cs = lambda a: pl.BlockSpec(a.shape, lambda b, j, seg: (0,) * a.ndim)
    tok = pl.BlockSpec((1, TM, d), lambda b, j, seg: (b, j, 0))
    grid_spec = pltpu.PrefetchScalarGridSpec(
        num_scalar_prefetch=1, grid=(bsz, nblk),
        in_specs=_meta_specs(bsz, nblk) + [pl.BlockSpec(memory_space=pl.ANY), tok, cs(mod), cs(fw)],
        out_specs=tok,
        scratch_shapes=[pltpu.VMEM((2, TM, d), F32), pltpu.SemaphoreType.DMA((2,))])
    return pl.pallas_call(
        _final_kernel, grid_spec=grid_spec, out_shape=jax.ShapeDtypeStruct((bsz, seq, d), F32),
        compiler_params=_params(2), name="combine_final_norm",
    )(seg, meta, meta, ys, x3, mod, fw)


def _grid_sincos(n_tokens, dim):
    rows = n_tokens // GRID_W
    r, col = jnp.meshgrid(jnp.arange(rows, dtype=F32), jnp.arange(GRID_W, dtype=F32), indexing='ij')
    quarter = dim // 4
    omega = 1.0 / (10000.0 ** (jnp.arange(quarter, dtype=F32) / quarter))

    def emb(p):
        ang = p.reshape(-1, 1) * omega
        return jnp.concatenate([jnp.sin(ang), jnp.cos(ang)], axis=-1)

    return jnp.concatenate([emb(r), emb(col)], axis=-1)


def _pad_heads(w, n_heads):
    dk = w.shape[-1] // n_heads
    w = w.reshape(w.shape[:-1] + (n_heads, dk))
    w = jnp.pad(w, [(0, 0)] * (w.ndim - 1) + [(0, HEAD_W - dk)])
    return w.reshape(w.shape[:-2] + (n_heads * HEAD_W,))


def _router_tables(router_w, router_bias):
    n_exp = router_w.shape[1]
    epg = n_exp // N_GROUPS
    assert epg == EXPERTS_PER_GROUP
    perm = np.array([epg * g + p for p in range(epg) for g in range(N_GROUPS)])
    wt = jnp.pad(router_w.T[perm], ((0, LANES - n_exp), (0, 0)))
    hi = wt.astype(BF16)
    lo = (wt - hi.astype(F32)).astype(BF16)
    bias = jnp.pad(router_bias[perm], (0, LANES - n_exp))
    return hi, lo, jnp.broadcast_to(bias[:, None], (LANES, TM)).astype(F32)


def _segments(counts, n_tokens):
    cnt = counts[:, 0].astype(I32)
    tiles = (cnt + TM - 1) // TM
    ends = jnp.cumsum(tiles)
    seg = (ends - tiles) * TM
    ntile = n_tokens // TM + N_GROUPS
    n_valid = ends[-1]
    tidx = jnp.minimum(jnp.arange(ntile, dtype=I32), n_valid - 1)
    tile_group = jnp.sum((tidx[:, None] >= ends[None, :]).astype(I32), axis=1)
    return seg.astype(I32), tile_group.astype(I32), n_valid.reshape(1).astype(I32), ntile * TM


def _moe_layer(meta, counts, pay, weights):
    seg, tile_group, n_valid, n_sorted = _segments(counts, pay.shape[0])
    hs = _dispatch(seg, meta, pay, n_sorted)
    ys = _moe(tile_group, n_valid, hs, *weights)
    return seg, ys


def kernel(x, c, ctx, c_ctx, ada_w, ada_b, norm_mix, norm_ffn, ab_w_in, ab_w_out, gla_a2, gla_ab, gla_norm, s5_lam_re, s5_lam_im, s5_log_dt, s5_b_re, s5_b_im, s5_c_re, s5_c_im, s5_d, s5_glu_w, s5_glu_b, hg_w_in, hg_w_out, hg_lb_logits, hg_norm, router_w, router_bias, moe_w_gate, moe_w_up, moe_w_down, final_norm):
    bsz, seq, d = x.shape
    nctx = ctx.shape[1]
    depth = ada_w.shape[0]
    assert depth == 2 and seq % TM == 0 and nctx % TM == 0 and bsz % SUBLANES == 0 and bsz < MOD_ROWS
    nctx_blk = nctx // TM
    nt = nctx + seq
    nc = nt // S5_L
    assert (nc * bsz) % LANES == 0

    cond = jnp.zeros((MOD_ROWS, d), F32).at[:bsz].set(c).at[bsz].set(c_ctx)
    mod = _modulation(cond, ada_w, ada_b)
    consts = _chunk_matrices()
    row = lambda v: v.reshape(1, -1)

    qk = gla_a2.shape[-1]
    gv = ab_w_out.shape[1] // 2
    rank = gla_a2.shape[2]
    w_in = ab_w_in[0]
    o_v, o_g, o_a, o_u = 2 * qk, 2 * qk + gv, 2 * qk + 2 * gv, 2 * qk + 2 * gv + 2 * rank
    wm = jnp.concatenate([_pad_heads(w_in[:, 0:qk], GLA_HEADS), _pad_heads(w_in[:, qk:o_v], GLA_HEADS),
                          w_in[:, o_v:o_g], w_in[:, o_g:o_a], w_in[:, o_u:]], axis=1).astype(BF16)
    wa = jnp.pad(w_in[:, o_a:o_u], ((0, 0), (0, LANES - 2 * rank))).astype(BF16)
    a2p = _pad_heads(gla_a2[0], GLA_HEADS)
    a2 = jnp.zeros((LANES, 2 * gv), F32).at[0:rank, 0:gv].set(a2p[0]).at[rank:2 * rank, gv:].set(a2p[1]).astype(BF16)
    ab = _pad_heads(gla_ab[0], GLA_HEADS).reshape(1, 2 * gv)
    pos = _grid_sincos(seq, d)
    (xs, qf, kf, qb, kb, v, vt, g, u_cr, cvf, cvb) = _inproj0(x, ctx, pos, mod[0], row(norm_mix[0]), wm, wa, a2, ab, consts,
                                                               float(qk // GLA_HEADS) ** -0.5)
    o_f, o_b = _recurrence(qf, kf, qb, kb, v, vt, cvf, cvb, GLA_HEADS, nctx_blk)

    assert 2 * s5_b_re.shape[-1] == PAIR_W
    ops = _s5_operators(s5_lam_re[0], s5_lam_im[0], s5_log_dt[0], s5_b_re[0], s5_b_im[0], s5_c_re[0], s5_c_im[0], s5_d[0])
    y5 = _s5(u_cr, ops, nc, nctx // S5_L, bsz)

    rwh, rwl, rb = _router_tables(router_w, router_bias)
    utri = jnp.asarray(np.triu(np.ones((TM, TM), np.float32), 1))
    ones = jnp.ones((TM, TM), F32)
    x1, pay, meta, counts = _mixout(True, GLA_HEADS, nctx_blk, 0, xs, o_f, o_b, g, y5, mod[0], row(gla_norm[0]),
                                    s5_glu_w[0].astype(BF16), row(s5_glu_b[0]), ab_w_out[0].astype(BF16),
                                    row(norm_ffn[0]), rwh, rwl, rb, utri, ones)
    assert moe_w_gate.shape[1] == N_GROUPS * EXPERTS_PER_GROUP
    seg, ys = _moe_layer(meta, counts, pay, (0, moe_w_gate, moe_w_up, moe_w_down))

    lb_all = jax.nn.softmax(hg_lb_logits.astype(F32), axis=0)
    lb_all = jnp.cumsum(lb_all, axis=0) - lb_all[0]
    (x2, qf, kf, qb, kb, v, vt, g, cvf, cvb) = _inproj1(seg, meta, ys, x1, mod[0], mod[1], row(norm_mix[1]),
                                                          hg_w_in[0].astype(BF16), row(lb_all[1]), consts, nctx_blk)
    o_f, o_b = _recurrence(qf, kf, qb, kb, v, vt, cvf, cvb, HG_HEADS, nctx_blk)
    x3, pay, meta, counts = _mixout(False, HG_HEADS, nctx_blk, nctx_blk, x2, o_f, o_b, g, None, mod[1], row(hg_norm[0]),
                                    None, None, hg_w_out[0].astype(BF16), row(norm_ffn[1]), rwh, rwl, rb, utri, ones)
    seg, ys = _moe_layer(meta, counts, pay, (1, moe_w_gate, moe_w_up, moe_w_down))
    return _final(seg, meta, ys, x3, mod[1], row(final_norm))
```

```python
import functools
import math

import numpy as np
import jax
import jax.numpy as jnp
from jax import lax
from jax.experimental import pallas as pl
from jax.experimental.pallas import tpu as pltpu

F32, BF16, I32 = jnp.float32, jnp.bfloat16, jnp.int32

EPS = 1e-6
CHUNK = 64
GRID_W = 64
GLA_HEADS = 4
GLA_GATE_NORM = 16.0
HG_HEADS = 8
N_GROUPS = 8
EXPERTS_PER_GROUP = 4
TOP_K = 2

LANES = 128
SUBLANES = 8
TM = 256
CPT = TM // CHUNK
HEAD_W = 128
S5_L = 16
MOD_ROWS = 16
RUN = SUBLANES
RUN_SHIFT = 3
LROWS = TM + N_GROUPS * RUN
TAB_START, TAB_COUNT, TAB_CARRY = 0, 1, 2
AUX_LPOS = 4
VMEM_LIMIT = 56 * 1024 * 1024


def _dot(a, b):
    return jnp.dot(a, b, preferred_element_type=F32)


def _dot_nt(a, b):
    return lax.dot_general(a, b, (((1,), (1,)), ((), ())), preferred_element_type=F32)


def _split(x):
    hi = x.astype(BF16)
    lo = (x - hi.astype(F32)).astype(BF16)
    return hi, lo


def _sigmoid(x):
    return 1.0 / (1.0 + jnp.exp(-x))


def _silu(x):
    return x * _sigmoid(x)


def _gelu_tanh(x):
    return 0.5 * x * (1.0 + jnp.tanh(math.sqrt(2.0 / math.pi) * (x + 0.044715 * (x * x * x))))


def _rms(x):
    return x * lax.rsqrt(jnp.mean(x * x, axis=-1, keepdims=True) + EPS)


def _head_rms(o, gain, n_heads):
    outs = []
    for h in range(n_heads):
        oh = o[:, h * HEAD_W:(h + 1) * HEAD_W]
        outs.append(_rms(oh) * gain)
    return jnp.concatenate(outs, axis=1)


def _params(n_grid_dims, vmem=VMEM_LIMIT):
    return pltpu.CompilerParams(dimension_semantics=("arbitrary",) * n_grid_dims, vmem_limit_bytes=vmem)


def _const_spec(shape):
    nd = len(shape)
    return pl.BlockSpec(shape, lambda *_: (0,) * nd)


def _mod_kernel(c_ref, w_ref, b_ref, o_ref):
    s = _silu(c_ref[...])
    shi, slo = _split(s)
    whi, wlo = _split(w_ref[0])
    o_ref[0] = _dot(shi, whi) + _dot(shi, wlo) + _dot(slo, whi) + b_ref[0]


def _modulation(cond, ada_w, ada_b):
    depth, d, six_d = ada_w.shape
    nchunk = six_d // d
    return pl.pallas_call(
        _mod_kernel,
        grid=(depth, nchunk),
        in_specs=[
            _const_spec((MOD_ROWS, d)),
            pl.BlockSpec((1, d, d), lambda l, n: (l, 0, n)),
            pl.BlockSpec((1, 1, d), lambda l, n: (l, 0, n)),
        ],
        out_specs=pl.BlockSpec((1, MOD_ROWS, d), lambda l, n: (l, 0, n)),
        out_shape=jax.ShapeDtypeStruct((depth, MOD_ROWS, six_d), F32),
        compiler_params=_params(2),
        name="modulation",
    )(cond, ada_w, ada_b.reshape(depth, 1, six_d))


def _chunk_matrices():
    t = np.arange(TM)
    c, l = t // CHUNK, t % CHUNK
    same = (c[:, None] == c[None, :]).astype(np.float32)
    li, ls = l[:, None], l[None, :]
    mid_f = CHUNK // 2 - 1
    mid_b = CHUNK // 2
    d1f = same * ((ls <= li).astype(np.float32) - (ls <= mid_f).astype(np.float32))
    d1b = same * ((ls >= li).astype(np.float32) - (ls >= mid_b).astype(np.float32))
    inchunk = (np.arange(SUBLANES)[:, None] == c[None, :]).astype(np.float32)
    mf = np.concatenate([inchunk * (l <= mid_f), inchunk * (l > mid_f), inchunk])
    mb = np.concatenate([inchunk * (l >= mid_b), inchunk * (l < mid_b), inchunk])
    as_bf16 = lambda a: jnp.asarray(a, dtype=BF16)
    return as_bf16(d1f), as_bf16(d1b), as_bf16(mf), as_bf16(mb)


def _gate_outputs(q, k, la, d1_ref, m_ref, q_out, k_out, cv_out):
    hi, lo = _split(la)
    d1m = d1_ref[...]
    d1 = _dot(d1m, hi) + _dot(d1m, lo)
    q_out[0] = (q * jnp.exp(d1)).astype(BF16)
    k_out[0] = (k * jnp.exp(-d1)).astype(BF16)
    mm = m_ref[...]
    cv_out[0, 0] = jnp.exp(_dot(mm, hi) + _dot(mm, lo))


PAIR_W = 32
PAIRS_PER_TILE = LANES // PAIR_W
CROWS = TM // S5_L


def _to_chunk_rows(u, us_ref, ucr_ref):
    for q in range(us_ref.shape[0]):
        us_ref[q] = u[:, q * LANES:(q + 1) * LANES]
    lane = lax.broadcasted_iota(I32, (CROWS, LANES), 1)
    for p in range(ucr_ref.shape[0]):
        q, pp = divmod(p, PAIRS_PER_TILE)
        tiles = []
        for t in range(S5_L // PAIRS_PER_TILE):
            acc = None
            for jj in range(PAIRS_PER_TILE):
                piece = us_ref[q, pl.ds(t * PAIRS_PER_TILE + jj, CROWS, stride=S5_L), :]
                shift = (PAIR_W * (jj - pp)) % LANES
                if shift:
                    piece = pltpu.roll(piece, shift, axis=1)
                here = (lane >= PAIR_W * jj) & (lane < PAIR_W * (jj + 1))
                acc = piece if acc is None else jnp.where(here, piece, acc)
            tiles.append(acc)
        ucr_ref[p] = jnp.concatenate(tiles, axis=1).astype(BF16)


def _from_chunk_rows(ycr_ref, ys_ref):
    lane = lax.broadcasted_iota(I32, (CROWS, LANES), 1)
    for q in range(ycr_ref.shape[0] // PAIRS_PER_TILE):
        for i in range(S5_L):
            t, ii = divmod(i, PAIRS_PER_TILE)
            acc = None
            for pp in range(PAIRS_PER_TILE):
                piece = ycr_ref[q * PAIRS_PER_TILE + pp, :, t * LANES:(t + 1) * LANES].astype(F32)
                shift = (PAIR_W * (pp - ii)) % LANES
                if shift:
                    piece = pltpu.roll(piece, shift, axis=1)
                here = (lane >= PAIR_W * pp) & (lane < PAIR_W * (pp + 1))
                acc = piece if acc is None else jnp.where(here, piece, acc)
            ys_ref[q, pl.ds(i, CROWS, stride=S5_L), :] = acc
    return jnp.concatenate([ys_ref[q] for q in range(ys_ref.shape[0])], axis=1)


def _inproj0_kernel(nctx_blk, nb, gla_w, q_scale, x_ref, ctx_ref, pos_ref, mod_ref, nw_ref, wm_ref, wa_ref, a2_ref,
                    ab_ref, d1f_ref, d1b_ref, mf_ref, mb_ref,
                    xs_ref, qf_ref, kf_ref, qb_ref, kb_ref, v_ref, vt_ref, g_ref, ucr_ref, cvf_ref, cvb_ref, us_ref):
    b, j = pl.program_id(0), pl.program_id(1)
    is_ctx = j < nctx_blk
    xin = jnp.where(is_ctx, ctx_ref[0], x_ref[0] + pos_ref[...])
    xs_ref[0] = xin
    d = xin.shape[-1]
    mod = mod_ref[pl.ds(jnp.where(is_ctx, nb, b), 1), :]
    h = _rms(xin) * nw_ref[...] * (1.0 + mod[:, d:2 * d]) + mod[:, 0:d]
    hb = h.astype(BF16)
    w = gla_w
    y = _dot(hb, wm_ref[...])
    q = y[:, 0:w] * q_scale
    k = y[:, w:2 * w]
    v = y[:, 2 * w:3 * w]
    g_ref[0] = y[:, 3 * w:4 * w].astype(BF16)
    _to_chunk_rows(y[:, 4 * w:5 * w], us_ref, ucr_ref)
    v_ref[0] = v.astype(BF16)
    vt_ref[0] = v.T.astype(BF16)
    a = _dot(hb, wa_ref[...])
    z = _dot(a.astype(BF16), a2_ref[...]) + ab_ref[...]
    la = (jnp.minimum(z, 0.0) - jnp.log(1.0 + jnp.exp(-jnp.abs(z)))) * (1.0 / GLA_GATE_NORM)
    _gate_outputs(q, k, la[:, 0:w], d1f_ref, mf_ref, qf_ref, kf_ref, cvf_ref)
    _gate_outputs(q, k, la[:, w:2 * w], d1b_ref, mb_ref, qb_ref, kb_ref, cvb_ref)


def _inproj0(x, ctx, pos, mod, nw, wm, wa, a2, ab, consts, q_scale):
    bsz, seq, d = x.shape
    nctx = ctx.shape[1]
    nctx_blk, nlat_blk = nctx // TM, seq // TM
    nblk = nctx_blk + nlat_blk
    nt = nblk * TM
    w = wm.shape[1] // 5
    d1f, d1b, mf, mb = consts
    lat = lambda b, j: (b, jnp.maximum(j - nctx_blk, 0), 0)
    tok = lambda b, j: (b, j, 0)
    in_specs = [
        pl.BlockSpec((1, TM, d), lat),
        pl.BlockSpec((1, TM, d), lambda b, j: (b, jnp.minimum(j, nctx_blk - 1), 0)),
        pl.BlockSpec((TM, d), lambda b, j: (jnp.maximum(j - nctx_blk, 0), 0)),
        _const_spec(mod.shape), _const_spec(nw.shape), _const_spec(wm.shape), _const_spec(wa.shape),
        _const_spec(a2.shape), _const_spec(ab.shape), _const_spec(d1f.shape), _const_spec(d1b.shape),
        _const_spec(mf.shape), _const_spec(mb.shape),
    ]
    tokspec = lambda width: pl.BlockSpec((1, TM, width), tok)
    cvspec = pl.BlockSpec((1, 1, 3 * SUBLANES, w), lambda b, j: (b, j, 0, 0))
    npair, crow_w = w // PAIR_W, S5_L * PAIR_W
    out_specs = [tokspec(d), tokspec(w), tokspec(w), tokspec(w), tokspec(w), tokspec(w),
                 pl.BlockSpec((1, w, TM), lambda b, j: (b, 0, j)), tokspec(w),
                 pl.BlockSpec((npair, CROWS, crow_w), lambda b, j: (0, b * nblk + j, 0)), cvspec, cvspec]
    sds = jax.ShapeDtypeStruct
    out_shape = [sds((bsz, nt, d), F32)] + [sds((bsz, nt, w), BF16)] * 5 + [sds((bsz, w, nt), BF16)] + \
                [sds((bsz, nt, w), BF16), sds((npair, bsz * nblk * CROWS, crow_w), BF16)] + \
                [sds((bsz, nblk, 3 * SUBLANES, w), F32)] * 2
    return pl.pallas_call(
        functools.partial(_inproj0_kernel, nctx_blk, bsz, w, q_scale),
        grid=(bsz, nblk), in_specs=in_specs, out_specs=out_specs, out_shape=out_shape,
        scratch_shapes=[pltpu.VMEM((w // LANES, TM, LANES), F32)],
        compiler_params=_params(2), name="inproj_gla_s5",
    )(x, ctx, pos, mod, nw, wm, wa, a2, ab, d1f, d1b, mf, mb)


def _recur_kernel(n_heads, qf_ref, kf_ref, vf_ref, vtf_ref, cvf_ref, qb_ref, kb_ref, vb_ref, vtb_ref, cvb_ref,
                  of_ref, ob_ref, s_ref):
    @pl.when(pl.program_id(1) == 0)
    def _():
        s_ref[...] = jnp.zeros_like(s_ref)

    ri = lax.broadcasted_iota(I32, (CHUNK, CHUNK), 0)
    ci = lax.broadcasted_iota(I32, (CHUNK, CHUNK), 1)
    causal = (ci <= ri, ci >= ri)
    prow = lax.broadcasted_iota(I32, (2 * CHUNK, HEAD_W), 0)
    dirs = ((qf_ref, kf_ref, vf_ref, vtf_ref, cvf_ref, of_ref), (qb_ref, kb_ref, vb_ref, vtb_ref, cvb_ref, ob_ref))
    for d, (q_ref, k_ref, v_ref, vt_ref, cv_ref, o_ref) in enumerate(dirs):
        for cc in range(CPT):
            c = cc if d == 0 else CPT - 1 - cc
            r0 = c * CHUNK
            p0 = (c // 2) * 2 * CHUNK
            in_chunk = (prow >= r0 - p0) & (prow < r0 - p0 + CHUNK)
            e_mid = cv_ref[0, 0, c:c + 1, :]
            e_rest = cv_ref[0, 0, SUBLANES + c:SUBLANES + c + 1, :]
            e_all = cv_ref[0, 0, 2 * SUBLANES + c:2 * SUBLANES + c + 1, :]
            for h in range(n_heads):
                hs = slice(h * HEAD_W, (h + 1) * HEAD_W)
                qi = q_ref[0, r0:r0 + CHUNK, hs]
                ki = k_ref[0, r0:r0 + CHUNK, hs]
                vv = v_ref[0, r0:r0 + CHUNK, hs]
                kp = k_ref[0, p0:p0 + 2 * CHUNK, hs]
                kp = jnp.where(in_chunk, kp, jnp.zeros_like(kp))
                vtp = vt_ref[0, hs, p0:p0 + 2 * CHUNK]
                st = s_ref[d, h]
                attn = jnp.where(causal[d], _dot_nt(qi, ki), 0.0).astype(BF16)
                o = _dot(attn, vv) + _dot_nt(qi, (st * e_mid[:, hs]).astype(BF16))
                o_ref[0, r0:r0 + CHUNK, hs] = o.astype(o_ref.dtype)
                s_ref[d, h] = st * e_all[:, hs] + _dot(vtp, kp) * e_rest[:, hs]


def _recurrence(qf, kf, qb, kb, v, vt, cvf, cvb, n_heads, nctx_blk):
    bsz, nt, w = qf.shape
    nblk = nt // TM
    fwd = lambda j: j
    bwd = lambda j: jnp.where(j < nctx_blk, nctx_blk - 1 - j, nblk - 1 - (j - nctx_blk))
    specs = []
    for order in (fwd, bwd):
        tok = pl.BlockSpec((1, TM, w), lambda b, j, o=order: (b, o(j), 0))
        specs += [tok, tok, tok,
                  pl.BlockSpec((1, w, TM), lambda b, j, o=order: (b, 0, o(j))),
                  pl.BlockSpec((1, 1, 3 * SUBLANES, w), lambda b, j, o=order: (b, o(j), 0, 0))]
    out_specs = [pl.BlockSpec((1, TM, w), lambda b, j: (b, j, 0)),
                 pl.BlockSpec((1, TM, w), lambda b, j: (b, bwd(j), 0))]
    return pl.pallas_call(
        functools.partial(_recur_kernel, n_heads),
        grid=(bsz, nblk), in_specs=specs, out_specs=out_specs,
        out_shape=[jax.ShapeDtypeStruct((bsz, nt, w), BF16)] * 2,
        scratch_shapes=[pltpu.VMEM((2, n_heads, HEAD_W, HEAD_W), F32)],
        compiler_params=_params(2), name=f"recurrence_h{n_heads}",
    )(qf, kf, v, vt, cvf, qb, kb, v, vt, cvb)


def _s5_operators(lam_re, lam_im, log_dt, b_re, b_im, c_re, c_im, d_skip):
    L = S5_L
    ndir, ng, p = lam_re.shape
    hs = b_re.shape[-1]
    dt = jnp.exp(log_dt)[..., None]
    lr, li = lam_re * dt, lam_im * dt
    mag = jnp.exp(lr)
    a_re, a_im = mag * jnp.cos(li), mag * jnp.sin(li)
    den = lam_re * lam_re + lam_im * lam_im
    f_re = ((a_re - 1.0) * lam_re + a_im * lam_im) / den
    f_im = (a_im * lam_re - (a_re - 1.0) * lam_im) / den
    bb_re = f_re[..., None] * b_re - f_im[..., None] * b_im
    bb_im = f_re[..., None] * b_im + f_im[..., None] * b_re
    m = jnp.arange(L + 1, dtype=F32)[:, None, None, None]
    pw_mag = jnp.exp(m * lr[None])
    pw_re, pw_im = pw_mag * jnp.cos(m * li[None]), pw_mag * jnp.sin(m * li[None])
    eye = jnp.eye(2, dtype=F32)
    npair, pw, cw = ng // 2, 2 * hs, 2 * L * hs

    toep = jnp.zeros((npair, cw, cw), F32)
    wz_p, vo_p = [], []
    for di in range(ndir):
        pr, pi = pw_re[:L, di, :, :, None], pw_im[:L, di, :, :, None]
        ab_re = pr * bb_re[di][None] - pi * bb_im[di][None]
        ab_im = pr * bb_im[di][None] + pi * bb_re[di][None]
        kf = jnp.einsum('mgnk,ghn->gkmh', ab_re, c_re[di]) - jnp.einsum('mgnk,ghn->gkmh', ab_im, c_im[di])
        if di == 1:
            kf = kf[:, :, ::-1, :]
        kp = jnp.einsum('pskmh,st->pskmth', kf.reshape(npair, 2, hs, L, hs), eye).reshape(npair, pw, cw)
        if di == 0:
            kpad = jnp.pad(kp, ((0, 0), (0, 0), (cw, 0)))
            blocks = [kpad[:, :, cw - pw * j:2 * cw - pw * j] for j in range(L)]
        else:
            kpad = jnp.pad(kp, ((0, 0), (0, 0), (0, cw)))
            blocks = [kpad[:, :, pw * (L - 1 - j):pw * (L - 1 - j) + cw] for j in range(L)]
        toep = toep + jnp.stack(blocks, axis=1).reshape(npair, cw, cw)
        sel_i = (lambda a: a[::-1]) if di == 0 else (lambda a: a)
        pack_w = lambda a: jnp.einsum('jpsnk,st->pjsktn', sel_i(a).reshape(L, npair, 2, p, hs), eye).reshape(npair, cw, 2 * p)
        wz_p.append(jnp.concatenate([pack_w(ab_re), pack_w(ab_im)], axis=2))
        qr, qi = pw_re[1:, di, :, None, :], pw_im[1:, di, :, None, :]
        ca_re = c_re[di][None] * qr - c_im[di][None] * qi
        ca_im = c_re[di][None] * qi + c_im[di][None] * qr
        sel_o = (lambda a: a) if di == 0 else (lambda a: a[::-1])
        pack_v = lambda a: jnp.einsum('ipshn,st->psnith', sel_o(a).reshape(L, npair, 2, hs, p), eye).reshape(npair, 2 * p, cw)
        vo_p.append(jnp.concatenate([pack_v(ca_re), -pack_v(ca_im)], axis=1))
    wz_p, vo_p = jnp.stack(wz_p, axis=1), jnp.stack(vo_p, axis=1)
    al = jnp.stack([pw_re[L], pw_im[L]], axis=1)
    al_p = jnp.transpose(al.reshape(ndir, 2, npair, 2 * p), (2, 0, 1, 3))
    d_p = jnp.tile(d_skip.reshape(npair, 1, pw), (1, L, 1)).reshape(npair, 1, cw)
    return wz_p.astype(BF16), vo_p.astype(BF16), toep.astype(BF16), al_p, d_p


def _s5_kernel(nc, ncc, bsz, u_ref, wz_ref, vo_ref, tp_ref, al_ref, d_ref, y_ref, zf, zb, xf, xb):
    rows_total = nc * bsz
    rb = LANES
    half = zf.shape[2]

    def z_body(i, carry):
        rows = pl.ds(pl.multiple_of(i * rb, rb), rb)
        ub = u_ref[0, rows, :]
        for z, di in ((zf, 0), (zb, 1)):
            zz = _dot(ub, wz_ref[0, di])
            z[0, rows, :] = zz[:, 0:half]
            z[1, rows, :] = zz[:, half:2 * half]
        return carry

    lax.fori_loop(0, rows_total // rb, z_body, 0)

    def scan(z, xs, di, order):
        a_re = jnp.broadcast_to(al_ref[0, di, 0:1, :], (bsz, half))
        a_im = jnp.broadcast_to(al_ref[0, di, 1:2, :], (bsz, half))

        def body(i, carry):
            x_re, x_im = carry
            rows = pl.ds(order(i), bsz, stride=nc)
            xs[0, rows, :] = x_re
            xs[1, rows, :] = x_im
            return (a_re * x_re - a_im * x_im + z[0, rows, :], a_re * x_im + a_im * x_re + z[1, rows, :])

        zero = jnp.zeros((bsz, half), F32)
        lax.fori_loop(0, nc, body, (zero, zero))

    scan(zf, xf, 0, lambda i: i)
    scan(zb, xb, 1, lambda i: jnp.where(i < ncc, ncc - 1 - i, nc - 1 - (i - ncc)))

    def y_body(i, carry):
        rows = pl.ds(pl.multiple_of(i * rb, rb), rb)
        ub = u_ref[0, rows, :]
        state = lambda xs: jnp.concatenate([xs[0, rows, :], xs[1, rows, :]], axis=1).astype(BF16)
        y = _dot(state(xf), vo_ref[0, 0]) + _dot(state(xb), vo_ref[0, 1])
        y_ref[0, rows, :] = (y + _dot(ub, tp_ref[0]) + ub.astype(F32) * d_ref[0]).astype(y_ref.dtype)
        return carry

    lax.fori_loop(0, rows_total // rb, y_body, 0)


def _s5(u_cr, ops, nc, ncc, bsz):
    wz_p, vo_p, toep_p, al_p, d_p = ops
    npair, rows, width = u_cr.shape
    per_pair = lambda a: pl.BlockSpec((1,) + a.shape[1:], lambda p: (p,) + (0,) * (a.ndim - 1))
    return pl.pallas_call(
        functools.partial(_s5_kernel, nc, ncc, bsz),
        grid=(npair,),
        in_specs=[per_pair(u_cr), per_pair(wz_p), per_pair(vo_p), per_pair(toep_p), per_pair(al_p), per_pair(d_p)],
        out_specs=per_pair(u_cr),
        out_shape=jax.ShapeDtypeStruct(u_cr.shape, BF16),
        scratch_shapes=[pltpu.VMEM((2, rows, wz_p.shape[-1] // 2), F32)] * 4,
        compiler_params=_params(1), name="s5",
    )(u_cr, wz_p, vo_p, toep_p, al_p, d_p)


def _route(h2, rwh_ref, rwl_ref, rb_ref, utri_ref, ones_ref, carry_ref):
    hi, lo = _split(h2)
    logits = _dot_nt(rwh_ref[...], hi) + _dot_nt(rwl_ref[...], hi) + _dot_nt(rwh_ref[...], lo)
    aff = _sigmoid(logits)
    sel = aff + rb_ref[...]
    epg = EXPERTS_PER_GROUP
    s = [sel[N_GROUPS * p:N_GROUPS * (p + 1), :] for p in range(epg)]
    a = [aff[N_GROUPS * p:N_GROUPS * (p + 1), :] for p in range(epg)]
    m1, n1 = jnp.maximum(s[0], s[1]), jnp.minimum(s[0], s[1])
    m2, n2 = jnp.maximum(s[2], s[3]), jnp.minimum(s[2], s[3])
    score = jnp.maximum(m1, m2) + jnp.maximum(jnp.minimum(m1, m2), jnp.maximum(n1, n2))
    gi = lax.broadcasted_iota(I32, score.shape, 0)
    best = jnp.max(score, axis=0, keepdims=True)
    gidx = jnp.min(jnp.where(score == best, gi, N_GROUPS), axis=0, keepdims=True)
    onehot = gi == gidx
    gates = []
    for p in range(epg):
        ahead = jnp.zeros_like(score)
        for q in range(epg):
            if q != p:
                beats = (s[q] >= s[p]) if q < p else (s[q] > s[p])
                ahead = ahead + jnp.where(beats, 1.0, 0.0)
        picked = jnp.where(onehot, jnp.where(ahead < float(TOP_K) - 0.5, a[p], 0.0), 0.0)
        gates.append(jnp.sum(picked, axis=0, keepdims=True))
    den = gates[0] + gates[1] + gates[2] + gates[3]
    gates = [g / den for g in gates]
    oh = jnp.where(onehot, 1.0, 0.0)
    before = _dot(oh, utri_ref[...])
    count = _dot(oh, ones_ref[...])
    run = jnp.floor((count + (RUN - 1.0)) * (1.0 / RUN)) * RUN
    carry = carry_ref[...]
    carry_ref[...] = carry + run
    starts, nxt = [], jnp.zeros_like(run[0:1])
    for g in range(N_GROUPS):
        starts.append(nxt)
        nxt = nxt + run[g:g + 1]
    lstart = jnp.concatenate(starts, axis=0)
    lpos = jnp.sum(jnp.where(onehot, lstart + before, 0.0), axis=0, keepdims=True)
    return lpos, gates, (lstart, count, carry)


def _mixout_kernel(has_s5, n_heads, nctx_blk, tok_off, nb, *refs):
    if has_s5:
        (x_ref, of_ref, ob_ref, g_ref, ycr_ref, mod_ref, gn_ref, gluw_ref, glub_ref, wo_ref, nf_ref,
         rwh_ref, rwl_ref, rb_ref, utri_ref, ones_ref,
         x1_ref, pay_ref, aux_ref, meta_ref, tab_ref, cnt_ref, carry_ref, ys_ref) = refs
    else:
        (x_ref, of_ref, ob_ref, g_ref, mod_ref, gn_ref, wo_ref, nf_ref,
         rwh_ref, rwl_ref, rb_ref, utri_ref, ones_ref,
         x1_ref, pay_ref, aux_ref, meta_ref, tab_ref, cnt_ref, carry_ref) = refs
    b, j = pl.program_id(0), pl.program_id(1)

    @pl.when((b == 0) & (j == 0))
    def _():
        carry_ref[...] = jnp.zeros_like(carry_ref)

    x = x_ref[0]
    d = x.shape[-1]
    mod = mod_ref[pl.ds(jnp.where(j + tok_off < nctx_blk, nb, b), 1), :]
    o = of_ref[0].astype(F32) + ob_ref[0].astype(F32)
    mixed = _head_rms(o, gn_ref[...], n_heads) * _silu(g_ref[0].astype(F32))
    if has_s5:
        act = _gelu_tanh(_from_chunk_rows(ycr_ref, ys_ref))
        glu = act * _sigmoid(_dot(act.astype(BF16), gluw_ref[...]) + glub_ref[...])
        mixed = jnp.concatenate([mixed, glu], axis=1)
    x1 = x + mod[:, 2 * d:3 * d] * _dot(mixed.astype(BF16), wo_ref[...])
    x1_ref[0] = x1
    h2 = _rms(x1) * nf_ref[...] * (1.0 + mod[:, 4 * d:5 * d]) + mod[:, 3 * d:4 * d]

    lpos, gates, (lstart, count, carry) = _route(h2, rwh_ref, rwl_ref, rb_ref, utri_ref, ones_ref, carry_ref)
    gi = lax.broadcasted_iota(I32, (SUBLANES, TM), 0)
    meta_ref[0] = jnp.where(gi == 0, lpos.astype(I32), 0)
    ti = lax.broadcasted_iota(I32, (N_GROUPS, LANES), 1)
    first = lambda a: a[:, 0:LANES].astype(I32)
    tab_ref[0] = jnp.where(ti == TAB_START, first(lstart),
                           jnp.where(ti == TAB_COUNT, first(count), jnp.where(ti == TAB_CARRY, first(carry), 0)))
    cnt_ref[...] = carry_ref[...]

    li = lax.broadcasted_iota(I32, (LANES, TM), 0)
    gt = jnp.where(li == AUX_LPOS, lpos, 0.0)
    for p, g in enumerate(gates):
        gt = gt + jnp.where(li == p, g, 0.0)
    aux = gt.T
    pay_ref[:, 0:d] = h2
    pay_ref[:, d:d + LANES] = aux
    aux_ref[...] = aux


def _mixout(has_s5, n_heads, nctx_blk, tok_off, x, of, ob, g, y5, mod, gn, gluw, glub, wo, nf, rwh, rwl, rb, utri, ones):
    bsz, _, d = x.shape
    w = of.shape[-1]
    nblk = of.shape[1] // TM - tok_off
    tok = lambda width: pl.BlockSpec((1, TM, width), lambda b, j: (b, j + tok_off, 0))
    ins, specs = [x, of, ob, g], [tok(d), tok(w), tok(w), tok(w)]
    scratch = [pltpu.VMEM((N_GROUPS, TM), F32)]
    if has_s5:
        ins.append(y5)
        specs.append(pl.BlockSpec((y5.shape[0], CROWS, y5.shape[2]), lambda b, j: (0, b * nblk + j, 0)))
        scratch.append(pltpu.VMEM((w // LANES, TM, LANES), F32))
    consts = [mod, gn] + ([gluw, glub] if has_s5 else []) + [wo, nf, rwh, rwl, rb, utri, ones]
    ins += consts
    specs += [_const_spec(c.shape) for c in consts]
    ntile = bsz * nblk
    pay_w = d + LANES
    lin = lambda b, j: b * nblk + j
    out_specs = [pl.BlockSpec((1, TM, d), lambda b, j: (b, j, 0)),
                 pl.BlockSpec((TM, pay_w), lambda b, j: (lin(b, j), 0)),
                 pl.BlockSpec((TM, LANES), lambda b, j: (lin(b, j), 0)),
                 pl.BlockSpec((1, SUBLANES, TM), lambda b, j: (lin(b, j), 0, 0)),
                 pl.BlockSpec((1, N_GROUPS, LANES), lambda b, j: (lin(b, j), 0, 0)),
                 _const_spec((N_GROUPS, TM))]
    sds = jax.ShapeDtypeStruct
    out_shape = [sds((bsz, nblk * TM, d), F32), sds((ntile * TM, pay_w), F32), sds((ntile * TM, LANES), F32),
                 sds((ntile, SUBLANES, TM), I32), sds((ntile, N_GROUPS, LANES), I32), sds((N_GROUPS, TM), F32)]
    return pl.pallas_call(
        functools.partial(_mixout_kernel, has_s5, n_heads, nctx_blk, tok_off, bsz),
        grid=(bsz, nblk), in_specs=specs, out_specs=out_specs, out_shape=out_shape,
        scratch_shapes=scratch,
        compiler_params=_params(2), name="mixout_s5" if has_s5 else "mixout",
    )(*ins)


def _for_each_run(seg_ref, tab_ref, fn):
    for g in range(N_GROUPS):
        lstart = tab_ref[0, g, TAB_START]
        first = seg_ref[g] + tab_ref[0, g, TAB_CARRY]
        n_pieces = lax.shift_right_logical(tab_ref[0, g, TAB_COUNT] + (RUN - 1), RUN_SHIFT)

        def piece(k, carry, lstart=lstart, first=first):
            fn(pl.ds(pl.multiple_of(lstart + k * RUN, RUN), RUN), pl.ds(pl.multiple_of(first + k * RUN, RUN), RUN))
            return carry

        lax.fori_loop(0, n_pieces, piece, 0)


def _dispatch_kernel(seg_ref, cnt_ref, tab_ref, meta_ref, pay_ref, hs_hbm, s_ref, z_ref, sem):
    d = pay_ref.shape[1] - LANES
    lpos = meta_ref[0, 0:1, :]
    rows = lax.broadcasted_iota(I32, (LROWS, TM), 0)
    pick = jnp.where(rows == lpos, 1.0, 0.0).astype(BF16)
    s_ref[:, 0:d] = _dot(pick, pay_ref[:, 0:d].astype(BF16))
    ghi, glo = _split(pay_ref[:, d:d + LANES])
    s_ref[:, d:d + LANES] = _dot(pick, ghi) + _dot(pick, glo)

    copy = lambda src, dst: pltpu.make_async_copy(s_ref.at[src], hs_hbm.at[dst], sem)
    _for_each_run(seg_ref, tab_ref, lambda src, dst: copy(src, dst).start())
    _for_each_run(seg_ref, tab_ref, lambda src, dst: copy(src, dst).wait())

    @pl.when(pl.program_id(0) == pl.num_programs(0) - 1)
    def _():
        z_ref[...] = jnp.zeros_like(z_ref)
        for op in ("start", "wait"):
            for g in range(N_GROUPS + 1):
                begin = seg_ref[g] + cnt_ref[g]

                def piece(k, carry, begin=begin):
                    dst = hs_hbm.at[pl.ds(pl.multiple_of(begin + k * RUN, RUN), RUN)]
                    getattr(pltpu.make_async_copy(z_ref, dst, sem), op)()
                    return carry

                lax.fori_loop(0, lax.shift_right_logical(seg_ref[g + 1] - begin, RUN_SHIFT), piece, 0)


def _dispatch(seg, cnt, tab, meta, pay, n_sorted):
    ntile = meta.shape[0]
    width = pay.shape[1]
    grid_spec = pltpu.PrefetchScalarGridSpec(
        num_scalar_prefetch=2, grid=(ntile,),
        in_specs=[pl.BlockSpec((1, N_GROUPS, LANES), lambda i, seg, cnt: (i, 0, 0), memory_space=pltpu.SMEM),
                  pl.BlockSpec((1, SUBLANES, TM), lambda i, seg, cnt: (i, 0, 0)),
                  pl.BlockSpec((TM, width), lambda i, seg, cnt: (i, 0))],
        out_specs=pl.BlockSpec(memory_space=pl.ANY),
        scratch_shapes=[pltpu.VMEM((LROWS, width), F32), pltpu.VMEM((RUN, width), F32), pltpu.SemaphoreType.DMA(())])
    return pl.pallas_call(
        _dispatch_kernel, grid_spec=grid_spec,
        out_shape=jax.ShapeDtypeStruct((n_sorted, width), F32),
        compiler_params=_params(1), name="moe_dispatch",
    )(seg, cnt, tab, meta, pay)


def _moe_kernel(tg_ref, nv_ref, hs_ref, wg_ref, wu_ref, wd_ref, ys_ref, wgb, wub, wdb):
    i = pl.program_id(0)
    valid = i < nv_ref[0]

    @pl.when(valid & ((i == 0) | (tg_ref[i] != tg_ref[jnp.maximum(i - 1, 0)])))
    def _():
        wgb[...] = wg_ref[...].astype(BF16)
        wub[...] = wu_ref[...].astype(BF16)
        wdb[...] = wd_ref[...].astype(BF16)

    @pl.when(valid)
    def _():
        d = wg_ref.shape[1]
        h = hs_ref[:, 0:d].astype(BF16)
        gate = hs_ref[:, d:d + LANES]
        acc = jnp.zeros((TM, d), F32)
        for p in range(EXPERTS_PER_GROUP):
            act = _silu(_dot(h, wgb[p])) * _dot(h, wub[p]) * gate[:, p:p + 1]
            acc = acc + _dot(act.astype(BF16), wdb[p])
        ys_ref[...] = acc

    @pl.when(jnp.logical_not(valid))
    def _():
        ys_ref[...] = jnp.zeros_like(ys_ref)


def _moe(tile_group, n_valid, hs, layer, wg, wu, wd):
    n_sorted, pay_w = hs.shape
    ntile = n_sorted // TM
    d = wg.shape[2]
    epg = EXPERTS_PER_GROUP
    clamp = lambda i, tg, nv: jnp.minimum(i, nv[0] - 1)
    wspec = lambda a: pl.BlockSpec((None, epg) + a.shape[2:], lambda i, tg, nv: (layer, tg[i], 0, 0))
    grid_spec = pltpu.PrefetchScalarGridSpec(
        num_scalar_prefetch=2, grid=(ntile,),
        in_specs=[pl.BlockSpec((TM, pay_w), lambda i, tg, nv: (clamp(i, tg, nv), 0)), wspec(wg), wspec(wu), wspec(wd)],
        out_specs=pl.BlockSpec((TM, d), lambda i, tg, nv: (i, 0)),
        scratch_shapes=[pltpu.VMEM((epg,) + wg.shape[2:], BF16), pltpu.VMEM((epg,) + wu.shape[2:], BF16),
                        pltpu.VMEM((epg,) + wd.shape[2:], BF16)])
    return pl.pallas_call(
        _moe_kernel, grid_spec=grid_spec,
        out_shape=jax.ShapeDtypeStruct((n_sorted, d), F32),
        compiler_params=_params(1), name="moe_experts",
    )(tile_group, n_valid, hs, wg, wu, wd)


def _route_specs(bsz, nblk):
    last = bsz * nblk - 1
    cur = pl.BlockSpec((1, N_GROUPS, LANES), lambda b, j, seg: (b * nblk + j, 0, 0), memory_space=pltpu.SMEM)
    nxt = pl.BlockSpec((1, N_GROUPS, LANES), lambda b, j, seg: (jnp.minimum(b * nblk + j + 1, last), 0, 0),
                       memory_space=pltpu.SMEM)
    aux = pl.BlockSpec((TM, LANES), lambda b, j, seg: (b * nblk + j, 0))
    return [cur, nxt, aux]


def _gather_ffn(step, n_steps, seg_ref, tab_ref, tab_next_ref, aux_ref, ys_hbm, buf, sem):
    def runs(t_ref, slot, op):
        copy = lambda dst, src: pltpu.make_async_copy(ys_hbm.at[src], buf.at[slot, dst], sem.at[slot])
        _for_each_run(seg_ref, t_ref, lambda dst, src: getattr(copy(dst, src), op)())

    slot = lax.rem(step, 2)

    @pl.when(step == 0)
    def _():
        buf[...] = jnp.zeros_like(buf)
        runs(tab_ref, 0, "start")

    @pl.when(step + 1 < n_steps)
    def _():
        runs(tab_next_ref, 1 - slot, "start")

    runs(tab_ref, slot, "wait")
    lpos = aux_ref[:, AUX_LPOS:AUX_LPOS + 1].astype(I32)
    cols = lax.broadcasted_iota(I32, (TM, LROWS), 1)
    pick = jnp.where(cols == lpos, 1.0, 0.0).astype(BF16)
    hi, lo = _split(buf[slot])
    return _dot(pick, hi) + _dot(pick, lo)


def _inproj1_kernel(nctx_blk, nb, seg_ref, tab_ref, tab_next_ref, aux_ref, ys_hbm, x_ref, mod0_ref, mod1_ref, nw_ref,
                    w_ref, lb_ref, d1f_ref, d1b_ref, mf_ref, mb_ref,
                    x2_ref, qf_ref, kf_ref, qb_ref, kb_ref, v_ref, vt_ref, g_ref, cvf_ref, cvb_ref, buf, sem):
    b, j = pl.program_id(0), pl.program_id(1)
    nblk = pl.num_programs(1)
    ffn = _gather_ffn(b * nblk + j, nb * nblk, seg_ref, tab_ref, tab_next_ref, aux_ref, ys_hbm, buf, sem)
    x = x_ref[0]
    d = x.shape[-1]
    row = jnp.where(j < nctx_blk, nb, b)
    mod0 = mod0_ref[pl.ds(row, 1), :]
    mod1 = mod1_ref[pl.ds(row, 1), :]
    x2 = x + mod0[:, 5 * d:6 * d] * ffn
    x2_ref[0] = x2
    hb = (_rms(x2) * nw_ref[...] * (1.0 + mod1[:, d:2 * d]) + mod1[:, 0:d]).astype(BF16)
    proj = lambda n: _dot(hb, w_ref[:, n * d:(n + 1) * d])
    q = _silu(proj(0))
    lb = lb_ref[...]
    v = proj(3)
    v_ref[0] = v.astype(BF16)
    vt_ref[0] = v.T.astype(BF16)
    g_ref[0] = proj(4).astype(BF16)
    f_f = lb + (1.0 - lb) * _sigmoid(proj(1))
    _gate_outputs(q, 1.0 - f_f, jnp.log(f_f), d1f_ref, mf_ref, qf_ref, kf_ref, cvf_ref)
    f_b = lb + (1.0 - lb) * _sigmoid(proj(2))
    _gate_outputs(q, 1.0 - f_b, jnp.log(f_b), d1b_ref, mb_ref, qb_ref, kb_ref, cvb_ref)


def _inproj1(seg, tab, aux, ys, x1, mod0, mod1, nw, w_in, lb, consts, nctx_blk):
    bsz, nt, d = x1.shape
    nblk = nt // TM
    d1f, d1b, mf, mb = consts
    cs = lambda a: pl.BlockSpec(a.shape, lambda b, j, seg: (0,) * a.ndim)
    tok = lambda width: pl.BlockSpec((1, TM, width), lambda b, j, seg: (b, j, 0))
    cvspec = pl.BlockSpec((1, 1, 3 * SUBLANES, d), lambda b, j, seg: (b, j, 0, 0))
    grid_spec = pltpu.PrefetchScalarGridSpec(
        num_scalar_prefetch=1, grid=(bsz, nblk),
        in_specs=_route_specs(bsz, nblk) + [
                  pl.BlockSpec(memory_space=pl.ANY), tok(d), cs(mod0), cs(mod1), cs(nw), cs(w_in), cs(lb),
                  cs(d1f), cs(d1b), cs(mf), cs(mb)],
        out_specs=[tok(d)] * 6 + [pl.BlockSpec((1, d, TM), lambda b, j, seg: (b, 0, j)), tok(d), cvspec, cvspec],
        scratch_shapes=[pltpu.VMEM((2, LROWS, d), F32), pltpu.SemaphoreType.DMA((2,))])
    sds = jax.ShapeDtypeStruct
    out_shape = [sds((bsz, nt, d), F32)] + [sds((bsz, nt, d), BF16)] * 5 + [sds((bsz, d, nt), BF16)] + \
                [sds((bsz, nt, d), BF16)] + [sds((bsz, nblk, 3 * SUBLANES, d), F32)] * 2
    return pl.pallas_call(
        functools.partial(_inproj1_kernel, nctx_blk, bsz), grid_spec=grid_spec, out_shape=out_shape,
        compiler_params=_params(2), name="combine_inproj_hgrn",
    )(seg, tab, tab, aux, ys, x1, mod0, mod1, nw, w_in, lb, d1f, d1b, mf, mb)


def _final_kernel(seg_ref, tab_ref, tab_next_ref, aux_ref, ys_hbm, x_ref, mod_ref, fw_ref, o_ref, buf, sem):
    nblk = pl.num_programs(1)
    step = pl.program_id(0) * nblk + pl.program_id(1)
    ffn = _gather_ffn(step, pl.num_programs(0) * nblk, seg_ref, tab_ref, tab_next_ref, aux_ref, ys_hbm, buf, sem)
    x = x_ref[0]
    d = x.shape[-1]
    mod = mod_ref[pl.ds(pl.program_id(0), 1), :]
    o_ref[0] = _rms(x + mod[:, 5 * d:6 * d] * ffn) * fw_ref[...]


def _final(seg, tab, aux, ys, x3, mod, fw):
    bsz, seq, d = x3.shape
    nblk = seq // TM
    cs = lambda a: pl.BlockSpec(a.shape, lambda b, j, seg: (0,) * a.ndim)
    tok = pl.BlockSpec((1, TM, d), lambda b, j, seg: (b, j, 0))
    grid_spec = pltpu.PrefetchScalarGridSpec(
        num_scalar_prefetch=1, grid=(bsz, nblk),
        in_specs=_route_specs(bsz, nblk) + [pl.BlockSpec(memory_space=pl.ANY), tok, cs(mod), cs(fw)],
        out_specs=tok,
        scratch_shapes=[pltpu.VMEM((2, LROWS, d), F32), pltpu.SemaphoreType.DMA((2,))])
    return pl.pallas_call(
        _final_kernel, grid_spec=grid_spec, out_shape=jax.ShapeDtypeStruct((bsz, seq, d), F32),
        compiler_params=_params(2), name="combine_final_norm",
    )(seg, tab, tab, aux, ys, x3, mod, fw)


def _grid_sincos(n_tokens, dim):
    rows = n_tokens // GRID_W
    r, col = jnp.meshgrid(jnp.arange(rows, dtype=F32), jnp.arange(GRID_W, dtype=F32), indexing='ij')
    quarter = dim // 4
    omega = 1.0 / (10000.0 ** (jnp.arange(quarter, dtype=F32) / quarter))

    def emb(p):
        ang = p.reshape(-1, 1) * omega
        return jnp.concatenate([jnp.sin(ang), jnp.cos(ang)], axis=-1)

    return jnp.concatenate([emb(r), emb(col)], axis=-1)


def _pad_heads(w, n_heads):
    dk = w.shape[-1] // n_heads
    w = w.reshape(w.shape[:-1] + (n_heads, dk))
    w = jnp.pad(w, [(0, 0)] * (w.ndim - 1) + [(0, HEAD_W - dk)])
    return w.reshape(w.shape[:-2] + (n_heads * HEAD_W,))


def _router_tables(router_w, router_bias):
    n_exp = router_w.shape[1]
    epg = n_exp // N_GROUPS
    assert epg == EXPERTS_PER_GROUP
    perm = np.array([epg * g + p for p in range(epg) for g in range(N_GROUPS)])
    wt = jnp.pad(router_w.T[perm], ((0, LANES - n_exp), (0, 0)))
    hi = wt.astype(BF16)
    lo = (wt - hi.astype(F32)).astype(BF16)
    bias = jnp.pad(router_bias[perm], (0, LANES - n_exp))
    return hi, lo, jnp.broadcast_to(bias[:, None], (LANES, TM)).astype(F32)


def _segments(counts, n_tokens):
    cnt = counts[:, 0].astype(I32)
    tiles = (cnt + TM - 1) // TM
    ends = jnp.cumsum(tiles)
    ntile = (n_tokens + (n_tokens // TM) * N_GROUPS * (RUN - 1) + TM - 1) // TM + N_GROUPS
    n_valid = ends[-1]
    seg = jnp.concatenate([(ends - tiles) * TM, (n_valid * TM).reshape(1), jnp.full((1,), ntile * TM, I32)])
    tidx = jnp.minimum(jnp.arange(ntile, dtype=I32), n_valid - 1)
    tile_group = jnp.sum((tidx[:, None] >= ends[None, :]).astype(I32), axis=1)
    cnt = jnp.concatenate([cnt, jnp.zeros((1,), I32)])
    return seg.astype(I32), cnt, tile_group.astype(I32), n_valid.reshape(1).astype(I32), ntile * TM


def _moe_layer(tab, meta, counts, pay, weights):
    seg, cnt, tile_group, n_valid, n_sorted = _segments(counts, pay.shape[0])
    hs = _dispatch(seg, cnt, tab, meta, pay, n_sorted)
    ys = _moe(tile_group, n_valid, hs, *weights)
    return seg, ys


def kernel(x, c, ctx, c_ctx, ada_w, ada_b, norm_mix, norm_ffn, ab_w_in, ab_w_out, gla_a2, gla_ab, gla_norm, s5_lam_re, s5_lam_im, s5_log_dt, s5_b_re, s5_b_im, s5_c_re, s5_c_im, s5_d, s5_glu_w, s5_glu_b, hg_w_in, hg_w_out, hg_lb_logits, hg_norm, router_w, router_bias, moe_w_gate, moe_w_up, moe_w_down, final_norm):
    bsz, seq, d = x.shape
    nctx = ctx.shape[1]
    depth = ada_w.shape[0]
    assert depth == 2 and seq % TM == 0 and nctx % TM == 0 and bsz % SUBLANES == 0 and bsz < MOD_ROWS
    nctx_blk = nctx // TM
    nt = nctx + seq
    nc = nt // S5_L
    assert (nc * bsz) % LANES == 0

    cond = jnp.zeros((MOD_ROWS, d), F32).at[:bsz].set(c).at[bsz].set(c_ctx)
    mod = _modulation(cond, ada_w, ada_b)
    consts = _chunk_matrices()
    row = lambda v: v.reshape(1, -1)

    qk = gla_a2.shape[-1]
    gv = ab_w_out.shape[1] // 2
    rank = gla_a2.shape[2]
    w_in = ab_w_in[0]
    o_v, o_g, o_a, o_u = 2 * qk, 2 * qk + gv, 2 * qk + 2 * gv, 2 * qk + 2 * gv + 2 * rank
    wm = jnp.concatenate([_pad_heads(w_in[:, 0:qk], GLA_HEADS), _pad_heads(w_in[:, qk:o_v], GLA_HEADS),
                          w_in[:, o_v:o_g], w_in[:, o_g:o_a], w_in[:, o_u:]], axis=1).astype(BF16)
    wa = jnp.pad(w_in[:, o_a:o_u], ((0, 0), (0, LANES - 2 * rank))).astype(BF16)
    a2p = _pad_heads(gla_a2[0], GLA_HEADS)
    a2 = jnp.zeros((LANES, 2 * gv), F32).at[0:rank, 0:gv].set(a2p[0]).at[rank:2 * rank, gv:].set(a2p[1]).astype(BF16)
    ab = _pad_heads(gla_ab[0], GLA_HEADS).reshape(1, 2 * gv)
    pos = _grid_sincos(seq, d)
    (xs, qf, kf, qb, kb, v, vt, g, u_cr, cvf, cvb) = _inproj0(x, ctx, pos, mod[0], row(norm_mix[0]), wm, wa, a2, ab, consts,
                                                               float(qk // GLA_HEADS) ** -0.5)
    o_f, o_b = _recurrence(qf, kf, qb, kb, v, vt, cvf, cvb, GLA_HEADS, nctx_blk)

    assert 2 * s5_b_re.shape[-1] == PAIR_W
    ops = _s5_operators(s5_lam_re[0], s5_lam_im[0], s5_log_dt[0], s5_b_re[0], s5_b_im[0], s5_c_re[0], s5_c_im[0], s5_d[0])
    y5 = _s5(u_cr, ops, nc, nctx // S5_L, bsz)

    rwh, rwl, rb = _router_tables(router_w, router_bias)
    utri = jnp.asarray(np.triu(np.ones((TM, TM), np.float32), 1))
    ones = jnp.ones((TM, TM), F32)
    x1, pay, aux, meta, tab, counts = _mixout(True, GLA_HEADS, nctx_blk, 0, xs, o_f, o_b, g, y5, mod[0],
                                              row(gla_norm[0]), s5_glu_w[0].astype(BF16), row(s5_glu_b[0]),
                                              ab_w_out[0].astype(BF16), row(norm_ffn[0]), rwh, rwl, rb, utri, ones)
    assert moe_w_gate.shape[1] == N_GROUPS * EXPERTS_PER_GROUP
    seg, ys = _moe_layer(tab, meta, counts, pay, (0, moe_w_gate, moe_w_up, moe_w_down))

    lb_all = jax.nn.softmax(hg_lb_logits.astype(F32), axis=0)
    lb_all = jnp.cumsum(lb_all, axis=0) - lb_all[0]
    (x2, qf, kf, qb, kb, v, vt, g, cvf, cvb) = _inproj1(seg, tab, aux, ys, x1, mod[0], mod[1], row(norm_mix[1]),
                                                          hg_w_in[0].astype(BF16), row(lb_all[1]), consts, nctx_blk)
    o_f, o_b = _recurrence(qf, kf, qb, kb, v, vt, cvf, cvb, HG_HEADS, nctx_blk)
    x3, pay, aux, meta, tab, counts = _mixout(False, HG_HEADS, nctx_blk, nctx_blk, x2, o_f, o_b, g, None, mod[1],
                                              row(hg_norm[0]), None, None, hg_w_out[0].astype(BF16),
                                              row(norm_ffn[1]), rwh, rwl, rb, utri, ones)
    seg, ys = _moe_layer(tab, meta, counts, pay, (1, moe_w_gate, moe_w_up, moe_w_down))
    return _final(seg, tab, aux, ys, x3, mod[1], row(final_norm))
```

```python
import functools
import math

import numpy as np
import jax
import jax.numpy as jnp
from jax import lax
from jax.experimental import pallas as pl
from jax.experimental.pallas import tpu as pltpu

F32, BF16, I32 = jnp.float32, jnp.bfloat16, jnp.int32

EPS = 1e-6
CHUNK = 64
GRID_W = 64
GLA_HEADS = 4
GLA_GATE_NORM = 16.0
HG_HEADS = 8
N_GROUPS = 8
EXPERTS_PER_GROUP = 4
TOP_K = 2

LANES = 128
SUBLANES = 8
TM = 256
CPT = TM // CHUNK
HEAD_W = 128
S5_L = 16
MOD_ROWS = 16
RUN = SUBLANES
RUN_SHIFT = 3
NSUB = 2
TR = NSUB * TM
LROWS = TR + N_GROUPS * RUN
TAB_START, TAB_COUNT, TAB_CARRY = 0, 1, 2
AUX_LPOS = 4
VMEM_LIMIT = 56 * 1024 * 1024


def _dot(a, b):
    return jnp.dot(a, b, preferred_element_type=F32)


def _dot_nt(a, b):
    return lax.dot_general(a, b, (((1,), (1,)), ((), ())), preferred_element_type=F32)


def _split(x):
    hi = x.astype(BF16)
    lo = (x - hi.astype(F32)).astype(BF16)
    return hi, lo


def _sigmoid(x):
    return 1.0 / (1.0 + jnp.exp(-x))


def _silu(x):
    return x * _sigmoid(x)


def _gelu_tanh(x):
    return 0.5 * x * (1.0 + jnp.tanh(math.sqrt(2.0 / math.pi) * (x + 0.044715 * (x * x * x))))


def _rms(x):
    return x * lax.rsqrt(jnp.mean(x * x, axis=-1, keepdims=True) + EPS)


def _head_rms(o, gain, n_heads):
    outs = []
    for h in range(n_heads):
        oh = o[:, h * HEAD_W:(h + 1) * HEAD_W]
        outs.append(_rms(oh) * gain)
    return jnp.concatenate(outs, axis=1)


def _pairs(a):
    return a.reshape((a.shape[0] // NSUB, NSUB) + a.shape[1:])


def _unpairs(a):
    return a.reshape((a.shape[0] * NSUB,) + a.shape[2:])


def _mod_row(mod_ref, is_ctx, ctx_row, batch_row):
    return mod_ref[pl.ds(jnp.where(is_ctx, ctx_row, batch_row), 1), :]


def _params(n_grid_dims, vmem=VMEM_LIMIT):
    return pltpu.CompilerParams(dimension_semantics=("arbitrary",) * n_grid_dims, vmem_limit_bytes=vmem)


def _const_spec(shape):
    nd = len(shape)
    return pl.BlockSpec(shape, lambda *_: (0,) * nd)


def _mod_kernel(c_ref, w_ref, b_ref, o_ref):
    s = _silu(c_ref[...])
    shi, slo = _split(s)
    whi, wlo = _split(w_ref[0])
    o_ref[0] = _dot(shi, whi) + _dot(shi, wlo) + _dot(slo, whi) + b_ref[0]


def _modulation(cond, ada_w, ada_b):
    depth, d, six_d = ada_w.shape
    nchunk = six_d // d
    return pl.pallas_call(
        _mod_kernel,
        grid=(depth, nchunk),
        in_specs=[
            _const_spec((MOD_ROWS, d)),
            pl.BlockSpec((1, d, d), lambda l, n: (l, 0, n)),
            pl.BlockSpec((1, 1, d), lambda l, n: (l, 0, n)),
        ],
        out_specs=pl.BlockSpec((1, MOD_ROWS, d), lambda l, n: (l, 0, n)),
        out_shape=jax.ShapeDtypeStruct((depth, MOD_ROWS, six_d), F32),
        compiler_params=_params(2),
        name="modulation",
    )(cond, ada_w, ada_b.reshape(depth, 1, six_d))


def _chunk_matrices():
    t = np.arange(TM)
    c, l = t // CHUNK, t % CHUNK
    same = (c[:, None] == c[None, :]).astype(np.float32)
    li, ls = l[:, None], l[None, :]
    mid_f = CHUNK // 2 - 1
    mid_b = CHUNK // 2
    d1f = same * ((ls <= li).astype(np.float32) - (ls <= mid_f).astype(np.float32))
    d1b = same * ((ls >= li).astype(np.float32) - (ls >= mid_b).astype(np.float32))
    inchunk = (np.arange(SUBLANES)[:, None] == c[None, :]).astype(np.float32)
    mf = np.concatenate([inchunk * (l <= mid_f), inchunk * (l > mid_f), inchunk])
    mb = np.concatenate([inchunk * (l >= mid_b), inchunk * (l < mid_b), inchunk])
    as_bf16 = lambda a: jnp.asarray(a, dtype=BF16)
    return as_bf16(d1f), as_bf16(d1b), as_bf16(mf), as_bf16(mb)


def _gate_outputs(q, k, la, d1_ref, m_ref, q_out, k_out, cv_out):
    hi, lo = _split(la)
    d1m = d1_ref[...]
    d1 = _dot(d1m, hi) + _dot(d1m, lo)
    q_out[...] = (q * jnp.exp(d1)).astype(BF16)
    k_out[...] = (k * jnp.exp(-d1)).astype(BF16)
    mm = m_ref[...]
    cv_out[...] = jnp.exp(_dot(mm, hi) + _dot(mm, lo))


PAIR_W = 32
PAIRS_PER_TILE = LANES // PAIR_W
CROWS = TM // S5_L


def _to_chunk_rows(u, us_ref, ucr_ref):
    for q in range(us_ref.shape[0]):
        us_ref[q] = u[:, q * LANES:(q + 1) * LANES]
    lane = lax.broadcasted_iota(I32, (CROWS, LANES), 1)
    for p in range(ucr_ref.shape[0]):
        q, pp = divmod(p, PAIRS_PER_TILE)
        tiles = []
        for t in range(S5_L // PAIRS_PER_TILE):
            acc = None
            for jj in range(PAIRS_PER_TILE):
                piece = us_ref[q, pl.ds(t * PAIRS_PER_TILE + jj, CROWS, stride=S5_L), :]
                shift = (PAIR_W * (jj - pp)) % LANES
                if shift:
                    piece = pltpu.roll(piece, shift, axis=1)
                here = (lane >= PAIR_W * jj) & (lane < PAIR_W * (jj + 1))
                acc = piece if acc is None else jnp.where(here, piece, acc)
            tiles.append(acc)
        ucr_ref[p] = jnp.concatenate(tiles, axis=1).astype(BF16)


def _from_chunk_rows(ycr_ref, ys_ref):
    lane = lax.broadcasted_iota(I32, (CROWS, LANES), 1)
    for q in range(ycr_ref.shape[0] // PAIRS_PER_TILE):
        for i in range(S5_L):
            t, ii = divmod(i, PAIRS_PER_TILE)
            acc = None
            for pp in range(PAIRS_PER_TILE):
                piece = ycr_ref[q * PAIRS_PER_TILE + pp, :, t * LANES:(t + 1) * LANES].astype(F32)
                shift = (PAIR_W * (pp - ii)) % LANES
                if shift:
                    piece = pltpu.roll(piece, shift, axis=1)
                here = (lane >= PAIR_W * pp) & (lane < PAIR_W * (pp + 1))
                acc = piece if acc is None else jnp.where(here, piece, acc)
            ys_ref[q, pl.ds(i, CROWS, stride=S5_L), :] = acc
    return jnp.concatenate([ys_ref[q] for q in range(ys_ref.shape[0])], axis=1)


def _inproj0_kernel(nctx_blk, nb, gla_w, q_scale, x_ref, ctx_ref, pos_ref, mod_ref, nw_ref, wm_ref, wa_ref, a2_ref,
                    ab_ref, d1f_ref, d1b_ref, mf_ref, mb_ref,
                    xs_ref, qf_ref, kf_ref, qb_ref, kb_ref, v_ref, vt_ref, g_ref, ucr_ref, cvf_ref, cvb_ref, us_ref):
    bp, j = pl.program_id(0), pl.program_id(1)
    is_ctx = j < nctx_blk
    xin = jnp.where(is_ctx, ctx_ref[0], x_ref[0] + pos_ref[...])
    xs_ref[0] = xin
    d = xin.shape[-1]
    hb = []
    for s in range(NSUB):
        mod = _mod_row(mod_ref, is_ctx, nb, bp * NSUB + s)
        hb.append((_rms(xin[s]) * nw_ref[...] * (1.0 + mod[:, d:2 * d]) + mod[:, 0:d]).astype(BF16))
    hb = jnp.concatenate(hb, axis=0)
    w = gla_w
    y = _dot(hb, wm_ref[...])
    a = _dot(hb, wa_ref[...])
    z = _dot(a.astype(BF16), a2_ref[...]) + ab_ref[...]
    la = (jnp.minimum(z, 0.0) - jnp.log(1.0 + jnp.exp(-jnp.abs(z)))) * (1.0 / GLA_GATE_NORM)
    for s in range(NSUB):
        rows = slice(s * TM, (s + 1) * TM)
        q = y[rows, 0:w] * q_scale
        k = y[rows, w:2 * w]
        v = y[rows, 2 * w:3 * w]
        g_ref[0, s] = y[rows, 3 * w:4 * w].astype(BF16)
        _to_chunk_rows(y[rows, 4 * w:5 * w], us_ref, ucr_ref.at[:, 0, s, 0])
        v_ref[0, s] = v.astype(BF16)
        vt_ref[0, s] = v.T.astype(BF16)
        _gate_outputs(q, k, la[rows, 0:w], d1f_ref, mf_ref, qf_ref.at[0, s], kf_ref.at[0, s], cvf_ref.at[0, s, 0])
        _gate_outputs(q, k, la[rows, w:2 * w], d1b_ref, mb_ref, qb_ref.at[0, s], kb_ref.at[0, s], cvb_ref.at[0, s, 0])


def _inproj0(x, ctx, pos, mod, nw, wm, wa, a2, ab, consts, q_scale):
    bsz, seq, d = x.shape
    nctx = ctx.shape[1]
    nctx_blk, nlat_blk = nctx // TM, seq // TM
    nblk = nctx_blk + nlat_blk
    nt = nblk * TM
    nbp = bsz // NSUB
    w = wm.shape[1] // 5
    d1f, d1b, mf, mb = consts
    in_specs = [
        pl.BlockSpec((1, NSUB, TM, d), lambda b, j: (b, 0, jnp.maximum(j - nctx_blk, 0), 0)),
        pl.BlockSpec((1, NSUB, TM, d), lambda b, j: (b, 0, jnp.minimum(j, nctx_blk - 1), 0)),
        pl.BlockSpec((TM, d), lambda b, j: (jnp.maximum(j - nctx_blk, 0), 0)),
        _const_spec(mod.shape), _const_spec(nw.shape), _const_spec(wm.shape), _const_spec(wa.shape),
        _const_spec(a2.shape), _const_spec(ab.shape), _const_spec(d1f.shape), _const_spec(d1b.shape),
        _const_spec(mf.shape), _const_spec(mb.shape),
    ]
    tokspec = lambda width: pl.BlockSpec((1, NSUB, TM, width), lambda b, j: (b, 0, j, 0))
    cvspec = pl.BlockSpec((1, NSUB, 1, 3 * SUBLANES, w), lambda b, j: (b, 0, j, 0, 0))
    npair, crow_w = w // PAIR_W, S5_L * PAIR_W
    out_specs = [tokspec(d), tokspec(w), tokspec(w), tokspec(w), tokspec(w), tokspec(w),
                 pl.BlockSpec((1, NSUB, w, TM), lambda b, j: (b, 0, 0, j)), tokspec(w),
                 pl.BlockSpec((npair, 1, NSUB, 1, CROWS, crow_w), lambda b, j: (0, b, 0, j, 0, 0)), cvspec, cvspec]
    sds = jax.ShapeDtypeStruct
    out_shape = [sds((nbp, NSUB, nt, d), F32)] + [sds((nbp, NSUB, nt, w), BF16)] * 5 + [sds((nbp, NSUB, w, nt), BF16)] + \
                [sds((nbp, NSUB, nt, w), BF16), sds((npair, nbp, NSUB, nblk, CROWS, crow_w), BF16)] + \
                [sds((nbp, NSUB, nblk, 3 * SUBLANES, w), F32)] * 2
    outs = pl.pallas_call(
        functools.partial(_inproj0_kernel, nctx_blk, bsz, w, q_scale),
        grid=(nbp, nblk), in_specs=in_specs, out_specs=out_specs, out_shape=out_shape,
        scratch_shapes=[pltpu.VMEM((w // LANES, TM, LANES), F32)],
        compiler_params=_params(2), name="inproj_gla_s5",
    )(_pairs(x), _pairs(ctx), pos, mod, nw, wm, wa, a2, ab, d1f, d1b, mf, mb)
    ucr = outs[8].reshape(npair, bsz * nblk * CROWS, crow_w)
    return [_unpairs(o) for o in outs[:8]] + [ucr] + [_unpairs(o) for o in outs[9:]]


def _recur_kernel(n_heads, qf_ref, kf_ref, vf_ref, vtf_ref, cvf_ref, qb_ref, kb_ref, vb_ref, vtb_ref, cvb_ref,
                  of_ref, ob_ref, s_ref):
    @pl.when(pl.program_id(1) == 0)
    def _():
        s_ref[...] = jnp.zeros_like(s_ref)

    ri = lax.broadcasted_iota(I32, (CHUNK, CHUNK), 0)
    ci = lax.broadcasted_iota(I32, (CHUNK, CHUNK), 1)
    causal = (ci <= ri, ci >= ri)
    prow = lax.broadcasted_iota(I32, (2 * CHUNK, HEAD_W), 0)
    dirs = ((qf_ref, kf_ref, vf_ref, vtf_ref, cvf_ref, of_ref), (qb_ref, kb_ref, vb_ref, vtb_ref, cvb_ref, ob_ref))
    for d, (q_ref, k_ref, v_ref, vt_ref, cv_ref, o_ref) in enumerate(dirs):
        for cc in range(CPT):
            c = cc if d == 0 else CPT - 1 - cc
            r0 = c * CHUNK
            p0 = (c // 2) * 2 * CHUNK
            in_chunk = (prow >= r0 - p0) & (prow < r0 - p0 + CHUNK)
            e_mid = cv_ref[0, 0, c:c + 1, :]
            e_rest = cv_ref[0, 0, SUBLANES + c:SUBLANES + c + 1, :]
            e_all = cv_ref[0, 0, 2 * SUBLANES + c:2 * SUBLANES + c + 1, :]
            for h in range(n_heads):
                hs = slice(h * HEAD_W, (h + 1) * HEAD_W)
                qi = q_ref[0, r0:r0 + CHUNK, hs]
                ki = k_ref[0, r0:r0 + CHUNK, hs]
                vv = v_ref[0, r0:r0 + CHUNK, hs]
                kp = k_ref[0, p0:p0 + 2 * CHUNK, hs]
                kp = jnp.where(in_chunk, kp, jnp.zeros_like(kp))
                vtp = vt_ref[0, hs, p0:p0 + 2 * CHUNK]
                st = s_ref[d, h]
                attn = jnp.where(causal[d], _dot_nt(qi, ki), 0.0).astype(BF16)
                o = _dot(attn, vv) + _dot_nt(qi, (st * e_mid[:, hs]).astype(BF16))
                o_ref[0, r0:r0 + CHUNK, hs] = o.astype(o_ref.dtype)
                s_ref[d, h] = st * e_all[:, hs] + _dot(vtp, kp) * e_rest[:, hs]


def _recurrence(qf, kf, qb, kb, v, vt, cvf, cvb, n_heads, nctx_blk):
    bsz, nt, w = qf.shape
    nblk = nt // TM
    fwd = lambda j: j
    bwd = lambda j: jnp.where(j < nctx_blk, nctx_blk - 1 - j, nblk - 1 - (j - nctx_blk))
    specs = []
    for order in (fwd, bwd):
        tok = pl.BlockSpec((1, TM, w), lambda b, j, o=order: (b, o(j), 0))
        specs += [tok, tok, tok,
                  pl.BlockSpec((1, w, TM), lambda b, j, o=order: (b, 0, o(j))),
                  pl.BlockSpec((1, 1, 3 * SUBLANES, w), lambda b, j, o=order: (b, o(j), 0, 0))]
    out_specs = [pl.BlockSpec((1, TM, w), lambda b, j: (b, j, 0)),
                 pl.BlockSpec((1, TM, w), lambda b, j: (b, bwd(j), 0))]
    return pl.pallas_call(
        functools.partial(_recur_kernel, n_heads),
        grid=(bsz, nblk), in_specs=specs, out_specs=out_specs,
        out_shape=[jax.ShapeDtypeStruct((bsz, nt, w), BF16)] * 2,
        scratch_shapes=[pltpu.VMEM((2, n_heads, HEAD_W, HEAD_W), F32)],
        compiler_params=_params(2), name=f"recurrence_h{n_heads}",
    )(qf, kf, v, vt, cvf, qb, kb, v, vt, cvb)


def _s5_operators(lam_re, lam_im, log_dt, b_re, b_im, c_re, c_im, d_skip):
    L = S5_L
    ndir, ng, p = lam_re.shape
    hs = b_re.shape[-1]
    dt = jnp.exp(log_dt)[..., None]
    lr, li = lam_re * dt, lam_im * dt
    mag = jnp.exp(lr)
    a_re, a_im = mag * jnp.cos(li), mag * jnp.sin(li)
    den = lam_re * lam_re + lam_im * lam_im
    f_re = ((a_re - 1.0) * lam_re + a_im * lam_im) / den
    f_im = (a_im * lam_re - (a_re - 1.0) * lam_im) / den
    bb_re = f_re[..., None] * b_re - f_im[..., None] * b_im
    bb_im = f_re[..., None] * b_im + f_im[..., None] * b_re
    m = jnp.arange(L + 1, dtype=F32)[:, None, None, None]
    pw_mag = jnp.exp(m * lr[None])
    pw_re, pw_im = pw_mag * jnp.cos(m * li[None]), pw_mag * jnp.sin(m * li[None])
    eye = jnp.eye(2, dtype=F32)
    npair, pw, cw = ng // 2, 2 * hs, 2 * L * hs

    toep = jnp.zeros((npair, cw, cw), F32)
    wz_p, vo_p = [], []
    for di in range(ndir):
        pr, pi = pw_re[:L, di, :, :, None], pw_im[:L, di, :, :, None]
        ab_re = pr * bb_re[di][None] - pi * bb_im[di][None]
        ab_im = pr * bb_im[di][None] + pi * bb_re[di][None]
        kf = jnp.einsum('mgnk,ghn->gkmh', ab_re, c_re[di]) - jnp.einsum('mgnk,ghn->gkmh', ab_im, c_im[di])
        if di == 1:
            kf = kf[:, :, ::-1, :]
        kp = jnp.einsum('pskmh,st->pskmth', kf.reshape(npair, 2, hs, L, hs), eye).reshape(npair, pw, cw)
        if di == 0:
            kpad = jnp.pad(kp, ((0, 0), (0, 0), (cw, 0)))
            blocks = [kpad[:, :, cw - pw * j:2 * cw - pw * j] for j in range(L)]
        else:
            kpad = jnp.pad(kp, ((0, 0), (0, 0), (0, cw)))
            blocks = [kpad[:, :, pw * (L - 1 - j):pw * (L - 1 - j) + cw] for j in range(L)]
        toep = toep + jnp.stack(blocks, axis=1).reshape(npair, cw, cw)
        sel_i = (lambda a: a[::-1]) if di == 0 else (lambda a: a)
        pack_w = lambda a: jnp.einsum('jpsnk,st->pjsktn', sel_i(a).reshape(L, npair, 2, p, hs), eye).reshape(npair, cw, 2 * p)
        wz_p.append(jnp.concatenate([pack_w(ab_re), pack_w(ab_im)], axis=2))
        qr, qi = pw_re[1:, di, :, None, :], pw_im[1:, di, :, None, :]
        ca_re = c_re[di][None] * qr - c_im[di][None] * qi
        ca_im = c_re[di][None] * qi + c_im[di][None] * qr
        sel_o = (lambda a: a) if di == 0 else (lambda a: a[::-1])
        pack_v = lambda a: jnp.einsum('ipshn,st->psnith', sel_o(a).reshape(L, npair, 2, hs, p), eye).reshape(npair, 2 * p, cw)
        vo_p.append(jnp.concatenate([pack_v(ca_re), -pack_v(ca_im)], axis=1))
    wz_p, vo_p = jnp.stack(wz_p, axis=1), jnp.stack(vo_p, axis=1)
    al = jnp.stack([pw_re[L], pw_im[L]], axis=1)
    al_p = jnp.transpose(al.reshape(ndir, 2, npair, 2 * p), (2, 0, 1, 3))
    d_p = jnp.tile(d_skip.reshape(npair, 1, pw), (1, L, 1)).reshape(npair, 1, cw)
    return wz_p.astype(BF16), vo_p.astype(BF16), toep.astype(BF16), al_p, d_p


def _s5_kernel(nc, ncc, bsz, u_ref, wz_ref, vo_ref, tp_ref, al_ref, d_ref, y_ref, zf, zb, xf, xb):
    rows_total = nc * bsz
    rb = LANES
    half = zf.shape[2]

    def z_body(i, carry):
        rows = pl.ds(pl.multiple_of(i * rb, rb), rb)
        ub = u_ref[0, rows, :]
        for z, di in ((zf, 0), (zb, 1)):
            zz = _dot(ub, wz_ref[0, di])
            z[0, rows, :] = zz[:, 0:half]
            z[1, rows, :] = zz[:, half:2 * half]
        return carry

    lax.fori_loop(0, rows_total // rb, z_body, 0)

    def scan(z, xs, di, order):
        a_re = jnp.broadcast_to(al_ref[0, di, 0:1, :], (bsz, half))
        a_im = jnp.broadcast_to(al_ref[0, di, 1:2, :], (bsz, half))

        def body(i, carry):
            x_re, x_im = carry
            rows = pl.ds(order(i), bsz, stride=nc)
            xs[0, rows, :] = x_re
            xs[1, rows, :] = x_im
            return (a_re * x_re - a_im * x_im + z[0, rows, :], a_re * x_im + a_im * x_re + z[1, rows, :])

        zero = jnp.zeros((bsz, half), F32)
        lax.fori_loop(0, nc, body, (zero, zero))

    scan(zf, xf, 0, lambda i: i)
    scan(zb, xb, 1, lambda i: jnp.where(i < ncc, ncc - 1 - i, nc - 1 - (i - ncc)))

    def y_body(i, carry):
        rows = pl.ds(pl.multiple_of(i * rb, rb), rb)
        ub = u_ref[0, rows, :]
        state = lambda xs: jnp.concatenate([xs[0, rows, :], xs[1, rows, :]], axis=1).astype(BF16)
        y = _dot(state(xf), vo_ref[0, 0]) + _dot(state(xb), vo_ref[0, 1])
        y_ref[0, rows, :] = (y + _dot(ub, tp_ref[0]) + ub.astype(F32) * d_ref[0]).astype(y_ref.dtype)
        return carry

    lax.fori_loop(0, rows_total // rb, y_body, 0)


def _s5(u_cr, ops, nc, ncc, bsz):
    wz_p, vo_p, toep_p, al_p, d_p = ops
    npair, rows, width = u_cr.shape
    per_pair = lambda a: pl.BlockSpec((1,) + a.shape[1:], lambda p: (p,) + (0,) * (a.ndim - 1))
    return pl.pallas_call(
        functools.partial(_s5_kernel, nc, ncc, bsz),
        grid=(npair,),
        in_specs=[per_pair(u_cr), per_pair(wz_p), per_pair(vo_p), per_pair(toep_p), per_pair(al_p), per_pair(d_p)],
        out_specs=per_pair(u_cr),
        out_shape=jax.ShapeDtypeStruct(u_cr.shape, BF16),
        scratch_shapes=[pltpu.VMEM((2, rows, wz_p.shape[-1] // 2), F32)] * 4,
        compiler_params=_params(1), name="s5",
    )(u_cr, wz_p, vo_p, toep_p, al_p, d_p)


def _route(h2, rwh_ref, rwl_ref, rb_ref, utri_ref, ones_ref, carry_ref):
    hi, lo = _split(h2)
    logits = _dot_nt(rwh_ref[...], hi) + _dot_nt(rwl_ref[...], hi) + _dot_nt(rwh_ref[...], lo)
    aff = _sigmoid(logits)
    sel = aff + rb_ref[...]
    epg = EXPERTS_PER_GROUP
    s = [sel[N_GROUPS * p:N_GROUPS * (p + 1), :] for p in range(epg)]
    a = [aff[N_GROUPS * p:N_GROUPS * (p + 1), :] for p in range(epg)]
    m1, n1 = jnp.maximum(s[0], s[1]), jnp.minimum(s[0], s[1])
    m2, n2 = jnp.maximum(s[2], s[3]), jnp.minimum(s[2], s[3])
    score = jnp.maximum(m1, m2) + jnp.maximum(jnp.minimum(m1, m2), jnp.maximum(n1, n2))
    gi = lax.broadcasted_iota(I32, score.shape, 0)
    best = jnp.max(score, axis=0, keepdims=True)
    gidx = jnp.min(jnp.where(score == best, gi, N_GROUPS), axis=0, keepdims=True)
    onehot = gi == gidx
    gates = []
    for p in range(epg):
        ahead = jnp.zeros_like(score)
        for q in range(epg):
            if q != p:
                beats = (s[q] >= s[p]) if q < p else (s[q] > s[p])
                ahead = ahead + jnp.where(beats, 1.0, 0.0)
        picked = jnp.where(onehot, jnp.where(ahead < float(TOP_K) - 0.5, a[p], 0.0), 0.0)
        gates.append(jnp.sum(picked, axis=0, keepdims=True))
    den = gates[0] + gates[1] + gates[2] + gates[3]
    gates = [g / den for g in gates]
    oh = jnp.where(onehot, 1.0, 0.0)
    before = _dot(oh, utri_ref[...])
    count = _dot(oh, ones_ref[...])
    run = jnp.floor((count + (RUN - 1.0)) * (1.0 / RUN)) * RUN
    carry = carry_ref[...]
    carry_ref[...] = carry + run
    starts, nxt = [], jnp.zeros_like(run[0:1])
    for g in range(N_GROUPS):
        starts.append(nxt)
        nxt = nxt + run[g:g + 1]
    lstart = jnp.concatenate(starts, axis=0)
    lpos = jnp.sum(jnp.where(onehot, lstart + before, 0.0), axis=0, keepdims=True)
    return lpos, gates, (lstart, count, carry)


def _mixout_kernel(has_s5, n_heads, nctx_blk, tok_off, nb, *refs):
    if has_s5:
        (x_ref, of_ref, ob_ref, g_ref, ycr_ref, mod_ref, gn_ref, gluw_ref, glub_ref, wo_ref, nf_ref,
         rwh_ref, rwl_ref, rb_ref, utri_ref, ones_ref,
         x1_ref, pay_ref, aux_ref, meta_ref, tab_ref, cnt_ref, carry_ref, ys_ref) = refs
    else:
        (x_ref, of_ref, ob_ref, g_ref, mod_ref, gn_ref, wo_ref, nf_ref,
         rwh_ref, rwl_ref, rb_ref, utri_ref, ones_ref,
         x1_ref, pay_ref, aux_ref, meta_ref, tab_ref, cnt_ref, carry_ref) = refs
    bp, j = pl.program_id(0), pl.program_id(1)

    @pl.when((bp == 0) & (j == 0))
    def _():
        carry_ref[...] = jnp.zeros_like(carry_ref)

    d = x_ref.shape[-1]
    merge = lambda ref: jnp.concatenate([ref[0, s] for s in range(NSUB)], axis=0)
    o = merge(of_ref).astype(F32) + merge(ob_ref).astype(F32)
    mixed = _head_rms(o, gn_ref[...], n_heads) * _silu(merge(g_ref).astype(F32))
    if has_s5:
        y5 = jnp.concatenate([_from_chunk_rows(ycr_ref.at[:, 0, s, 0], ys_ref) for s in range(NSUB)], axis=0)
        act = _gelu_tanh(y5)
        glu = act * _sigmoid(_dot(act.astype(BF16), gluw_ref[...]) + glub_ref[...])
        mixed = jnp.concatenate([mixed, glu], axis=1)
    proj = _dot(mixed.astype(BF16), wo_ref[...])
    h2 = []
    for s in range(NSUB):
        mod = _mod_row(mod_ref, j + tok_off < nctx_blk, nb, bp * NSUB + s)
        x1 = x_ref[0, s] + mod[:, 2 * d:3 * d] * proj[s * TM:(s + 1) * TM]
        x1_ref[0, s] = x1
        h2.append(_rms(x1) * nf_ref[...] * (1.0 + mod[:, 4 * d:5 * d]) + mod[:, 3 * d:4 * d])
    h2 = jnp.concatenate(h2, axis=0)

    lpos, gates, (lstart, count, carry) = _route(h2, rwh_ref, rwl_ref, rb_ref, utri_ref, ones_ref, carry_ref)
    gi = lax.broadcasted_iota(I32, (SUBLANES, TR), 0)
    meta_ref[0] = jnp.where(gi == 0, lpos.astype(I32), 0)
    ti = lax.broadcasted_iota(I32, (N_GROUPS, LANES), 1)
    first = lambda a: a[:, 0:LANES].astype(I32)
    tab_ref[0] = jnp.where(ti == TAB_START, first(lstart),
                           jnp.where(ti == TAB_COUNT, first(count), jnp.where(ti == TAB_CARRY, first(carry), 0)))
    cnt_ref[...] = carry_ref[...]

    li = lax.broadcasted_iota(I32, (LANES, TR), 0)
    gt = jnp.where(li == AUX_LPOS, lpos, 0.0)
    for p, g in enumerate(gates):
        gt = gt + jnp.where(li == p, g, 0.0)
    aux = gt.T
    pay_ref[:, 0:d] = h2
    pay_ref[:, d:d + LANES] = aux
    aux_ref[...] = aux


def _mixout(has_s5, n_heads, nctx_blk, tok_off, x, of, ob, g, y5, mod, gn, gluw, glub, wo, nf, rwh, rwl, rb, utri, ones):
    bsz, _, d = x.shape
    w = of.shape[-1]
    nblk_all = of.shape[1] // TM
    nblk = nblk_all - tok_off
    nbp = bsz // NSUB
    tok = lambda width: pl.BlockSpec((1, NSUB, TM, width), lambda b, j: (b, 0, j + tok_off, 0))
    ins, specs = [_pairs(x), _pairs(of), _pairs(ob), _pairs(g)], [tok(d), tok(w), tok(w), tok(w)]
    scratch = [pltpu.VMEM((N_GROUPS, TR), F32)]
    if has_s5:
        npair, _, crow_w = y5.shape
        ins.append(y5.reshape(npair, nbp, NSUB, nblk_all, CROWS, crow_w))
        specs.append(pl.BlockSpec((npair, 1, NSUB, 1, CROWS, crow_w), lambda b, j: (0, b, 0, j, 0, 0)))
        scratch.append(pltpu.VMEM((w // LANES, TM, LANES), F32))
    consts = [mod, gn] + ([gluw, glub] if has_s5 else []) + [wo, nf, rwh, rwl, rb, utri, ones]
    ins += consts
    specs += [_const_spec(c.shape) for c in consts]
    ntile = nbp * nblk
    pay_w = d + LANES
    lin = lambda b, j: b * nblk + j
    out_specs = [pl.BlockSpec((1, NSUB, TM, d), lambda b, j: (b, 0, j, 0)),
                 pl.BlockSpec((TR, pay_w), lambda b, j: (lin(b, j), 0)),
                 pl.BlockSpec((TR, LANES), lambda b, j: (lin(b, j), 0)),
                 pl.BlockSpec((1, SUBLANES, TR), lambda b, j: (lin(b, j), 0, 0)),
                 pl.BlockSpec((1, N_GROUPS, LANES), lambda b, j: (lin(b, j), 0, 0)),
                 _const_spec((N_GROUPS, TR))]
    sds = jax.ShapeDtypeStruct
    out_shape = [sds((nbp, NSUB, nblk * TM, d), F32), sds((ntile * TR, pay_w), F32), sds((ntile * TR, LANES), F32),
                 sds((ntile, SUBLANES, TR), I32), sds((ntile, N_GROUPS, LANES), I32), sds((N_GROUPS, TR), F32)]
    outs = pl.pallas_call(
        functools.partial(_mixout_kernel, has_s5, n_heads, nctx_blk, tok_off, bsz),
        grid=(nbp, nblk), in_specs=specs, out_specs=out_specs, out_shape=out_shape,
        scratch_shapes=scratch,
        compiler_params=_params(2), name="mixout_s5" if has_s5 else "mixout",
    )(*ins)
    return [_unpairs(outs[0])] + list(outs[1:])


def _for_each_run(seg_ref, tab_ref, fn):
    for g in range(N_GROUPS):
        lstart = tab_ref[0, g, TAB_START]
        first = seg_ref[g] + tab_ref[0, g, TAB_CARRY]
        n_pieces = lax.shift_right_logical(tab_ref[0, g, TAB_COUNT] + (RUN - 1), RUN_SHIFT)

        def piece(k, carry, lstart=lstart, first=first):
            fn(pl.ds(pl.multiple_of(lstart + k * RUN, RUN), RUN), pl.ds(pl.multiple_of(first + k * RUN, RUN), RUN))
            return carry

        lax.fori_loop(0, n_pieces, piece, 0)


def _dispatch_kernel(seg_ref, cnt_ref, tab_ref, meta_ref, pay_ref, hs_hbm, s_ref, z_ref, sem):
    d = pay_ref.shape[1] - LANES
    lpos = meta_ref[0, 0:1, :]
    rows = lax.broadcasted_iota(I32, (LROWS, TR), 0)
    pick = jnp.where(rows == lpos, 1.0, 0.0).astype(BF16)
    s_ref[:, 0:d] = _dot(pick, pay_ref[:, 0:d].astype(BF16))
    ghi, glo = _split(pay_ref[:, d:d + LANES])
    s_ref[:, d:d + LANES] = _dot(pick, ghi) + _dot(pick, glo)

    copy = lambda src, dst: pltpu.make_async_copy(s_ref.at[src], hs_hbm.at[dst], sem)
    _for_each_run(seg_ref, tab_ref, lambda src, dst: copy(src, dst).start())
    _for_each_run(seg_ref, tab_ref, lambda src, dst: copy(src, dst).wait())

    @pl.when(pl.program_id(0) == pl.num_programs(0) - 1)
    def _():
        z_ref[...] = jnp.zeros_like(z_ref)
        for op in ("start", "wait"):
            for g in range(N_GROUPS + 1):
                begin = seg_ref[g] + cnt_ref[g]

                def piece(k, carry, begin=begin):
                    dst = hs_hbm.at[pl.ds(pl.multiple_of(begin + k * RUN, RUN), RUN)]
                    getattr(pltpu.make_async_copy(z_ref, dst, sem), op)()
                    return carry

                lax.fori_loop(0, lax.shift_right_logical(seg_ref[g + 1] - begin, RUN_SHIFT), piece, 0)


def _dispatch(seg, cnt, tab, meta, pay, n_sorted):
    ntile = meta.shape[0]
    width = pay.shape[1]
    grid_spec = pltpu.PrefetchScalarGridSpec(
        num_scalar_prefetch=2, grid=(ntile,),
        in_specs=[pl.BlockSpec((1, N_GROUPS, LANES), lambda i, seg, cnt: (i, 0, 0), memory_space=pltpu.SMEM),
                  pl.BlockSpec((1, SUBLANES, TR), lambda i, seg, cnt: (i, 0, 0)),
                  pl.BlockSpec((TR, width), lambda i, seg, cnt: (i, 0))],
        out_specs=pl.BlockSpec(memory_space=pl.ANY),
        scratch_shapes=[pltpu.VMEM((LROWS, width), F32), pltpu.VMEM((RUN, width), F32), pltpu.SemaphoreType.DMA(())])
    return pl.pallas_call(
        _dispatch_kernel, grid_spec=grid_spec,
        out_shape=jax.ShapeDtypeStruct((n_sorted, width), F32),
        compiler_params=_params(1), name="moe_dispatch",
    )(seg, cnt, tab, meta, pay)


def _moe_kernel(tg_ref, nv_ref, hs_ref, wg_ref, wu_ref, wd_ref, ys_ref, wgb, wub, wdb):
    i = pl.program_id(0)
    valid = i < nv_ref[0]

    @pl.when(valid & ((i == 0) | (tg_ref[i] != tg_ref[jnp.maximum(i - 1, 0)])))
    def _():
        wgb[...] = wg_ref[...].astype(BF16)
        wub[...] = wu_ref[...].astype(BF16)
        wdb[...] = wd_ref[...].astype(BF16)

    @pl.when(valid)
    def _():
        d = wg_ref.shape[1]
        h = hs_ref[:, 0:d].astype(BF16)
        gate = hs_ref[:, d:d + LANES]
        acc = jnp.zeros((TR, d), F32)
        for p in range(EXPERTS_PER_GROUP):
            act = _silu(_dot(h, wgb[p])) * _dot(h, wub[p]) * gate[:, p:p + 1]
            acc = acc + _dot(act.astype(BF16), wdb[p])
        ys_ref[...] = acc

    @pl.when(jnp.logical_not(valid))
    def _():
        ys_ref[...] = jnp.zeros_like(ys_ref)


def _moe(tile_group, n_valid, hs, layer, wg, wu, wd):
    n_sorted, pay_w = hs.shape
    ntile = n_sorted // TR
    d = wg.shape[2]
    epg = EXPERTS_PER_GROUP
    clamp = lambda i, tg, nv: jnp.minimum(i, nv[0] - 1)
    wspec = lambda a: pl.BlockSpec((None, epg) + a.shape[2:], lambda i, tg, nv: (layer, tg[i], 0, 0))
    grid_spec = pltpu.PrefetchScalarGridSpec(
        num_scalar_prefetch=2, grid=(ntile,),
        in_specs=[pl.BlockSpec((TR, pay_w), lambda i, tg, nv: (clamp(i, tg, nv), 0)), wspec(wg), wspec(wu), wspec(wd)],
        out_specs=pl.BlockSpec((TR, d), lambda i, tg, nv: (i, 0)),
        scratch_shapes=[pltpu.VMEM((epg,) + wg.shape[2:], BF16), pltpu.VMEM((epg,) + wu.shape[2:], BF16),
                        pltpu.VMEM((epg,) + wd.shape[2:], BF16)])
    return pl.pallas_call(
        _moe_kernel, grid_spec=grid_spec,
        out_shape=jax.ShapeDtypeStruct((n_sorted, d), F32),
        compiler_params=_params(1), name="moe_experts",
    )(tile_group, n_valid, hs, wg, wu, wd)


def _route_specs(bsz, nblk):
    last = bsz * nblk - 1
    cur = pl.BlockSpec((1, N_GROUPS, LANES), lambda b, j, seg: (b * nblk + j, 0, 0), memory_space=pltpu.SMEM)
    nxt = pl.BlockSpec((1, N_GROUPS, LANES), lambda b, j, seg: (jnp.minimum(b * nblk + j + 1, last), 0, 0),
                       memory_space=pltpu.SMEM)
    aux = pl.BlockSpec((TR, LANES), lambda b, j, seg: (b * nblk + j, 0))
    return [cur, nxt, aux]


def _gather_ffn(step, n_steps, seg_ref, tab_ref, tab_next_ref, aux_ref, ys_hbm, buf, sem):
    def runs(t_ref, slot, op):
        copy = lambda dst, src: pltpu.make_async_copy(ys_hbm.at[src], buf.at[slot, dst], sem.at[slot])
        _for_each_run(seg_ref, t_ref, lambda dst, src: getattr(copy(dst, src), op)())

    slot = lax.rem(step, 2)

    @pl.when(step == 0)
    def _():
        buf[...] = jnp.zeros_like(buf)
        runs(tab_ref, 0, "start")

    @pl.when(step + 1 < n_steps)
    def _():
        runs(tab_next_ref, 1 - slot, "start")

    runs(tab_ref, slot, "wait")
    lpos = aux_ref[:, AUX_LPOS:AUX_LPOS + 1].astype(I32)
    cols = lax.broadcasted_iota(I32, (TR, LROWS), 1)
    pick = jnp.where(cols == lpos, 1.0, 0.0).astype(BF16)
    hi, lo = _split(buf[slot])
    return _dot(pick, hi) + _dot(pick, lo)


def _inproj1_kernel(nctx_blk, nb, seg_ref, tab_ref, tab_next_ref, aux_ref, ys_hbm, x_ref, mod0_ref, mod1_ref, nw_ref,
                    w_ref, lb_ref, d1f_ref, d1b_ref, mf_ref, mb_ref,
                    x2_ref, qf_ref, kf_ref, qb_ref, kb_ref, v_ref, vt_ref, g_ref, cvf_ref, cvb_ref, buf, sem):
    bp, j = pl.program_id(0), pl.program_id(1)
    nblk = pl.num_programs(1)
    ffn = _gather_ffn(bp * nblk + j, pl.num_programs(0) * nblk, seg_ref, tab_ref, tab_next_ref, aux_ref, ys_hbm,
                      buf, sem)
    d = x_ref.shape[-1]
    hb = []
    for s in range(NSUB):
        mod0 = _mod_row(mod0_ref, j < nctx_blk, nb, bp * NSUB + s)
        mod1 = _mod_row(mod1_ref, j < nctx_blk, nb, bp * NSUB + s)
        x2 = x_ref[0, s] + mod0[:, 5 * d:6 * d] * ffn[s * TM:(s + 1) * TM]
        x2_ref[0, s] = x2
        hb.append((_rms(x2) * nw_ref[...] * (1.0 + mod1[:, d:2 * d]) + mod1[:, 0:d]).astype(BF16))
    hb = jnp.concatenate(hb, axis=0)
    proj = lambda n: _dot(hb, w_ref[:, n * d:(n + 1) * d])
    q = _silu(proj(0))
    lb = lb_ref[...]
    v = proj(3)
    g = proj(4)
    f_f = lb + (1.0 - lb) * _sigmoid(proj(1))
    f_b = lb + (1.0 - lb) * _sigmoid(proj(2))
    for s in range(NSUB):
        rows = slice(s * TM, (s + 1) * TM)
        v_ref[0, s] = v[rows].astype(BF16)
        vt_ref[0, s] = v[rows].T.astype(BF16)
        g_ref[0, s] = g[rows].astype(BF16)
        _gate_outputs(q[rows], 1.0 - f_f[rows], jnp.log(f_f[rows]), d1f_ref, mf_ref,
                      qf_ref.at[0, s], kf_ref.at[0, s], cvf_ref.at[0, s, 0])
        _gate_outputs(q[rows], 1.0 - f_b[rows], jnp.log(f_b[rows]), d1b_ref, mb_ref,
                      qb_ref.at[0, s], kb_ref.at[0, s], cvb_ref.at[0, s, 0])


def _inproj1(seg, tab, aux, ys, x1, mod0, mod1, nw, w_in, lb, consts, nctx_blk):
    bsz, nt, d = x1.shape
    nblk = nt // TM
    nbp = bsz // NSUB
    d1f, d1b, mf, mb = consts
    cs = lambda a: pl.BlockSpec(a.shape, lambda b, j, seg: (0,) * a.ndim)
    tok = lambda width: pl.BlockSpec((1, NSUB, TM, width), lambda b, j, seg: (b, 0, j, 0))
    cvspec = pl.BlockSpec((1, NSUB, 1, 3 * SUBLANES, d), lambda b, j, seg: (b, 0, j, 0, 0))
    grid_spec = pltpu.PrefetchScalarGridSpec(
        num_scalar_prefetch=1, grid=(nbp, nblk),
        in_specs=_route_specs(nbp, nblk) + [
                  pl.BlockSpec(memory_space=pl.ANY), tok(d), cs(mod0), cs(mod1), cs(nw),
                  pl.BlockSpec(w_in.shape, lambda b, j, seg: (0, 0), pipeline_mode=pl.Buffered(1)), cs(lb),
                  cs(d1f), cs(d1b), cs(mf), cs(mb)],
        out_specs=[tok(d)] * 6 + [pl.BlockSpec((1, NSUB, d, TM), lambda b, j, seg: (b, 0, 0, j)), tok(d), cvspec, cvspec],
        scratch_shapes=[pltpu.VMEM((2, LROWS, d), F32), pltpu.SemaphoreType.DMA((2,))])
    sds = jax.ShapeDtypeStruct
    out_shape = [sds((nbp, NSUB, nt, d), F32)] + [sds((nbp, NSUB, nt, d), BF16)] * 5 + [sds((nbp, NSUB, d, nt), BF16)] + \
                [sds((nbp, NSUB, nt, d), BF16)] + [sds((nbp, NSUB, nblk, 3 * SUBLANES, d), F32)] * 2
    outs = pl.pallas_call(
        functools.partial(_inproj1_kernel, nctx_blk, bsz), grid_spec=grid_spec, out_shape=out_shape,
        compiler_params=_params(2), name="combine_inproj_hgrn",
    )(seg, tab, tab, aux, ys, _pairs(x1), mod0, mod1, nw, w_in, lb, d1f, d1b, mf, mb)
    return [_unpairs(o) for o in outs]


def _final_kernel(seg_ref, tab_ref, tab_next_ref, aux_ref, ys_hbm, x_ref, mod_ref, fw_ref, o_ref, buf, sem):
    nblk = pl.num_programs(1)
    bp = pl.program_id(0)
    step = bp * nblk + pl.program_id(1)
    ffn = _gather_ffn(step, pl.num_programs(0) * nblk, seg_ref, tab_ref, tab_next_ref, aux_ref, ys_hbm, buf, sem)
    d = x_ref.shape[-1]
    for s in range(NSUB):
        mod = mod_ref[pl.ds(bp * NSUB + s, 1), :]
        o_ref[0, s] = _rms(x_ref[0, s] + mod[:, 5 * d:6 * d] * ffn[s * TM:(s + 1) * TM]) * fw_ref[...]


def _final(seg, tab, aux, ys, x3, mod, fw):
    bsz, seq, d = x3.shape
    nblk = seq // TM
    nbp = bsz // NSUB
    cs = lambda a: pl.BlockSpec(a.shape, lambda b, j, seg: (0,) * a.ndim)
    tok = pl.BlockSpec((1, NSUB, TM, d), lambda b, j, seg: (b, 0, j, 0))
    grid_spec = pltpu.PrefetchScalarGridSpec(
        num_scalar_prefetch=1, grid=(nbp, nblk),
        in_specs=_route_specs(nbp, nblk) + [pl.BlockSpec(memory_space=pl.ANY), tok, cs(mod), cs(fw)],
        out_specs=tok,
        scratch_shapes=[pltpu.VMEM((2, LROWS, d), F32), pltpu.SemaphoreType.DMA((2,))])
    out = pl.pallas_call(
        _final_kernel, grid_spec=grid_spec, out_shape=jax.ShapeDtypeStruct((nbp, NSUB, seq, d), F32),
        compiler_params=_params(2), name="combine_final_norm",
    )(seg, tab, tab, aux, ys, _pairs(x3), mod, fw)
    return _unpairs(out)


def _grid_sincos(n_tokens, dim):
    rows = n_tokens // GRID_W
    r, col = jnp.meshgrid(jnp.arange(rows, dtype=F32), jnp.arange(GRID_W, dtype=F32), indexing='ij')
    quarter = dim // 4
    omega = 1.0 / (10000.0 ** (jnp.arange(quarter, dtype=F32) / quarter))

    def emb(p):
        ang = p.reshape(-1, 1) * omega
        return jnp.concatenate([jnp.sin(ang), jnp.cos(ang)], axis=-1)

    return jnp.concatenate([emb(r), emb(col)], axis=-1)


def _pad_heads(w, n_heads):
    dk = w.shape[-1] // n_heads
    w = w.reshape(w.shape[:-1] + (n_heads, dk))
    w = jnp.pad(w, [(0, 0)] * (w.ndim - 1) + [(0, HEAD_W - dk)])
    return w.reshape(w.shape[:-2] + (n_heads * HEAD_W,))


def _router_tables(router_w, router_bias):
    n_exp = router_w.shape[1]
    epg = n_exp // N_GROUPS
    assert epg == EXPERTS_PER_GROUP
    perm = np.array([epg * g + p for p in range(epg) for g in range(N_GROUPS)])
    wt = jnp.pad(router_w.T[perm], ((0, LANES - n_exp), (0, 0)))
    hi = wt.astype(BF16)
    lo = (wt - hi.astype(F32)).astype(BF16)
    bias = jnp.pad(router_bias[perm], (0, LANES - n_exp))
    return hi, lo, jnp.broadcast_to(bias[:, None], (LANES, TR)).astype(F32)


def _segments(counts, n_tokens):
    cnt = counts[:, 0].astype(I32)
    tiles = (cnt + TR - 1) // TR
    ends = jnp.cumsum(tiles)
    ntile = (n_tokens + (n_tokens // TR) * N_GROUPS * (RUN - 1) + TR - 1) // TR + N_GROUPS
    n_valid = ends[-1]
    seg = jnp.concatenate([(ends - tiles) * TR, (n_valid * TR).reshape(1), jnp.full((1,), ntile * TR, I32)])
    tidx = jnp.minimum(jnp.arange(ntile, dtype=I32), n_valid - 1)
    tile_group = jnp.sum((tidx[:, None] >= ends[None, :]).astype(I32), axis=1)
    cnt = jnp.concatenate([cnt, jnp.zeros((1,), I32)])
    return seg.astype(I32), cnt, tile_group.astype(I32), n_valid.reshape(1).astype(I32), ntile * TR


def _moe_layer(tab, meta, counts, pay, weights):
    seg, cnt, tile_group, n_valid, n_sorted = _segments(counts, pay.shape[0])
    hs = _dispatch(seg, cnt, tab, meta, pay, n_sorted)
    ys = _moe(tile_group, n_valid, hs, *weights)
    return seg, ys


def kernel(x, c, ctx, c_ctx, ada_w, ada_b, norm_mix, norm_ffn, ab_w_in, ab_w_out, gla_a2, gla_ab, gla_norm, s5_lam_re, s5_lam_im, s5_log_dt, s5_b_re, s5_b_im, s5_c_re, s5_c_im, s5_d, s5_glu_w, s5_glu_b, hg_w_in, hg_w_out, hg_lb_logits, hg_norm, router_w, router_bias, moe_w_gate, moe_w_up, moe_w_down, final_norm):
    bsz, seq, d = x.shape
    nctx = ctx.shape[1]
    depth = ada_w.shape[0]
    assert depth == 2 and seq % TM == 0 and nctx % TM == 0 and bsz % SUBLANES == 0 and bsz < MOD_ROWS
    assert bsz % NSUB == 0
    nctx_blk = nctx // TM
    nt = nctx + seq
    nc = nt // S5_L
    assert (nc * bsz) % LANES == 0

    cond = jnp.zeros((MOD_ROWS, d), F32).at[:bsz].set(c).at[bsz].set(c_ctx)
    mod = _modulation(cond, ada_w, ada_b)
    consts = _chunk_matrices()
    row = lambda v: v.reshape(1, -1)

    qk = gla_a2.shape[-1]
    gv = ab_w_out.shape[1] // 2
    rank = gla_a2.shape[2]
    w_in = ab_w_in[0]
    o_v, o_g, o_a, o_u = 2 * qk, 2 * qk + gv, 2 * qk + 2 * gv, 2 * qk + 2 * gv + 2 * rank
    wm = jnp.concatenate([_pad_heads(w_in[:, 0:qk], GLA_HEADS), _pad_heads(w_in[:, qk:o_v], GLA_HEADS),
                          w_in[:, o_v:o_g], w_in[:, o_g:o_a], w_in[:, o_u:]], axis=1).astype(BF16)
    wa = jnp.pad(w_in[:, o_a:o_u], ((0, 0), (0, LANES - 2 * rank))).astype(BF16)
    a2p = _pad_heads(gla_a2[0], GLA_HEADS)
    a2 = jnp.zeros((LANES, 2 * gv), F32).at[0:rank, 0:gv].set(a2p[0]).at[rank:2 * rank, gv:].set(a2p[1]).astype(BF16)
    ab = _pad_heads(gla_ab[0], GLA_HEADS).reshape(1, 2 * gv)
    pos = _grid_sincos(seq, d)
    (xs, qf, kf, qb, kb, v, vt, g, u_cr, cvf, cvb) = _inproj0(x, ctx, pos, mod[0], row(norm_mix[0]), wm, wa, a2, ab, consts,
                                                               float(qk // GLA_HEADS) ** -0.5)
    o_f, o_b = _recurrence(qf, kf, qb, kb, v, vt, cvf, cvb, GLA_HEADS, nctx_blk)

    assert 2 * s5_b_re.shape[-1] == PAIR_W
    ops = _s5_operators(s5_lam_re[0], s5_lam_im[0], s5_log_dt[0], s5_b_re[0], s5_b_im[0], s5_c_re[0], s5_c_im[0], s5_d[0])
    y5 = _s5(u_cr, ops, nc, nctx // S5_L, bsz)

    rwh, rwl, rb = _router_tables(router_w, router_bias)
    utri = jnp.asarray(np.triu(np.ones((TR, TR), np.float32), 1))
    ones = jnp.ones((TR, TR), F32)
    x1, pay, aux, meta, tab, counts = _mixout(True, GLA_HEADS, nctx_blk, 0, xs, o_f, o_b, g, y5, mod[0],
                                              row(gla_norm[0]), s5_glu_w[0].astype(BF16), row(s5_glu_b[0]),
                                              ab_w_out[0].astype(BF16), row(norm_ffn[0]), rwh, rwl, rb, utri, ones)
    assert moe_w_gate.shape[1] == N_GROUPS * EXPERTS_PER_GROUP
    seg, ys = _moe_layer(tab, meta, counts, pay, (0, moe_w_gate, moe_w_up, moe_w_down))

    lb_all = jax.nn.softmax(hg_lb_logits.astype(F32), axis=0)
    lb_all = jnp.cumsum(lb_all, axis=0) - lb_all[0]
    (x2, qf, kf, qb, kb, v, vt, g, cvf, cvb) = _inproj1(seg, tab, aux, ys, x1, mod[0], mod[1], row(norm_mix[1]),
                                                          hg_w_in[0].astype(BF16), row(lb_all[1]), consts, nctx_blk)
    o_f, o_b = _recurrence(qf, kf, qb, kb, v, vt, cvf, cvb, HG_HEADS, nctx_blk)
    x3, pay, aux, meta, tab, counts = _mixout(False, HG_HEADS, nctx_blk, nctx_blk, x2, o_f, o_b, g, None, mod[1],
                                              row(hg_norm[0]), None, None, hg_w_out[0].astype(BF16),
                                              row(norm_ffn[1]), rwh, rwl, rb, utri, ones)
    seg, ys = _moe_layer(tab, meta, counts, pay, (1, moe_w_gate, moe_w_up, moe_w_down))
    return _final(seg, tab, aux, ys, x3, mod[1], row(final_norm))
```

```python
import functools
import math

import numpy as np
import jax
import jax.numpy as jnp
from jax import lax
from jax.experimental import pallas as pl
from jax.experimental.pallas import tpu as pltpu

F32, BF16, I32 = jnp.float32, jnp.bfloat16, jnp.int32

EPS = 1e-6
CHUNK = 64
GRID_W = 64
GLA_HEADS = 4
GLA_GATE_NORM = 16.0
HG_HEADS = 8
N_GROUPS = 8
EXPERTS_PER_GROUP = 4
TOP_K = 2

LANES = 128
SUBLANES = 8
TM = 256
CPT = TM // CHUNK
HEAD_W = 128
S5_L = 16
MOD_ROWS = 16
COL_BLK = 256
SCAN_UNROLL = 4
STACK_HEADS = 4
RUN = SUBLANES
RUN_SHIFT = 3
NSUB = 2
TR = NSUB * TM
LROWS = TR + N_GROUPS * RUN
TAB_START, TAB_COUNT, TAB_CARRY = 0, 1, 2
AUX_LPOS = 4
VMEM_LIMIT = 56 * 1024 * 1024


def _dot(a, b):
    return jnp.dot(a, b, preferred_element_type=F32)


def _dot_nt(a, b):
    return lax.dot_general(a, b, (((1,), (1,)), ((), ())), preferred_element_type=F32)


def _split(x):
    hi = x.astype(BF16)
    lo = (x - hi.astype(F32)).astype(BF16)
    return hi, lo


def _sigmoid(x):
    return 1.0 / (1.0 + jnp.exp(-x))


def _silu(x):
    return x * _sigmoid(x)


def _gelu_tanh(x):
    return 0.5 * x * (1.0 + jnp.tanh(math.sqrt(2.0 / math.pi) * (x + 0.044715 * (x * x * x))))


def _rms(x):
    return x * lax.rsqrt(jnp.mean(x * x, axis=-1, keepdims=True) + EPS)


def _head_rms(o, gain, n_heads):
    outs = []
    for h in range(n_heads):
        oh = o[:, h * HEAD_W:(h + 1) * HEAD_W]
        outs.append(_rms(oh) * gain)
    return jnp.concatenate(outs, axis=1)


def _pairs(a):
    return a.reshape((a.shape[0] // NSUB, NSUB) + a.shape[1:])


def _unpairs(a):
    return a.reshape((a.shape[0] * NSUB,) + a.shape[2:])


def _mod_row(mod_ref, is_ctx, ctx_row, batch_row):
    return mod_ref[pl.ds(jnp.where(is_ctx, ctx_row, batch_row), 1), :]


def _params(n_grid_dims, vmem=VMEM_LIMIT, flags=None):
    return pltpu.CompilerParams(dimension_semantics=("arbitrary",) * n_grid_dims, vmem_limit_bytes=vmem, flags=flags)


def _const_spec(shape):
    nd = len(shape)
    return pl.BlockSpec(shape, lambda *_: (0,) * nd)


def _mod_kernel(c_ref, w_ref, b_ref, o_ref):
    s = _silu(c_ref[...])
    shi, slo = _split(s)
    whi, wlo = _split(w_ref[0])
    o_ref[0] = _dot(shi, whi) + _dot(shi, wlo) + _dot(slo, whi) + b_ref[0]


def _modulation(cond, ada_w, ada_b):
    depth, d, six_d = ada_w.shape
    nchunk = six_d // d
    return pl.pallas_call(
        _mod_kernel,
        grid=(depth, nchunk),
        in_specs=[
            _const_spec((MOD_ROWS, d)),
            pl.BlockSpec((1, d, d), lambda l, n: (l, 0, n)),
            pl.BlockSpec((1, 1, d), lambda l, n: (l, 0, n)),
        ],
        out_specs=pl.BlockSpec((1, MOD_ROWS, d), lambda l, n: (l, 0, n)),
        out_shape=jax.ShapeDtypeStruct((depth, MOD_ROWS, six_d), F32),
        compiler_params=_params(2),
        name="modulation",
    )(cond, ada_w, ada_b.reshape(depth, 1, six_d))


def _chunk_matrices():
    t = np.arange(TM)
    c, l = t // CHUNK, t % CHUNK
    same = (c[:, None] == c[None, :]).astype(np.float32)
    li, ls = l[:, None], l[None, :]
    mid_f = CHUNK // 2 - 1
    mid_b = CHUNK // 2
    d1f = same * ((ls <= li).astype(np.float32) - (ls <= mid_f).astype(np.float32))
    d1b = same * ((ls >= li).astype(np.float32) - (ls >= mid_b).astype(np.float32))
    inchunk = (np.arange(SUBLANES)[:, None] == c[None, :]).astype(np.float32)
    mf = np.concatenate([inchunk * (l <= mid_f), inchunk * (l > mid_f), inchunk])
    mb = np.concatenate([inchunk * (l >= mid_b), inchunk * (l < mid_b), inchunk])
    as_bf16 = lambda a: jnp.asarray(a, dtype=BF16)
    return as_bf16(d1f), as_bf16(d1b), as_bf16(mf), as_bf16(mb)


def _gate_outputs(q, k, la, d1_ref, m_ref, q_out, k_out, cv_out):
    hi, lo = _split(la)
    d1m = d1_ref[...]
    d1 = _dot(d1m, hi) + _dot(d1m, lo)
    q_out[...] = (q * jnp.exp(d1)).astype(BF16)
    k_out[...] = (k * jnp.exp(-d1)).astype(BF16)
    mm = m_ref[...]
    cv_out[...] = jnp.exp(_dot(mm, hi) + _dot(mm, lo))


PAIR_W = 32
PAIRS_PER_TILE = LANES // PAIR_W
CROWS = TM // S5_L


def _to_chunk_rows(u, us_ref, ucr_ref):
    for q in range(us_ref.shape[0]):
        us_ref[q] = u[:, q * LANES:(q + 1) * LANES]
    lane = lax.broadcasted_iota(I32, (CROWS, LANES), 1)
    for p in range(ucr_ref.shape[0]):
        q, pp = divmod(p, PAIRS_PER_TILE)
        tiles = []
        for t in range(S5_L // PAIRS_PER_TILE):
            acc = None
            for jj in range(PAIRS_PER_TILE):
                piece = us_ref[q, pl.ds(t * PAIRS_PER_TILE + jj, CROWS, stride=S5_L), :]
                shift = (PAIR_W * (jj - pp)) % LANES
                if shift:
                    piece = pltpu.roll(piece, shift, axis=1)
                here = (lane >= PAIR_W * jj) & (lane < PAIR_W * (jj + 1))
                acc = piece if acc is None else jnp.where(here, piece, acc)
            tiles.append(acc)
        ucr_ref[p] = jnp.concatenate(tiles, axis=1).astype(BF16)


def _from_chunk_rows(ycr_ref, ys_ref):
    lane = lax.broadcasted_iota(I32, (CROWS, LANES), 1)
    for q in range(ycr_ref.shape[0] // PAIRS_PER_TILE):
        for i in range(S5_L):
            t, ii = divmod(i, PAIRS_PER_TILE)
            acc = None
            for pp in range(PAIRS_PER_TILE):
                piece = ycr_ref[q * PAIRS_PER_TILE + pp, :, t * LANES:(t + 1) * LANES].astype(F32)
                shift = (PAIR_W * (pp - ii)) % LANES
                if shift:
                    piece = pltpu.roll(piece, shift, axis=1)
                here = (lane >= PAIR_W * pp) & (lane < PAIR_W * (pp + 1))
                acc = piece if acc is None else jnp.where(here, piece, acc)
            ys_ref[q, pl.ds(i, CROWS, stride=S5_L), :] = acc
    return jnp.concatenate([ys_ref[q] for q in range(ys_ref.shape[0])], axis=1)


def _inproj0_kernel(nctx_blk, nb, gla_w, q_scale, x_ref, ctx_ref, pos_ref, mod_ref, nw_ref, wm_ref, wa_ref, a2_ref,
                    ab_ref, d1f_ref, d1b_ref, mf_ref, mb_ref,
                    xs_ref, qf_ref, kf_ref, qb_ref, kb_ref, v_ref, vt_ref, g_ref, ucr_ref, cvf_ref, cvb_ref, us_ref):
    bp, j = pl.program_id(0), pl.program_id(1)
    is_ctx = j < nctx_blk
    xin = jnp.where(is_ctx, ctx_ref[0], x_ref[0] + pos_ref[...])
    xs_ref[0] = xin
    d = xin.shape[-1]
    hb = []
    for s in range(NSUB):
        mod = _mod_row(mod_ref, is_ctx, nb, bp * NSUB + s)
        hb.append((_rms(xin[s]) * nw_ref[...] * (1.0 + mod[:, d:2 * d]) + mod[:, 0:d]).astype(BF16))
    hb = jnp.concatenate(hb, axis=0)
    w = gla_w
    u = _dot(hb, wm_ref[:, 4 * w:5 * w])
    for s in range(NSUB):
        _to_chunk_rows(u[s * TM:(s + 1) * TM], us_ref, ucr_ref.at[:, 0, s, 0])
    a = _dot(hb, wa_ref[...]).astype(BF16)
    for n in range(w // COL_BLK):
        cs = slice(n * COL_BLK, (n + 1) * COL_BLK)
        proj = lambda i: _dot(hb, wm_ref[:, i * w + n * COL_BLK:i * w + (n + 1) * COL_BLK])
        q = proj(0) * q_scale
        k = proj(1)
        v = proj(2)
        g = proj(3)
        las = []
        for di in range(2):
            zs = slice(di * w + n * COL_BLK, di * w + (n + 1) * COL_BLK)
            z = _dot(a, a2_ref[:, zs]) + ab_ref[:, zs]
            las.append((jnp.minimum(z, 0.0) - jnp.log(1.0 + jnp.exp(-jnp.abs(z)))) * (1.0 / GLA_GATE_NORM))
        for s in range(NSUB):
            rows = slice(s * TM, (s + 1) * TM)
            g_ref[0, s, :, cs] = g[rows].astype(BF16)
            v_ref[0, s, :, cs] = v[rows].astype(BF16)
            vt_ref[0, s, cs, :] = v[rows].T.astype(BF16)
            _gate_outputs(q[rows], k[rows], las[0][rows], d1f_ref, mf_ref,
                          qf_ref.at[0, s, :, cs], kf_ref.at[0, s, :, cs], cvf_ref.at[0, s, 0, :, cs])
            _gate_outputs(q[rows], k[rows], las[1][rows], d1b_ref, mb_ref,
                          qb_ref.at[0, s, :, cs], kb_ref.at[0, s, :, cs], cvb_ref.at[0, s, 0, :, cs])


def _inproj0(x, ctx, pos, mod, nw, wm, wa, a2, ab, consts, q_scale):
    bsz, seq, d = x.shape
    nctx = ctx.shape[1]
    nctx_blk, nlat_blk = nctx // TM, seq // TM
    nblk = nctx_blk + nlat_blk
    nt = nblk * TM
    nbp = bsz // NSUB
    w = wm.shape[1] // 5
    d1f, d1b, mf, mb = consts
    in_specs = [
        pl.BlockSpec((1, NSUB, TM, d), lambda b, j: (b, 0, jnp.maximum(j - nctx_blk, 0), 0)),
        pl.BlockSpec((1, NSUB, TM, d), lambda b, j: (b, 0, jnp.minimum(j, nctx_blk - 1), 0)),
        pl.BlockSpec((TM, d), lambda b, j: (jnp.maximum(j - nctx_blk, 0), 0)),
        _const_spec(mod.shape), _const_spec(nw.shape), _const_spec(wm.shape), _const_spec(wa.shape),
        _const_spec(a2.shape), _const_spec(ab.shape), _const_spec(d1f.shape), _const_spec(d1b.shape),
        _const_spec(mf.shape), _const_spec(mb.shape),
    ]
    tokspec = lambda width: pl.BlockSpec((1, NSUB, TM, width), lambda b, j: (b, 0, j, 0))
    cvspec = pl.BlockSpec((1, NSUB, 1, 3 * SUBLANES, w), lambda b, j: (b, 0, j, 0, 0))
    npair, crow_w = w // PAIR_W, S5_L * PAIR_W
    out_specs = [tokspec(d), tokspec(w), tokspec(w), tokspec(w), tokspec(w), tokspec(w),
                 pl.BlockSpec((1, NSUB, w, TM), lambda b, j: (b, 0, 0, j)), tokspec(w),
                 pl.BlockSpec((npair, 1, NSUB, 1, CROWS, crow_w), lambda b, j: (0, b, 0, j, 0, 0)), cvspec, cvspec]
    sds = jax.ShapeDtypeStruct
    out_shape = [sds((nbp, NSUB, nt, d), F32)] + [sds((nbp, NSUB, nt, w), BF16)] * 5 + [sds((nbp, NSUB, w, nt), BF16)] + \
                [sds((nbp, NSUB, nt, w), BF16), sds((npair, nbp, NSUB, nblk, CROWS, crow_w), BF16)] + \
                [sds((nbp, NSUB, nblk, 3 * SUBLANES, w), F32)] * 2
    outs = pl.pallas_call(
        functools.partial(_inproj0_kernel, nctx_blk, bsz, w, q_scale),
        grid=(nbp, nblk), in_specs=in_specs, out_specs=out_specs, out_shape=out_shape,
        scratch_shapes=[pltpu.VMEM((w // LANES, TM, LANES), F32)],
        compiler_params=_params(2), name="inproj_gla_s5",
    )(_pairs(x), _pairs(ctx), pos, mod, nw, wm, wa, a2, ab, d1f, d1b, mf, mb)
    ucr = outs[8].reshape(npair, bsz * nblk * CROWS, crow_w)
    return [_unpairs(o) for o in outs[:8]] + [ucr] + [_unpairs(o) for o in outs[9:]]


def _recur_kernel(n_heads, qf_ref, kf_ref, vf_ref, vtf_ref, cvf_ref, qb_ref, kb_ref, vb_ref, vtb_ref, cvb_ref,
                  of_ref, ob_ref, s_ref):
    @pl.when(pl.program_id(1) == 0)
    def _():
        s_ref[...] = jnp.zeros_like(s_ref)

    hrows = STACK_HEADS * CHUNK
    ri = lax.broadcasted_iota(I32, (hrows, hrows), 0)
    ci = lax.broadcasted_iota(I32, (hrows, hrows), 1)
    same_head = (ri // CHUNK) == (ci // CHUNK)
    keep = (same_head & (ci <= ri), same_head & (ci >= ri))
    pw = 2 * HEAD_W
    prow = lax.broadcasted_iota(I32, (2 * CHUNK, pw), 0)
    zero_blk = jnp.zeros((HEAD_W, HEAD_W), BF16)
    stack = lambda ref, r0, h_first: jnp.concatenate(
        [ref[0, r0:r0 + CHUNK, h * HEAD_W:(h + 1) * HEAD_W] for h in range(h_first, h_first + STACK_HEADS)], axis=0)
    dirs = ((qf_ref, kf_ref, vf_ref, vtf_ref, cvf_ref, of_ref), (qb_ref, kb_ref, vb_ref, vtb_ref, cvb_ref, ob_ref))
    for cc in range(CPT):
        for d, (q_ref, k_ref, v_ref, vt_ref, cv_ref, o_ref) in enumerate(dirs):
            c = cc if d == 0 else CPT - 1 - cc
            r0 = c * CHUNK
            p0 = (c // 2) * 2 * CHUNK
            in_chunk = (prow >= r0 - p0) & (prow < r0 - p0 + CHUNK)
            e_mid = cv_ref[0, 0, c:c + 1, :]
            e_rest = cv_ref[0, 0, SUBLANES + c:SUBLANES + c + 1, :]
            e_all = cv_ref[0, 0, 2 * SUBLANES + c:2 * SUBLANES + c + 1, :]
            o_intra = []
            for h_first in range(0, n_heads, STACK_HEADS):
                scores = _dot_nt(stack(q_ref, r0, h_first), stack(k_ref, r0, h_first))
                attn = jnp.where(keep[d], scores, 0.0).astype(BF16)
                o_intra.append(_dot(attn, stack(v_ref, r0, h_first)))
            o_intra = jnp.concatenate(o_intra, axis=0)
            for p in range(n_heads // 2):
                ps = slice(p * pw, (p + 1) * pw)
                h0, h1 = 2 * p, 2 * p + 1
                hs0, hs1 = slice(h0 * HEAD_W, h1 * HEAD_W), slice(h1 * HEAD_W, (h1 + 1) * HEAD_W)
                st0, st1 = s_ref[d, h0], s_ref[d, h1]
                sb0 = (st0 * e_mid[:, hs0]).astype(BF16)
                sb1 = (st1 * e_mid[:, hs1]).astype(BF16)
                carried = jnp.concatenate([jnp.concatenate([sb0, zero_blk], axis=1),
                                           jnp.concatenate([zero_blk, sb1], axis=1)], axis=0)
                o_inter = _dot_nt(q_ref[0, r0:r0 + CHUNK, ps], carried)
                o_ref[0, r0:r0 + CHUNK, hs0] = (o_inter[:, 0:HEAD_W] + o_intra[h0 * CHUNK:h1 * CHUNK]).astype(o_ref.dtype)
                o_ref[0, r0:r0 + CHUNK, hs1] = (o_inter[:, HEAD_W:pw]
                                                + o_intra[h1 * CHUNK:(h1 + 1) * CHUNK]).astype(o_ref.dtype)
                kp = k_ref[0, p0:p0 + 2 * CHUNK, ps]
                kp = jnp.where(in_chunk, kp, jnp.zeros_like(kp))
                ds = _dot(vt_ref[0, ps, p0:p0 + 2 * CHUNK], kp)
                s_ref[d, h0] = st0 * e_all[:, hs0] + ds[0:HEAD_W, 0:HEAD_W] * e_rest[:, hs0]
                s_ref[d, h1] = st1 * e_all[:, hs1] + ds[HEAD_W:pw, HEAD_W:pw] * e_rest[:, hs1]


def _recurrence(qf, kf, qb, kb, v, vt, cvf, cvb, n_heads, nctx_blk):
    bsz, nt, w = qf.shape
    nblk = nt // TM
    fwd = lambda j: j
    bwd = lambda j: jnp.where(j < nctx_blk, nctx_blk - 1 - j, nblk - 1 - (j - nctx_blk))
    specs = []
    for order in (fwd, bwd):
        tok = pl.BlockSpec((1, TM, w), lambda b, j, o=order: (b, o(j), 0))
        specs += [tok, tok, tok,
                  pl.BlockSpec((1, w, TM), lambda b, j, o=order: (b, 0, o(j))),
                  pl.BlockSpec((1, 1, 3 * SUBLANES, w), lambda b, j, o=order: (b, o(j), 0, 0))]
    out_specs = [pl.BlockSpec((1, TM, w), lambda b, j: (b, j, 0)),
                 pl.BlockSpec((1, TM, w), lambda b, j: (b, bwd(j), 0))]
    return pl.pallas_call(
        functools.partial(_recur_kernel, n_heads),
        grid=(bsz, nblk), in_specs=specs, out_specs=out_specs,
        out_shape=[jax.ShapeDtypeStruct((bsz, nt, w), BF16)] * 2,
        scratch_shapes=[pltpu.VMEM((2, n_heads, HEAD_W, HEAD_W), F32)],
        compiler_params=_params(2), name=f"recurrence_h{n_heads}",
    )(qf, kf, v, vt, cvf, qb, kb, v, vt, cvb)


def _s5_operators(lam_re, lam_im, log_dt, b_re, b_im, c_re, c_im, d_skip):
    L = S5_L
    ndir, ng, p = lam_re.shape
    hs = b_re.shape[-1]
    dt = jnp.exp(log_dt)[..., None]
    lr, li = lam_re * dt, lam_im * dt
    mag = jnp.exp(lr)
    a_re, a_im = mag * jnp.cos(li), mag * jnp.sin(li)
    den = lam_re * lam_re + lam_im * lam_im
    f_re = ((a_re - 1.0) * lam_re + a_im * lam_im) / den
    f_im = (a_im * lam_re - (a_re - 1.0) * lam_im) / den
    bb_re = f_re[..., None] * b_re - f_im[..., None] * b_im
    bb_im = f_re[..., None] * b_im + f_im[..., None] * b_re
    m = jnp.arange(L + 1, dtype=F32)[:, None, None, None]
    pw_mag = jnp.exp(m * lr[None])
    pw_re, pw_im = pw_mag * jnp.cos(m * li[None]), pw_mag * jnp.sin(m * li[None])
    eye = jnp.eye(2, dtype=F32)
    npair, pw, cw = ng // 2, 2 * hs, 2 * L * hs

    toep = jnp.zeros((npair, cw, cw), F32)
    wz_p, vo_p = [], []
    for di in range(ndir):
        pr, pi = pw_re[:L, di, :, :, None], pw_im[:L, di, :, :, None]
        ab_re = pr * bb_re[di][None] - pi * bb_im[di][None]
        ab_im = pr * bb_im[di][None] + pi * bb_re[di][None]
        kf = jnp.einsum('mgnk,ghn->gkmh', ab_re, c_re[di]) - jnp.einsum('mgnk,ghn->gkmh', ab_im, c_im[di])
        if di == 1:
            kf = kf[:, :, ::-1, :]
        kp = jnp.einsum('pskmh,st->pskmth', kf.reshape(npair, 2, hs, L, hs), eye).reshape(npair, pw, cw)
        if di == 0:
            kpad = jnp.pad(kp, ((0, 0), (0, 0), (cw, 0)))
            blocks = [kpad[:, :, cw - pw * j:2 * cw - pw * j] for j in range(L)]
        else:
            kpad = jnp.pad(kp, ((0, 0), (0, 0), (0, cw)))
            blocks = [kpad[:, :, pw * (L - 1 - j):pw * (L - 1 - j) + cw] for j in range(L)]
        toep = toep + jnp.stack(blocks, axis=1).reshape(npair, cw, cw)
        sel_i = (lambda a: a[::-1]) if di == 0 else (lambda a: a)
        pack_w = lambda a: jnp.einsum('jpsnk,st->pjsktn', sel_i(a).reshape(L, npair, 2, p, hs), eye).reshape(npair, cw, 2 * p)
        wz_p.append(jnp.concatenate([pack_w(ab_re), pack_w(ab_im)], axis=2))
        qr, qi = pw_re[1:, di, :, None, :], pw_im[1:, di, :, None, :]
        ca_re = c_re[di][None] * qr - c_im[di][None] * qi
        ca_im = c_re[di][None] * qi + c_im[di][None] * qr
        sel_o = (lambda a: a) if di == 0 else (lambda a: a[::-1])
        pack_v = lambda a: jnp.einsum('ipshn,st->psnith', sel_o(a).reshape(L, npair, 2, hs, p), eye).reshape(npair, 2 * p, cw)
        vo_p.append(jnp.concatenate([pack_v(ca_re), -pack_v(ca_im)], axis=1))
    wz_p, vo_p = jnp.stack(wz_p, axis=1), jnp.stack(vo_p, axis=1)
    al = jnp.stack([pw_re[L], pw_im[L]], axis=1)
    al_p = jnp.transpose(al.reshape(ndir, 2, npair, 2 * p), (2, 0, 1, 3))
    d_p = jnp.tile(d_skip.reshape(npair, 1, pw), (1, L, 1)).reshape(npair, 1, cw)
    return wz_p.astype(BF16), vo_p.astype(BF16), toep.astype(BF16), al_p, d_p


def _s5_kernel(nc, ncc, bsz, u_ref, wz_ref, vo_ref, tp_ref, al_ref, d_ref, y_ref, zf, zb, xf, xb):
    rows_total = nc * bsz
    rb = LANES
    half = zf.shape[2]

    def z_body(i, carry):
        rows = pl.ds(pl.multiple_of(i * rb, rb), rb)
        ub = u_ref[0, rows, :]
        for z, di in ((zf, 0), (zb, 1)):
            zz = _dot(ub, wz_ref[0, di])
            z[0, rows, :] = zz[:, 0:half]
            z[1, rows, :] = zz[:, half:2 * half]
        return carry

    lax.fori_loop(0, rows_total // rb, z_body, 0)

    coef = [[jnp.broadcast_to(al_ref[0, di, ri:ri + 1, :], (bsz, half)) for ri in range(2)] for di in range(2)]
    orders = (lambda i: i, lambda i: jnp.where(i < ncc, ncc - 1 - i, nc - 1 - (i - ncc)))

    def scan_body(i, carry):
        new = []
        for di, (z, xs) in enumerate(((zf, xf), (zb, xb))):
            x_re, x_im = carry[2 * di], carry[2 * di + 1]
            a_re, a_im = coef[di]
            rows = pl.ds(orders[di](i), bsz, stride=nc)
            xs[0, rows, :] = x_re
            xs[1, rows, :] = x_im
            new += [a_re * x_re - a_im * x_im + z[0, rows, :], a_re * x_im + a_im * x_re + z[1, rows, :]]
        return tuple(new)

    zero = jnp.zeros((bsz, half), F32)
    lax.fori_loop(0, nc, scan_body, (zero,) * 4, unroll=SCAN_UNROLL)

    def y_body(i, carry):
        rows = pl.ds(pl.multiple_of(i * rb, rb), rb)
        ub = u_ref[0, rows, :]
        state = lambda xs: jnp.concatenate([xs[0, rows, :], xs[1, rows, :]], axis=1).astype(BF16)
        y = _dot(state(xf), vo_ref[0, 0]) + _dot(state(xb), vo_ref[0, 1])
        y_ref[0, rows, :] = (y + _dot(ub, tp_ref[0]) + ub.astype(F32) * d_ref[0]).astype(y_ref.dtype)
        return carry

    lax.fori_loop(0, rows_total // rb, y_body, 0)


def _s5(u_cr, ops, nc, ncc, bsz):
    wz_p, vo_p, toep_p, al_p, d_p = ops
    npair, rows, width = u_cr.shape
    per_pair = lambda a: pl.BlockSpec((1,) + a.shape[1:], lambda p: (p,) + (0,) * (a.ndim - 1))
    return pl.pallas_call(
        functools.partial(_s5_kernel, nc, ncc, bsz),
        grid=(npair,),
        in_specs=[per_pair(u_cr), per_pair(wz_p), per_pair(vo_p), per_pair(toep_p), per_pair(al_p), per_pair(d_p)],
        out_specs=per_pair(u_cr),
        out_shape=jax.ShapeDtypeStruct(u_cr.shape, BF16),
        scratch_shapes=[pltpu.VMEM((2, rows, wz_p.shape[-1] // 2), F32)] * 4,
        compiler_params=_params(1), name="s5",
    )(u_cr, wz_p, vo_p, toep_p, al_p, d_p)


def _route(h2, rwh_ref, rwl_ref, rb_ref, utri_ref, ones_ref, carry_ref):
    hi, lo = _split(h2)
    logits = _dot_nt(rwh_ref[...], hi) + _dot_nt(rwl_ref[...], hi) + _dot_nt(rwh_ref[...], lo)
    aff = _sigmoid(logits)
    sel = aff + rb_ref[...]
    epg = EXPERTS_PER_GROUP
    s = [sel[N_GROUPS * p:N_GROUPS * (p + 1), :] for p in range(epg)]
    a = [aff[N_GROUPS * p:N_GROUPS * (p + 1), :] for p in range(epg)]
    m1, n1 = jnp.maximum(s[0], s[1]), jnp.minimum(s[0], s[1])
    m2, n2 = jnp.maximum(s[2], s[3]), jnp.minimum(s[2], s[3])
    score = jnp.maximum(m1, m2) + jnp.maximum(jnp.minimum(m1, m2), jnp.maximum(n1, n2))
    gi = lax.broadcasted_iota(I32, score.shape, 0)
    best = jnp.max(score, axis=0, keepdims=True)
    gidx = jnp.min(jnp.where(score == best, gi, N_GROUPS), axis=0, keepdims=True)
    onehot = gi == gidx
    gates = []
    for p in range(epg):
        ahead = jnp.zeros_like(score)
        for q in range(epg):
            if q != p:
                beats = (s[q] >= s[p]) if q < p else (s[q] > s[p])
                ahead = ahead + jnp.where(beats, 1.0, 0.0)
        picked = jnp.where(onehot, jnp.where(ahead < float(TOP_K) - 0.5, a[p], 0.0), 0.0)
        gates.append(jnp.sum(picked, axis=0, keepdims=True))
    den = gates[0] + gates[1] + gates[2] + gates[3]
    gates = [g / den for g in gates]
    oh = jnp.where(onehot, 1.0, 0.0)
    before = _dot(oh, utri_ref[...])
    count = _dot(oh, ones_ref[...])
    run = jnp.floor((count + (RUN - 1.0)) * (1.0 / RUN)) * RUN
    carry = carry_ref[...]
    carry_ref[...] = carry + run
    starts, nxt = [], jnp.zeros_like(run[0:1])
    for g in range(N_GROUPS):
        starts.append(nxt)
        nxt = nxt + run[g:g + 1]
    lstart = jnp.concatenate(starts, axis=0)
    lpos = jnp.sum(jnp.where(onehot, lstart + before, 0.0), axis=0, keepdims=True)
    return lpos, gates, (lstart, count, carry)


def _mixout_kernel(has_s5, n_heads, nctx_blk, tok_off, nb, *refs):
    if has_s5:
        (x_ref, of_ref, ob_ref, g_ref, ycr_ref, mod_ref, gn_ref, gluw_ref, glub_ref, wo_ref, nf_ref,
         rwh_ref, rwl_ref, rb_ref, utri_ref, ones_ref,
         x1_ref, pay_ref, aux_ref, meta_ref, tab_ref, cnt_ref, carry_ref, ys_ref) = refs
    else:
        (x_ref, of_ref, ob_ref, g_ref, mod_ref, gn_ref, wo_ref, nf_ref,
         rwh_ref, rwl_ref, rb_ref, utri_ref, ones_ref,
         x1_ref, pay_ref, aux_ref, meta_ref, tab_ref, cnt_ref, carry_ref) = refs
    bp, j = pl.program_id(0), pl.program_id(1)

    @pl.when((bp == 0) & (j == 0))
    def _():
        carry_ref[...] = jnp.zeros_like(carry_ref)

    d = x_ref.shape[-1]
    merge = lambda ref: jnp.concatenate([ref[0, s] for s in range(NSUB)], axis=0)
    o = merge(of_ref).astype(F32) + merge(ob_ref).astype(F32)
    mixed = _head_rms(o, gn_ref[...], n_heads) * _silu(merge(g_ref).astype(F32))
    if has_s5:
        y5 = jnp.concatenate([_from_chunk_rows(ycr_ref.at[:, 0, s, 0], ys_ref) for s in range(NSUB)], axis=0)
        act = _gelu_tanh(y5)
        glu = act * _sigmoid(_dot(act.astype(BF16), gluw_ref[...]) + glub_ref[...])
        mixed = jnp.concatenate([mixed, glu], axis=1)
    proj = _dot(mixed.astype(BF16), wo_ref[...])
    h2 = []
    for s in range(NSUB):
        mod = _mod_row(mod_ref, j + tok_off < nctx_blk, nb, bp * NSUB + s)
        x1 = x_ref[0, s] + mod[:, 2 * d:3 * d] * proj[s * TM:(s + 1) * TM]
        x1_ref[0, s] = x1
        h2.append(_rms(x1) * nf_ref[...] * (1.0 + mod[:, 4 * d:5 * d]) + mod[:, 3 * d:4 * d])
    h2 = jnp.concatenate(h2, axis=0)

    lpos, gates, (lstart, count, carry) = _route(h2, rwh_ref, rwl_ref, rb_ref, utri_ref, ones_ref, carry_ref)
    gi = lax.broadcasted_iota(I32, (SUBLANES, TR), 0)
    meta_ref[0] = jnp.where(gi == 0, lpos.astype(I32), 0)
    ti = lax.broadcasted_iota(I32, (N_GROUPS, LANES), 1)
    first = lambda a: a[:, 0:LANES].astype(I32)
    tab_ref[0] = jnp.where(ti == TAB_START, first(lstart),
                           jnp.where(ti == TAB_COUNT, first(count), jnp.where(ti == TAB_CARRY, first(carry), 0)))
    cnt_ref[...] = carry_ref[...]

    li = lax.broadcasted_iota(I32, (LANES, TR), 0)
    gt = jnp.where(li == AUX_LPOS, lpos, 0.0)
    for p, g in enumerate(gates):
        gt = gt + jnp.where(li == p, g, 0.0)
    aux = gt.T
    pay_ref[:, 0:d] = h2
    pay_ref[:, d:d + LANES] = aux
    aux_ref[...] = aux


def _mixout(has_s5, n_heads, nctx_blk, tok_off, x, of, ob, g, y5, mod, gn, gluw, glub, wo, nf, rwh, rwl, rb, utri, ones):
    bsz, _, d = x.shape
    w = of.shape[-1]
    nblk_all = of.shape[1] // TM
    nblk = nblk_all - tok_off
    nbp = bsz // NSUB
    tok = lambda width: pl.BlockSpec((1, NSUB, TM, width), lambda b, j: (b, 0, j + tok_off, 0))
    ins, specs = [_pairs(x), _pairs(of), _pairs(ob), _pairs(g)], [tok(d), tok(w), tok(w), tok(w)]
    scratch = [pltpu.VMEM((N_GROUPS, TR), F32)]
    if has_s5:
        npair, _, crow_w = y5.shape
        ins.append(y5.reshape(npair, nbp, NSUB, nblk_all, CROWS, crow_w))
        specs.append(pl.BlockSpec((npair, 1, NSUB, 1, CROWS, crow_w), lambda b, j: (0, b, 0, j, 0, 0)))
        scratch.append(pltpu.VMEM((w // LANES, TM, LANES), F32))
    consts = [mod, gn] + ([gluw, glub] if has_s5 else []) + [wo, nf, rwh, rwl, rb, utri, ones]
    ins += consts
    specs += [_const_spec(c.shape) for c in consts]
    ntile = nbp * nblk
    pay_w = d + LANES
    lin = lambda b, j: b * nblk + j
    out_specs = [pl.BlockSpec((1, NSUB, TM, d), lambda b, j: (b, 0, j, 0)),
                 pl.BlockSpec((TR, pay_w), lambda b, j: (lin(b, j), 0)),
                 pl.BlockSpec((TR, LANES), lambda b, j: (lin(b, j), 0)),
                 pl.BlockSpec((1, SUBLANES, TR), lambda b, j: (lin(b, j), 0, 0)),
                 pl.BlockSpec((1, N_GROUPS, LANES), lambda b, j: (lin(b, j), 0, 0)),
                 _const_spec((N_GROUPS, TR))]
    sds = jax.ShapeDtypeStruct
    out_shape = [sds((nbp, NSUB, nblk * TM, d), F32), sds((ntile * TR, pay_w), F32), sds((ntile * TR, LANES), F32),
                 sds((ntile, SUBLANES, TR), I32), sds((ntile, N_GROUPS, LANES), I32), sds((N_GROUPS, TR), F32)]
    outs = pl.pallas_call(
        functools.partial(_mixout_kernel, has_s5, n_heads, nctx_blk, tok_off, bsz),
        grid=(nbp, nblk), in_specs=specs, out_specs=out_specs, out_shape=out_shape,
        scratch_shapes=scratch,
        compiler_params=_params(2), name="mixout_s5" if has_s5 else "mixout",
    )(*ins)
    return [_unpairs(outs[0])] + list(outs[1:])


def _for_each_run(seg_ref, tab_ref, fn):
    for g in range(N_GROUPS):
        lstart = tab_ref[0, g, TAB_START]
        first = seg_ref[g] + tab_ref[0, g, TAB_CARRY]
        n_pieces = lax.shift_right_logical(tab_ref[0, g, TAB_COUNT] + (RUN - 1), RUN_SHIFT)

        def piece(k, carry, lstart=lstart, first=first):
            fn(pl.ds(pl.multiple_of(lstart + k * RUN, RUN), RUN), pl.ds(pl.multiple_of(first + k * RUN, RUN), RUN))
            return carry

        lax.fori_loop(0, n_pieces, piece, 0)


def _dispatch_kernel(seg_ref, cnt_ref, tab_ref, meta_ref, pay_ref, hs_hbm, s_ref, z_ref, sem):
    d = pay_ref.shape[1] - LANES
    lpos = meta_ref[0, 0:1, :]
    rows = lax.broadcasted_iota(I32, (LROWS, TR), 0)
    pick = jnp.where(rows == lpos, 1.0, 0.0).astype(BF16)
    s_ref[:, 0:d] = _dot(pick, pay_ref[:, 0:d].astype(BF16))
    ghi, glo = _split(pay_ref[:, d:d + LANES])
    s_ref[:, d:d + LANES] = _dot(pick, ghi) + _dot(pick, glo)

    copy = lambda src, dst: pltpu.make_async_copy(s_ref.at[src], hs_hbm.at[dst], sem)
    _for_each_run(seg_ref, tab_ref, lambda src, dst: copy(src, dst).start())
    _for_each_run(seg_ref, tab_ref, lambda src, dst: copy(src, dst).wait())

    @pl.when(pl.program_id(0) == pl.num_programs(0) - 1)
    def _():
        z_ref[...] = jnp.zeros_like(z_ref)
        for op in ("start", "wait"):
            for g in range(N_GROUPS + 1):
                begin = seg_ref[g] + cnt_ref[g]

                def piece(k, carry, begin=begin):
                    dst = hs_hbm.at[pl.ds(pl.multiple_of(begin + k * RUN, RUN), RUN)]
                    getattr(pltpu.make_async_copy(z_ref, dst, sem), op)()
                    return carry

                lax.fori_loop(0, lax.shift_right_logical(seg_ref[g + 1] - begin, RUN_SHIFT), piece, 0)


def _dispatch(seg, cnt, tab, meta, pay, n_sorted):
    ntile = meta.shape[0]
    width = pay.shape[1]
    grid_spec = pltpu.PrefetchScalarGridSpec(
        num_scalar_prefetch=2, grid=(ntile,),
        in_specs=[pl.BlockSpec((1, N_GROUPS, LANES), lambda i, seg, cnt: (i, 0, 0), memory_space=pltpu.SMEM),
                  pl.BlockSpec((1, SUBLANES, TR), lambda i, seg, cnt: (i, 0, 0)),
                  pl.BlockSpec((TR, width), lambda i, seg, cnt: (i, 0))],
        out_specs=pl.BlockSpec(memory_space=pl.ANY),
        scratch_shapes=[pltpu.VMEM((LROWS, width), F32), pltpu.VMEM((RUN, width), F32), pltpu.SemaphoreType.DMA(())])
    return pl.pallas_call(
        _dispatch_kernel, grid_spec=grid_spec,
        out_shape=jax.ShapeDtypeStruct((n_sorted, width), F32),
        compiler_params=_params(1), name="moe_dispatch",
    )(seg, cnt, tab, meta, pay)


def _moe_kernel(tg_ref, nv_ref, hs_ref, wg_ref, wu_ref, wd_ref, ys_ref, wgb, wub, wdb):
    i = pl.program_id(0)
    valid = i < nv_ref[0]

    @pl.when(valid & ((i == 0) | (tg_ref[i] != tg_ref[jnp.maximum(i - 1, 0)])))
    def _():
        wgb[...] = wg_ref[...].astype(BF16)
        wub[...] = wu_ref[...].astype(BF16)
        wdb[...] = wd_ref[...].astype(BF16)

    @pl.when(valid)
    def _():
        d = wg_ref.shape[1]
        h = hs_ref[:, 0:d].astype(BF16)
        gate = hs_ref[:, d:d + LANES]
        acc = jnp.zeros((TR, d), F32)
        for p in range(EXPERTS_PER_GROUP):
            act = _silu(_dot(h, wgb[p])) * _dot(h, wub[p]) * gate[:, p:p + 1]
            acc = acc + _dot(act.astype(BF16), wdb[p])
        ys_ref[...] = acc

    @pl.when(jnp.logical_not(valid))
    def _():
        ys_ref[...] = jnp.zeros_like(ys_ref)


def _moe(tile_group, n_valid, hs, layer, wg, wu, wd):
    n_sorted, pay_w = hs.shape
    ntile = n_sorted // TR
    d = wg.shape[2]
    epg = EXPERTS_PER_GROUP
    clamp = lambda i, tg, nv: jnp.minimum(i, nv[0] - 1)
    wspec = lambda a: pl.BlockSpec((None, epg) + a.shape[2:], lambda i, tg, nv: (layer, tg[i], 0, 0))
    grid_spec = pltpu.PrefetchScalarGridSpec(
        num_scalar_prefetch=2, grid=(ntile,),
        in_specs=[pl.BlockSpec((TR, pay_w), lambda i, tg, nv: (clamp(i, tg, nv), 0)), wspec(wg), wspec(wu), wspec(wd)],
        out_specs=pl.BlockSpec((TR, d), lambda i, tg, nv: (i, 0)),
        scratch_shapes=[pltpu.VMEM((epg,) + wg.shape[2:], BF16), pltpu.VMEM((epg,) + wu.shape[2:], BF16),
                        pltpu.VMEM((epg,) + wd.shape[2:], BF16)])
    return pl.pallas_call(
        _moe_kernel, grid_spec=grid_spec,
        out_shape=jax.ShapeDtypeStruct((n_sorted, d), F32),
        compiler_params=_params(1), name="moe_experts",
    )(tile_group, n_valid, hs, wg, wu, wd)


def _route_specs(bsz, nblk):
    last = bsz * nblk - 1
    cur = pl.BlockSpec((1, N_GROUPS, LANES), lambda b, j, seg: (b * nblk + j, 0, 0), memory_space=pltpu.SMEM)
    nxt = pl.BlockSpec((1, N_GROUPS, LANES), lambda b, j, seg: (jnp.minimum(b * nblk + j + 1, last), 0, 0),
                       memory_space=pltpu.SMEM)
    aux = pl.BlockSpec((TR, LANES), lambda b, j, seg: (b * nblk + j, 0))
    return [cur, nxt, aux]


def _gather_ffn(step, n_steps, seg_ref, tab_ref, tab_next_ref, aux_ref, ys_hbm, buf, sem):
    def runs(t_ref, slot, op):
        copy = lambda dst, src: pltpu.make_async_copy(ys_hbm.at[src], buf.at[slot, dst], sem.at[slot])
        _for_each_run(seg_ref, t_ref, lambda dst, src: getattr(copy(dst, src), op)())

    slot = lax.rem(step, 2)

    @pl.when(step == 0)
    def _():
        buf[...] = jnp.zeros_like(buf)
        runs(tab_ref, 0, "start")

    @pl.when(step + 1 < n_steps)
    def _():
        runs(tab_next_ref, 1 - slot, "start")

    runs(tab_ref, slot, "wait")
    lpos = aux_ref[:, AUX_LPOS:AUX_LPOS + 1].astype(I32)
    cols = lax.broadcasted_iota(I32, (TR, LROWS), 1)
    pick = jnp.where(cols == lpos, 1.0, 0.0).astype(BF16)
    hi, lo = _split(buf[slot])
    return _dot(pick, hi) + _dot(pick, lo)


def _inproj1_kernel(nctx_blk, nb, seg_ref, tab_ref, tab_next_ref, aux_ref, ys_hbm, x_ref, mod0_ref, mod1_ref, nw_ref,
                    w_ref, lb_ref, d1f_ref, d1b_ref, mf_ref, mb_ref,
                    x2_ref, qf_ref, kf_ref, qb_ref, kb_ref, v_ref, vt_ref, g_ref, cvf_ref, cvb_ref, buf, sem):
    bp, j = pl.program_id(0), pl.program_id(1)
    nblk = pl.num_programs(1)
    ffn = _gather_ffn(bp * nblk + j, pl.num_programs(0) * nblk, seg_ref, tab_ref, tab_next_ref, aux_ref, ys_hbm,
                      buf, sem)
    d = x_ref.shape[-1]
    hb = []
    for s in range(NSUB):
        mod0 = _mod_row(mod0_ref, j < nctx_blk, nb, bp * NSUB + s)
        mod1 = _mod_row(mod1_ref, j < nctx_blk, nb, bp * NSUB + s)
        x2 = x_ref[0, s] + mod0[:, 5 * d:6 * d] * ffn[s * TM:(s + 1) * TM]
        x2_ref[0, s] = x2
        hb.append((_rms(x2) * nw_ref[...] * (1.0 + mod1[:, d:2 * d]) + mod1[:, 0:d]).astype(BF16))
    hb = jnp.concatenate(hb, axis=0)
    proj = lambda n: _dot(hb, w_ref[:, n * d:(n + 1) * d])
    q = _silu(proj(0))
    lb = lb_ref[...]
    v = proj(3)
    g = proj(4)
    f_f = lb + (1.0 - lb) * _sigmoid(proj(1))
    f_b = lb + (1.0 - lb) * _sigmoid(proj(2))
    for s in range(NSUB):
        rows = slice(s * TM, (s + 1) * TM)
        v_ref[0, s] = v[rows].astype(BF16)
        vt_ref[0, s] = v[rows].T.astype(BF16)
        g_ref[0, s] = g[rows].astype(BF16)
        _gate_outputs(q[rows], 1.0 - f_f[rows], jnp.log(f_f[rows]), d1f_ref, mf_ref,
                      qf_ref.at[0, s], kf_ref.at[0, s], cvf_ref.at[0, s, 0])
        _gate_outputs(q[rows], 1.0 - f_b[rows], jnp.log(f_b[rows]), d1b_ref, mb_ref,
                      qb_ref.at[0, s], kb_ref.at[0, s], cvb_ref.at[0, s, 0])


def _inproj1(seg, tab, aux, ys, x1, mod0, mod1, nw, w_in, lb, consts, nctx_blk):
    bsz, nt, d = x1.shape
    nblk = nt // TM
    nbp = bsz // NSUB
    d1f, d1b, mf, mb = consts
    cs = lambda a: pl.BlockSpec(a.shape, lambda b, j, seg: (0,) * a.ndim)
    tok = lambda width: pl.BlockSpec((1, NSUB, TM, width), lambda b, j, seg: (b, 0, j, 0))
    cvspec = pl.BlockSpec((1, NSUB, 1, 3 * SUBLANES, d), lambda b, j, seg: (b, 0, j, 0, 0))
    grid_spec = pltpu.PrefetchScalarGridSpec(
        num_scalar_prefetch=1, grid=(nbp, nblk),
        in_specs=_route_specs(nbp, nblk) + [
                  pl.BlockSpec(memory_space=pl.ANY), tok(d), cs(mod0), cs(mod1), cs(nw),
                  pl.BlockSpec(w_in.shape, lambda b, j, seg: (0, 0), pipeline_mode=pl.Buffered(1)), cs(lb),
                  cs(d1f), cs(d1b), cs(mf), cs(mb)],
        out_specs=[tok(d)] * 6 + [pl.BlockSpec((1, NSUB, d, TM), lambda b, j, seg: (b, 0, 0, j)), tok(d), cvspec, cvspec],
        scratch_shapes=[pltpu.VMEM((2, LROWS, d), F32), pltpu.SemaphoreType.DMA((2,))])
    sds = jax.ShapeDtypeStruct
    out_shape = [sds((nbp, NSUB, nt, d), F32)] + [sds((nbp, NSUB, nt, d), BF16)] * 5 + [sds((nbp, NSUB, d, nt), BF16)] + \
                [sds((nbp, NSUB, nt, d), BF16)] + [sds((nbp, NSUB, nblk, 3 * SUBLANES, d), F32)] * 2
    outs = pl.pallas_call(
        functools.partial(_inproj1_kernel, nctx_blk, bsz), grid_spec=grid_spec, out_shape=out_shape,
        compiler_params=_params(2), name="combine_inproj_hgrn",
    )(seg, tab, tab, aux, ys, _pairs(x1), mod0, mod1, nw, w_in, lb, d1f, d1b, mf, mb)
    return [_unpairs(o) for o in outs]


def _final_kernel(seg_ref, tab_ref, tab_next_ref, aux_ref, ys_hbm, x_ref, mod_ref, fw_ref, o_ref, buf, sem):
    nblk = pl.num_programs(1)
    bp = pl.program_id(0)
    step = bp * nblk + pl.program_id(1)
    ffn = _gather_ffn(step, pl.num_programs(0) * nblk, seg_ref, tab_ref, tab_next_ref, aux_ref, ys_hbm, buf, sem)
    d = x_ref.shape[-1]
    for s in range(NSUB):
        mod = mod_ref[pl.ds(bp * NSUB + s, 1), :]
        o_ref[0, s] = _rms(x_ref[0, s] + mod[:, 5 * d:6 * d] * ffn[s * TM:(s + 1) * TM]) * fw_ref[...]


def _final(seg, tab, aux, ys, x3, mod, fw):
    bsz, seq, d = x3.shape
    nblk = seq // TM
    nbp = bsz // NSUB
    cs = lambda a: pl.BlockSpec(a.shape, lambda b, j, seg: (0,) * a.ndim)
    tok = pl.BlockSpec((1, NSUB, TM, d), lambda b, j, seg: (b, 0, j, 0))
    grid_spec = pltpu.PrefetchScalarGridSpec(
        num_scalar_prefetch=1, grid=(nbp, nblk),
        in_specs=_route_specs(nbp, nblk) + [pl.BlockSpec(memory_space=pl.ANY), tok, cs(mod), cs(fw)],
        out_specs=tok,
        scratch_shapes=[pltpu.VMEM((2, LROWS, d), F32), pltpu.SemaphoreType.DMA((2,))])
    out = pl.pallas_call(
        _final_kernel, grid_spec=grid_spec, out_shape=jax.ShapeDtypeStruct((nbp, NSUB, seq, d), F32),
        compiler_params=_params(2), name="combine_final_norm",
    )(seg, tab, tab, aux, ys, _pairs(x3), mod, fw)
    return _unpairs(out)


def _grid_sincos(n_tokens, dim):
    rows = n_tokens // GRID_W
    r, col = jnp.meshgrid(jnp.arange(rows, dtype=F32), jnp.arange(GRID_W, dtype=F32), indexing='ij')
    quarter = dim // 4
    omega = 1.0 / (10000.0 ** (jnp.arange(quarter, dtype=F32) / quarter))

    def emb(p):
        ang = p.reshape(-1, 1) * omega
        return jnp.concatenate([jnp.sin(ang), jnp.cos(ang)], axis=-1)

    return jnp.concatenate([emb(r), emb(col)], axis=-1)


def _pad_heads(w, n_heads):
    dk = w.shape[-1] // n_heads
    w = w.reshape(w.shape[:-1] + (n_heads, dk))
    w = jnp.pad(w, [(0, 0)] * (w.ndim - 1) + [(0, HEAD_W - dk)])
    return w.reshape(w.shape[:-2] + (n_heads * HEAD_W,))


def _router_tables(router_w, router_bias):
    n_exp = router_w.shape[1]
    epg = n_exp // N_GROUPS
    assert epg == EXPERTS_PER_GROUP
    perm = np.array([epg * g + p for p in range(epg) for g in range(N_GROUPS)])
    wt = jnp.pad(router_w.T[perm], ((0, LANES - n_exp), (0, 0)))
    hi = wt.astype(BF16)
    lo = (wt - hi.astype(F32)).astype(BF16)
    bias = jnp.pad(router_bias[perm], (0, LANES - n_exp))
    return hi, lo, jnp.broadcast_to(bias[:, None], (LANES, TR)).astype(F32)


def _segments(counts, n_tokens):
    cnt = counts[:, 0].astype(I32)
    tiles = (cnt + TR - 1) // TR
    ends = jnp.cumsum(tiles)
    ntile = (n_tokens + (n_tokens // TR) * N_GROUPS * (RUN - 1) + TR - 1) // TR + N_GROUPS
    n_valid = ends[-1]
    seg = jnp.concatenate([(ends - tiles) * TR, (n_valid * TR).reshape(1), jnp.full((1,), ntile * TR, I32)])
    tidx = jnp.minimum(jnp.arange(ntile, dtype=I32), n_valid - 1)
    tile_group = jnp.sum((tidx[:, None] >= ends[None, :]).astype(I32), axis=1)
    cnt = jnp.concatenate([cnt, jnp.zeros((1,), I32)])
    return seg.astype(I32), cnt, tile_group.astype(I32), n_valid.reshape(1).astype(I32), ntile * TR


def _moe_layer(tab, meta, counts, pay, weights):
    seg, cnt, tile_group, n_valid, n_sorted = _segments(counts, pay.shape[0])
    hs = _dispatch(seg, cnt, tab, meta, pay, n_sorted)
    ys = _moe(tile_group, n_valid, hs, *weights)
    return seg, ys


def kernel(x, c, ctx, c_ctx, ada_w, ada_b, norm_mix, norm_ffn, ab_w_in, ab_w_out, gla_a2, gla_ab, gla_norm, s5_lam_re, s5_lam_im, s5_log_dt, s5_b_re, s5_b_im, s5_c_re, s5_c_im, s5_d, s5_glu_w, s5_glu_b, hg_w_in, hg_w_out, hg_lb_logits, hg_norm, router_w, router_bias, moe_w_gate, moe_w_up, moe_w_down, final_norm):
    bsz, seq, d = x.shape
    nctx = ctx.shape[1]
    depth = ada_w.shape[0]
    assert depth == 2 and seq % TM == 0 and nctx % TM == 0 and bsz % SUBLANES == 0 and bsz < MOD_ROWS
    assert bsz % NSUB == 0
    nctx_blk = nctx // TM
    nt = nctx + seq
    nc = nt // S5_L
    assert (nc * bsz) % LANES == 0

    cond = jnp.zeros((MOD_ROWS, d), F32).at[:bsz].set(c).at[bsz].set(c_ctx)
    mod = _modulation(cond, ada_w, ada_b)
    consts = _chunk_matrices()
    row = lambda v: v.reshape(1, -1)

    qk = gla_a2.shape[-1]
    gv = ab_w_out.shape[1] // 2
    rank = gla_a2.shape[2]
    w_in = ab_w_in[0]
    o_v, o_g, o_a, o_u = 2 * qk, 2 * qk + gv, 2 * qk + 2 * gv, 2 * qk + 2 * gv + 2 * rank
    wm = jnp.concatenate([_pad_heads(w_in[:, 0:qk], GLA_HEADS), _pad_heads(w_in[:, qk:o_v], GLA_HEADS),
                          w_in[:, o_v:o_g], w_in[:, o_g:o_a], w_in[:, o_u:]], axis=1).astype(BF16)
    wa = jnp.pad(w_in[:, o_a:o_u], ((0, 0), (0, LANES - 2 * rank))).astype(BF16)
    a2p = _pad_heads(gla_a2[0], GLA_HEADS)
    a2 = jnp.zeros((LANES, 2 * gv), F32).at[0:rank, 0:gv].set(a2p[0]).at[rank:2 * rank, gv:].set(a2p[1]).astype(BF16)
    ab = _pad_heads(gla_ab[0], GLA_HEADS).reshape(1, 2 * gv)
    pos = _grid_sincos(seq, d)
    (xs, qf, kf, qb, kb, v, vt, g, u_cr, cvf, cvb) = _inproj0(x, ctx, pos, mod[0], row(norm_mix[0]), wm, wa, a2, ab, consts,
                                                               float(qk // GLA_HEADS) ** -0.5)
    o_f, o_b = _recurrence(qf, kf, qb, kb, v, vt, cvf, cvb, GLA_HEADS, nctx_blk)

    assert 2 * s5_b_re.shape[-1] == PAIR_W
    ops = _s5_operators(s5_lam_re[0], s5_lam_im[0], s5_log_dt[0], s5_b_re[0], s5_b_im[0], s5_c_re[0], s5_c_im[0], s5_d[0])
    y5 = _s5(u_cr, ops, nc, nctx // S5_L, bsz)

    rwh, rwl, rb = _router_tables(router_w, router_bias)
    utri = jnp.asarray(np.triu(np.ones((TR, TR), np.float32), 1))
    ones = jnp.ones((TR, TR), F32)
    x1, pay, aux, meta, tab, counts = _mixout(True, GLA_HEADS, nctx_blk, 0, xs, o_f, o_b, g, y5, mod[0],
                                              row(gla_norm[0]), s5_glu_w[0].astype(BF16), row(s5_glu_b[0]),
                                              ab_w_out[0].astype(BF16), row(norm_ffn[0]), rwh, rwl, rb, utri, ones)
    assert moe_w_gate.shape[1] == N_GROUPS * EXPERTS_PER_GROUP
    seg, ys = _moe_layer(tab, meta, counts, pay, (0, moe_w_gate, moe_w_up, moe_w_down))

    lb_all = jax.nn.softmax(hg_lb_logits.astype(F32), axis=0)
    lb_all = jnp.cumsum(lb_all, axis=0) - lb_all[0]
    (x2, qf, kf, qb, kb, v, vt, g, cvf, cvb) = _inproj1(seg, tab, aux, ys, x1, mod[0], mod[1], row(norm_mix[1]),
                                                          hg_w_in[0].astype(BF16), row(lb_all[1]), consts, nctx_blk)
    o_f, o_b = _recurrence(qf, kf, qb, kb, v, vt, cvf, cvb, HG_HEADS, nctx_blk)
    x3, pay, aux, meta, tab, counts = _mixout(False, HG_HEADS, nctx_blk, nctx_blk, x2, o_f, o_b, g, None, mod[1],
                                              row(hg_norm[0]), None, None, hg_w_out[0].astype(BF16),
                                              row(norm_ffn[1]), rwh, rwl, rb, utri, ones)
    seg, ys = _moe_layer(tab, meta, counts, pay, (1, moe_w_gate, moe_w_up, moe_w_down))
    return _final(seg, tab, aux, ys, x3, mod[1], row(final_norm))
```

```python
import functools
import math

import numpy as np
import jax
import jax.numpy as jnp
from jax import lax
from jax.experimental import pallas as pl
from jax.experimental.pallas import tpu as pltpu

F32, BF16, I32 = jnp.float32, jnp.bfloat16, jnp.int32

EPS = 1e-6
CHUNK = 64
GRID_W = 64
GLA_HEADS = 4
GLA_GATE_NORM = 16.0
HG_HEADS = 8
N_GROUPS = 8
EXPERTS_PER_GROUP = 4
TOP_K = 2

LANES = 128
SUBLANES = 8
TM = 256
CPT = TM // CHUNK
HEAD_W = 128
S5_L = 16
MOD_ROWS = 16
COL_BLK = 256
CSUM_ROWS = TM + 32
SCAN_UNROLL = 4
STACK_HEADS = 4
RUN = SUBLANES
RUN_SHIFT = 3
NSUB = 2
TR = NSUB * TM
LROWS = TR + N_GROUPS * RUN
TAB_START, TAB_COUNT, TAB_CARRY = 0, 1, 2
AUX_LPOS = 4
VMEM_LIMIT = 56 * 1024 * 1024


def _dot(a, b):
    return jnp.dot(a, b, preferred_element_type=F32)


def _dot_nt(a, b):
    return lax.dot_general(a, b, (((1,), (1,)), ((), ())), preferred_element_type=F32)


def _split(x):
    hi = x.astype(BF16)
    lo = (x - hi.astype(F32)).astype(BF16)
    return hi, lo


def _sigmoid(x):
    return 1.0 / (1.0 + jnp.exp(-x))


def _silu(x):
    return x * _sigmoid(x)


def _gelu_tanh(x):
    return 0.5 * x * (1.0 + jnp.tanh(math.sqrt(2.0 / math.pi) * (x + 0.044715 * (x * x * x))))


def _rms(x):
    return x * lax.rsqrt(jnp.mean(x * x, axis=-1, keepdims=True) + EPS)


def _head_rms(o, gain, n_heads):
    outs = []
    for h in range(n_heads):
        oh = o[:, h * HEAD_W:(h + 1) * HEAD_W]
        outs.append(_rms(oh) * gain)
    return jnp.concatenate(outs, axis=1)


def _pairs(a):
    return a.reshape((a.shape[0] // NSUB, NSUB) + a.shape[1:])


def _unpairs(a):
    return a.reshape((a.shape[0] * NSUB,) + a.shape[2:])


def _mod_row(mod_ref, is_ctx, ctx_row, batch_row):
    return mod_ref[pl.ds(jnp.where(is_ctx, ctx_row, batch_row), 1), :]


def _params(n_grid_dims, vmem=VMEM_LIMIT, flags=None):
    return pltpu.CompilerParams(dimension_semantics=("arbitrary",) * n_grid_dims, vmem_limit_bytes=vmem, flags=flags)


def _const_spec(shape):
    nd = len(shape)
    return pl.BlockSpec(shape, lambda *_: (0,) * nd)


def _mod_kernel(c_ref, w_ref, b_ref, o_ref):
    s = _silu(c_ref[...])
    shi, slo = _split(s)
    whi, wlo = _split(w_ref[0])
    o_ref[0] = _dot(shi, whi) + _dot(shi, wlo) + _dot(slo, whi) + b_ref[0]


def _modulation(cond, ada_w, ada_b):
    depth, d, six_d = ada_w.shape
    nchunk = six_d // d
    return pl.pallas_call(
        _mod_kernel,
        grid=(depth, nchunk),
        in_specs=[
            _const_spec((MOD_ROWS, d)),
            pl.BlockSpec((1, d, d), lambda l, n: (l, 0, n)),
            pl.BlockSpec((1, 1, d), lambda l, n: (l, 0, n)),
        ],
        out_specs=pl.BlockSpec((1, MOD_ROWS, d), lambda l, n: (l, 0, n)),
        out_shape=jax.ShapeDtypeStruct((depth, MOD_ROWS, six_d), F32),
        compiler_params=_params(2),
        name="modulation",
    )(cond, ada_w, ada_b.reshape(depth, 1, six_d))


def _chunk_matrices():
    t = np.arange(TM)
    c, l = t // CHUNK, t % CHUNK
    same = (c[:, None] == c[None, :]).astype(np.float32)
    li, ls = l[:, None], l[None, :]
    mid_f = CHUNK // 2 - 1
    mid_b = CHUNK // 2
    d1f = same * ((ls <= li).astype(np.float32) - (ls <= mid_f).astype(np.float32))
    d1b = same * ((ls >= li).astype(np.float32) - (ls >= mid_b).astype(np.float32))
    inchunk = (np.arange(SUBLANES)[:, None] == c[None, :]).astype(np.float32)
    mf = np.concatenate([inchunk * (l <= mid_f), inchunk * (l > mid_f), inchunk])
    mb = np.concatenate([inchunk * (l >= mid_b), inchunk * (l < mid_b), inchunk])
    pad = np.zeros((CSUM_ROWS - TM - 3 * SUBLANES, TM), np.float32)
    as_bf16 = lambda a: jnp.asarray(a, dtype=BF16)
    return as_bf16(np.concatenate([d1f, mf, pad])), as_bf16(np.concatenate([d1b, mb, pad]))


def _gate_outputs(q, k, la, csum_ref, q_out, k_out, cv_out):
    hi, lo = _split(la)
    op = csum_ref[...]
    sums = _dot(op, hi) + _dot(op, lo)
    d1 = sums[0:TM]
    q_out[...] = (q * jnp.exp(d1)).astype(BF16)
    k_out[...] = (k * jnp.exp(-d1)).astype(BF16)
    cv_out[...] = jnp.exp(sums[TM:TM + 3 * SUBLANES])


PAIR_W = 32
PAIRS_PER_TILE = LANES // PAIR_W
CROWS = TM // S5_L


def _to_chunk_rows(u, us_ref, ucr_ref):
    for q in range(us_ref.shape[0]):
        us_ref[q] = u[:, q * LANES:(q + 1) * LANES]
    lane = lax.broadcasted_iota(I32, (CROWS, LANES), 1)
    for p in range(ucr_ref.shape[0]):
        q, pp = divmod(p, PAIRS_PER_TILE)
        tiles = []
        for t in range(S5_L // PAIRS_PER_TILE):
            acc = None
            for jj in range(PAIRS_PER_TILE):
                piece = us_ref[q, pl.ds(t * PAIRS_PER_TILE + jj, CROWS, stride=S5_L), :]
                shift = (PAIR_W * (jj - pp)) % LANES
                if shift:
                    piece = pltpu.roll(piece, shift, axis=1)
                here = (lane >= PAIR_W * jj) & (lane < PAIR_W * (jj + 1))
                acc = piece if acc is None else jnp.where(here, piece, acc)
            tiles.append(acc)
        ucr_ref[p] = jnp.concatenate(tiles, axis=1).astype(BF16)


def _from_chunk_rows(ycr_ref, ys_ref):
    lane = lax.broadcasted_iota(I32, (CROWS, LANES), 1)
    for q in range(ycr_ref.shape[0] // PAIRS_PER_TILE):
        for i in range(S5_L):
            t, ii = divmod(i, PAIRS_PER_TILE)
            acc = None
            for pp in range(PAIRS_PER_TILE):
                piece = ycr_ref[q * PAIRS_PER_TILE + pp, :, t * LANES:(t + 1) * LANES].astype(F32)
                shift = (PAIR_W * (pp - ii)) % LANES
                if shift:
                    piece = pltpu.roll(piece, shift, axis=1)
                here = (lane >= PAIR_W * pp) & (lane < PAIR_W * (pp + 1))
                acc = piece if acc is None else jnp.where(here, piece, acc)
            ys_ref[q, pl.ds(i, CROWS, stride=S5_L), :] = acc
    return jnp.concatenate([ys_ref[q] for q in range(ys_ref.shape[0])], axis=1)


def _inproj0_kernel(nctx_blk, nb, gla_w, q_scale, x_ref, ctx_ref, pos_ref, mod_ref, nw_ref, wm_ref, wa_ref, a2_ref,
                    ab_ref, csf_ref, csb_ref,
                    xs_ref, qf_ref, kf_ref, qb_ref, kb_ref, v_ref, vt_ref, g_ref, ucr_ref, cvf_ref, cvb_ref, us_ref):
    bp, j = pl.program_id(0), pl.program_id(1)
    is_ctx = j < nctx_blk
    xin = jnp.where(is_ctx, ctx_ref[0], x_ref[0] + pos_ref[...])
    xs_ref[0] = xin
    d = xin.shape[-1]
    hb = []
    for s in range(NSUB):
        mod = _mod_row(mod_ref, is_ctx, nb, bp * NSUB + s)
        hb.append((_rms(xin[s]) * nw_ref[...] * (1.0 + mod[:, d:2 * d]) + mod[:, 0:d]).astype(BF16))
    hb = jnp.concatenate(hb, axis=0)
    w = gla_w
    u = _dot(hb, wm_ref[:, 4 * w:5 * w])
    for s in range(NSUB):
        _to_chunk_rows(u[s * TM:(s + 1) * TM], us_ref, ucr_ref.at[:, 0, s, 0])
    a = _dot(hb, wa_ref[...]).astype(BF16)
    for n in range(w // COL_BLK):
        cs = slice(n * COL_BLK, (n + 1) * COL_BLK)
        proj = lambda i: _dot(hb, wm_ref[:, i * w + n * COL_BLK:i * w + (n + 1) * COL_BLK])
        q = proj(0) * q_scale
        k = proj(1)
        v = proj(2)
        g = proj(3)
        las = []
        for di in range(2):
            zs = slice(di * w + n * COL_BLK, di * w + (n + 1) * COL_BLK)
            z = _dot(a, a2_ref[:, zs]) + ab_ref[:, zs]
            las.append((jnp.minimum(z, 0.0) - jnp.log(1.0 + jnp.exp(-jnp.abs(z)))) * (1.0 / GLA_GATE_NORM))
        for s in range(NSUB):
            rows = slice(s * TM, (s + 1) * TM)
            g_ref[0, s, :, cs] = g[rows].astype(BF16)
            v_ref[0, s, :, cs] = v[rows].astype(BF16)
            vt_ref[0, s, cs, :] = v[rows].T.astype(BF16)
            _gate_outputs(q[rows], k[rows], las[0][rows], csf_ref,
                          qf_ref.at[0, s, :, cs], kf_ref.at[0, s, :, cs], cvf_ref.at[0, s, 0, :, cs])
            _gate_outputs(q[rows], k[rows], las[1][rows], csb_ref,
                          qb_ref.at[0, s, :, cs], kb_ref.at[0, s, :, cs], cvb_ref.at[0, s, 0, :, cs])


def _inproj0(x, ctx, pos, mod, nw, wm, wa, a2, ab, consts, q_scale):
    bsz, seq, d = x.shape
    nctx = ctx.shape[1]
    nctx_blk, nlat_blk = nctx // TM, seq // TM
    nblk = nctx_blk + nlat_blk
    nt = nblk * TM
    nbp = bsz // NSUB
    w = wm.shape[1] // 5
    csf, csb = consts
    in_specs = [
        pl.BlockSpec((1, NSUB, TM, d), lambda b, j: (b, 0, jnp.maximum(j - nctx_blk, 0), 0)),
        pl.BlockSpec((1, NSUB, TM, d), lambda b, j: (b, 0, jnp.minimum(j, nctx_blk - 1), 0)),
        pl.BlockSpec((TM, d), lambda b, j: (jnp.maximum(j - nctx_blk, 0), 0)),
        _const_spec(mod.shape), _const_spec(nw.shape), _const_spec(wm.shape), _const_spec(wa.shape),
        _const_spec(a2.shape), _const_spec(ab.shape), _const_spec(csf.shape), _const_spec(csb.shape),
    ]
    tokspec = lambda width: pl.BlockSpec((1, NSUB, TM, width), lambda b, j: (b, 0, j, 0))
    cvspec = pl.BlockSpec((1, NSUB, 1, 3 * SUBLANES, w), lambda b, j: (b, 0, j, 0, 0))
    npair, crow_w = w // PAIR_W, S5_L * PAIR_W
    out_specs = [tokspec(d), tokspec(w), tokspec(w), tokspec(w), tokspec(w), tokspec(w),
                 pl.BlockSpec((1, NSUB, w, TM), lambda b, j: (b, 0, 0, j)), tokspec(w),
                 pl.BlockSpec((npair, 1, NSUB, 1, CROWS, crow_w), lambda b, j: (0, b, 0, j, 0, 0)), cvspec, cvspec]
    sds = jax.ShapeDtypeStruct
    out_shape = [sds((nbp, NSUB, nt, d), F32)] + [sds((nbp, NSUB, nt, w), BF16)] * 5 + [sds((nbp, NSUB, w, nt), BF16)] + \
                [sds((nbp, NSUB, nt, w), BF16), sds((npair, nbp, NSUB, nblk, CROWS, crow_w), BF16)] + \
                [sds((nbp, NSUB, nblk, 3 * SUBLANES, w), F32)] * 2
    outs = pl.pallas_call(
        functools.partial(_inproj0_kernel, nctx_blk, bsz, w, q_scale),
        grid=(nbp, nblk), in_specs=in_specs, out_specs=out_specs, out_shape=out_shape,
        scratch_shapes=[pltpu.VMEM((w // LANES, TM, LANES), F32)],
        compiler_params=_params(2), name="inproj_gla_s5",
    )(_pairs(x), _pairs(ctx), pos, mod, nw, wm, wa, a2, ab, csf, csb)
    ucr = outs[8].reshape(npair, bsz * nblk * CROWS, crow_w)
    return [_unpairs(o) for o in outs[:8]] + [ucr] + [_unpairs(o) for o in outs[9:]]


def _recur_kernel(n_heads, qf_ref, kf_ref, vf_ref, vtf_ref, cvf_ref, qb_ref, kb_ref, vb_ref, vtb_ref, cvb_ref,
                  of_ref, ob_ref, s_ref):
    @pl.when(pl.program_id(1) == 0)
    def _():
        s_ref[...] = jnp.zeros_like(s_ref)

    hrows = STACK_HEADS * CHUNK
    ri = lax.broadcasted_iota(I32, (hrows, hrows), 0)
    ci = lax.broadcasted_iota(I32, (hrows, hrows), 1)
    same_head = (ri // CHUNK) == (ci // CHUNK)
    keep = (same_head & (ci <= ri), same_head & (ci >= ri))
    pw = 2 * HEAD_W
    prow = lax.broadcasted_iota(I32, (2 * CHUNK, pw), 0)
    zero_blk = jnp.zeros((HEAD_W, HEAD_W), BF16)
    stack = lambda ref, r0, h_first: jnp.concatenate(
        [ref[0, r0:r0 + CHUNK, h * HEAD_W:(h + 1) * HEAD_W] for h in range(h_first, h_first + STACK_HEADS)], axis=0)
    dirs = ((qf_ref, kf_ref, vf_ref, vtf_ref, cvf_ref, of_ref), (qb_ref, kb_ref, vb_ref, vtb_ref, cvb_ref, ob_ref))
    for cc in range(CPT):
        for d, (q_ref, k_ref, v_ref, vt_ref, cv_ref, o_ref) in enumerate(dirs):
            c = cc if d == 0 else CPT - 1 - cc
            r0 = c * CHUNK
            p0 = (c // 2) * 2 * CHUNK
            in_chunk = (prow >= r0 - p0) & (prow < r0 - p0 + CHUNK)
            e_mid = cv_ref[0, 0, c:c + 1, :]
            e_rest = cv_ref[0, 0, SUBLANES + c:SUBLANES + c + 1, :]
            e_all = cv_ref[0, 0, 2 * SUBLANES + c:2 * SUBLANES + c + 1, :]
            o_intra = []
            for h_first in range(0, n_heads, STACK_HEADS):
                scores = _dot_nt(stack(q_ref, r0, h_first), stack(k_ref, r0, h_first))
                attn = jnp.where(keep[d], scores, 0.0).astype(BF16)
                o_intra.append(_dot(attn, stack(v_ref, r0, h_first)))
            o_intra = jnp.concatenate(o_intra, axis=0)
            for p in range(n_heads // 2):
                ps = slice(p * pw, (p + 1) * pw)
                h0, h1 = 2 * p, 2 * p + 1
                hs0, hs1 = slice(h0 * HEAD_W, h1 * HEAD_W), slice(h1 * HEAD_W, (h1 + 1) * HEAD_W)
                st0, st1 = s_ref[d, h0], s_ref[d, h1]
                sb0 = (st0 * e_mid[:, hs0]).astype(BF16)
                sb1 = (st1 * e_mid[:, hs1]).astype(BF16)
                carried = jnp.concatenate([jnp.concatenate([sb0, zero_blk], axis=1),
                                           jnp.concatenate([zero_blk, sb1], axis=1)], axis=0)
                o_inter = _dot_nt(q_ref[0, r0:r0 + CHUNK, ps], carried)
                o_ref[0, r0:r0 + CHUNK, hs0] = (o_inter[:, 0:HEAD_W] + o_intra[h0 * CHUNK:h1 * CHUNK]).astype(o_ref.dtype)
                o_ref[0, r0:r0 + CHUNK, hs1] = (o_inter[:, HEAD_W:pw]
                                                + o_intra[h1 * CHUNK:(h1 + 1) * CHUNK]).astype(o_ref.dtype)
                kp = k_ref[0, p0:p0 + 2 * CHUNK, ps]
                kp = jnp.where(in_chunk, kp, jnp.zeros_like(kp))
                ds = _dot(vt_ref[0, ps, p0:p0 + 2 * CHUNK], kp)
                s_ref[d, h0] = st0 * e_all[:, hs0] + ds[0:HEAD_W, 0:HEAD_W] * e_rest[:, hs0]
                s_ref[d, h1] = st1 * e_all[:, hs1] + ds[HEAD_W:pw, HEAD_W:pw] * e_rest[:, hs1]


def _recurrence(qf, kf, qb, kb, v, vt, cvf, cvb, n_heads, nctx_blk):
    bsz, nt, w = qf.shape
    nblk = nt // TM
    fwd = lambda j: j
    bwd = lambda j: jnp.where(j < nctx_blk, nctx_blk - 1 - j, nblk - 1 - (j - nctx_blk))
    specs = []
    for order in (fwd, bwd):
        tok = pl.BlockSpec((1, TM, w), lambda b, j, o=order: (b, o(j), 0))
        specs += [tok, tok, tok,
                  pl.BlockSpec((1, w, TM), lambda b, j, o=order: (b, 0, o(j))),
                  pl.BlockSpec((1, 1, 3 * SUBLANES, w), lambda b, j, o=order: (b, o(j), 0, 0))]
    out_specs = [pl.BlockSpec((1, TM, w), lambda b, j: (b, j, 0)),
                 pl.BlockSpec((1, TM, w), lambda b, j: (b, bwd(j), 0))]
    return pl.pallas_call(
        functools.partial(_recur_kernel, n_heads),
        grid=(bsz, nblk), in_specs=specs, out_specs=out_specs,
        out_shape=[jax.ShapeDtypeStruct((bsz, nt, w), BF16)] * 2,
        scratch_shapes=[pltpu.VMEM((2, n_heads, HEAD_W, HEAD_W), F32)],
        compiler_params=_params(2), name=f"recurrence_h{n_heads}",
    )(qf, kf, v, vt, cvf, qb, kb, v, vt, cvb)


def _s5_operators(lam_re, lam_im, log_dt, b_re, b_im, c_re, c_im, d_skip):
    L = S5_L
    ndir, ng, p = lam_re.shape
    hs = b_re.shape[-1]
    dt = jnp.exp(log_dt)[..., None]
    lr, li = lam_re * dt, lam_im * dt
    mag = jnp.exp(lr)
    a_re, a_im = mag * jnp.cos(li), mag * jnp.sin(li)
    den = lam_re * lam_re + lam_im * lam_im
    f_re = ((a_re - 1.0) * lam_re + a_im * lam_im) / den
    f_im = (a_im * lam_re - (a_re - 1.0) * lam_im) / den
    bb_re = f_re[..., None] * b_re - f_im[..., None] * b_im
    bb_im = f_re[..., None] * b_im + f_im[..., None] * b_re
    m = jnp.arange(L + 1, dtype=F32)[:, None, None, None]
    pw_mag = jnp.exp(m * lr[None])
    pw_re, pw_im = pw_mag * jnp.cos(m * li[None]), pw_mag * jnp.sin(m * li[None])
    eye = jnp.eye(2, dtype=F32)
    npair, pw, cw = ng // 2, 2 * hs, 2 * L * hs

    toep = jnp.zeros((npair, cw, cw), F32)
    wz_p, vo_p = [], []
    for di in range(ndir):
        pr, pi = pw_re[:L, di, :, :, None], pw_im[:L, di, :, :, None]
        ab_re = pr * bb_re[di][None] - pi * bb_im[di][None]
        ab_im = pr * bb_im[di][None] + pi * bb_re[di][None]
        kf = jnp.einsum('mgnk,ghn->gkmh', ab_re, c_re[di]) - jnp.einsum('mgnk,ghn->gkmh', ab_im, c_im[di])
        if di == 1:
            kf = kf[:, :, ::-1, :]
        kp = jnp.einsum('pskmh,st->pskmth', kf.reshape(npair, 2, hs, L, hs), eye).reshape(npair, pw, cw)
        if di == 0:
            kpad = jnp.pad(kp, ((0, 0), (0, 0), (cw, 0)))
            blocks = [kpad[:, :, cw - pw * j:2 * cw - pw * j] for j in range(L)]
        else:
            kpad = jnp.pad(kp, ((0, 0), (0, 0), (0, cw)))
            blocks = [kpad[:, :, pw * (L - 1 - j):pw * (L - 1 - j) + cw] for j in range(L)]
        toep = toep + jnp.stack(blocks, axis=1).reshape(npair, cw, cw)
        sel_i = (lambda a: a[::-1]) if di == 0 else (lambda a: a)
        pack_w = lambda a: jnp.einsum('jpsnk,st->pjsktn', sel_i(a).reshape(L, npair, 2, p, hs), eye).reshape(npair, cw, 2 * p)
        wz_p.append(jnp.concatenate([pack_w(ab_re), pack_w(ab_im)], axis=2))
        qr, qi = pw_re[1:, di, :, None, :], pw_im[1:, di, :, None, :]
        ca_re = c_re[di][None] * qr - c_im[di][None] * qi
        ca_im = c_re[di][None] * qi + c_im[di][None] * qr
        sel_o = (lambda a: a) if di == 0 else (lambda a: a[::-1])
        pack_v = lambda a: jnp.einsum('ipshn,st->psnith', sel_o(a).reshape(L, npair, 2, hs, p), eye).reshape(npair, 2 * p, cw)
        vo_p.append(jnp.concatenate([pack_v(ca_re), -pack_v(ca_im)], axis=1))
    wz_p, vo_p = jnp.stack(wz_p, axis=1), jnp.stack(vo_p, axis=1)
    al = jnp.stack([pw_re[L], pw_im[L]], axis=1)
    al_p = jnp.transpose(al.reshape(ndir, 2, npair, 2 * p), (2, 0, 1, 3))
    d_p = jnp.tile(d_skip.reshape(npair, 1, pw), (1, L, 1)).reshape(npair, 1, cw)
    return wz_p.astype(BF16), vo_p.astype(BF16), toep.astype(BF16), al_p, d_p


def _s5_kernel(nc, ncc, bsz, u_ref, wz_ref, vo_ref, tp_ref, al_ref, d_ref, y_ref, zf, zb, xf, xb):
    rows_total = nc * bsz
    rb = LANES
    half = zf.shape[2]

    def z_body(i, carry):
        rows = pl.ds(pl.multiple_of(i * rb, rb), rb)
        ub = u_ref[0, rows, :]
        for z, di in ((zf, 0), (zb, 1)):
            zz = _dot(ub, wz_ref[0, di])
            z[0, rows, :] = zz[:, 0:half]
            z[1, rows, :] = zz[:, half:2 * half]
        return carry

    lax.fori_loop(0, rows_total // rb, z_body, 0)

    coef = [[jnp.broadcast_to(al_ref[0, di, ri:ri + 1, :], (bsz, half)) for ri in range(2)] for di in range(2)]
    orders = (lambda i: i, lambda i: jnp.where(i < ncc, ncc - 1 - i, nc - 1 - (i - ncc)))

    def scan_body(i, carry):
        new = []
        for di, (z, xs) in enumerate(((zf, xf), (zb, xb))):
            x_re, x_im = carry[2 * di], carry[2 * di + 1]
            a_re, a_im = coef[di]
            rows = pl.ds(orders[di](i), bsz, stride=nc)
            xs[0, rows, :] = x_re
            xs[1, rows, :] = x_im
            new += [a_re * x_re - a_im * x_im + z[0, rows, :], a_re * x_im + a_im * x_re + z[1, rows, :]]
        return tuple(new)

    zero = jnp.zeros((bsz, half), F32)
    lax.fori_loop(0, nc, scan_body, (zero,) * 4, unroll=SCAN_UNROLL)

    def y_body(i, carry):
        rows = pl.ds(pl.multiple_of(i * rb, rb), rb)
        ub = u_ref[0, rows, :]
        state = lambda xs: jnp.concatenate([xs[0, rows, :], xs[1, rows, :]], axis=1).astype(BF16)
        y = _dot(state(xf), vo_ref[0, 0]) + _dot(state(xb), vo_ref[0, 1])
        y_ref[0, rows, :] = (y + _dot(ub, tp_ref[0]) + ub.astype(F32) * d_ref[0]).astype(y_ref.dtype)
        return carry

    lax.fori_loop(0, rows_total // rb, y_body, 0)


def _s5(u_cr, ops, nc, ncc, bsz):
    wz_p, vo_p, toep_p, al_p, d_p = ops
    npair, rows, width = u_cr.shape
    per_pair = lambda a: pl.BlockSpec((1,) + a.shape[1:], lambda p: (p,) + (0,) * (a.ndim - 1))
    return pl.pallas_call(
        functools.partial(_s5_kernel, nc, ncc, bsz),
        grid=(npair,),
        in_specs=[per_pair(u_cr), per_pair(wz_p), per_pair(vo_p), per_pair(toep_p), per_pair(al_p), per_pair(d_p)],
        out_specs=per_pair(u_cr),
        out_shape=jax.ShapeDtypeStruct(u_cr.shape, BF16),
        scratch_shapes=[pltpu.VMEM((2, rows, wz_p.shape[-1] // 2), F32)] * 4,
        compiler_params=_params(1), name="s5",
    )(u_cr, wz_p, vo_p, toep_p, al_p, d_p)


def _route(h2, rwh_ref, rwl_ref, rb_ref, utri_ref, ones_ref, carry_ref):
    hi, lo = _split(h2)
    logits = _dot_nt(rwh_ref[...], hi) + _dot_nt(rwl_ref[...], hi) + _dot_nt(rwh_ref[...], lo)
    aff = _sigmoid(logits)
    sel = aff + rb_ref[...]
    epg = EXPERTS_PER_GROUP
    s = [sel[N_GROUPS * p:N_GROUPS * (p + 1), :] for p in range(epg)]
    a = [aff[N_GROUPS * p:N_GROUPS * (p + 1), :] for p in range(epg)]
    m1, n1 = jnp.maximum(s[0], s[1]), jnp.minimum(s[0], s[1])
    m2, n2 = jnp.maximum(s[2], s[3]), jnp.minimum(s[2], s[3])
    score = jnp.maximum(m1, m2) + jnp.maximum(jnp.minimum(m1, m2), jnp.maximum(n1, n2))
    gi = lax.broadcasted_iota(I32, score.shape, 0)
    best = jnp.max(score, axis=0, keepdims=True)
    gidx = jnp.min(jnp.where(score == best, gi, N_GROUPS), axis=0, keepdims=True)
    onehot = gi == gidx
    gates = []
    for p in range(epg):
        ahead = jnp.zeros_like(score)
        for q in range(epg):
            if q != p:
                beats = (s[q] >= s[p]) if q < p else (s[q] > s[p])
                ahead = ahead + jnp.where(beats, 1.0, 0.0)
        picked = jnp.where(onehot, jnp.where(ahead < float(TOP_K) - 0.5, a[p], 0.0), 0.0)
        gates.append(jnp.sum(picked, axis=0, keepdims=True))
    den = gates[0] + gates[1] + gates[2] + gates[3]
    gates = [g / den for g in gates]
    oh = jnp.where(onehot, 1.0, 0.0)
    oh = jnp.concatenate([oh, jnp.zeros_like(oh)], axis=0).astype(BF16)
    before = _dot(oh, utri_ref[...])[0:N_GROUPS]
    count = _dot(oh, ones_ref[...])[0:N_GROUPS]
    run = jnp.floor((count + (RUN - 1.0)) * (1.0 / RUN)) * RUN
    carry = carry_ref[...]
    carry_ref[...] = carry + run
    starts, nxt = [], jnp.zeros_like(run[0:1])
    for g in range(N_GROUPS):
        starts.append(nxt)
        nxt = nxt + run[g:g + 1]
    lstart = jnp.concatenate(starts, axis=0)
    lpos = jnp.sum(jnp.where(onehot, lstart + before, 0.0), axis=0, keepdims=True)
    return lpos, gates, (lstart, count, carry)


def _mixout_kernel(has_s5, n_heads, nctx_blk, tok_off, nb, *refs):
    if has_s5:
        (x_ref, of_ref, ob_ref, g_ref, ycr_ref, mod_ref, gn_ref, gluw_ref, glub_ref, wo_ref, nf_ref,
         rwh_ref, rwl_ref, rb_ref, utri_ref, ones_ref,
         x1_ref, pay_ref, aux_ref, meta_ref, tab_ref, cnt_ref, carry_ref, ys_ref) = refs
    else:
        (x_ref, of_ref, ob_ref, g_ref, mod_ref, gn_ref, wo_ref, nf_ref,
         rwh_ref, rwl_ref, rb_ref, utri_ref, ones_ref,
         x1_ref, pay_ref, aux_ref, meta_ref, tab_ref, cnt_ref, carry_ref) = refs
    bp, j = pl.program_id(0), pl.program_id(1)

    @pl.when((bp == 0) & (j == 0))
    def _():
        carry_ref[...] = jnp.zeros_like(carry_ref)

    d = x_ref.shape[-1]
    merge = lambda ref: jnp.concatenate([ref[0, s] for s in range(NSUB)], axis=0)
    o = merge(of_ref).astype(F32) + merge(ob_ref).astype(F32)
    mixed = _head_rms(o, gn_ref[...], n_heads) * _silu(merge(g_ref).astype(F32))
    if has_s5:
        y5 = jnp.concatenate([_from_chunk_rows(ycr_ref.at[:, 0, s, 0], ys_ref) for s in range(NSUB)], axis=0)
        act = _gelu_tanh(y5)
        glu = act * _sigmoid(_dot(act.astype(BF16), gluw_ref[...]) + glub_ref[...])
        mixed = jnp.concatenate([mixed, glu], axis=1)
    proj = _dot(mixed.astype(BF16), wo_ref[...])
    h2 = []
    for s in range(NSUB):
        mod = _mod_row(mod_ref, j + tok_off < nctx_blk, nb, bp * NSUB + s)
        x1 = x_ref[0, s] + mod[:, 2 * d:3 * d] * proj[s * TM:(s + 1) * TM]
        x1_ref[0, s] = x1
        h2.append(_rms(x1) * nf_ref[...] * (1.0 + mod[:, 4 * d:5 * d]) + mod[:, 3 * d:4 * d])
    h2 = jnp.concatenate(h2, axis=0)

    lpos, gates, (lstart, count, carry) = _route(h2, rwh_ref, rwl_ref, rb_ref, utri_ref, ones_ref, carry_ref)
    gi = lax.broadcasted_iota(I32, (SUBLANES, TR), 0)
    meta_ref[0] = jnp.where(gi == 0, lpos.astype(I32), 0)
    ti = lax.broadcasted_iota(I32, (N_GROUPS, LANES), 1)
    first = lambda a: a[:, 0:LANES].astype(I32)
    tab_ref[0] = jnp.where(ti == TAB_START, first(lstart),
                           jnp.where(ti == TAB_COUNT, first(count), jnp.where(ti == TAB_CARRY, first(carry), 0)))
    cnt_ref[...] = carry_ref[...]

    li = lax.broadcasted_iota(I32, (LANES, TR), 0)
    gt = jnp.where(li == AUX_LPOS, lpos, 0.0)
    for p, g in enumerate(gates):
        gt = gt + jnp.where(li == p, g, 0.0)
    aux = gt.T
    pay_ref[:, 0:d] = h2
    pay_ref[:, d:d + LANES] = aux
    aux_ref[...] = aux


def _mixout(has_s5, n_heads, nctx_blk, tok_off, x, of, ob, g, y5, mod, gn, gluw, glub, wo, nf, rwh, rwl, rb, utri, ones):
    bsz, _, d = x.shape
    w = of.shape[-1]
    nblk_all = of.shape[1] // TM
    nblk = nblk_all - tok_off
    nbp = bsz // NSUB
    tok = lambda width: pl.BlockSpec((1, NSUB, TM, width), lambda b, j: (b, 0, j + tok_off, 0))
    ins, specs = [_pairs(x), _pairs(of), _pairs(ob), _pairs(g)], [tok(d), tok(w), tok(w), tok(w)]
    scratch = [pltpu.VMEM((N_GROUPS, TR), F32)]
    if has_s5:
        npair, _, crow_w = y5.shape
        ins.append(y5.reshape(npair, nbp, NSUB, nblk_all, CROWS, crow_w))
        specs.append(pl.BlockSpec((npair, 1, NSUB, 1, CROWS, crow_w), lambda b, j: (0, b, 0, j, 0, 0)))
        scratch.append(pltpu.VMEM((w // LANES, TM, LANES), F32))
    consts = [mod, gn] + ([gluw, glub] if has_s5 else []) + [wo, nf, rwh, rwl, rb, utri, ones]
    ins += consts
    specs += [_const_spec(c.shape) for c in consts]
    ntile = nbp * nblk
    pay_w = d + LANES
    lin = lambda b, j: b * nblk + j
    out_specs = [pl.BlockSpec((1, NSUB, TM, d), lambda b, j: (b, 0, j, 0)),
                 pl.BlockSpec((TR, pay_w), lambda b, j: (lin(b, j), 0)),
                 pl.BlockSpec((TR, LANES), lambda b, j: (lin(b, j), 0)),
                 pl.BlockSpec((1, SUBLANES, TR), lambda b, j: (lin(b, j), 0, 0)),
                 pl.BlockSpec((1, N_GROUPS, LANES), lambda b, j: (lin(b, j), 0, 0)),
                 _const_spec((N_GROUPS, TR))]
    sds = jax.ShapeDtypeStruct
    out_shape = [sds((nbp, NSUB, nblk * TM, d), F32), sds((ntile * TR, pay_w), F32), sds((ntile * TR, LANES), F32),
                 sds((ntile, SUBLANES, TR), I32), sds((ntile, N_GROUPS, LANES), I32), sds((N_GROUPS, TR), F32)]
    outs = pl.pallas_call(
        functools.partial(_mixout_kernel, has_s5, n_heads, nctx_blk, tok_off, bsz),
        grid=(nbp, nblk), in_specs=specs, out_specs=out_specs, out_shape=out_shape,
        scratch_shapes=scratch,
        compiler_params=_params(2), name="mixout_s5" if has_s5 else "mixout",
    )(*ins)
    return [_unpairs(outs[0])] + list(outs[1:])


def _for_each_run(seg_ref, tab_ref, fn):
    for g in range(N_GROUPS):
        lstart = tab_ref[0, g, TAB_START]
        first = seg_ref[g] + tab_ref[0, g, TAB_CARRY]
        n_pieces = lax.shift_right_logical(tab_ref[0, g, TAB_COUNT] + (RUN - 1), RUN_SHIFT)

        def piece(k, carry, lstart=lstart, first=first):
            fn(pl.ds(pl.multiple_of(lstart + k * RUN, RUN), RUN), pl.ds(pl.multiple_of(first + k * RUN, RUN), RUN))
            return carry

        lax.fori_loop(0, n_pieces, piece, 0)


def _dispatch_kernel(seg_ref, cnt_ref, tab_ref, tab_prev_ref, meta_ref, pay_ref, hs_hbm, s_ref, z_ref, sem, zsem):
    i = pl.program_id(0)
    last = pl.num_programs(0) - 1
    slot = lax.rem(i, 2)
    d = pay_ref.shape[1] - LANES
    lpos = meta_ref[0, 0:1, :]
    rows = lax.broadcasted_iota(I32, (LROWS, TR), 0)
    pick = jnp.where(rows == lpos, 1.0, 0.0).astype(BF16)
    s_ref[slot, :, 0:d] = _dot(pick, pay_ref[:, 0:d].astype(BF16))
    ghi, glo = _split(pay_ref[:, d:d + LANES])
    s_ref[slot, :, d:d + LANES] = _dot(pick, ghi) + _dot(pick, glo)

    def runs(t_ref, sl, op):
        copy = lambda src, dst: pltpu.make_async_copy(s_ref.at[sl, src], hs_hbm.at[dst], sem.at[sl])
        _for_each_run(seg_ref, t_ref, lambda src, dst: getattr(copy(src, dst), op)())

    runs(tab_ref, slot, "start")

    @pl.when(i > 0)
    def _():
        runs(tab_prev_ref, 1 - slot, "wait")

    @pl.when(i == last)
    def _():
        runs(tab_ref, slot, "wait")
        z_ref[...] = jnp.zeros_like(z_ref)
        for op in ("start", "wait"):
            for g in range(N_GROUPS + 1):
                begin = seg_ref[g] + cnt_ref[g]

                def piece(k, carry, begin=begin):
                    dst = hs_hbm.at[pl.ds(pl.multiple_of(begin + k * RUN, RUN), RUN)]
                    getattr(pltpu.make_async_copy(z_ref, dst, zsem), op)()
                    return carry

                lax.fori_loop(0, lax.shift_right_logical(seg_ref[g + 1] - begin, RUN_SHIFT), piece, 0)


def _dispatch(seg, cnt, tab, meta, pay, n_sorted):
    ntile = meta.shape[0]
    width = pay.shape[1]
    grid_spec = pltpu.PrefetchScalarGridSpec(
        num_scalar_prefetch=2, grid=(ntile,),
        in_specs=[pl.BlockSpec((1, N_GROUPS, LANES), lambda i, seg, cnt: (i, 0, 0), memory_space=pltpu.SMEM),
                  pl.BlockSpec((1, N_GROUPS, LANES), lambda i, seg, cnt: (jnp.maximum(i - 1, 0), 0, 0),
                               memory_space=pltpu.SMEM),
                  pl.BlockSpec((1, SUBLANES, TR), lambda i, seg, cnt: (i, 0, 0)),
                  pl.BlockSpec((TR, width), lambda i, seg, cnt: (i, 0))],
        out_specs=pl.BlockSpec(memory_space=pl.ANY),
        scratch_shapes=[pltpu.VMEM((2, LROWS, width), F32), pltpu.VMEM((RUN, width), F32),
                        pltpu.SemaphoreType.DMA((2,)), pltpu.SemaphoreType.DMA(())])
    return pl.pallas_call(
        _dispatch_kernel, grid_spec=grid_spec,
        out_shape=jax.ShapeDtypeStruct((n_sorted, width), F32),
        compiler_params=_params(1), name="moe_dispatch",
    )(seg, cnt, tab, tab, meta, pay)


def _moe_kernel(tg_ref, nv_ref, hs_ref, wg_ref, wu_ref, wd_ref, ys_ref, wgb, wub, wdb):
    i = pl.program_id(0)
    valid = i < nv_ref[0]

    @pl.when(valid & ((i == 0) | (tg_ref[i] != tg_ref[jnp.maximum(i - 1, 0)])))
    def _():
        wgb[...] = wg_ref[...].astype(BF16)
        wub[...] = wu_ref[...].astype(BF16)
        wdb[...] = wd_ref[...].astype(BF16)

    @pl.when(valid)
    def _():
        d = wg_ref.shape[1]
        h = hs_ref[:, 0:d].astype(BF16)
        gate = hs_ref[:, d:d + LANES]
        acc = jnp.zeros((TR, d), F32)
        for p in range(EXPERTS_PER_GROUP):
            act = _silu(_dot(h, wgb[p])) * _dot(h, wub[p]) * gate[:, p:p + 1]
            acc = acc + _dot(act.astype(BF16), wdb[p])
        ys_ref[...] = acc

    @pl.when(jnp.logical_not(valid))
    def _():
        ys_ref[...] = jnp.zeros_like(ys_ref)


def _moe(tile_group, n_valid, hs, layer, wg, wu, wd):
    n_sorted, pay_w = hs.shape
    ntile = n_sorted // TR
    d = wg.shape[2]
    epg = EXPERTS_PER_GROUP
    clamp = lambda i, tg, nv: jnp.minimum(i, nv[0] - 1)
    wspec = lambda a: pl.BlockSpec((None, epg) + a.shape[2:], lambda i, tg, nv: (layer, tg[i], 0, 0))
    grid_spec = pltpu.PrefetchScalarGridSpec(
        num_scalar_prefetch=2, grid=(ntile,),
        in_specs=[pl.BlockSpec((TR, pay_w), lambda i, tg, nv: (clamp(i, tg, nv), 0)), wspec(wg), wspec(wu), wspec(wd)],
        out_specs=pl.BlockSpec((TR, d), lambda i, tg, nv: (i, 0)),
        scratch_shapes=[pltpu.VMEM((epg,) + wg.shape[2:], BF16), pltpu.VMEM((epg,) + wu.shape[2:], BF16),
                        pltpu.VMEM((epg,) + wd.shape[2:], BF16)])
    return pl.pallas_call(
        _moe_kernel, grid_spec=grid_spec,
        out_shape=jax.ShapeDtypeStruct((n_sorted, d), F32),
        compiler_params=_params(1), name="moe_experts",
    )(tile_group, n_valid, hs, wg, wu, wd)


def _route_specs(bsz, nblk):
    last = bsz * nblk - 1
    cur = pl.BlockSpec((1, N_GROUPS, LANES), lambda b, j, seg: (b * nblk + j, 0, 0), memory_space=pltpu.SMEM)
    nxt = pl.BlockSpec((1, N_GROUPS, LANES), lambda b, j, seg: (jnp.minimum(b * nblk + j + 1, last), 0, 0),
                       memory_space=pltpu.SMEM)
    aux = pl.BlockSpec((TR, LANES), lambda b, j, seg: (b * nblk + j, 0))
    return [cur, nxt, aux]


def _gather_ffn(step, n_steps, seg_ref, tab_ref, tab_next_ref, aux_ref, ys_hbm, buf, sem):
    def runs(t_ref, slot, op):
        copy = lambda dst, src: pltpu.make_async_copy(ys_hbm.at[src], buf.at[slot, dst], sem.at[slot])
        _for_each_run(seg_ref, t_ref, lambda dst, src: getattr(copy(dst, src), op)())

    slot = lax.rem(step, 2)

    @pl.when(step == 0)
    def _():
        buf[...] = jnp.zeros_like(buf)
        runs(tab_ref, 0, "start")

    @pl.when(step + 1 < n_steps)
    def _():
        runs(tab_next_ref, 1 - slot, "start")

    runs(tab_ref, slot, "wait")
    lpos = aux_ref[:, AUX_LPOS:AUX_LPOS + 1].astype(I32)
    cols = lax.broadcasted_iota(I32, (TR, LROWS), 1)
    pick = jnp.where(cols == lpos, 1.0, 0.0).astype(BF16)
    return _dot(pick, buf[slot].astype(BF16))


def _inproj1_kernel(nctx_blk, nb, seg_ref, tab_ref, tab_next_ref, aux_ref, ys_hbm, x_ref, mod0_ref, mod1_ref, nw_ref,
                    w_ref, lb_ref, csf_ref, csb_ref,
                    x2_ref, qf_ref, kf_ref, qb_ref, kb_ref, v_ref, vt_ref, g_ref, cvf_ref, cvb_ref, buf, sem):
    bp, j = pl.program_id(0), pl.program_id(1)
    nblk = pl.num_programs(1)
    ffn = _gather_ffn(bp * nblk + j, pl.num_programs(0) * nblk, seg_ref, tab_ref, tab_next_ref, aux_ref, ys_hbm,
                      buf, sem)
    d = x_ref.shape[-1]
    hb = []
    for s in range(NSUB):
        mod0 = _mod_row(mod0_ref, j < nctx_blk, nb, bp * NSUB + s)
        mod1 = _mod_row(mod1_ref, j < nctx_blk, nb, bp * NSUB + s)
        x2 = x_ref[0, s] + mod0[:, 5 * d:6 * d] * ffn[s * TM:(s + 1) * TM]
        x2_ref[0, s] = x2
        hb.append((_rms(x2) * nw_ref[...] * (1.0 + mod1[:, d:2 * d]) + mod1[:, 0:d]).astype(BF16))
    hb = jnp.concatenate(hb, axis=0)
    proj = lambda n: _dot(hb, w_ref[:, n * d:(n + 1) * d])
    q = _silu(proj(0))
    lb = lb_ref[...]
    v = proj(3)
    g = proj(4)
    f_f = lb + (1.0 - lb) * _sigmoid(proj(1))
    f_b = lb + (1.0 - lb) * _sigmoid(proj(2))
    for s in range(NSUB):
        rows = slice(s * TM, (s + 1) * TM)
        v_ref[0, s] = v[rows].astype(BF16)
        vt_ref[0, s] = v[rows].T.astype(BF16)
        g_ref[0, s] = g[rows].astype(BF16)
        _gate_outputs(q[rows], 1.0 - f_f[rows], jnp.log(f_f[rows]), csf_ref,
                      qf_ref.at[0, s], kf_ref.at[0, s], cvf_ref.at[0, s, 0])
        _gate_outputs(q[rows], 1.0 - f_b[rows], jnp.log(f_b[rows]), csb_ref,
                      qb_ref.at[0, s], kb_ref.at[0, s], cvb_ref.at[0, s, 0])


def _inproj1(seg, tab, aux, ys, x1, mod0, mod1, nw, w_in, lb, consts, nctx_blk):
    bsz, nt, d = x1.shape
    nblk = nt // TM
    nbp = bsz // NSUB
    csf, csb = consts
    cs = lambda a: pl.BlockSpec(a.shape, lambda b, j, seg: (0,) * a.ndim)
    tok = lambda width: pl.BlockSpec((1, NSUB, TM, width), lambda b, j, seg: (b, 0, j, 0))
    cvspec = pl.BlockSpec((1, NSUB, 1, 3 * SUBLANES, d), lambda b, j, seg: (b, 0, j, 0, 0))
    grid_spec = pltpu.PrefetchScalarGridSpec(
        num_scalar_prefetch=1, grid=(nbp, nblk),
        in_specs=_route_specs(nbp, nblk) + [
                  pl.BlockSpec(memory_space=pl.ANY), tok(d), cs(mod0), cs(mod1), cs(nw),
                  pl.BlockSpec(w_in.shape, lambda b, j, seg: (0, 0), pipeline_mode=pl.Buffered(1)), cs(lb),
                  cs(csf), cs(csb)],
        out_specs=[tok(d)] * 6 + [pl.BlockSpec((1, NSUB, d, TM), lambda b, j, seg: (b, 0, 0, j)), tok(d), cvspec, cvspec],
        scratch_shapes=[pltpu.VMEM((2, LROWS, d), F32), pltpu.SemaphoreType.DMA((2,))])
    sds = jax.ShapeDtypeStruct
    out_shape = [sds((nbp, NSUB, nt, d), F32)] + [sds((nbp, NSUB, nt, d), BF16)] * 5 + [sds((nbp, NSUB, d, nt), BF16)] + \
                [sds((nbp, NSUB, nt, d), BF16)] + [sds((nbp, NSUB, nblk, 3 * SUBLANES, d), F32)] * 2
    outs = pl.pallas_call(
        functools.partial(_inproj1_kernel, nctx_blk, bsz), grid_spec=grid_spec, out_shape=out_shape,
        compiler_params=_params(2), name="combine_inproj_hgrn",
    )(seg, tab, tab, aux, ys, _pairs(x1), mod0, mod1, nw, w_in, lb, csf, csb)
    return [_unpairs(o) for o in outs]


def _final_kernel(seg_ref, tab_ref, tab_next_ref, aux_ref, ys_hbm, x_ref, mod_ref, fw_ref, o_ref, buf, sem):
    nblk = pl.num_programs(1)
    bp = pl.program_id(0)
    step = bp * nblk + pl.program_id(1)
    ffn = _gather_ffn(step, pl.num_programs(0) * nblk, seg_ref, tab_ref, tab_next_ref, aux_ref, ys_hbm, buf, sem)
    d = x_ref.shape[-1]
    for s in range(NSUB):
        mod = mod_ref[pl.ds(bp * NSUB + s, 1), :]
        o_ref[0, s] = _rms(x_ref[0, s] + mod[:, 5 * d:6 * d] * ffn[s * TM:(s + 1) * TM]) * fw_ref[...]


def _final(seg, tab, aux, ys, x3, mod, fw):
    bsz, seq, d = x3.shape
    nblk = seq // TM
    nbp = bsz // NSUB
    cs = lambda a: pl.BlockSpec(a.shape, lambda b, j, seg: (0,) * a.ndim)
    tok = pl.BlockSpec((1, NSUB, TM, d), lambda b, j, seg: (b, 0, j, 0))
    grid_spec = pltpu.PrefetchScalarGridSpec(
        num_scalar_prefetch=1, grid=(nbp, nblk),
        in_specs=_route_specs(nbp, nblk) + [pl.BlockSpec(memory_space=pl.ANY), tok, cs(mod), cs(fw)],
        out_specs=tok,
        scratch_shapes=[pltpu.VMEM((2, LROWS, d), F32), pltpu.SemaphoreType.DMA((2,))])
    out = pl.pallas_call(
        _final_kernel, grid_spec=grid_spec, out_shape=jax.ShapeDtypeStruct((nbp, NSUB, seq, d), F32),
        compiler_params=_params(2), name="combine_final_norm",
    )(seg, tab, tab, aux, ys, _pairs(x3), mod, fw)
    return _unpairs(out)


def _grid_sincos(n_tokens, dim):
    rows = n_tokens // GRID_W
    r, col = jnp.meshgrid(jnp.arange(rows, dtype=F32), jnp.arange(GRID_W, dtype=F32), indexing='ij')
    quarter = dim // 4
    omega = 1.0 / (10000.0 ** (jnp.arange(quarter, dtype=F32) / quarter))

    def emb(p):
        ang = p.reshape(-1, 1) * omega
        return jnp.concatenate([jnp.sin(ang), jnp.cos(ang)], axis=-1)

    return jnp.concatenate([emb(r), emb(col)], axis=-1)


def _pad_heads(w, n_heads):
    dk = w.shape[-1] // n_heads
    w = w.reshape(w.shape[:-1] + (n_heads, dk))
    w = jnp.pad(w, [(0, 0)] * (w.ndim - 1) + [(0, HEAD_W - dk)])
    return w.reshape(w.shape[:-2] + (n_heads * HEAD_W,))


def _router_tables(router_w, router_bias):
    n_exp = router_w.shape[1]
    epg = n_exp // N_GROUPS
    assert epg == EXPERTS_PER_GROUP
    perm = np.array([epg * g + p for p in range(epg) for g in range(N_GROUPS)])
    wt = jnp.pad(router_w.T[perm], ((0, LANES - n_exp), (0, 0)))
    hi = wt.astype(BF16)
    lo = (wt - hi.astype(F32)).astype(BF16)
    bias = jnp.pad(router_bias[perm], (0, LANES - n_exp))
    return hi, lo, jnp.broadcast_to(bias[:, None], (LANES, TR)).astype(F32)


def _segments(counts, n_tokens):
    cnt = counts[:, 0].astype(I32)
    tiles = (cnt + TR - 1) // TR
    ends = jnp.cumsum(tiles)
    ntile = (n_tokens + (n_tokens // TR) * N_GROUPS * (RUN - 1) + TR - 1) // TR + N_GROUPS
    n_valid = ends[-1]
    seg = jnp.concatenate([(ends - tiles) * TR, (n_valid * TR).reshape(1), jnp.full((1,), ntile * TR, I32)])
    tidx = jnp.minimum(jnp.arange(ntile, dtype=I32), n_valid - 1)
    tile_group = jnp.sum((tidx[:, None] >= ends[None, :]).astype(I32), axis=1)
    cnt = jnp.concatenate([cnt, jnp.zeros((1,), I32)])
    return seg.astype(I32), cnt, tile_group.astype(I32), n_valid.reshape(1).astype(I32), ntile * TR


def _moe_layer(tab, meta, counts, pay, weights):
    seg, cnt, tile_group, n_valid, n_sorted = _segments(counts, pay.shape[0])
    hs = _dispatch(seg, cnt, tab, meta, pay, n_sorted)
    ys = _moe(tile_group, n_valid, hs, *weights)
    return seg, ys


def kernel(x, c, ctx, c_ctx, ada_w, ada_b, norm_mix, norm_ffn, ab_w_in, ab_w_out, gla_a2, gla_ab, gla_norm, s5_lam_re, s5_lam_im, s5_log_dt, s5_b_re, s5_b_im, s5_c_re, s5_c_im, s5_d, s5_glu_w, s5_glu_b, hg_w_in, hg_w_out, hg_lb_logits, hg_norm, router_w, router_bias, moe_w_gate, moe_w_up, moe_w_down, final_norm):
    bsz, seq, d = x.shape
    nctx = ctx.shape[1]
    depth = ada_w.shape[0]
    assert depth == 2 and seq % TM == 0 and nctx % TM == 0 and bsz % SUBLANES == 0 and bsz < MOD_ROWS
    assert bsz % NSUB == 0
    nctx_blk = nctx // TM
    nt = nctx + seq
    nc = nt // S5_L
    assert (nc * bsz) % LANES == 0

    cond = jnp.zeros((MOD_ROWS, d), F32).at[:bsz].set(c).at[bsz].set(c_ctx)
    mod = _modulation(cond, ada_w, ada_b)
    consts = _chunk_matrices()
    row = lambda v: v.reshape(1, -1)

    qk = gla_a2.shape[-1]
    gv = ab_w_out.shape[1] // 2
    rank = gla_a2.shape[2]
    w_in = ab_w_in[0]
    o_v, o_g, o_a, o_u = 2 * qk, 2 * qk + gv, 2 * qk + 2 * gv, 2 * qk + 2 * gv + 2 * rank
    wm = jnp.concatenate([_pad_heads(w_in[:, 0:qk], GLA_HEADS), _pad_heads(w_in[:, qk:o_v], GLA_HEADS),
                          w_in[:, o_v:o_g], w_in[:, o_g:o_a], w_in[:, o_u:]], axis=1).astype(BF16)
    wa = jnp.pad(w_in[:, o_a:o_u], ((0, 0), (0, LANES - 2 * rank))).astype(BF16)
    a2p = _pad_heads(gla_a2[0], GLA_HEADS)
    a2 = jnp.zeros((LANES, 2 * gv), F32).at[0:rank, 0:gv].set(a2p[0]).at[rank:2 * rank, gv:].set(a2p[1]).astype(BF16)
    ab = _pad_heads(gla_ab[0], GLA_HEADS).reshape(1, 2 * gv)
    pos = _grid_sincos(seq, d)
    (xs, qf, kf, qb, kb, v, vt, g, u_cr, cvf, cvb) = _inproj0(x, ctx, pos, mod[0], row(norm_mix[0]), wm, wa, a2, ab, consts,
                                                               float(qk // GLA_HEADS) ** -0.5)
    o_f, o_b = _recurrence(qf, kf, qb, kb, v, vt, cvf, cvb, GLA_HEADS, nctx_blk)

    assert 2 * s5_b_re.shape[-1] == PAIR_W
    ops = _s5_operators(s5_lam_re[0], s5_lam_im[0], s5_log_dt[0], s5_b_re[0], s5_b_im[0], s5_c_re[0], s5_c_im[0], s5_d[0])
    y5 = _s5(u_cr, ops, nc, nctx // S5_L, bsz)

    rwh, rwl, rb = _router_tables(router_w, router_bias)
    utri = jnp.asarray(np.triu(np.ones((TR, TR), np.float32), 1), dtype=BF16)
    ones = jnp.ones((TR, TR), BF16)
    x1, pay, aux, meta, tab, counts = _mixout(True, GLA_HEADS, nctx_blk, 0, xs, o_f, o_b, g, y5, mod[0],
                                              row(gla_norm[0]), s5_glu_w[0].astype(BF16), row(s5_glu_b[0]),
                                              ab_w_out[0].astype(BF16), row(norm_ffn[0]), rwh, rwl, rb, utri, ones)
    assert moe_w_gate.shape[1] == N_GROUPS * EXPERTS_PER_GROUP
    seg, ys = _moe_layer(tab, meta, counts, pay, (0, moe_w_gate, moe_w_up, moe_w_down))

    lb_all = jax.nn.softmax(hg_lb_logits.astype(F32), axis=0)
    lb_all = jnp.cumsum(lb_all, axis=0) - lb_all[0]
    (x2, qf, kf, qb, kb, v, vt, g, cvf, cvb) = _inproj1(seg, tab, aux, ys, x1, mod[0], mod[1], row(norm_mix[1]),
                                                          hg_w_in[0].astype(BF16), row(lb_all[1]), consts, nctx_blk)
    o_f, o_b = _recurrence(qf, kf, qb, kb, v, vt, cvf, cvb, HG_HEADS, nctx_blk)
    x3, pay, aux, meta, tab, counts = _mixout(False, HG_HEADS, nctx_blk, nctx_blk, x2, o_f, o_b, g, None, mod[1],
                                              row(hg_norm[0]), None, None, hg_w_out[0].astype(BF16),
                                              row(norm_ffn[1]), rwh, rwl, rb, utri, ones)
    seg, ys = _moe_layer(tab, meta, counts, pay, (1, moe_w_gate, moe_w_up, moe_w_down))
    return _final(seg, tab, aux, ys, x3, mod[1], row(final_norm))
```

```python
import functools
import math

import numpy as np
import jax
import jax.numpy as jnp
from jax import lax
from jax.experimental import pallas as pl
from jax.experimental.pallas import tpu as pltpu

F32, BF16, I32 = jnp.float32, jnp.bfloat16, jnp.int32

EPS = 1e-6
CHUNK = 64
GRID_W = 64
GLA_HEADS = 4
GLA_GATE_NORM = 16.0
HG_HEADS = 8
N_GROUPS = 8
EXPERTS_PER_GROUP = 4
TOP_K = 2

LANES = 128
SUBLANES = 8
TM = 256
CPT = TM // CHUNK
HEAD_W = 128
S5_L = 16
MOD_ROWS = 16
COL_BLK = 256
CSUM_ROWS = TM + 32
S5_MAX_ROWS = 1024
SCAN_UNROLL = 4
STACK_HEADS = 4
RUN = SUBLANES
RUN_SHIFT = 3
NSUB = 2
TR = NSUB * TM
LROWS = TR + N_GROUPS * RUN
TAB_START, TAB_COUNT, TAB_CARRY = 0, 1, 2
AUX_LPOS = 4
VMEM_LIMIT = 56 * 1024 * 1024


def _dot(a, b):
    return jnp.dot(a, b, preferred_element_type=F32)


def _dot_nt(a, b):
    return lax.dot_general(a, b, (((1,), (1,)), ((), ())), preferred_element_type=F32)


def _split(x):
    hi = x.astype(BF16)
    lo = (x - hi.astype(F32)).astype(BF16)
    return hi, lo


def _sigmoid(x):
    return 1.0 / (1.0 + jnp.exp(-x))


def _silu(x):
    return x * _sigmoid(x)


def _gelu_tanh(x):
    return 0.5 * x * (1.0 + jnp.tanh(math.sqrt(2.0 / math.pi) * (x + 0.044715 * (x * x * x))))


def _rms(x):
    return x * lax.rsqrt(jnp.mean(x * x, axis=-1, keepdims=True) + EPS)


def _head_rms(o, gain, n_heads):
    outs = []
    for h in range(n_heads):
        oh = o[:, h * HEAD_W:(h + 1) * HEAD_W]
        outs.append(_rms(oh) * gain)
    return jnp.concatenate(outs, axis=1)


def _pairs(a):
    return a.reshape((a.shape[0] // NSUB, NSUB) + a.shape[1:])


def _unpairs(a):
    return a.reshape((a.shape[0] * NSUB,) + a.shape[2:])


def _mod_row(mod_ref, is_ctx, ctx_row, batch_row):
    return mod_ref[pl.ds(jnp.where(is_ctx, ctx_row, batch_row), 1), :]


def _params(n_grid_dims, vmem=VMEM_LIMIT, flags=None):
    return pltpu.CompilerParams(dimension_semantics=("arbitrary",) * n_grid_dims, vmem_limit_bytes=vmem, flags=flags)


def _const_spec(shape):
    nd = len(shape)
    return pl.BlockSpec(shape, lambda *_: (0,) * nd)


def _mod_kernel(c_ref, w_ref, b_ref, o_ref):
    s = _silu(c_ref[...])
    shi, slo = _split(s)
    whi, wlo = _split(w_ref[0])
    o_ref[0] = _dot(shi, whi) + _dot(shi, wlo) + _dot(slo, whi) + b_ref[0]


def _modulation(cond, ada_w, ada_b):
    depth, d, six_d = ada_w.shape
    nchunk = six_d // d
    return pl.pallas_call(
        _mod_kernel,
        grid=(depth, nchunk),
        in_specs=[
            _const_spec((MOD_ROWS, d)),
            pl.BlockSpec((1, d, d), lambda l, n: (l, 0, n)),
            pl.BlockSpec((1, 1, d), lambda l, n: (l, 0, n)),
        ],
        out_specs=pl.BlockSpec((1, MOD_ROWS, d), lambda l, n: (l, 0, n)),
        out_shape=jax.ShapeDtypeStruct((depth, MOD_ROWS, six_d), F32),
        compiler_params=_params(2),
        name="modulation",
    )(cond, ada_w, ada_b.reshape(depth, 1, six_d))


def _chunk_matrices():
    t = np.arange(TM)
    c, l = t // CHUNK, t % CHUNK
    same = (c[:, None] == c[None, :]).astype(np.float32)
    li, ls = l[:, None], l[None, :]
    mid_f = CHUNK // 2 - 1
    mid_b = CHUNK // 2
    d1f = same * ((ls <= li).astype(np.float32) - (ls <= mid_f).astype(np.float32))
    d1b = same * ((ls >= li).astype(np.float32) - (ls >= mid_b).astype(np.float32))
    inchunk = (np.arange(SUBLANES)[:, None] == c[None, :]).astype(np.float32)
    mf = np.concatenate([inchunk * (l <= mid_f), inchunk * (l > mid_f), inchunk])
    mb = np.concatenate([inchunk * (l >= mid_b), inchunk * (l < mid_b), inchunk])
    pad = np.zeros((CSUM_ROWS - TM - 3 * SUBLANES, TM), np.float32)
    as_bf16 = lambda a: jnp.asarray(a, dtype=BF16)
    return as_bf16(np.concatenate([d1f, mf, pad])), as_bf16(np.concatenate([d1b, mb, pad]))


def _gate_outputs(q, k, la, csum_ref, q_out, k_out, cv_out):
    hi, lo = _split(la)
    op = csum_ref[...]
    sums = _dot(op, hi) + _dot(op, lo)
    d1 = sums[0:TM]
    q_out[...] = (q * jnp.exp(d1)).astype(BF16)
    k_out[...] = (k * jnp.exp(-d1)).astype(BF16)
    cv_out[...] = jnp.exp(sums[TM:TM + 3 * SUBLANES])


PAIR_W = 32
PAIRS_PER_TILE = LANES // PAIR_W
CROWS = TM // S5_L


def _to_chunk_rows(u, us_ref, ucr_ref):
    for q in range(us_ref.shape[0]):
        us_ref[q] = u[:, q * LANES:(q + 1) * LANES]
    lane = lax.broadcasted_iota(I32, (CROWS, LANES), 1)
    for q in range(us_ref.shape[0]):
        tiles = [[] for _ in range(PAIRS_PER_TILE)]
        for t in range(S5_L // PAIRS_PER_TILE):
            acc = [None] * PAIRS_PER_TILE
            for jj in range(PAIRS_PER_TILE):
                piece = us_ref[q, pl.ds(t * PAIRS_PER_TILE + jj, CROWS, stride=S5_L), :]
                here = (lane >= PAIR_W * jj) & (lane < PAIR_W * (jj + 1))
                for pp in range(PAIRS_PER_TILE):
                    shift = (PAIR_W * (jj - pp)) % LANES
                    moved = pltpu.roll(piece, shift, axis=1) if shift else piece
                    acc[pp] = moved if acc[pp] is None else jnp.where(here, moved, acc[pp])
            for pp in range(PAIRS_PER_TILE):
                tiles[pp].append(acc[pp])
        for pp in range(PAIRS_PER_TILE):
            ucr_ref[q * PAIRS_PER_TILE + pp] = jnp.concatenate(tiles[pp], axis=1).astype(BF16)


def _from_chunk_rows(ycr_ref, ys_ref):
    lane = lax.broadcasted_iota(I32, (CROWS, LANES), 1)
    for q in range(ycr_ref.shape[0] // PAIRS_PER_TILE):
        for i in range(S5_L):
            t, ii = divmod(i, PAIRS_PER_TILE)
            acc = None
            for pp in range(PAIRS_PER_TILE):
                piece = ycr_ref[q * PAIRS_PER_TILE + pp, :, t * LANES:(t + 1) * LANES].astype(F32)
                shift = (PAIR_W * (pp - ii)) % LANES
                if shift:
                    piece = pltpu.roll(piece, shift, axis=1)
                here = (lane >= PAIR_W * pp) & (lane < PAIR_W * (pp + 1))
                acc = piece if acc is None else jnp.where(here, piece, acc)
            ys_ref[q, pl.ds(i, CROWS, stride=S5_L), :] = acc
    return jnp.concatenate([ys_ref[q] for q in range(ys_ref.shape[0])], axis=1)


def _inproj0_kernel(nctx_blk, nb, gla_w, q_scale, x_ref, ctx_ref, pos_ref, mod_ref, nw_ref, wm_ref, wa_ref, a2_ref,
                    ab_ref, csf_ref, csb_ref,
                    xs_ref, qf_ref, kf_ref, qb_ref, kb_ref, v_ref, vt_ref, g_ref, ucr_ref, cvf_ref, cvb_ref, us_ref):
    bp, j = pl.program_id(0), pl.program_id(1)
    is_ctx = j < nctx_blk
    xin = jnp.where(is_ctx, ctx_ref[0], x_ref[0] + pos_ref[...])
    xs_ref[0] = xin
    d = xin.shape[-1]
    hb = []
    for s in range(NSUB):
        mod = _mod_row(mod_ref, is_ctx, nb, bp * NSUB + s)
        hb.append((_rms(xin[s]) * nw_ref[...] * (1.0 + mod[:, d:2 * d]) + mod[:, 0:d]).astype(BF16))
    hb = jnp.concatenate(hb, axis=0)
    w = gla_w
    u = _dot(hb, wm_ref[:, 4 * w:5 * w])
    for s in range(NSUB):
        _to_chunk_rows(u[s * TM:(s + 1) * TM], us_ref, ucr_ref.at[:, 0, s, 0])
    a = _dot(hb, wa_ref[...]).astype(BF16)
    for n in range(w // COL_BLK):
        cs = slice(n * COL_BLK, (n + 1) * COL_BLK)
        proj = lambda i: _dot(hb, wm_ref[:, i * w + n * COL_BLK:i * w + (n + 1) * COL_BLK])
        q = proj(0) * q_scale
        k = proj(1)
        v = proj(2)
        g = proj(3)
        las = []
        for di in range(2):
            zs = slice(di * w + n * COL_BLK, di * w + (n + 1) * COL_BLK)
            z = _dot(a, a2_ref[:, zs]) + ab_ref[:, zs]
            las.append((jnp.minimum(z, 0.0) - jnp.log(1.0 + jnp.exp(-jnp.abs(z)))) * (1.0 / GLA_GATE_NORM))
        for s in range(NSUB):
            rows = slice(s * TM, (s + 1) * TM)
            g_ref[0, s, :, cs] = g[rows].astype(BF16)
            v_ref[0, s, :, cs] = v[rows].astype(BF16)
            vt_ref[0, s, cs, :] = v[rows].T.astype(BF16)
            _gate_outputs(q[rows], k[rows], las[0][rows], csf_ref,
                          qf_ref.at[0, s, :, cs], kf_ref.at[0, s, :, cs], cvf_ref.at[0, s, 0, :, cs])
            _gate_outputs(q[rows], k[rows], las[1][rows], csb_ref,
                          qb_ref.at[0, s, :, cs], kb_ref.at[0, s, :, cs], cvb_ref.at[0, s, 0, :, cs])


def _inproj0(x, ctx, pos, mod, nw, wm, wa, a2, ab, consts, q_scale):
    bsz, seq, d = x.shape
    nctx = ctx.shape[1]
    nctx_blk, nlat_blk = nctx // TM, seq // TM
    nblk = nctx_blk + nlat_blk
    nt = nblk * TM
    nbp = bsz // NSUB
    w = wm.shape[1] // 5
    csf, csb = consts
    in_specs = [
        pl.BlockSpec((1, NSUB, TM, d), lambda b, j: (b, 0, jnp.maximum(j - nctx_blk, 0), 0)),
        pl.BlockSpec((1, NSUB, TM, d), lambda b, j: (b, 0, jnp.minimum(j, nctx_blk - 1), 0)),
        pl.BlockSpec((TM, d), lambda b, j: (jnp.maximum(j - nctx_blk, 0), 0)),
        _const_spec(mod.shape), _const_spec(nw.shape), _const_spec(wm.shape), _const_spec(wa.shape),
        _const_spec(a2.shape), _const_spec(ab.shape), _const_spec(csf.shape), _const_spec(csb.shape),
    ]
    tokspec = lambda width: pl.BlockSpec((1, NSUB, TM, width), lambda b, j: (b, 0, j, 0))
    cvspec = pl.BlockSpec((1, NSUB, 1, 3 * SUBLANES, w), lambda b, j: (b, 0, j, 0, 0))
    npair, crow_w = w // PAIR_W, S5_L * PAIR_W
    out_specs = [tokspec(d), tokspec(w), tokspec(w), tokspec(w), tokspec(w), tokspec(w),
                 pl.BlockSpec((1, NSUB, w, TM), lambda b, j: (b, 0, 0, j)), tokspec(w),
                 pl.BlockSpec((npair, 1, NSUB, 1, CROWS, crow_w), lambda b, j: (0, b, 0, j, 0, 0)), cvspec, cvspec]
    sds = jax.ShapeDtypeStruct
    out_shape = [sds((nbp, NSUB, nt, d), F32)] + [sds((nbp, NSUB, nt, w), BF16)] * 5 + [sds((nbp, NSUB, w, nt), BF16)] + \
                [sds((nbp, NSUB, nt, w), BF16), sds((npair, nbp, NSUB, nblk, CROWS, crow_w), BF16)] + \
                [sds((nbp, NSUB, nblk, 3 * SUBLANES, w), F32)] * 2
    outs = pl.pallas_call(
        functools.partial(_inproj0_kernel, nctx_blk, bsz, w, q_scale),
        grid=(nbp, nblk), in_specs=in_specs, out_specs=out_specs, out_shape=out_shape,
        scratch_shapes=[pltpu.VMEM((w // LANES, TM, LANES), F32)],
        compiler_params=_params(2), name="inproj_gla_s5",
    )(_pairs(x), _pairs(ctx), pos, mod, nw, wm, wa, a2, ab, csf, csb)
    ucr = outs[8].reshape(npair, bsz * nblk * CROWS, crow_w)
    return [_unpairs(o) for o in outs[:8]] + [ucr] + [_unpairs(o) for o in outs[9:]]


def _recur_kernel(n_heads, qf_ref, kf_ref, vf_ref, vtf_ref, cvf_ref, qb_ref, kb_ref, vb_ref, vtb_ref, cvb_ref,
                  of_ref, ob_ref, s_ref):
    @pl.when(pl.program_id(1) == 0)
    def _():
        s_ref[...] = jnp.zeros_like(s_ref)

    hrows = STACK_HEADS * CHUNK
    ri = lax.broadcasted_iota(I32, (hrows, hrows), 0)
    ci = lax.broadcasted_iota(I32, (hrows, hrows), 1)
    same_head = (ri // CHUNK) == (ci // CHUNK)
    keep = (same_head & (ci <= ri), same_head & (ci >= ri))
    pw = 2 * HEAD_W
    prow = lax.broadcasted_iota(I32, (2 * CHUNK, pw), 0)
    zero_blk = jnp.zeros((HEAD_W, HEAD_W), BF16)
    stack = lambda ref, r0, h_first: jnp.concatenate(
        [ref[0, r0:r0 + CHUNK, h * HEAD_W:(h + 1) * HEAD_W] for h in range(h_first, h_first + STACK_HEADS)], axis=0)
    dirs = ((qf_ref, kf_ref, vf_ref, vtf_ref, cvf_ref, of_ref), (qb_ref, kb_ref, vb_ref, vtb_ref, cvb_ref, ob_ref))
    for cc in range(CPT):
        for d, (q_ref, k_ref, v_ref, vt_ref, cv_ref, o_ref) in enumerate(dirs):
            c = cc if d == 0 else CPT - 1 - cc
            r0 = c * CHUNK
            p0 = (c // 2) * 2 * CHUNK
            in_chunk = (prow >= r0 - p0) & (prow < r0 - p0 + CHUNK)
            e_mid = cv_ref[0, 0, c:c + 1, :]
            e_rest = cv_ref[0, 0, SUBLANES + c:SUBLANES + c + 1, :]
            e_all = cv_ref[0, 0, 2 * SUBLANES + c:2 * SUBLANES + c + 1, :]
            o_intra = []
            for h_first in range(0, n_heads, STACK_HEADS):
                scores = _dot_nt(stack(q_ref, r0, h_first), stack(k_ref, r0, h_first))
                attn = jnp.where(keep[d], scores, 0.0).astype(BF16)
                o_intra.append(_dot(attn, stack(v_ref, r0, h_first)))
            o_intra = jnp.concatenate(o_intra, axis=0)
            for p in range(n_heads // 2):
                ps = slice(p * pw, (p + 1) * pw)
                h0, h1 = 2 * p, 2 * p + 1
                hs0, hs1 = slice(h0 * HEAD_W, h1 * HEAD_W), slice(h1 * HEAD_W, (h1 + 1) * HEAD_W)
                st0, st1 = s_ref[d, h0], s_ref[d, h1]
                sb0 = (st0 * e_mid[:, hs0]).astype(BF16)
                sb1 = (st1 * e_mid[:, hs1]).astype(BF16)
                carried = jnp.concatenate([jnp.concatenate([sb0, zero_blk], axis=1),
                                           jnp.concatenate([zero_blk, sb1], axis=1)], axis=0)
                o_inter = _dot_nt(q_ref[0, r0:r0 + CHUNK, ps], carried)
                o_ref[0, r0:r0 + CHUNK, hs0] = (o_inter[:, 0:HEAD_W] + o_intra[h0 * CHUNK:h1 * CHUNK]).astype(o_ref.dtype)
                o_ref[0, r0:r0 + CHUNK, hs1] = (o_inter[:, HEAD_W:pw]
                                                + o_intra[h1 * CHUNK:(h1 + 1) * CHUNK]).astype(o_ref.dtype)
                kp = k_ref[0, p0:p0 + 2 * CHUNK, ps]
                kp = jnp.where(in_chunk, kp, jnp.zeros_like(kp))
                ds = _dot(vt_ref[0, ps, p0:p0 + 2 * CHUNK], kp)
                s_ref[d, h0] = st0 * e_all[:, hs0] + ds[0:HEAD_W, 0:HEAD_W] * e_rest[:, hs0]
                s_ref[d, h1] = st1 * e_all[:, hs1] + ds[HEAD_W:pw, HEAD_W:pw] * e_rest[:, hs1]


def _recurrence(qf, kf, qb, kb, v, vt, cvf, cvb, n_heads, nctx_blk):
    bsz, nt, w = qf.shape
    nblk = nt // TM
    fwd = lambda j: j
    bwd = lambda j: jnp.where(j < nctx_blk, nctx_blk - 1 - j, nblk - 1 - (j - nctx_blk))
    specs = []
    for order in (fwd, bwd):
        tok = pl.BlockSpec((1, TM, w), lambda b, j, o=order: (b, o(j), 0))
        specs += [tok, tok, tok,
                  pl.BlockSpec((1, w, TM), lambda b, j, o=order: (b, 0, o(j))),
                  pl.BlockSpec((1, 1, 3 * SUBLANES, w), lambda b, j, o=order: (b, o(j), 0, 0))]
    out_specs = [pl.BlockSpec((1, TM, w), lambda b, j: (b, j, 0)),
                 pl.BlockSpec((1, TM, w), lambda b, j: (b, bwd(j), 0))]
    return pl.pallas_call(
        functools.partial(_recur_kernel, n_heads),
        grid=(bsz, nblk), in_specs=specs, out_specs=out_specs,
        out_shape=[jax.ShapeDtypeStruct((bsz, nt, w), BF16)] * 2,
        scratch_shapes=[pltpu.VMEM((2, n_heads, HEAD_W, HEAD_W), F32)],
        compiler_params=_params(2), name=f"recurrence_h{n_heads}",
    )(qf, kf, v, vt, cvf, qb, kb, v, vt, cvb)


def _s5_operators(lam_re, lam_im, log_dt, b_re, b_im, c_re, c_im, d_skip):
    L = S5_L
    ndir, ng, p = lam_re.shape
    hs = b_re.shape[-1]
    dt = jnp.exp(log_dt)[..., None]
    lr, li = lam_re * dt, lam_im * dt
    mag = jnp.exp(lr)
    a_re, a_im = mag * jnp.cos(li), mag * jnp.sin(li)
    den = lam_re * lam_re + lam_im * lam_im
    f_re = ((a_re - 1.0) * lam_re + a_im * lam_im) / den
    f_im = (a_im * lam_re - (a_re - 1.0) * lam_im) / den
    bb_re = f_re[..., None] * b_re - f_im[..., None] * b_im
    bb_im = f_re[..., None] * b_im + f_im[..., None] * b_re
    m = jnp.arange(L + 1, dtype=F32)[:, None, None, None]
    pw_mag = jnp.exp(m * lr[None])
    pw_re, pw_im = pw_mag * jnp.cos(m * li[None]), pw_mag * jnp.sin(m * li[None])
    eye = jnp.eye(2, dtype=F32)
    npair, pw, cw = ng // 2, 2 * hs, 2 * L * hs

    toep = jnp.zeros((npair, cw, cw), F32)
    wz_p, vo_p = [], []
    for di in range(ndir):
        pr, pi = pw_re[:L, di, :, :, None], pw_im[:L, di, :, :, None]
        ab_re = pr * bb_re[di][None] - pi * bb_im[di][None]
        ab_im = pr * bb_im[di][None] + pi * bb_re[di][None]
        kf = jnp.einsum('mgnk,ghn->gkmh', ab_re, c_re[di]) - jnp.einsum('mgnk,ghn->gkmh', ab_im, c_im[di])
        if di == 1:
            kf = kf[:, :, ::-1, :]
        kp = jnp.einsum('pskmh,st->pskmth', kf.reshape(npair, 2, hs, L, hs), eye).reshape(npair, pw, cw)
        if di == 0:
            kpad = jnp.pad(kp, ((0, 0), (0, 0), (cw, 0)))
            blocks = [kpad[:, :, cw - pw * j:2 * cw - pw * j] for j in range(L)]
        else:
            kpad = jnp.pad(kp, ((0, 0), (0, 0), (0, cw)))
            blocks = [kpad[:, :, pw * (L - 1 - j):pw * (L - 1 - j) + cw] for j in range(L)]
        toep = toep + jnp.stack(blocks, axis=1).reshape(npair, cw, cw)
        sel_i = (lambda a: a[::-1]) if di == 0 else (lambda a: a)
        pack_w = lambda a: jnp.einsum('jpsnk,st->pjsktn', sel_i(a).reshape(L, npair, 2, p, hs), eye).reshape(npair, cw, 2 * p)
        wz_p.append(jnp.concatenate([pack_w(ab_re), pack_w(ab_im)], axis=2))
        qr, qi = pw_re[1:, di, :, None, :], pw_im[1:, di, :, None, :]
        ca_re = c_re[di][None] * qr - c_im[di][None] * qi
        ca_im = c_re[di][None] * qi + c_im[di][None] * qr
        sel_o = (lambda a: a) if di == 0 else (lambda a: a[::-1])
        pack_v = lambda a: jnp.einsum('ipshn,st->psnith', sel_o(a).reshape(L, npair, 2, hs, p), eye).reshape(npair, 2 * p, cw)
        vo_p.append(jnp.concatenate([pack_v(ca_re), -pack_v(ca_im)], axis=1))
    wz_p, vo_p = jnp.stack(wz_p, axis=1), jnp.stack(vo_p, axis=1)
    al = jnp.stack([pw_re[L], pw_im[L]], axis=1)
    al_p = jnp.transpose(al.reshape(ndir, 2, npair, 2 * p), (2, 0, 1, 3))
    d_p = jnp.tile(d_skip.reshape(npair, 1, pw), (1, L, 1)).reshape(npair, 1, cw)
    return wz_p.astype(BF16), vo_p.astype(BF16), toep.astype(BF16), al_p, d_p


def _s5_row_block(rows_total):
    tile = 2 * SUBLANES
    return max(r for r in range(tile, S5_MAX_ROWS + 1, tile) if rows_total % r == 0)


def _s5_kernel(nc, ncc, bsz, u_ref, wz_ref, vo_ref, tp_ref, al_ref, d_ref, y_ref, zf, zb, xf, xb):
    rows_total = nc * bsz
    rb = _s5_row_block(rows_total)
    half = zf.shape[2]

    def z_body(i, carry):
        rows = pl.ds(pl.multiple_of(i * rb, rb), rb)
        ub = u_ref[0, rows, :]
        for z, di in ((zf, 0), (zb, 1)):
            zz = _dot(ub, wz_ref[0, di])
            z[0, rows, :] = zz[:, 0:half]
            z[1, rows, :] = zz[:, half:2 * half]
        return carry

    lax.fori_loop(0, rows_total // rb, z_body, 0)

    coef = [[jnp.broadcast_to(al_ref[0, di, ri:ri + 1, :], (bsz, half)) for ri in range(2)] for di in range(2)]
    orders = (lambda i: i, lambda i: jnp.where(i < ncc, ncc - 1 - i, nc - 1 - (i - ncc)))

    def scan_body(i, carry):
        new = []
        for di, (z, xs) in enumerate(((zf, xf), (zb, xb))):
            x_re, x_im = carry[2 * di], carry[2 * di + 1]
            a_re, a_im = coef[di]
            rows = pl.ds(orders[di](i), bsz, stride=nc)
            xs[0, rows, :] = x_re
            xs[1, rows, :] = x_im
            new += [a_re * x_re - a_im * x_im + z[0, rows, :], a_re * x_im + a_im * x_re + z[1, rows, :]]
        return tuple(new)

    zero = jnp.zeros((bsz, half), F32)
    lax.fori_loop(0, nc, scan_body, (zero,) * 4, unroll=SCAN_UNROLL)

    def y_body(i, carry):
        rows = pl.ds(pl.multiple_of(i * rb, rb), rb)
        ub = u_ref[0, rows, :]
        state = lambda xs: jnp.concatenate([xs[0, rows, :], xs[1, rows, :]], axis=1).astype(BF16)
        y = _dot(state(xf), vo_ref[0, 0]) + _dot(state(xb), vo_ref[0, 1])
        y_ref[0, rows, :] = (y + _dot(ub, tp_ref[0]) + ub.astype(F32) * d_ref[0]).astype(y_ref.dtype)
        return carry

    lax.fori_loop(0, rows_total // rb, y_body, 0)


def _s5(u_cr, ops, nc, ncc, bsz):
    wz_p, vo_p, toep_p, al_p, d_p = ops
    npair, rows, width = u_cr.shape
    per_pair = lambda a: pl.BlockSpec((1,) + a.shape[1:], lambda p: (p,) + (0,) * (a.ndim - 1))
    return pl.pallas_call(
        functools.partial(_s5_kernel, nc, ncc, bsz),
        grid=(npair,),
        in_specs=[per_pair(u_cr), per_pair(wz_p), per_pair(vo_p), per_pair(toep_p), per_pair(al_p), per_pair(d_p)],
        out_specs=per_pair(u_cr),
        out_shape=jax.ShapeDtypeStruct(u_cr.shape, BF16),
        scratch_shapes=[pltpu.VMEM((2, rows, wz_p.shape[-1] // 2), F32)] * 4,
        compiler_params=_params(1), name="s5",
    )(u_cr, wz_p, vo_p, toep_p, al_p, d_p)


def _route(h2, rwh_ref, rwl_ref, rb_ref, utri_ref, ones_ref, carry_ref):
    hi, lo = _split(h2)
    logits = _dot_nt(rwh_ref[...], hi) + _dot_nt(rwl_ref[...], hi) + _dot_nt(rwh_ref[...], lo)
    aff = _sigmoid(logits)
    sel = aff + rb_ref[...]
    epg = EXPERTS_PER_GROUP
    s = [sel[N_GROUPS * p:N_GROUPS * (p + 1), :] for p in range(epg)]
    a = [aff[N_GROUPS * p:N_GROUPS * (p + 1), :] for p in range(epg)]
    m1, n1 = jnp.maximum(s[0], s[1]), jnp.minimum(s[0], s[1])
    m2, n2 = jnp.maximum(s[2], s[3]), jnp.minimum(s[2], s[3])
    score = jnp.maximum(m1, m2) + jnp.maximum(jnp.minimum(m1, m2), jnp.maximum(n1, n2))
    gi = lax.broadcasted_iota(I32, score.shape, 0)
    best = jnp.max(score, axis=0, keepdims=True)
    gidx = jnp.min(jnp.where(score == best, gi, N_GROUPS), axis=0, keepdims=True)
    onehot = gi == gidx
    gates = []
    for p in range(epg):
        ahead = jnp.zeros_like(score)
        for q in range(epg):
            if q != p:
                beats = (s[q] >= s[p]) if q < p else (s[q] > s[p])
                ahead = ahead + jnp.where(beats, 1.0, 0.0)
        picked = jnp.where(onehot, jnp.where(ahead < float(TOP_K) - 0.5, a[p], 0.0), 0.0)
        gates.append(jnp.sum(picked, axis=0, keepdims=True))
    den = gates[0] + gates[1] + gates[2] + gates[3]
    gates = [g / den for g in gates]
    oh = jnp.where(onehot, 1.0, 0.0)
    oh = jnp.concatenate([oh, jnp.zeros_like(oh)], axis=0).astype(BF16)
    before = _dot(oh, utri_ref[...])[0:N_GROUPS]
    count = _dot(oh, ones_ref[...])[0:N_GROUPS]
    run = jnp.floor((count + (RUN - 1.0)) * (1.0 / RUN)) * RUN
    carry = carry_ref[...]
    carry_ref[...] = carry + run
    starts, nxt = [], jnp.zeros_like(run[0:1])
    for g in range(N_GROUPS):
        starts.append(nxt)
        nxt = nxt + run[g:g + 1]
    lstart = jnp.concatenate(starts, axis=0)
    lpos = jnp.sum(jnp.where(onehot, lstart + before, 0.0), axis=0, keepdims=True)
    return lpos, gates, (lstart, count, carry)


def _mixout_kernel(has_s5, n_heads, nctx_blk, tok_off, nb, *refs):
    if has_s5:
        (x_ref, of_ref, ob_ref, g_ref, ycr_ref, mod_ref, gn_ref, gluw_ref, glub_ref, wo_ref, nf_ref,
         rwh_ref, rwl_ref, rb_ref, utri_ref, ones_ref,
         x1_ref, pay_ref, aux_ref, meta_ref, tab_ref, cnt_ref, carry_ref, ys_ref) = refs
    else:
        (x_ref, of_ref, ob_ref, g_ref, mod_ref, gn_ref, wo_ref, nf_ref,
         rwh_ref, rwl_ref, rb_ref, utri_ref, ones_ref,
         x1_ref, pay_ref, aux_ref, meta_ref, tab_ref, cnt_ref, carry_ref) = refs
    bp, j = pl.program_id(0), pl.program_id(1)

    @pl.when((bp == 0) & (j == 0))
    def _():
        carry_ref[...] = jnp.zeros_like(carry_ref)

    d = x_ref.shape[-1]
    merge = lambda ref: jnp.concatenate([ref[0, s] for s in range(NSUB)], axis=0)
    o = merge(of_ref).astype(F32) + merge(ob_ref).astype(F32)
    mixed = _head_rms(o, gn_ref[...], n_heads) * _silu(merge(g_ref).astype(F32))
    if has_s5:
        y5 = jnp.concatenate([_from_chunk_rows(ycr_ref.at[:, 0, s, 0], ys_ref) for s in range(NSUB)], axis=0)
        act = _gelu_tanh(y5)
        glu = act * _sigmoid(_dot(act.astype(BF16), gluw_ref[...]) + glub_ref[...])
        mixed = jnp.concatenate([mixed, glu], axis=1)
    proj = _dot(mixed.astype(BF16), wo_ref[...])
    h2 = []
    for s in range(NSUB):
        mod = _mod_row(mod_ref, j + tok_off < nctx_blk, nb, bp * NSUB + s)
        x1 = x_ref[0, s] + mod[:, 2 * d:3 * d] * proj[s * TM:(s + 1) * TM]
        x1_ref[0, s] = x1
        h2.append(_rms(x1) * nf_ref[...] * (1.0 + mod[:, 4 * d:5 * d]) + mod[:, 3 * d:4 * d])
    h2 = jnp.concatenate(h2, axis=0)

    lpos, gates, (lstart, count, carry) = _route(h2, rwh_ref, rwl_ref, rb_ref, utri_ref, ones_ref, carry_ref)
    gi = lax.broadcasted_iota(I32, (SUBLANES, TR), 0)
    meta_ref[0] = jnp.where(gi == 0, lpos.astype(I32), 0)
    ti = lax.broadcasted_iota(I32, (N_GROUPS, LANES), 1)
    first = lambda a: a[:, 0:LANES].astype(I32)
    tab_ref[0] = jnp.where(ti == TAB_START, first(lstart),
                           jnp.where(ti == TAB_COUNT, first(count), jnp.where(ti == TAB_CARRY, first(carry), 0)))
    cnt_ref[...] = carry_ref[...]

    li = lax.broadcasted_iota(I32, (LANES, TR), 0)
    gt = jnp.where(li == AUX_LPOS, lpos, 0.0)
    for p, g in enumerate(gates):
        gt = gt + jnp.where(li == p, g, 0.0)
    aux = gt.T
    pay_ref[:, 0:d] = h2
    pay_ref[:, d:d + LANES] = aux
    aux_ref[...] = aux


def _mixout(has_s5, n_heads, nctx_blk, tok_off, x, of, ob, g, y5, mod, gn, gluw, glub, wo, nf, rwh, rwl, rb, utri, ones):
    bsz, _, d = x.shape
    w = of.shape[-1]
    nblk_all = of.shape[1] // TM
    nblk = nblk_all - tok_off
    nbp = bsz // NSUB
    tok = lambda width: pl.BlockSpec((1, NSUB, TM, width), lambda b, j: (b, 0, j + tok_off, 0))
    ins, specs = [_pairs(x), _pairs(of), _pairs(ob), _pairs(g)], [tok(d), tok(w), tok(w), tok(w)]
    scratch = [pltpu.VMEM((N_GROUPS, TR), F32)]
    if has_s5:
        npair, _, crow_w = y5.shape
        ins.append(y5.reshape(npair, nbp, NSUB, nblk_all, CROWS, crow_w))
        specs.append(pl.BlockSpec((npair, 1, NSUB, 1, CROWS, crow_w), lambda b, j: (0, b, 0, j, 0, 0)))
        scratch.append(pltpu.VMEM((w // LANES, TM, LANES), F32))
    consts = [mod, gn] + ([gluw, glub] if has_s5 else []) + [wo, nf, rwh, rwl, rb, utri, ones]
    ins += consts
    specs += [_const_spec(c.shape) for c in consts]
    ntile = nbp * nblk
    pay_w = d + LANES
    lin = lambda b, j: b * nblk + j
    out_specs = [pl.BlockSpec((1, NSUB, TM, d), lambda b, j: (b, 0, j, 0)),
                 pl.BlockSpec((TR, pay_w), lambda b, j: (lin(b, j), 0)),
                 pl.BlockSpec((TR, LANES), lambda b, j: (lin(b, j), 0)),
                 pl.BlockSpec((1, SUBLANES, TR), lambda b, j: (lin(b, j), 0, 0)),
                 pl.BlockSpec((1, N_GROUPS, LANES), lambda b, j: (lin(b, j), 0, 0)),
                 _const_spec((N_GROUPS, TR))]
    sds = jax.ShapeDtypeStruct
    out_shape = [sds((nbp, NSUB, nblk * TM, d), F32), sds((ntile * TR, pay_w), F32), sds((ntile * TR, LANES), F32),
                 sds((ntile, SUBLANES, TR), I32), sds((ntile, N_GROUPS, LANES), I32), sds((N_GROUPS, TR), F32)]
    outs = pl.pallas_call(
        functools.partial(_mixout_kernel, has_s5, n_heads, nctx_blk, tok_off, bsz),
        grid=(nbp, nblk), in_specs=specs, out_specs=out_specs, out_shape=out_shape,
        scratch_shapes=scratch,
        compiler_params=_params(2), name="mixout_s5" if has_s5 else "mixout",
    )(*ins)
    return [_unpairs(outs[0])] + list(outs[1:])


def _for_each_run(seg_ref, tab_ref, fn):
    for g in range(N_GROUPS):
        lstart = tab_ref[0, g, TAB_START]
        first = seg_ref[g] + tab_ref[0, g, TAB_CARRY]
        n_pieces = lax.shift_right_logical(tab_ref[0, g, TAB_COUNT] + (RUN - 1), RUN_SHIFT)

        def piece(k, carry, lstart=lstart, first=first):
            fn(pl.ds(pl.multiple_of(lstart + k * RUN, RUN), RUN), pl.ds(pl.multiple_of(first + k * RUN, RUN), RUN))
            return carry

        lax.fori_loop(0, n_pieces, piece, 0)


def _dispatch_kernel(seg_ref, cnt_ref, tab_ref, tab_prev_ref, meta_ref, pay_ref, hs_hbm, s_ref, z_ref, sem, zsem):
    i = pl.program_id(0)
    last = pl.num_programs(0) - 1
    slot = lax.rem(i, 2)
    d = pay_ref.shape[1] - LANES
    lpos = meta_ref[0, 0:1, :]
    rows = lax.broadcasted_iota(I32, (LROWS, TR), 0)
    pick = jnp.where(rows == lpos, 1.0, 0.0).astype(BF16)
    s_ref[slot, :, 0:d] = _dot(pick, pay_ref[:, 0:d].astype(BF16))
    ghi, glo = _split(pay_ref[:, d:d + LANES])
    s_ref[slot, :, d:d + LANES] = _dot(pick, ghi) + _dot(pick, glo)

    def runs(t_ref, sl, op):
        copy = lambda src, dst: pltpu.make_async_copy(s_ref.at[sl, src], hs_hbm.at[dst], sem.at[sl])
        _for_each_run(seg_ref, t_ref, lambda src, dst: getattr(copy(src, dst), op)())

    runs(tab_ref, slot, "start")

    @pl.when(i > 0)
    def _():
        runs(tab_prev_ref, 1 - slot, "wait")

    @pl.when(i == last)
    def _():
        runs(tab_ref, slot, "wait")
        z_ref[...] = jnp.zeros_like(z_ref)
        for op in ("start", "wait"):
            for g in range(N_GROUPS + 1):
                begin = seg_ref[g] + cnt_ref[g]

                def piece(k, carry, begin=begin):
                    dst = hs_hbm.at[pl.ds(pl.multiple_of(begin + k * RUN, RUN), RUN)]
                    getattr(pltpu.make_async_copy(z_ref, dst, zsem), op)()
                    return carry

                lax.fori_loop(0, lax.shift_right_logical(seg_ref[g + 1] - begin, RUN_SHIFT), piece, 0)


def _dispatch(seg, cnt, tab, meta, pay, n_sorted):
    ntile = meta.shape[0]
    width = pay.shape[1]
    grid_spec = pltpu.PrefetchScalarGridSpec(
        num_scalar_prefetch=2, grid=(ntile,),
        in_specs=[pl.BlockSpec((1, N_GROUPS, LANES), lambda i, seg, cnt: (i, 0, 0), memory_space=pltpu.SMEM),
                  pl.BlockSpec((1, N_GROUPS, LANES), lambda i, seg, cnt: (jnp.maximum(i - 1, 0), 0, 0),
                               memory_space=pltpu.SMEM),
                  pl.BlockSpec((1, SUBLANES, TR), lambda i, seg, cnt: (i, 0, 0)),
                  pl.BlockSpec((TR, width), lambda i, seg, cnt: (i, 0))],
        out_specs=pl.BlockSpec(memory_space=pl.ANY),
        scratch_shapes=[pltpu.VMEM((2, LROWS, width), F32), pltpu.VMEM((RUN, width), F32),
                        pltpu.SemaphoreType.DMA((2,)), pltpu.SemaphoreType.DMA(())])
    return pl.pallas_call(
        _dispatch_kernel, grid_spec=grid_spec,
        out_shape=jax.ShapeDtypeStruct((n_sorted, width), F32),
        compiler_params=_params(1), name="moe_dispatch",
    )(seg, cnt, tab, tab, meta, pay)


def _moe_kernel(tg_ref, nv_ref, hs_ref, wg_ref, wu_ref, wd_ref, ys_ref, wgb, wub, wdb):
    i = pl.program_id(0)
    valid = i < nv_ref[0]

    @pl.when(valid & ((i == 0) | (tg_ref[i] != tg_ref[jnp.maximum(i - 1, 0)])))
    def _():
        wgb[...] = wg_ref[...].astype(BF16)
        wub[...] = wu_ref[...].astype(BF16)
        wdb[...] = wd_ref[...].astype(BF16)

    @pl.when(valid)
    def _():
        d = wg_ref.shape[1]
        h = hs_ref[:, 0:d].astype(BF16)
        gate = hs_ref[:, d:d + LANES]
        acc = jnp.zeros((TR, d), F32)
        for p in range(EXPERTS_PER_GROUP):
            act = _silu(_dot(h, wgb[p])) * _dot(h, wub[p]) * gate[:, p:p + 1]
            acc = acc + _dot(act.astype(BF16), wdb[p])
        ys_ref[...] = acc

    @pl.when(jnp.logical_not(valid))
    def _():
        ys_ref[...] = jnp.zeros_like(ys_ref)


def _moe(tile_group, n_valid, hs, layer, wg, wu, wd):
    n_sorted, pay_w = hs.shape
    ntile = n_sorted // TR
    d = wg.shape[2]
    epg = EXPERTS_PER_GROUP
    clamp = lambda i, tg, nv: jnp.minimum(i, nv[0] - 1)
    wspec = lambda a: pl.BlockSpec((None, epg) + a.shape[2:], lambda i, tg, nv: (layer, tg[i], 0, 0))
    grid_spec = pltpu.PrefetchScalarGridSpec(
        num_scalar_prefetch=2, grid=(ntile,),
        in_specs=[pl.BlockSpec((TR, pay_w), lambda i, tg, nv: (clamp(i, tg, nv), 0)), wspec(wg), wspec(wu), wspec(wd)],
        out_specs=pl.BlockSpec((TR, d), lambda i, tg, nv: (i, 0)),
        scratch_shapes=[pltpu.VMEM((epg,) + wg.shape[2:], BF16), pltpu.VMEM((epg,) + wu.shape[2:], BF16),
                        pltpu.VMEM((epg,) + wd.shape[2:], BF16)])
    return pl.pallas_call(
        _moe_kernel, grid_spec=grid_spec,
        out_shape=jax.ShapeDtypeStruct((n_sorted, d), F32),
        compiler_params=_params(1), name="moe_experts",
    )(tile_group, n_valid, hs, wg, wu, wd)


def _route_specs(bsz, nblk):
    last = bsz * nblk - 1
    cur = pl.BlockSpec((1, N_GROUPS, LANES), lambda b, j, seg: (b * nblk + j, 0, 0), memory_space=pltpu.SMEM)
    nxt = pl.BlockSpec((1, N_GROUPS, LANES), lambda b, j, seg: (jnp.minimum(b * nblk + j + 1, last), 0, 0),
                       memory_space=pltpu.SMEM)
    aux = pl.BlockSpec((TR, LANES), lambda b, j, seg: (b * nblk + j, 0))
    return [cur, nxt, aux]


def _gather_ffn(step, n_steps, seg_ref, tab_ref, tab_next_ref, aux_ref, ys_hbm, buf, sem):
    def runs(t_ref, slot, op):
        copy = lambda dst, src: pltpu.make_async_copy(ys_hbm.at[src], buf.at[slot, dst], sem.at[slot])
        _for_each_run(seg_ref, t_ref, lambda dst, src: getattr(copy(dst, src), op)())

    slot = lax.rem(step, 2)

    @pl.when(step == 0)
    def _():
        buf[...] = jnp.zeros_like(buf)
        runs(tab_ref, 0, "start")

    @pl.when(step + 1 < n_steps)
    def _():
        runs(tab_next_ref, 1 - slot, "start")

    runs(tab_ref, slot, "wait")
    lpos = aux_ref[:, AUX_LPOS:AUX_LPOS + 1].astype(I32)
    cols = lax.broadcasted_iota(I32, (TR, LROWS), 1)
    pick = jnp.where(cols == lpos, 1.0, 0.0).astype(BF16)
    return _dot(pick, buf[slot].astype(BF16))


def _inproj1_kernel(nctx_blk, nb, seg_ref, tab_ref, tab_next_ref, aux_ref, ys_hbm, x_ref, mod0_ref, mod1_ref, nw_ref,
                    w_ref, lb_ref, csf_ref, csb_ref,
                    x2_ref, qf_ref, kf_ref, qb_ref, kb_ref, v_ref, vt_ref, g_ref, cvf_ref, cvb_ref, buf, sem):
    bp, j = pl.program_id(0), pl.program_id(1)
    nblk = pl.num_programs(1)
    ffn = _gather_ffn(bp * nblk + j, pl.num_programs(0) * nblk, seg_ref, tab_ref, tab_next_ref, aux_ref, ys_hbm,
                      buf, sem)
    d = x_ref.shape[-1]
    hb = []
    for s in range(NSUB):
        mod0 = _mod_row(mod0_ref, j < nctx_blk, nb, bp * NSUB + s)
        mod1 = _mod_row(mod1_ref, j < nctx_blk, nb, bp * NSUB + s)
        x2 = x_ref[0, s] + mod0[:, 5 * d:6 * d] * ffn[s * TM:(s + 1) * TM]
        x2_ref[0, s] = x2
        hb.append((_rms(x2) * nw_ref[...] * (1.0 + mod1[:, d:2 * d]) + mod1[:, 0:d]).astype(BF16))
    hb = jnp.concatenate(hb, axis=0)
    proj = lambda n: _dot(hb, w_ref[:, n * d:(n + 1) * d])
    q = _silu(proj(0))
    lb = lb_ref[...]
    v = proj(3)
    g = proj(4)
    f_f = lb + (1.0 - lb) * _sigmoid(proj(1))
    f_b = lb + (1.0 - lb) * _sigmoid(proj(2))
    for s in range(NSUB):
        rows = slice(s * TM, (s + 1) * TM)
        v_ref[0, s] = v[rows].astype(BF16)
        vt_ref[0, s] = v[rows].T.astype(BF16)
        g_ref[0, s] = g[rows].astype(BF16)
        _gate_outputs(q[rows], 1.0 - f_f[rows], jnp.log(f_f[rows]), csf_ref,
                      qf_ref.at[0, s], kf_ref.at[0, s], cvf_ref.at[0, s, 0])
        _gate_outputs(q[rows], 1.0 - f_b[rows], jnp.log(f_b[rows]), csb_ref,
                      qb_ref.at[0, s], kb_ref.at[0, s], cvb_ref.at[0, s, 0])


def _inproj1(seg, tab, aux, ys, x1, mod0, mod1, nw, w_in, lb, consts, nctx_blk):
    bsz, nt, d = x1.shape
    nblk = nt // TM
    nbp = bsz // NSUB
    csf, csb = consts
    cs = lambda a: pl.BlockSpec(a.shape, lambda b, j, seg: (0,) * a.ndim)
    tok = lambda width: pl.BlockSpec((1, NSUB, TM, width), lambda b, j, seg: (b, 0, j, 0))
    cvspec = pl.BlockSpec((1, NSUB, 1, 3 * SUBLANES, d), lambda b, j, seg: (b, 0, j, 0, 0))
    grid_spec = pltpu.PrefetchScalarGridSpec(
        num_scalar_prefetch=1, grid=(nbp, nblk),
        in_specs=_route_specs(nbp, nblk) + [
                  pl.BlockSpec(memory_space=pl.ANY), tok(d), cs(mod0), cs(mod1), cs(nw),
                  pl.BlockSpec(w_in.shape, lambda b, j, seg: (0, 0), pipeline_mode=pl.Buffered(1)), cs(lb),
                  cs(csf), cs(csb)],
        out_specs=[tok(d)] * 6 + [pl.BlockSpec((1, NSUB, d, TM), lambda b, j, seg: (b, 0, 0, j)), tok(d), cvspec, cvspec],
        scratch_shapes=[pltpu.VMEM((2, LROWS, d), F32), pltpu.SemaphoreType.DMA((2,))])
    sds = jax.ShapeDtypeStruct
    out_shape = [sds((nbp, NSUB, nt, d), F32)] + [sds((nbp, NSUB, nt, d), BF16)] * 5 + [sds((nbp, NSUB, d, nt), BF16)] + \
                [sds((nbp, NSUB, nt, d), BF16)] + [sds((nbp, NSUB, nblk, 3 * SUBLANES, d), F32)] * 2
    outs = pl.pallas_call(
        functools.partial(_inproj1_kernel, nctx_blk, bsz), grid_spec=grid_spec, out_shape=out_shape,
        compiler_params=_params(2), name="combine_inproj_hgrn",
    )(seg, tab, tab, aux, ys, _pairs(x1), mod0, mod1, nw, w_in, lb, csf, csb)
    return [_unpairs(o) for o in outs]


def _final_kernel(seg_ref, tab_ref, tab_next_ref, aux_ref, ys_hbm, x_ref, mod_ref, fw_ref, o_ref, buf, sem):
    nblk = pl.num_programs(1)
    bp = pl.program_id(0)
    step = bp * nblk + pl.program_id(1)
    ffn = _gather_ffn(step, pl.num_programs(0) * nblk, seg_ref, tab_ref, tab_next_ref, aux_ref, ys_hbm, buf, sem)
    d = x_ref.shape[-1]
    for s in range(NSUB):
        mod = mod_ref[pl.ds(bp * NSUB + s, 1), :]
        o_ref[0, s] = _rms(x_ref[0, s] + mod[:, 5 * d:6 * d] * ffn[s * TM:(s + 1) * TM]) * fw_ref[...]


def _final(seg, tab, aux, ys, x3, mod, fw):
    bsz, seq, d = x3.shape
    nblk = seq // TM
    nbp = bsz // NSUB
    cs = lambda a: pl.BlockSpec(a.shape, lambda b, j, seg: (0,) * a.ndim)
    tok = pl.BlockSpec((1, NSUB, TM, d), lambda b, j, seg: (b, 0, j, 0))
    grid_spec = pltpu.PrefetchScalarGridSpec(
        num_scalar_prefetch=1, grid=(nbp, nblk),
        in_specs=_route_specs(nbp, nblk) + [pl.BlockSpec(memory_space=pl.ANY), tok, cs(mod), cs(fw)],
        out_specs=tok,
        scratch_shapes=[pltpu.VMEM((2, LROWS, d), F32), pltpu.SemaphoreType.DMA((2,))])
    out = pl.pallas_call(
        _final_kernel, grid_spec=grid_spec, out_shape=jax.ShapeDtypeStruct((nbp, NSUB, seq, d), F32),
        compiler_params=_params(2), name="combine_final_norm",
    )(seg, tab, tab, aux, ys, _pairs(x3), mod, fw)
    return _unpairs(out)


def _grid_sincos(n_tokens, dim):
    rows = n_tokens // GRID_W
    r, col = jnp.meshgrid(jnp.arange(rows, dtype=F32), jnp.arange(GRID_W, dtype=F32), indexing='ij')
    quarter = dim // 4
    omega = 1.0 / (10000.0 ** (jnp.arange(quarter, dtype=F32) / quarter))

    def emb(p):
        ang = p.reshape(-1, 1) * omega
        return jnp.concatenate([jnp.sin(ang), jnp.cos(ang)], axis=-1)

    return jnp.concatenate([emb(r), emb(col)], axis=-1)


def _pad_heads(w, n_heads):
    dk = w.shape[-1] // n_heads
    w = w.reshape(w.shape[:-1] + (n_heads, dk))
    w = jnp.pad(w, [(0, 0)] * (w.ndim - 1) + [(0, HEAD_W - dk)])
    return w.reshape(w.shape[:-2] + (n_heads * HEAD_W,))


def _router_tables(router_w, router_bias):
    n_exp = router_w.shape[1]
    epg = n_exp // N_GROUPS
    assert epg == EXPERTS_PER_GROUP
    perm = np.array([epg * g + p for p in range(epg) for g in range(N_GROUPS)])
    wt = jnp.pad(router_w.T[perm], ((0, LANES - n_exp), (0, 0)))
    hi = wt.astype(BF16)
    lo = (wt - hi.astype(F32)).astype(BF16)
    bias = jnp.pad(router_bias[perm], (0, LANES - n_exp))
    return hi, lo, jnp.broadcast_to(bias[:, None], (LANES, TR)).astype(F32)


def _segments(counts, n_tokens):
    cnt = counts[:, 0].astype(I32)
    tiles = (cnt + TR - 1) // TR
    ends = jnp.cumsum(tiles)
    ntile = (n_tokens + (n_tokens // TR) * N_GROUPS * (RUN - 1) + TR - 1) // TR + N_GROUPS
    n_valid = ends[-1]
    seg = jnp.concatenate([(ends - tiles) * TR, (n_valid * TR).reshape(1), jnp.full((1,), ntile * TR, I32)])
    tidx = jnp.minimum(jnp.arange(ntile, dtype=I32), n_valid - 1)
    tile_group = jnp.sum((tidx[:, None] >= ends[None, :]).astype(I32), axis=1)
    cnt = jnp.concatenate([cnt, jnp.zeros((1,), I32)])
    return seg.astype(I32), cnt, tile_group.astype(I32), n_valid.reshape(1).astype(I32), ntile * TR


def _moe_layer(tab, meta, counts, pay, weights):
    seg, cnt, tile_group, n_valid, n_sorted = _segments(counts, pay.shape[0])
    hs = _dispatch(seg, cnt, tab, meta, pay, n_sorted)
    ys = _moe(tile_group, n_valid, hs, *weights)
    return seg, ys


def kernel(x, c, ctx, c_ctx, ada_w, ada_b, norm_mix, norm_ffn, ab_w_in, ab_w_out, gla_a2, gla_ab, gla_norm, s5_lam_re, s5_lam_im, s5_log_dt, s5_b_re, s5_b_im, s5_c_re, s5_c_im, s5_d, s5_glu_w, s5_glu_b, hg_w_in, hg_w_out, hg_lb_logits, hg_norm, router_w, router_bias, moe_w_gate, moe_w_up, moe_w_down, final_norm):
    bsz, seq, d = x.shape
    nctx = ctx.shape[1]
    depth = ada_w.shape[0]
    assert depth == 2 and seq % TM == 0 and nctx % TM == 0 and bsz % SUBLANES == 0 and bsz < MOD_ROWS
    assert bsz % NSUB == 0
    nctx_blk = nctx // TM
    nt = nctx + seq
    nc = nt // S5_L
    assert (nc * bsz) % LANES == 0

    cond = jnp.zeros((MOD_ROWS, d), F32).at[:bsz].set(c).at[bsz].set(c_ctx)
    mod = _modulation(cond, ada_w, ada_b)
    consts = _chunk_matrices()
    row = lambda v: v.reshape(1, -1)

    qk = gla_a2.shape[-1]
    gv = ab_w_out.shape[1] // 2
    rank = gla_a2.shape[2]
    w_in = ab_w_in[0]
    o_v, o_g, o_a, o_u = 2 * qk, 2 * qk + gv, 2 * qk + 2 * gv, 2 * qk + 2 * gv + 2 * rank
    wm = jnp.concatenate([_pad_heads(w_in[:, 0:qk], GLA_HEADS), _pad_heads(w_in[:, qk:o_v], GLA_HEADS),
                          w_in[:, o_v:o_g], w_in[:, o_g:o_a], w_in[:, o_u:]], axis=1).astype(BF16)
    wa = jnp.pad(w_in[:, o_a:o_u], ((0, 0), (0, LANES - 2 * rank))).astype(BF16)
    a2p = _pad_heads(gla_a2[0], GLA_HEADS)
    a2 = jnp.zeros((LANES, 2 * gv), F32).at[0:rank, 0:gv].set(a2p[0]).at[rank:2 * rank, gv:].set(a2p[1]).astype(BF16)
    ab = _pad_heads(gla_ab[0], GLA_HEADS).reshape(1, 2 * gv)
    pos = _grid_sincos(seq, d)
    (xs, qf, kf, qb, kb, v, vt, g, u_cr, cvf, cvb) = _inproj0(x, ctx, pos, mod[0], row(norm_mix[0]), wm, wa, a2, ab, consts,
                                                               float(qk // GLA_HEADS) ** -0.5)
    o_f, o_b = _recurrence(qf, kf, qb, kb, v, vt, cvf, cvb, GLA_HEADS, nctx_blk)

    assert 2 * s5_b_re.shape[-1] == PAIR_W
    ops = _s5_operators(s5_lam_re[0], s5_lam_im[0], s5_log_dt[0], s5_b_re[0], s5_b_im[0], s5_c_re[0], s5_c_im[0], s5_d[0])
    y5 = _s5(u_cr, ops, nc, nctx // S5_L, bsz)

    rwh, rwl, rb = _router_tables(router_w, router_bias)
    utri = jnp.asarray(np.triu(np.ones((TR, TR), np.float32), 1), dtype=BF16)
    ones = jnp.ones((TR, TR), BF16)
    x1, pay, aux, meta, tab, counts = _mixout(True, GLA_HEADS, nctx_blk, 0, xs, o_f, o_b, g, y5, mod[0],
                                              row(gla_norm[0]), s5_glu_w[0].astype(BF16), row(s5_glu_b[0]),
                                              ab_w_out[0].astype(BF16), row(norm_ffn[0]), rwh, rwl, rb, utri, ones)
    assert moe_w_gate.shape[1] == N_GROUPS * EXPERTS_PER_GROUP
    seg, ys = _moe_layer(tab, meta, counts, pay, (0, moe_w_gate, moe_w_up, moe_w_down))

    lb_all = jax.nn.softmax(hg_lb_logits.astype(F32), axis=0)
    lb_all = jnp.cumsum(lb_all, axis=0) - lb_all[0]
    (x2, qf, kf, qb, kb, v, vt, g, cvf, cvb) = _inproj1(seg, tab, aux, ys, x1, mod[0], mod[1], row(norm_mix[1]),
                                                          hg_w_in[0].astype(BF16), row(lb_all[1]), consts, nctx_blk)
    o_f, o_b = _recurrence(qf, kf, qb, kb, v, vt, cvf, cvb, HG_HEADS, nctx_blk)
    x3, pay, aux, meta, tab, counts = _mixout(False, HG_HEADS, nctx_blk, nctx_blk, x2, o_f, o_b, g, None, mod[1],
                                              row(hg_norm[0]), None, None, hg_w_out[0].astype(BF16),
                                              row(norm_ffn[1]), rwh, rwl, rb, utri, ones)
    seg, ys = _moe_layer(tab, meta, counts, pay, (1, moe_w_gate, moe_w_up, moe_w_down))
    return _final(seg, tab, aux, ys, x3, mod[1], row(final_norm))
```

```python
import functools
import math

import numpy as np
import jax
import jax.numpy as jnp
from jax import lax
from jax.experimental import pallas as pl
from jax.experimental.pallas import tpu as pltpu

F32, BF16, I32 = jnp.float32, jnp.bfloat16, jnp.int32

EPS = 1e-6
CHUNK = 64
GRID_W = 64
GLA_HEADS = 4
GLA_GATE_NORM = 16.0
HG_HEADS = 8
N_GROUPS = 8
EXPERTS_PER_GROUP = 4
TOP_K = 2

LANES = 128
SUBLANES = 8
TM = 256
CPT = TM // CHUNK
HEAD_W = 128
S5_L = 16
MOD_ROWS = 16
COL_BLK = 256
CSUM_ROWS = TM + 32
S5_MAX_ROWS = 1024
SCAN_UNROLL = 8
STACK_HEADS = 4
RUN = SUBLANES
RUN_SHIFT = 3
NSUB = 2
TR = NSUB * TM
LROWS = TR + N_GROUPS * RUN
TAB_START, TAB_COUNT, TAB_CARRY = 0, 1, 2
AUX_LPOS = 4
VMEM_LIMIT = 56 * 1024 * 1024


def _dot(a, b):
    return jnp.dot(a, b, preferred_element_type=F32)


def _dot_nt(a, b):
    return lax.dot_general(a, b, (((1,), (1,)), ((), ())), preferred_element_type=F32)


def _split(x):
    hi = x.astype(BF16)
    lo = (x - hi.astype(F32)).astype(BF16)
    return hi, lo


def _sigmoid(x):
    return 1.0 / (1.0 + jnp.exp(-x))


def _silu(x):
    return x * _sigmoid(x)


def _gelu_tanh(x):
    return 0.5 * x * (1.0 + jnp.tanh(math.sqrt(2.0 / math.pi) * (x + 0.044715 * (x * x * x))))


def _rms(x):
    return x * lax.rsqrt(jnp.mean(x * x, axis=-1, keepdims=True) + EPS)


def _head_rms(o, gain, n_heads):
    outs = []
    for h in range(n_heads):
        oh = o[:, h * HEAD_W:(h + 1) * HEAD_W]
        outs.append(_rms(oh) * gain)
    return jnp.concatenate(outs, axis=1)


def _pairs(a):
    return a.reshape((a.shape[0] // NSUB, NSUB) + a.shape[1:])


def _unpairs(a):
    return a.reshape((a.shape[0] * NSUB,) + a.shape[2:])


def _mod_row(mod_ref, is_ctx, ctx_row, batch_row):
    return mod_ref[pl.ds(jnp.where(is_ctx, ctx_row, batch_row), 1), :]


def _params(n_grid_dims, vmem=VMEM_LIMIT, flags=None):
    return pltpu.CompilerParams(dimension_semantics=("arbitrary",) * n_grid_dims, vmem_limit_bytes=vmem, flags=flags)


def _const_spec(shape):
    nd = len(shape)
    return pl.BlockSpec(shape, lambda *_: (0,) * nd)


def _mod_kernel(c_ref, w_ref, b_ref, o_ref):
    s = _silu(c_ref[...])
    shi, slo = _split(s)
    whi, wlo = _split(w_ref[0])
    o_ref[0] = _dot(shi, whi) + _dot(shi, wlo) + _dot(slo, whi) + b_ref[0]


def _modulation(cond, ada_w, ada_b):
    depth, d, six_d = ada_w.shape
    nchunk = six_d // d
    return pl.pallas_call(
        _mod_kernel,
        grid=(depth, nchunk),
        in_specs=[
            _const_spec((MOD_ROWS, d)),
            pl.BlockSpec((1, d, d), lambda l, n: (l, 0, n)),
            pl.BlockSpec((1, 1, d), lambda l, n: (l, 0, n)),
        ],
        out_specs=pl.BlockSpec((1, MOD_ROWS, d), lambda l, n: (l, 0, n)),
        out_shape=jax.ShapeDtypeStruct((depth, MOD_ROWS, six_d), F32),
        compiler_params=_params(2),
        name="modulation",
    )(cond, ada_w, ada_b.reshape(depth, 1, six_d))


def _chunk_matrices():
    t = np.arange(TM)
    c, l = t // CHUNK, t % CHUNK
    same = (c[:, None] == c[None, :]).astype(np.float32)
    li, ls = l[:, None], l[None, :]
    mid_f = CHUNK // 2 - 1
    mid_b = CHUNK // 2
    d1f = same * ((ls <= li).astype(np.float32) - (ls <= mid_f).astype(np.float32))
    d1b = same * ((ls >= li).astype(np.float32) - (ls >= mid_b).astype(np.float32))
    inchunk = (np.arange(SUBLANES)[:, None] == c[None, :]).astype(np.float32)
    mf = np.concatenate([inchunk * (l <= mid_f), inchunk * (l > mid_f), inchunk])
    mb = np.concatenate([inchunk * (l >= mid_b), inchunk * (l < mid_b), inchunk])
    pad = np.zeros((CSUM_ROWS - TM - 3 * SUBLANES, TM), np.float32)
    as_bf16 = lambda a: jnp.asarray(a, dtype=BF16)
    return as_bf16(np.concatenate([d1f, mf, pad])), as_bf16(np.concatenate([d1b, mb, pad]))


def _gate_outputs(q, k, la, csum_ref, q_out, k_out, cv_out):
    hi, lo = _split(la)
    op = csum_ref[...]
    sums = _dot(op, hi) + _dot(op, lo)
    d1 = sums[0:TM]
    q_out[...] = (q * jnp.exp(d1)).astype(BF16)
    k_out[...] = (k * jnp.exp(-d1)).astype(BF16)
    cv_out[...] = jnp.exp(sums[TM:TM + 3 * SUBLANES])


PAIR_W = 32
PAIRS_PER_TILE = LANES // PAIR_W
CROWS = TM // S5_L


def _to_chunk_rows(u, us_ref, ucr_ref):
    for q in range(us_ref.shape[0]):
        us_ref[q] = u[:, q * LANES:(q + 1) * LANES]
    lane = lax.broadcasted_iota(I32, (CROWS, LANES), 1)
    for q in range(us_ref.shape[0]):
        tiles = [[] for _ in range(PAIRS_PER_TILE)]
        for t in range(S5_L // PAIRS_PER_TILE):
            acc = [None] * PAIRS_PER_TILE
            for jj in range(PAIRS_PER_TILE):
                piece = us_ref[q, pl.ds(t * PAIRS_PER_TILE + jj, CROWS, stride=S5_L), :]
                here = (lane >= PAIR_W * jj) & (lane < PAIR_W * (jj + 1))
                for pp in range(PAIRS_PER_TILE):
                    shift = (PAIR_W * (jj - pp)) % LANES
                    moved = pltpu.roll(piece, shift, axis=1) if shift else piece
                    acc[pp] = moved if acc[pp] is None else jnp.where(here, moved, acc[pp])
            for pp in range(PAIRS_PER_TILE):
                tiles[pp].append(acc[pp])
        for pp in range(PAIRS_PER_TILE):
            ucr_ref[q * PAIRS_PER_TILE + pp] = jnp.concatenate(tiles[pp], axis=1).astype(BF16)


def _from_chunk_rows(ycr_ref, ys_ref):
    lane = lax.broadcasted_iota(I32, (CROWS, LANES), 1)
    for q in range(ycr_ref.shape[0] // PAIRS_PER_TILE):
        for i in range(S5_L):
            t, ii = divmod(i, PAIRS_PER_TILE)
            acc = None
            for pp in range(PAIRS_PER_TILE):
                piece = ycr_ref[q * PAIRS_PER_TILE + pp, :, t * LANES:(t + 1) * LANES].astype(F32)
                shift = (PAIR_W * (pp - ii)) % LANES
                if shift:
                    piece = pltpu.roll(piece, shift, axis=1)
                here = (lane >= PAIR_W * pp) & (lane < PAIR_W * (pp + 1))
                acc = piece if acc is None else jnp.where(here, piece, acc)
            ys_ref[q, pl.ds(i, CROWS, stride=S5_L), :] = acc
    return jnp.concatenate([ys_ref[q] for q in range(ys_ref.shape[0])], axis=1)


def _inproj0_kernel(nctx_blk, nb, gla_w, q_scale, x_ref, ctx_ref, pos_ref, mod_ref, nw_ref, wm_ref, wa_ref, a2_ref,
                    ab_ref, csf_ref, csb_ref,
                    xs_ref, qf_ref, kf_ref, qb_ref, kb_ref, v_ref, vt_ref, g_ref, ucr_ref, cvf_ref, cvb_ref, us_ref):
    bp, j = pl.program_id(0), pl.program_id(1)
    is_ctx = j < nctx_blk
    xin = jnp.where(is_ctx, ctx_ref[0], x_ref[0] + pos_ref[...])
    xs_ref[0] = xin
    d = xin.shape[-1]
    hb = []
    for s in range(NSUB):
        mod = _mod_row(mod_ref, is_ctx, nb, bp * NSUB + s)
        hb.append((_rms(xin[s]) * nw_ref[...] * (1.0 + mod[:, d:2 * d]) + mod[:, 0:d]).astype(BF16))
    hb = jnp.concatenate(hb, axis=0)
    w = gla_w
    u = _dot(hb, wm_ref[:, 4 * w:5 * w])
    for s in range(NSUB):
        _to_chunk_rows(u[s * TM:(s + 1) * TM], us_ref, ucr_ref.at[:, 0, s, 0])
    a = _dot(hb, wa_ref[...]).astype(BF16)
    for n in range(w // COL_BLK):
        cs = slice(n * COL_BLK, (n + 1) * COL_BLK)
        proj = lambda i: _dot(hb, wm_ref[:, i * w + n * COL_BLK:i * w + (n + 1) * COL_BLK])
        q = proj(0) * q_scale
        k = proj(1)
        v = proj(2)
        g = proj(3)
        las = []
        for di in range(2):
            zs = slice(di * w + n * COL_BLK, di * w + (n + 1) * COL_BLK)
            z = _dot(a, a2_ref[:, zs]) + ab_ref[:, zs]
            las.append((jnp.minimum(z, 0.0) - jnp.log(1.0 + jnp.exp(-jnp.abs(z)))) * (1.0 / GLA_GATE_NORM))
        for s in range(NSUB):
            rows = slice(s * TM, (s + 1) * TM)
            g_ref[0, s, :, cs] = g[rows].astype(BF16)
            v_ref[0, s, :, cs] = v[rows].astype(BF16)
            vt_ref[0, s, cs, :] = v[rows].T.astype(BF16)
            _gate_outputs(q[rows], k[rows], las[0][rows], csf_ref,
                          qf_ref.at[0, s, :, cs], kf_ref.at[0, s, :, cs], cvf_ref.at[0, s, 0, :, cs])
            _gate_outputs(q[rows], k[rows], las[1][rows], csb_ref,
                          qb_ref.at[0, s, :, cs], kb_ref.at[0, s, :, cs], cvb_ref.at[0, s, 0, :, cs])


def _inproj0(x, ctx, pos, mod, nw, wm, wa, a2, ab, consts, q_scale):
    bsz, seq, d = x.shape
    nctx = ctx.shape[1]
    nctx_blk, nlat_blk = nctx // TM, seq // TM
    nblk = nctx_blk + nlat_blk
    nt = nblk * TM
    nbp = bsz // NSUB
    w = wm.shape[1] // 5
    csf, csb = consts
    in_specs = [
        pl.BlockSpec((1, NSUB, TM, d), lambda b, j: (b, 0, jnp.maximum(j - nctx_blk, 0), 0)),
        pl.BlockSpec((1, NSUB, TM, d), lambda b, j: (b, 0, jnp.minimum(j, nctx_blk - 1), 0)),
        pl.BlockSpec((TM, d), lambda b, j: (jnp.maximum(j - nctx_blk, 0), 0)),
        _const_spec(mod.shape), _const_spec(nw.shape), _const_spec(wm.shape), _const_spec(wa.shape),
        _const_spec(a2.shape), _const_spec(ab.shape), _const_spec(csf.shape), _const_spec(csb.shape),
    ]
    tokspec = lambda width: pl.BlockSpec((1, NSUB, TM, width), lambda b, j: (b, 0, j, 0))
    cvspec = pl.BlockSpec((1, NSUB, 1, 3 * SUBLANES, w), lambda b, j: (b, 0, j, 0, 0))
    npair, crow_w = w // PAIR_W, S5_L * PAIR_W
    out_specs = [tokspec(d), tokspec(w), tokspec(w), tokspec(w), tokspec(w), tokspec(w),
                 pl.BlockSpec((1, NSUB, w, TM), lambda b, j: (b, 0, 0, j)), tokspec(w),
                 pl.BlockSpec((npair, 1, NSUB, 1, CROWS, crow_w), lambda b, j: (0, b, 0, j, 0, 0)), cvspec, cvspec]
    sds = jax.ShapeDtypeStruct
    out_shape = [sds((nbp, NSUB, nt, d), F32)] + [sds((nbp, NSUB, nt, w), BF16)] * 5 + [sds((nbp, NSUB, w, nt), BF16)] + \
                [sds((nbp, NSUB, nt, w), BF16), sds((npair, nbp, NSUB, nblk, CROWS, crow_w), BF16)] + \
                [sds((nbp, NSUB, nblk, 3 * SUBLANES, w), F32)] * 2
    outs = pl.pallas_call(
        functools.partial(_inproj0_kernel, nctx_blk, bsz, w, q_scale),
        grid=(nbp, nblk), in_specs=in_specs, out_specs=out_specs, out_shape=out_shape,
        scratch_shapes=[pltpu.VMEM((w // LANES, TM, LANES), F32)],
        compiler_params=_params(2), name="inproj_gla_s5",
    )(_pairs(x), _pairs(ctx), pos, mod, nw, wm, wa, a2, ab, csf, csb)
    ucr = outs[8].reshape(npair, bsz * nblk * CROWS, crow_w)
    return [_unpairs(o) for o in outs[:8]] + [ucr] + [_unpairs(o) for o in outs[9:]]


def _recur_kernel(n_heads, qf_ref, kf_ref, vf_ref, vtf_ref, cvf_ref, qb_ref, kb_ref, vb_ref, vtb_ref, cvb_ref,
                  of_ref, ob_ref, s_ref):
    @pl.when(pl.program_id(1) == 0)
    def _():
        s_ref[...] = jnp.zeros_like(s_ref)

    hrows = STACK_HEADS * CHUNK
    ri = lax.broadcasted_iota(I32, (hrows, hrows), 0)
    ci = lax.broadcasted_iota(I32, (hrows, hrows), 1)
    same_head = (ri // CHUNK) == (ci // CHUNK)
    keep = (same_head & (ci <= ri), same_head & (ci >= ri))
    pw = 2 * HEAD_W
    prow = lax.broadcasted_iota(I32, (2 * CHUNK, pw), 0)
    zero_blk = jnp.zeros((HEAD_W, HEAD_W), BF16)
    stack = lambda ref, r0, h_first: jnp.concatenate(
        [ref[0, r0:r0 + CHUNK, h * HEAD_W:(h + 1) * HEAD_W] for h in range(h_first, h_first + STACK_HEADS)], axis=0)
    dirs = ((qf_ref, kf_ref, vf_ref, vtf_ref, cvf_ref, of_ref), (qb_ref, kb_ref, vb_ref, vtb_ref, cvb_ref, ob_ref))
    for cc in range(CPT):
        for d, (q_ref, k_ref, v_ref, vt_ref, cv_ref, o_ref) in enumerate(dirs):
            c = cc if d == 0 else CPT - 1 - cc
            r0 = c * CHUNK
            p0 = (c // 2) * 2 * CHUNK
            in_chunk = (prow >= r0 - p0) & (prow < r0 - p0 + CHUNK)
            e_mid = cv_ref[0, 0, c:c + 1, :]
            e_rest = cv_ref[0, 0, SUBLANES + c:SUBLANES + c + 1, :]
            e_all = cv_ref[0, 0, 2 * SUBLANES + c:2 * SUBLANES + c + 1, :]
            o_intra = []
            for h_first in range(0, n_heads, STACK_HEADS):
                scores = _dot_nt(stack(q_ref, r0, h_first), stack(k_ref, r0, h_first))
                attn = jnp.where(keep[d], scores, 0.0).astype(BF16)
                o_intra.append(_dot(attn, stack(v_ref, r0, h_first)))
            o_intra = jnp.concatenate(o_intra, axis=0)
            for p in range(n_heads // 2):
                ps = slice(p * pw, (p + 1) * pw)
                h0, h1 = 2 * p, 2 * p + 1
                hs0, hs1 = slice(h0 * HEAD_W, h1 * HEAD_W), slice(h1 * HEAD_W, (h1 + 1) * HEAD_W)
                st0, st1 = s_ref[d, h0], s_ref[d, h1]
                sb0 = (st0 * e_mid[:, hs0]).astype(BF16)
                sb1 = (st1 * e_mid[:, hs1]).astype(BF16)
                carried = jnp.concatenate([jnp.concatenate([sb0, zero_blk], axis=1),
                                           jnp.concatenate([zero_blk, sb1], axis=1)], axis=0)
                o_inter = _dot_nt(q_ref[0, r0:r0 + CHUNK, ps], carried)
                o_ref[0, r0:r0 + CHUNK, hs0] = (o_inter[:, 0:HEAD_W] + o_intra[h0 * CHUNK:h1 * CHUNK]).astype(o_ref.dtype)
                o_ref[0, r0:r0 + CHUNK, hs1] = (o_inter[:, HEAD_W:pw]
                                                + o_intra[h1 * CHUNK:(h1 + 1) * CHUNK]).astype(o_ref.dtype)
                kp = k_ref[0, p0:p0 + 2 * CHUNK, ps]
                kp = jnp.where(in_chunk, kp, jnp.zeros_like(kp))
                ds = _dot(vt_ref[0, ps, p0:p0 + 2 * CHUNK], kp)
                s_ref[d, h0] = st0 * e_all[:, hs0] + ds[0:HEAD_W, 0:HEAD_W] * e_rest[:, hs0]
                s_ref[d, h1] = st1 * e_all[:, hs1] + ds[HEAD_W:pw, HEAD_W:pw] * e_rest[:, hs1]


def _recurrence(qf, kf, qb, kb, v, vt, cvf, cvb, n_heads, nctx_blk):
    bsz, nt, w = qf.shape
    nblk = nt // TM
    fwd = lambda j: j
    bwd = lambda j: jnp.where(j < nctx_blk, nctx_blk - 1 - j, nblk - 1 - (j - nctx_blk))
    specs = []
    for order in (fwd, bwd):
        tok = pl.BlockSpec((1, TM, w), lambda b, j, o=order: (b, o(j), 0))
        specs += [tok, tok, tok,
                  pl.BlockSpec((1, w, TM), lambda b, j, o=order: (b, 0, o(j))),
                  pl.BlockSpec((1, 1, 3 * SUBLANES, w), lambda b, j, o=order: (b, o(j), 0, 0))]
    out_specs = [pl.BlockSpec((1, TM, w), lambda b, j: (b, j, 0)),
                 pl.BlockSpec((1, TM, w), lambda b, j: (b, bwd(j), 0))]
    return pl.pallas_call(
        functools.partial(_recur_kernel, n_heads),
        grid=(bsz, nblk), in_specs=specs, out_specs=out_specs,
        out_shape=[jax.ShapeDtypeStruct((bsz, nt, w), BF16)] * 2,
        scratch_shapes=[pltpu.VMEM((2, n_heads, HEAD_W, HEAD_W), F32)],
        compiler_params=_params(2), name=f"recurrence_h{n_heads}",
    )(qf, kf, v, vt, cvf, qb, kb, v, vt, cvb)


def _s5_operators(lam_re, lam_im, log_dt, b_re, b_im, c_re, c_im, d_skip):
    L = S5_L
    ndir, ng, p = lam_re.shape
    hs = b_re.shape[-1]
    dt = jnp.exp(log_dt)[..., None]
    lr, li = lam_re * dt, lam_im * dt
    mag = jnp.exp(lr)
    a_re, a_im = mag * jnp.cos(li), mag * jnp.sin(li)
    den = lam_re * lam_re + lam_im * lam_im
    f_re = ((a_re - 1.0) * lam_re + a_im * lam_im) / den
    f_im = (a_im * lam_re - (a_re - 1.0) * lam_im) / den
    bb_re = f_re[..., None] * b_re - f_im[..., None] * b_im
    bb_im = f_re[..., None] * b_im + f_im[..., None] * b_re
    m = jnp.arange(L + 1, dtype=F32)[:, None, None, None]
    pw_mag = jnp.exp(m * lr[None])
    pw_re, pw_im = pw_mag * jnp.cos(m * li[None]), pw_mag * jnp.sin(m * li[None])
    eye = jnp.eye(2, dtype=F32)
    npair, pw, cw = ng // 2, 2 * hs, 2 * L * hs

    toep = jnp.zeros((npair, cw, cw), F32)
    wz_p, vo_p = [], []
    for di in range(ndir):
        pr, pi = pw_re[:L, di, :, :, None], pw_im[:L, di, :, :, None]
        ab_re = pr * bb_re[di][None] - pi * bb_im[di][None]
        ab_im = pr * bb_im[di][None] + pi * bb_re[di][None]
        kf = jnp.einsum('mgnk,ghn->gkmh', ab_re, c_re[di]) - jnp.einsum('mgnk,ghn->gkmh', ab_im, c_im[di])
        if di == 1:
            kf = kf[:, :, ::-1, :]
        kp = jnp.einsum('pskmh,st->pskmth', kf.reshape(npair, 2, hs, L, hs), eye).reshape(npair, pw, cw)
        if di == 0:
            kpad = jnp.pad(kp, ((0, 0), (0, 0), (cw, 0)))
            blocks = [kpad[:, :, cw - pw * j:2 * cw - pw * j] for j in range(L)]
        else:
            kpad = jnp.pad(kp, ((0, 0), (0, 0), (0, cw)))
            blocks = [kpad[:, :, pw * (L - 1 - j):pw * (L - 1 - j) + cw] for j in range(L)]
        toep = toep + jnp.stack(blocks, axis=1).reshape(npair, cw, cw)
        sel_i = (lambda a: a[::-1]) if di == 0 else (lambda a: a)
        pack_w = lambda a: jnp.einsum('jpsnk,st->pjsktn', sel_i(a).reshape(L, npair, 2, p, hs), eye).reshape(npair, cw, 2 * p)
        wz_p.append(jnp.concatenate([pack_w(ab_re), pack_w(ab_im)], axis=2))
        qr, qi = pw_re[1:, di, :, None, :], pw_im[1:, di, :, None, :]
        ca_re = c_re[di][None] * qr - c_im[di][None] * qi
        ca_im = c_re[di][None] * qi + c_im[di][None] * qr
        sel_o = (lambda a: a) if di == 0 else (lambda a: a[::-1])
        pack_v = lambda a: jnp.einsum('ipshn,st->psnith', sel_o(a).reshape(L, npair, 2, hs, p), eye).reshape(npair, 2 * p, cw)
        vo_p.append(jnp.concatenate([pack_v(ca_re), -pack_v(ca_im)], axis=1))
    wz_p, vo_p = jnp.stack(wz_p, axis=1), jnp.stack(vo_p, axis=1)
    al = jnp.stack([pw_re[L], pw_im[L]], axis=1)
    al_p = jnp.transpose(al.reshape(ndir, 2, npair, 2 * p), (2, 0, 1, 3))
    d_p = jnp.tile(d_skip.reshape(npair, 1, pw), (1, L, 1)).reshape(npair, 1, cw)
    return wz_p.astype(BF16), vo_p.astype(BF16), toep.astype(BF16), al_p, d_p


def _s5_row_block(rows_total):
    tile = 2 * SUBLANES
    return max(r for r in range(tile, S5_MAX_ROWS + 1, tile) if rows_total % r == 0)


def _s5_kernel(nc, ncc, bsz, u_ref, wz_ref, vo_ref, tp_ref, al_ref, d_ref, y_ref, zf, zb, xf, xb):
    rows_total = nc * bsz
    rb = _s5_row_block(rows_total)
    half = zf.shape[2]

    def z_body(i, carry):
        rows = pl.ds(pl.multiple_of(i * rb, rb), rb)
        ub = u_ref[0, rows, :]
        for z, di in ((zf, 0), (zb, 1)):
            zz = _dot(ub, wz_ref[0, di])
            z[0, rows, :] = zz[:, 0:half]
            z[1, rows, :] = zz[:, half:2 * half]
        return carry

    lax.fori_loop(0, rows_total // rb, z_body, 0)

    coef = [[jnp.broadcast_to(al_ref[0, di, ri:ri + 1, :], (bsz, half)) for ri in range(2)] for di in range(2)]
    orders = (lambda i: i, lambda i: jnp.where(i < ncc, ncc - 1 - i, nc - 1 - (i - ncc)))

    def scan_body(i, carry):
        new = []
        for di, (z, xs) in enumerate(((zf, xf), (zb, xb))):
            x_re, x_im = carry[2 * di], carry[2 * di + 1]
            a_re, a_im = coef[di]
            rows = pl.ds(orders[di](i), bsz, stride=nc)
            xs[0, rows, :] = x_re
            xs[1, rows, :] = x_im
            new += [a_re * x_re - a_im * x_im + z[0, rows, :], a_re * x_im + a_im * x_re + z[1, rows, :]]
        return tuple(new)

    zero = jnp.zeros((bsz, half), F32)
    lax.fori_loop(0, nc, scan_body, (zero,) * 4, unroll=SCAN_UNROLL)

    def y_body(i, carry):
        rows = pl.ds(pl.multiple_of(i * rb, rb), rb)
        ub = u_ref[0, rows, :]
        state = lambda xs: jnp.concatenate([xs[0, rows, :], xs[1, rows, :]], axis=1).astype(BF16)
        y = _dot(state(xf), vo_ref[0, 0]) + _dot(state(xb), vo_ref[0, 1])
        y_ref[0, rows, :] = (y + _dot(ub, tp_ref[0]) + ub.astype(F32) * d_ref[0]).astype(y_ref.dtype)
        return carry

    lax.fori_loop(0, rows_total // rb, y_body, 0)


def _s5(u_cr, ops, nc, ncc, bsz):
    wz_p, vo_p, toep_p, al_p, d_p = ops
    npair, rows, _ = u_cr.shape
    per_pair = lambda a: pl.BlockSpec((1,) + a.shape[1:], lambda p: (p,) + (0,) * (a.ndim - 1))
    return pl.pallas_call(
        functools.partial(_s5_kernel, nc, ncc, bsz),
        grid=(npair,),
        in_specs=[per_pair(u_cr), per_pair(wz_p), per_pair(vo_p), per_pair(toep_p), per_pair(al_p), per_pair(d_p)],
        out_specs=per_pair(u_cr),
        out_shape=jax.ShapeDtypeStruct(u_cr.shape, BF16),
        scratch_shapes=[pltpu.VMEM((2, rows, wz_p.shape[-1] // 2), F32)] * 4,
        compiler_params=_params(1), name="s5",
    )(u_cr, wz_p, vo_p, toep_p, al_p, d_p)


def _route(h2, rwh_ref, rwl_ref, rb_ref, utri_ref, ones_ref, carry_ref):
    hi, lo = _split(h2)
    logits = _dot_nt(rwh_ref[...], hi) + _dot_nt(rwl_ref[...], hi) + _dot_nt(rwh_ref[...], lo)
    aff = _sigmoid(logits)
    sel = aff + rb_ref[...]
    epg = EXPERTS_PER_GROUP
    s = [sel[N_GROUPS * p:N_GROUPS * (p + 1), :] for p in range(epg)]
    a = [aff[N_GROUPS * p:N_GROUPS * (p + 1), :] for p in range(epg)]
    m1, n1 = jnp.maximum(s[0], s[1]), jnp.minimum(s[0], s[1])
    m2, n2 = jnp.maximum(s[2], s[3]), jnp.minimum(s[2], s[3])
    score = jnp.maximum(m1, m2) + jnp.maximum(jnp.minimum(m1, m2), jnp.maximum(n1, n2))
    gi = lax.broadcasted_iota(I32, score.shape, 0)
    best = jnp.max(score, axis=0, keepdims=True)
    gidx = jnp.min(jnp.where(score == best, gi, N_GROUPS), axis=0, keepdims=True)
    onehot = gi == gidx
    gates = []
    for p in range(epg):
        ahead = jnp.zeros_like(score)
        for q in range(epg):
            if q != p:
                beats = (s[q] >= s[p]) if q < p else (s[q] > s[p])
                ahead = ahead + jnp.where(beats, 1.0, 0.0)
        picked = jnp.where(onehot, jnp.where(ahead < float(TOP_K) - 0.5, a[p], 0.0), 0.0)
        gates.append(jnp.sum(picked, axis=0, keepdims=True))
    den = gates[0] + gates[1] + gates[2] + gates[3]
    gates = [g / den for g in gates]
    oh = jnp.where(onehot, 1.0, 0.0)
    oh = jnp.concatenate([oh, jnp.zeros_like(oh)], axis=0).astype(BF16)
    before = _dot(oh, utri_ref[...])[0:N_GROUPS]
    count = _dot(oh, ones_ref[...])[0:N_GROUPS]
    run = jnp.floor((count + (RUN - 1.0)) * (1.0 / RUN)) * RUN
    carry = carry_ref[...]
    carry_ref[...] = carry + run
    starts, nxt = [], jnp.zeros_like(run[0:1])
    for g in range(N_GROUPS):
        starts.append(nxt)
        nxt = nxt + run[g:g + 1]
    lstart = jnp.concatenate(starts, axis=0)
    lpos = jnp.sum(jnp.where(onehot, lstart + before, 0.0), axis=0, keepdims=True)
    return lpos, gates, (lstart, count, carry)


def _mixout_kernel(has_s5, n_heads, nctx_blk, tok_off, nb, *refs):
    if has_s5:
        (x_ref, of_ref, ob_ref, g_ref, ycr_ref, mod_ref, gn_ref, gluw_ref, glub_ref, wo_ref, nf_ref,
         rwh_ref, rwl_ref, rb_ref, utri_ref, ones_ref,
         x1_ref, pay_ref, aux_ref, meta_ref, tab_ref, cnt_ref, carry_ref, ys_ref) = refs
    else:
        (x_ref, of_ref, ob_ref, g_ref, mod_ref, gn_ref, wo_ref, nf_ref,
         rwh_ref, rwl_ref, rb_ref, utri_ref, ones_ref,
         x1_ref, pay_ref, aux_ref, meta_ref, tab_ref, cnt_ref, carry_ref) = refs
    bp, j = pl.program_id(0), pl.program_id(1)

    @pl.when((bp == 0) & (j == 0))
    def _():
        carry_ref[...] = jnp.zeros_like(carry_ref)

    d = x_ref.shape[-1]
    merge = lambda ref: jnp.concatenate([ref[0, s] for s in range(NSUB)], axis=0)
    o = merge(of_ref).astype(F32) + merge(ob_ref).astype(F32)
    mixed = _head_rms(o, gn_ref[...], n_heads) * _silu(merge(g_ref).astype(F32))
    if has_s5:
        y5 = jnp.concatenate([_from_chunk_rows(ycr_ref.at[:, 0, s, 0], ys_ref) for s in range(NSUB)], axis=0)
        act = _gelu_tanh(y5)
        glu = act * _sigmoid(_dot(act.astype(BF16), gluw_ref[...]) + glub_ref[...])
        mixed = jnp.concatenate([mixed, glu], axis=1)
    proj = _dot(mixed.astype(BF16), wo_ref[...])
    h2 = []
    for s in range(NSUB):
        mod = _mod_row(mod_ref, j + tok_off < nctx_blk, nb, bp * NSUB + s)
        x1 = x_ref[0, s] + mod[:, 2 * d:3 * d] * proj[s * TM:(s + 1) * TM]
        x1_ref[0, s] = x1
        h2.append(_rms(x1) * nf_ref[...] * (1.0 + mod[:, 4 * d:5 * d]) + mod[:, 3 * d:4 * d])
    h2 = jnp.concatenate(h2, axis=0)

    lpos, gates, (lstart, count, carry) = _route(h2, rwh_ref, rwl_ref, rb_ref, utri_ref, ones_ref, carry_ref)
    gi = lax.broadcasted_iota(I32, (SUBLANES, TR), 0)
    meta_ref[0] = jnp.where(gi == 0, lpos.astype(I32), 0)
    ti = lax.broadcasted_iota(I32, (N_GROUPS, LANES), 1)
    first = lambda a: a[:, 0:LANES].astype(I32)
    tab_ref[0] = jnp.where(ti == TAB_START, first(lstart),
                           jnp.where(ti == TAB_COUNT, first(count), jnp.where(ti == TAB_CARRY, first(carry), 0)))
    cnt_ref[...] = carry_ref[...]

    li = lax.broadcasted_iota(I32, (LANES, TR), 0)
    gt = jnp.where(li == AUX_LPOS, lpos, 0.0)
    for p, g in enumerate(gates):
        gt = gt + jnp.where(li == p, g, 0.0)
    aux = gt.T
    pay_ref[...] = h2.astype(BF16)
    aux_ref[...] = aux


def _mixout(has_s5, n_heads, nctx_blk, tok_off, x, of, ob, g, y5, mod, gn, gluw, glub, wo, nf, rwh, rwl, rb, utri, ones):
    bsz, _, d = x.shape
    w = of.shape[-1]
    nblk_all = of.shape[1] // TM
    nblk = nblk_all - tok_off
    nbp = bsz // NSUB
    tok = lambda width: pl.BlockSpec((1, NSUB, TM, width), lambda b, j: (b, 0, j + tok_off, 0))
    ins, specs = [_pairs(x), _pairs(of), _pairs(ob), _pairs(g)], [tok(d), tok(w), tok(w), tok(w)]
    scratch = [pltpu.VMEM((N_GROUPS, TR), F32)]
    if has_s5:
        npair, _, crow_w = y5.shape
        ins.append(y5.reshape(npair, nbp, NSUB, nblk_all, CROWS, crow_w))
        specs.append(pl.BlockSpec((npair, 1, NSUB, 1, CROWS, crow_w), lambda b, j: (0, b, 0, j, 0, 0)))
        scratch.append(pltpu.VMEM((w // LANES, TM, LANES), F32))
    consts = [mod, gn] + ([gluw, glub] if has_s5 else []) + [wo, nf, rwh, rwl, rb, utri, ones]
    ins += consts
    specs += [_const_spec(c.shape) for c in consts]
    ntile = nbp * nblk
    lin = lambda b, j: b * nblk + j
    out_specs = [pl.BlockSpec((1, NSUB, TM, d), lambda b, j: (b, 0, j, 0)),
                 pl.BlockSpec((TR, d), lambda b, j: (lin(b, j), 0)),
                 pl.BlockSpec((TR, LANES), lambda b, j: (lin(b, j), 0)),
                 pl.BlockSpec((1, SUBLANES, TR), lambda b, j: (lin(b, j), 0, 0)),
                 pl.BlockSpec((1, N_GROUPS, LANES), lambda b, j: (lin(b, j), 0, 0)),
                 _const_spec((N_GROUPS, TR))]
    sds = jax.ShapeDtypeStruct
    out_shape = [sds((nbp, NSUB, nblk * TM, d), F32), sds((ntile * TR, d), BF16), sds((ntile * TR, LANES), F32),
                 sds((ntile, SUBLANES, TR), I32), sds((ntile, N_GROUPS, LANES), I32), sds((N_GROUPS, TR), F32)]
    outs = pl.pallas_call(
        functools.partial(_mixout_kernel, has_s5, n_heads, nctx_blk, tok_off, bsz),
        grid=(nbp, nblk), in_specs=specs, out_specs=out_specs, out_shape=out_shape,
        scratch_shapes=scratch,
        compiler_params=_params(2), name="mixout_s5" if has_s5 else "mixout",
    )(*ins)
    return [_unpairs(outs[0])] + list(outs[1:])


def _for_each_run(seg_ref, tab_ref, fn):
    for g in range(N_GROUPS):
        lstart = tab_ref[0, g, TAB_START]
        first = seg_ref[g] + tab_ref[0, g, TAB_CARRY]
        n_pieces = lax.shift_right_logical(tab_ref[0, g, TAB_COUNT] + (RUN - 1), RUN_SHIFT)

        def piece(k, carry, lstart=lstart, first=first):
            fn(pl.ds(pl.multiple_of(lstart + k * RUN, RUN), RUN), pl.ds(pl.multiple_of(first + k * RUN, RUN), RUN))
            return carry

        lax.fori_loop(0, n_pieces, piece, 0)


def _dispatch_kernel(seg_ref, cnt_ref, tab_ref, tab_prev_ref, meta_ref, pay_ref, aux_ref, hs_hbm, s_ref, z_ref, sem, zsem):
    i = pl.program_id(0)
    last = pl.num_programs(0) - 1
    slot = lax.rem(i, 2)
    d = pay_ref.shape[1]
    lpos = meta_ref[0, 0:1, :]
    rows = lax.broadcasted_iota(I32, (LROWS, TR), 0)
    pick = jnp.where(rows == lpos, 1.0, 0.0).astype(BF16)
    s_ref[slot, :, 0:d] = _dot(pick, pay_ref[...])
    ghi, glo = _split(aux_ref[...])
    s_ref[slot, :, d:d + LANES] = _dot(pick, ghi) + _dot(pick, glo)

    def runs(t_ref, sl, op):
        copy = lambda src, dst: pltpu.make_async_copy(s_ref.at[sl, src], hs_hbm.at[dst], sem.at[sl])
        _for_each_run(seg_ref, t_ref, lambda src, dst: getattr(copy(src, dst), op)())

    runs(tab_ref, slot, "start")

    @pl.when(i > 0)
    def _():
        runs(tab_prev_ref, 1 - slot, "wait")

    @pl.when(i == last)
    def _():
        runs(tab_ref, slot, "wait")
        z_ref[...] = jnp.zeros_like(z_ref)
        for op in ("start", "wait"):
            for g in range(N_GROUPS + 1):
                begin = seg_ref[g] + cnt_ref[g]

                def piece(k, carry, begin=begin):
                    dst = hs_hbm.at[pl.ds(pl.multiple_of(begin + k * RUN, RUN), RUN)]
                    getattr(pltpu.make_async_copy(z_ref, dst, zsem), op)()
                    return carry

                lax.fori_loop(0, lax.shift_right_logical(seg_ref[g + 1] - begin, RUN_SHIFT), piece, 0)


def _dispatch(seg, cnt, tab, meta, pay, aux, n_sorted):
    ntile = meta.shape[0]
    d = pay.shape[1]
    width = d + LANES
    grid_spec = pltpu.PrefetchScalarGridSpec(
        num_scalar_prefetch=2, grid=(ntile,),
        in_specs=[pl.BlockSpec((1, N_GROUPS, LANES), lambda i, seg, cnt: (i, 0, 0), memory_space=pltpu.SMEM),
                  pl.BlockSpec((1, N_GROUPS, LANES), lambda i, seg, cnt: (jnp.maximum(i - 1, 0), 0, 0),
                               memory_space=pltpu.SMEM),
                  pl.BlockSpec((1, SUBLANES, TR), lambda i, seg, cnt: (i, 0, 0)),
                  pl.BlockSpec((TR, d), lambda i, seg, cnt: (i, 0)),
                  pl.BlockSpec((TR, LANES), lambda i, seg, cnt: (i, 0))],
        out_specs=pl.BlockSpec(memory_space=pl.ANY),
        scratch_shapes=[pltpu.VMEM((2, LROWS, width), F32), pltpu.VMEM((RUN, width), F32),
                        pltpu.SemaphoreType.DMA((2,)), pltpu.SemaphoreType.DMA(())])
    return pl.pallas_call(
        _dispatch_kernel, grid_spec=grid_spec,
        out_shape=jax.ShapeDtypeStruct((n_sorted, width), F32),
        compiler_params=_params(1), name="moe_dispatch",
    )(seg, cnt, tab, tab, meta, pay, aux)


def _moe_kernel(tg_ref, nv_ref, hs_ref, wg_ref, wu_ref, wd_ref, ys_ref, wgb, wub, wdb):
    i = pl.program_id(0)
    valid = i < nv_ref[0]

    @pl.when(valid & ((i == 0) | (tg_ref[i] != tg_ref[jnp.maximum(i - 1, 0)])))
    def _():
        wgb[...] = wg_ref[...].astype(BF16)
        wub[...] = wu_ref[...].astype(BF16)
        wdb[...] = wd_ref[...].astype(BF16)

    @pl.when(valid)
    def _():
        d = wg_ref.shape[1]
        h = hs_ref[:, 0:d].astype(BF16)
        gate = hs_ref[:, d:d + LANES]
        acc = jnp.zeros((TR, d), F32)
        for p in range(EXPERTS_PER_GROUP):
            act = _silu(_dot(h, wgb[p])) * _dot(h, wub[p]) * gate[:, p:p + 1]
            acc = acc + _dot(act.astype(BF16), wdb[p])
        ys_ref[...] = acc

    @pl.when(jnp.logical_not(valid))
    def _():
        ys_ref[...] = jnp.zeros_like(ys_ref)


def _moe(tile_group, n_valid, hs, layer, wg, wu, wd):
    n_sorted, pay_w = hs.shape
    ntile = n_sorted // TR
    d = wg.shape[2]
    epg = EXPERTS_PER_GROUP
    clamp = lambda i, tg, nv: jnp.minimum(i, nv[0] - 1)
    wspec = lambda a: pl.BlockSpec((None, epg) + a.shape[2:], lambda i, tg, nv: (layer, tg[i], 0, 0))
    grid_spec = pltpu.PrefetchScalarGridSpec(
        num_scalar_prefetch=2, grid=(ntile,),
        in_specs=[pl.BlockSpec((TR, pay_w), lambda i, tg, nv: (clamp(i, tg, nv), 0)), wspec(wg), wspec(wu), wspec(wd)],
        out_specs=pl.BlockSpec((TR, d), lambda i, tg, nv: (i, 0)),
        scratch_shapes=[pltpu.VMEM((epg,) + wg.shape[2:], BF16), pltpu.VMEM((epg,) + wu.shape[2:], BF16),
                        pltpu.VMEM((epg,) + wd.shape[2:], BF16)])
    return pl.pallas_call(
        _moe_kernel, grid_spec=grid_spec,
        out_shape=jax.ShapeDtypeStruct((n_sorted, d), F32),
        compiler_params=_params(1), name="moe_experts",
    )(tile_group, n_valid, hs, wg, wu, wd)


def _route_specs(bsz, nblk):
    last = bsz * nblk - 1
    cur = pl.BlockSpec((1, N_GROUPS, LANES), lambda b, j, seg: (b * nblk + j, 0, 0), memory_space=pltpu.SMEM)
    nxt = pl.BlockSpec((1, N_GROUPS, LANES), lambda b, j, seg: (jnp.minimum(b * nblk + j + 1, last), 0, 0),
                       memory_space=pltpu.SMEM)
    aux = pl.BlockSpec((TR, LANES), lambda b, j, seg: (b * nblk + j, 0))
    return [cur, nxt, aux]


def _gather_ffn(step, n_steps, seg_ref, tab_ref, tab_next_ref, aux_ref, ys_hbm, buf, sem):
    def runs(t_ref, slot, op):
        copy = lambda dst, src: pltpu.make_async_copy(ys_hbm.at[src], buf.at[slot, dst], sem.at[slot])
        _for_each_run(seg_ref, t_ref, lambda dst, src: getattr(copy(dst, src), op)())

    slot = lax.rem(step, 2)

    @pl.when(step == 0)
    def _():
        buf[...] = jnp.zeros_like(buf)
        runs(tab_ref, 0, "start")

    @pl.when(step + 1 < n_steps)
    def _():
        runs(tab_next_ref, 1 - slot, "start")

    runs(tab_ref, slot, "wait")
    lpos = aux_ref[:, AUX_LPOS:AUX_LPOS + 1].astype(I32)
    cols = lax.broadcasted_iota(I32, (TR, LROWS), 1)
    pick = jnp.where(cols == lpos, 1.0, 0.0).astype(BF16)
    return _dot(pick, buf[slot].astype(BF16))


def _inproj1_kernel(nctx_blk, nb, seg_ref, tab_ref, tab_next_ref, aux_ref, ys_hbm, x_ref, mod0_ref, mod1_ref, nw_ref,
                    w_ref, lb_ref, csf_ref, csb_ref,
                    x2_ref, qf_ref, kf_ref, qb_ref, kb_ref, v_ref, vt_ref, g_ref, cvf_ref, cvb_ref, buf, sem):
    bp, j = pl.program_id(0), pl.program_id(1)
    nblk = pl.num_programs(1)
    ffn = _gather_ffn(bp * nblk + j, pl.num_programs(0) * nblk, seg_ref, tab_ref, tab_next_ref, aux_ref, ys_hbm,
                      buf, sem)
    d = x_ref.shape[-1]
    hb = []
    for s in range(NSUB):
        mod0 = _mod_row(mod0_ref, j < nctx_blk, nb, bp * NSUB + s)
        mod1 = _mod_row(mod1_ref, j < nctx_blk, nb, bp * NSUB + s)
        x2 = x_ref[0, s] + mod0[:, 5 * d:6 * d] * ffn[s * TM:(s + 1) * TM]
        x2_ref[0, s] = x2
        hb.append((_rms(x2) * nw_ref[...] * (1.0 + mod1[:, d:2 * d]) + mod1[:, 0:d]).astype(BF16))
    hb = jnp.concatenate(hb, axis=0)
    proj = lambda n: _dot(hb, w_ref[:, n * d:(n + 1) * d])
    q = _silu(proj(0))
    lb = lb_ref[...]
    v = proj(3)
    g = proj(4)
    f_f = lb + (1.0 - lb) * _sigmoid(proj(1))
    f_b = lb + (1.0 - lb) * _sigmoid(proj(2))
    for s in range(NSUB):
        rows = slice(s * TM, (s + 1) * TM)
        v_ref[0, s] = v[rows].astype(BF16)
        vt_ref[0, s] = v[rows].T.astype(BF16)
        g_ref[0, s] = g[rows].astype(BF16)
        _gate_outputs(q[rows], 1.0 - f_f[rows], jnp.log(f_f[rows]), csf_ref,
                      qf_ref.at[0, s], kf_ref.at[0, s], cvf_ref.at[0, s, 0])
        _gate_outputs(q[rows], 1.0 - f_b[rows], jnp.log(f_b[rows]), csb_ref,
                      qb_ref.at[0, s], kb_ref.at[0, s], cvb_ref.at[0, s, 0])


def _inproj1(seg, tab, aux, ys, x1, mod0, mod1, nw, w_in, lb, consts, nctx_blk):
    bsz, nt, d = x1.shape
    nblk = nt // TM
    nbp = bsz // NSUB
    csf, csb = consts
    cs = lambda a: pl.BlockSpec(a.shape, lambda b, j, seg: (0,) * a.ndim)
    tok = lambda width: pl.BlockSpec((1, NSUB, TM, width), lambda b, j, seg: (b, 0, j, 0))
    cvspec = pl.BlockSpec((1, NSUB, 1, 3 * SUBLANES, d), lambda b, j, seg: (b, 0, j, 0, 0))
    grid_spec = pltpu.PrefetchScalarGridSpec(
        num_scalar_prefetch=1, grid=(nbp, nblk),
        in_specs=_route_specs(nbp, nblk) + [
                  pl.BlockSpec(memory_space=pl.ANY), tok(d), cs(mod0), cs(mod1), cs(nw),
                  pl.BlockSpec(w_in.shape, lambda b, j, seg: (0, 0), pipeline_mode=pl.Buffered(1)), cs(lb),
                  cs(csf), cs(csb)],
        out_specs=[tok(d)] * 6 + [pl.BlockSpec((1, NSUB, d, TM), lambda b, j, seg: (b, 0, 0, j)), tok(d), cvspec, cvspec],
        scratch_shapes=[pltpu.VMEM((2, LROWS, d), F32), pltpu.SemaphoreType.DMA((2,))])
    sds = jax.ShapeDtypeStruct
    out_shape = [sds((nbp, NSUB, nt, d), F32)] + [sds((nbp, NSUB, nt, d), BF16)] * 5 + [sds((nbp, NSUB, d, nt), BF16)] + \
                [sds((nbp, NSUB, nt, d), BF16)] + [sds((nbp, NSUB, nblk, 3 * SUBLANES, d), F32)] * 2
    outs = pl.pallas_call(
        functools.partial(_inproj1_kernel, nctx_blk, bsz), grid_spec=grid_spec, out_shape=out_shape,
        compiler_params=_params(2), name="combine_inproj_hgrn",
    )(seg, tab, tab, aux, ys, _pairs(x1), mod0, mod1, nw, w_in, lb, csf, csb)
    return [_unpairs(o) for o in outs]


def _final_kernel(seg_ref, tab_ref, tab_next_ref, aux_ref, ys_hbm, x_ref, mod_ref, fw_ref, o_ref, buf, sem):
    nblk = pl.num_programs(1)
    bp = pl.program_id(0)
    step = bp * nblk + pl.program_id(1)
    ffn = _gather_ffn(step, pl.num_programs(0) * nblk, seg_ref, tab_ref, tab_next_ref, aux_ref, ys_hbm, buf, sem)
    d = x_ref.shape[-1]
    for s in range(NSUB):
        mod = mod_ref[pl.ds(bp * NSUB + s, 1), :]
        o_ref[0, s] = _rms(x_ref[0, s] + mod[:, 5 * d:6 * d] * ffn[s * TM:(s + 1) * TM]) * fw_ref[...]


def _final(seg, tab, aux, ys, x3, mod, fw):
    bsz, seq, d = x3.shape
    nblk = seq // TM
    nbp = bsz // NSUB
    cs = lambda a: pl.BlockSpec(a.shape, lambda b, j, seg: (0,) * a.ndim)
    tok = pl.BlockSpec((1, NSUB, TM, d), lambda b, j, seg: (b, 0, j, 0))
    grid_spec = pltpu.PrefetchScalarGridSpec(
        num_scalar_prefetch=1, grid=(nbp, nblk),
        in_specs=_route_specs(nbp, nblk) + [pl.BlockSpec(memory_space=pl.ANY), tok, cs(mod), cs(fw)],
        out_specs=tok,
        scratch_shapes=[pltpu.VMEM((2, LROWS, d), F32), pltpu.SemaphoreType.DMA((2,))])
    out = pl.pallas_call(
        _final_kernel, grid_spec=grid_spec, out_shape=jax.ShapeDtypeStruct((nbp, NSUB, seq, d), F32),
        compiler_params=_params(2), name="combine_final_norm",
    )(seg, tab, tab, aux, ys, _pairs(x3), mod, fw)
    return _unpairs(out)


def _grid_sincos(n_tokens, dim):
    rows = n_tokens // GRID_W
    r, col = jnp.meshgrid(jnp.arange(rows, dtype=F32), jnp.arange(GRID_W, dtype=F32), indexing='ij')
    quarter = dim // 4
    omega = 1.0 / (10000.0 ** (jnp.arange(quarter, dtype=F32) / quarter))

    def emb(p):
        ang = p.reshape(-1, 1) * omega
        return jnp.concatenate([jnp.sin(ang), jnp.cos(ang)], axis=-1)

    return jnp.concatenate([emb(r), emb(col)], axis=-1)


def _pad_heads(w, n_heads):
    dk = w.shape[-1] // n_heads
    w = w.reshape(w.shape[:-1] + (n_heads, dk))
    w = jnp.pad(w, [(0, 0)] * (w.ndim - 1) + [(0, HEAD_W - dk)])
    return w.reshape(w.shape[:-2] + (n_heads * HEAD_W,))


def _router_tables(router_w, router_bias):
    n_exp = router_w.shape[1]
    epg = n_exp // N_GROUPS
    assert epg == EXPERTS_PER_GROUP
    perm = np.array([epg * g + p for p in range(epg) for g in range(N_GROUPS)])
    wt = jnp.pad(router_w.T[perm], ((0, LANES - n_exp), (0, 0)))
    hi = wt.astype(BF16)
    lo = (wt - hi.astype(F32)).astype(BF16)
    bias = jnp.pad(router_bias[perm], (0, LANES - n_exp))
    return hi, lo, jnp.broadcast_to(bias[:, None], (LANES, TR)).astype(F32)


def _segments(counts, n_tokens):
    cnt = counts[:, 0].astype(I32)
    tiles = (cnt + TR - 1) // TR
    ends = jnp.cumsum(tiles)
    ntile = (n_tokens + (n_tokens // TR) * N_GROUPS * (RUN - 1) + TR - 1) // TR + N_GROUPS
    n_valid = ends[-1]
    seg = jnp.concatenate([(ends - tiles) * TR, (n_valid * TR).reshape(1), jnp.full((1,), ntile * TR, I32)])
    tidx = jnp.minimum(jnp.arange(ntile, dtype=I32), n_valid - 1)
    tile_group = jnp.sum((tidx[:, None] >= ends[None, :]).astype(I32), axis=1)
    cnt = jnp.concatenate([cnt, jnp.zeros((1,), I32)])
    return seg.astype(I32), cnt, tile_group.astype(I32), n_valid.reshape(1).astype(I32), ntile * TR


def _moe_layer(tab, meta, counts, pay, aux, weights):
    seg, cnt, tile_group, n_valid, n_sorted = _segments(counts, pay.shape[0])
    hs = _dispatch(seg, cnt, tab, meta, pay, aux, n_sorted)
    ys = _moe(tile_group, n_valid, hs, *weights)
    return seg, ys


def kernel(x, c, ctx, c_ctx, ada_w, ada_b, norm_mix, norm_ffn, ab_w_in, ab_w_out, gla_a2, gla_ab, gla_norm, s5_lam_re, s5_lam_im, s5_log_dt, s5_b_re, s5_b_im, s5_c_re, s5_c_im, s5_d, s5_glu_w, s5_glu_b, hg_w_in, hg_w_out, hg_lb_logits, hg_norm, router_w, router_bias, moe_w_gate, moe_w_up, moe_w_down, final_norm):
    bsz, seq, d = x.shape
    nctx = ctx.shape[1]
    depth = ada_w.shape[0]
    assert depth == 2 and seq % TM == 0 and nctx % TM == 0 and bsz % SUBLANES == 0 and bsz < MOD_ROWS
    assert bsz % NSUB == 0
    nctx_blk = nctx // TM
    nt = nctx + seq
    nc = nt // S5_L
    assert (nc * bsz) % (2 * SUBLANES) == 0

    cond = jnp.zeros((MOD_ROWS, d), F32).at[:bsz].set(c).at[bsz].set(c_ctx)
    mod = _modulation(cond, ada_w, ada_b)
    consts = _chunk_matrices()
    row = lambda v: v.reshape(1, -1)

    qk = gla_a2.shape[-1]
    gv = ab_w_out.shape[1] // 2
    rank = gla_a2.shape[2]
    w_in = ab_w_in[0]
    o_v, o_g, o_a, o_u = 2 * qk, 2 * qk + gv, 2 * qk + 2 * gv, 2 * qk + 2 * gv + 2 * rank
    wm = jnp.concatenate([_pad_heads(w_in[:, 0:qk], GLA_HEADS), _pad_heads(w_in[:, qk:o_v], GLA_HEADS),
                          w_in[:, o_v:o_g], w_in[:, o_g:o_a], w_in[:, o_u:]], axis=1).astype(BF16)
    wa = jnp.pad(w_in[:, o_a:o_u], ((0, 0), (0, LANES - 2 * rank))).astype(BF16)
    a2p = _pad_heads(gla_a2[0], GLA_HEADS)
    a2 = jnp.zeros((LANES, 2 * gv), F32).at[0:rank, 0:gv].set(a2p[0]).at[rank:2 * rank, gv:].set(a2p[1]).astype(BF16)
    ab = _pad_heads(gla_ab[0], GLA_HEADS).reshape(1, 2 * gv)
    pos = _grid_sincos(seq, d)
    (xs, qf, kf, qb, kb, v, vt, g, u_cr, cvf, cvb) = _inproj0(x, ctx, pos, mod[0], row(norm_mix[0]), wm, wa, a2, ab, consts,
                                                               float(qk // GLA_HEADS) ** -0.5)
    o_f, o_b = _recurrence(qf, kf, qb, kb, v, vt, cvf, cvb, GLA_HEADS, nctx_blk)

    assert 2 * s5_b_re.shape[-1] == PAIR_W
    ops = _s5_operators(s5_lam_re[0], s5_lam_im[0], s5_log_dt[0], s5_b_re[0], s5_b_im[0], s5_c_re[0], s5_c_im[0], s5_d[0])
    y5 = _s5(u_cr, ops, nc, nctx // S5_L, bsz)

    rwh, rwl, rb = _router_tables(router_w, router_bias)
    utri = jnp.asarray(np.triu(np.ones((TR, TR), np.float32), 1), dtype=BF16)
    ones = jnp.ones((TR, TR), BF16)
    x1, pay, aux, meta, tab, counts = _mixout(True, GLA_HEADS, nctx_blk, 0, xs, o_f, o_b, g, y5, mod[0],
                                              row(gla_norm[0]), s5_glu_w[0].astype(BF16), row(s5_glu_b[0]),
                                              ab_w_out[0].astype(BF16), row(norm_ffn[0]), rwh, rwl, rb, utri, ones)
    assert moe_w_gate.shape[1] == N_GROUPS * EXPERTS_PER_GROUP
    seg, ys = _moe_layer(tab, meta, counts, pay, aux, (0, moe_w_gate, moe_w_up, moe_w_down))

    lb_all = jax.nn.softmax(hg_lb_logits.astype(F32), axis=0)
    lb_all = jnp.cumsum(lb_all, axis=0) - lb_all[0]
    (x2, qf, kf, qb, kb, v, vt, g, cvf, cvb) = _inproj1(seg, tab, aux, ys, x1, mod[0], mod[1], row(norm_mix[1]),
                                                          hg_w_in[0].astype(BF16), row(lb_all[1]), consts, nctx_blk)
    o_f, o_b = _recurrence(qf, kf, qb, kb, v, vt, cvf, cvb, HG_HEADS, nctx_blk)
    x3, pay, aux, meta, tab, counts = _mixout(False, HG_HEADS, nctx_blk, nctx_blk, x2, o_f, o_b, g, None, mod[1],
                                              row(hg_norm[0]), None, None, hg_w_out[0].astype(BF16),
                                              row(norm_ffn[1]), rwh, rwl, rb, utri, ones)
    seg, ys = _moe_layer(tab, meta, counts, pay, aux, (1, moe_w_gate, moe_w_up, moe_w_down))
    return _final(seg, tab, aux, ys, x3, mod[1], row(final_norm))
```

```python
import functools
import math

import numpy as np
import jax
import jax.numpy as jnp
from jax import lax
from jax.experimental import pallas as pl
from jax.experimental.pallas import tpu as pltpu

F32, BF16, I32 = jnp.float32, jnp.bfloat16, jnp.int32

EPS = 1e-6
CHUNK = 64
GRID_W = 64
GLA_HEADS = 4
GLA_GATE_NORM = 16.0
HG_HEADS = 8
N_GROUPS = 8
EXPERTS_PER_GROUP = 4
TOP_K = 2

LANES = 128
SUBLANES = 8
TM = 256
CPT = TM // CHUNK
HEAD_W = 128
S5_L = 16
MOD_ROWS = 16
COL_BLK = 256
CSUM_ROWS = TM + 4 * SUBLANES
S5_MAX_ROWS = 1024
SCAN_UNROLL = 8
STACK_HEADS = 4
RUN = SUBLANES
RUN_SHIFT = 3
NSUB = 2
TR = NSUB * TM
LROWS = TR + N_GROUPS * RUN
TAB_START, TAB_COUNT, TAB_CARRY = 0, 1, 2
AUX_LPOS = 4
VMEM_LIMIT = 56 * 1024 * 1024


def _dot(a, b):
    return jnp.dot(a, b, preferred_element_type=F32)


def _dot_nt(a, b):
    return lax.dot_general(a, b, (((1,), (1,)), ((), ())), preferred_element_type=F32)


def _split(x):
    hi = x.astype(BF16)
    lo = (x - hi.astype(F32)).astype(BF16)
    return hi, lo


def _sigmoid(x):
    return 1.0 / (1.0 + jnp.exp(-x))


def _silu(x):
    return x * _sigmoid(x)


def _gelu_tanh(x):
    return 0.5 * x * (1.0 + jnp.tanh(math.sqrt(2.0 / math.pi) * (x + 0.044715 * (x * x * x))))


def _rms(x):
    return x * lax.rsqrt(jnp.mean(x * x, axis=-1, keepdims=True) + EPS)


def _head_rms(o, gain, n_heads):
    outs = []
    for h in range(n_heads):
        oh = o[:, h * HEAD_W:(h + 1) * HEAD_W]
        outs.append(_rms(oh) * gain)
    return jnp.concatenate(outs, axis=1)


def _pairs(a):
    return a.reshape((a.shape[0] // NSUB, NSUB) + a.shape[1:])


def _unpairs(a):
    return a.reshape((a.shape[0] * NSUB,) + a.shape[2:])


def _mod_row(mod_ref, is_ctx, ctx_row, batch_row):
    return mod_ref[pl.ds(jnp.where(is_ctx, ctx_row, batch_row), 1), :]


def _params(n_grid_dims, vmem=VMEM_LIMIT, flags=None):
    return pltpu.CompilerParams(dimension_semantics=("arbitrary",) * n_grid_dims, vmem_limit_bytes=vmem, flags=flags)


def _const_spec(shape):
    nd = len(shape)
    return pl.BlockSpec(shape, lambda *_: (0,) * nd)


def _mod_kernel(c_ref, w_ref, b_ref, o_ref):
    s = _silu(c_ref[...])
    shi, slo = _split(s)
    whi, wlo = _split(w_ref[0])
    o_ref[0] = _dot(shi, whi) + _dot(shi, wlo) + _dot(slo, whi) + b_ref[0]


def _modulation(cond, ada_w, ada_b):
    depth, d, six_d = ada_w.shape
    nchunk = six_d // d
    return pl.pallas_call(
        _mod_kernel,
        grid=(depth, nchunk),
        in_specs=[
            _const_spec((MOD_ROWS, d)),
            pl.BlockSpec((1, d, d), lambda l, n: (l, 0, n)),
            pl.BlockSpec((1, 1, d), lambda l, n: (l, 0, n)),
        ],
        out_specs=pl.BlockSpec((1, MOD_ROWS, d), lambda l, n: (l, 0, n)),
        out_shape=jax.ShapeDtypeStruct((depth, MOD_ROWS, six_d), F32),
        compiler_params=_params(2),
        name="modulation",
    )(cond, ada_w, ada_b.reshape(depth, 1, six_d))


def _chunk_matrices():
    t = np.arange(TM)
    c, l = t // CHUNK, t % CHUNK
    same = (c[:, None] == c[None, :]).astype(np.float32)
    li, ls = l[:, None], l[None, :]
    mid_f = CHUNK // 2 - 1
    mid_b = CHUNK // 2
    d1f = same * ((ls <= li).astype(np.float32) - (ls <= mid_f).astype(np.float32))
    d1b = same * ((ls >= li).astype(np.float32) - (ls >= mid_b).astype(np.float32))
    inchunk = (np.arange(SUBLANES)[:, None] == c[None, :]).astype(np.float32)
    mf = np.concatenate([inchunk * (l <= mid_f), inchunk * (l > mid_f), inchunk])
    mb = np.concatenate([inchunk * (l >= mid_b), inchunk * (l < mid_b), inchunk])
    pad = np.zeros((CSUM_ROWS - TM - 3 * SUBLANES, TM), np.float32)
    as_bf16 = lambda a: jnp.asarray(a, dtype=BF16)
    return as_bf16(np.concatenate([d1f, mf, pad])), as_bf16(np.concatenate([d1b, mb, pad]))


def _gate_outputs(q, k, la, csum_ref, q_out, k_out, cv_out):
    hi, lo = _split(la)
    op = csum_ref[...]
    sums = _dot(op, hi) + _dot(op, lo)
    d1 = sums[0:TM]
    q_out[...] = (q * jnp.exp(d1)).astype(BF16)
    k_out[...] = (k * jnp.exp(-d1)).astype(BF16)
    cv_out[...] = jnp.exp(sums[TM:TM + 3 * SUBLANES])


PAIR_W = 32
PAIRS_PER_TILE = LANES // PAIR_W
CROWS = TM // S5_L


def _to_chunk_rows(u, us_ref, ucr_ref):
    for q in range(us_ref.shape[0]):
        us_ref[q] = u[:, q * LANES:(q + 1) * LANES]
    lane = lax.broadcasted_iota(I32, (CROWS, LANES), 1)
    for q in range(us_ref.shape[0]):
        tiles = [[] for _ in range(PAIRS_PER_TILE)]
        for t in range(S5_L // PAIRS_PER_TILE):
            acc = [None] * PAIRS_PER_TILE
            for jj in range(PAIRS_PER_TILE):
                piece = us_ref[q, pl.ds(t * PAIRS_PER_TILE + jj, CROWS, stride=S5_L), :]
                here = (lane >= PAIR_W * jj) & (lane < PAIR_W * (jj + 1))
                for pp in range(PAIRS_PER_TILE):
                    shift = (PAIR_W * (jj - pp)) % LANES
                    moved = pltpu.roll(piece, shift, axis=1) if shift else piece
                    acc[pp] = moved if acc[pp] is None else jnp.where(here, moved, acc[pp])
            for pp in range(PAIRS_PER_TILE):
                tiles[pp].append(acc[pp])
        for pp in range(PAIRS_PER_TILE):
            ucr_ref[q * PAIRS_PER_TILE + pp] = jnp.concatenate(tiles[pp], axis=1).astype(BF16)


def _from_chunk_rows(ycr_ref, ys_ref):
    lane = lax.broadcasted_iota(I32, (CROWS, LANES), 1)
    for q in range(ycr_ref.shape[0] // PAIRS_PER_TILE):
        for i in range(S5_L):
            t, ii = divmod(i, PAIRS_PER_TILE)
            acc = None
            for pp in range(PAIRS_PER_TILE):
                piece = ycr_ref[q * PAIRS_PER_TILE + pp, :, t * LANES:(t + 1) * LANES].astype(F32)
                shift = (PAIR_W * (pp - ii)) % LANES
                if shift:
                    piece = pltpu.roll(piece, shift, axis=1)
                here = (lane >= PAIR_W * pp) & (lane < PAIR_W * (pp + 1))
                acc = piece if acc is None else jnp.where(here, piece, acc)
            ys_ref[q, pl.ds(i, CROWS, stride=S5_L), :] = acc
    return jnp.concatenate([ys_ref[q] for q in range(ys_ref.shape[0])], axis=1)


def _inproj0_kernel(nctx_blk, nb, gla_w, q_scale, x_ref, ctx_ref, pos_ref, mod_ref, nw_ref, wm_ref, wa_ref, a2_ref,
                    ab_ref, csf_ref, csb_ref,
                    xs_ref, qf_ref, kf_ref, qb_ref, kb_ref, v_ref, vt_ref, g_ref, ucr_ref, cvf_ref, cvb_ref, us_ref):
    bp, j = pl.program_id(0), pl.program_id(1)
    is_ctx = j < nctx_blk
    xin = jnp.where(is_ctx, ctx_ref[0], x_ref[0] + pos_ref[...])
    xs_ref[0] = xin
    d = xin.shape[-1]
    hb = []
    for s in range(NSUB):
        mod = _mod_row(mod_ref, is_ctx, nb, bp * NSUB + s)
        hb.append((_rms(xin[s]) * nw_ref[...] * (1.0 + mod[:, d:2 * d]) + mod[:, 0:d]).astype(BF16))
    hb = jnp.concatenate(hb, axis=0)
    w = gla_w
    u = _dot(hb, wm_ref[:, 4 * w:5 * w])
    for s in range(NSUB):
        _to_chunk_rows(u[s * TM:(s + 1) * TM], us_ref, ucr_ref.at[:, 0, s, 0])
    a = _dot(hb, wa_ref[...]).astype(BF16)
    for n in range(w // COL_BLK):
        cs = slice(n * COL_BLK, (n + 1) * COL_BLK)
        proj = lambda i: _dot(hb, wm_ref[:, i * w + n * COL_BLK:i * w + (n + 1) * COL_BLK])
        q = proj(0) * q_scale
        k = proj(1)
        v = proj(2)
        g = proj(3)
        las = []
        for di in range(2):
            zs = slice(di * w + n * COL_BLK, di * w + (n + 1) * COL_BLK)
            z = _dot(a, a2_ref[:, zs]) + ab_ref[:, zs]
            las.append((jnp.minimum(z, 0.0) - jnp.log(1.0 + jnp.exp(-jnp.abs(z)))) * (1.0 / GLA_GATE_NORM))
        for s in range(NSUB):
            rows = slice(s * TM, (s + 1) * TM)
            g_ref[0, s, :, cs] = g[rows].astype(BF16)
            v_ref[0, s, :, cs] = v[rows].astype(BF16)
            vt_ref[0, s, cs, :] = v[rows].T.astype(BF16)
            _gate_outputs(q[rows], k[rows], las[0][rows], csf_ref,
                          qf_ref.at[0, s, :, cs], kf_ref.at[0, s, :, cs], cvf_ref.at[0, s, 0, :, cs])
            _gate_outputs(q[rows], k[rows], las[1][rows], csb_ref,
                          qb_ref.at[0, s, :, cs], kb_ref.at[0, s, :, cs], cvb_ref.at[0, s, 0, :, cs])


def _inproj0(x, ctx, pos, mod, nw, wm, wa, a2, ab, consts, q_scale):
    bsz, seq, d = x.shape
    nctx = ctx.shape[1]
    nctx_blk, nlat_blk = nctx // TM, seq // TM
    nblk = nctx_blk + nlat_blk
    nt = nblk * TM
    nbp = bsz // NSUB
    w = wm.shape[1] // 5
    csf, csb = consts
    in_specs = [
        pl.BlockSpec((1, NSUB, TM, d), lambda b, j: (b, 0, jnp.maximum(j - nctx_blk, 0), 0)),
        pl.BlockSpec((1, NSUB, TM, d), lambda b, j: (b, 0, jnp.minimum(j, nctx_blk - 1), 0)),
        pl.BlockSpec((TM, d), lambda b, j: (jnp.maximum(j - nctx_blk, 0), 0)),
        _const_spec(mod.shape), _const_spec(nw.shape), _const_spec(wm.shape), _const_spec(wa.shape),
        _const_spec(a2.shape), _const_spec(ab.shape), _const_spec(csf.shape), _const_spec(csb.shape),
    ]
    tokspec = lambda width: pl.BlockSpec((1, NSUB, TM, width), lambda b, j: (b, 0, j, 0))
    cvspec = pl.BlockSpec((1, NSUB, 1, 3 * SUBLANES, w), lambda b, j: (b, 0, j, 0, 0))
    npair, crow_w = w // PAIR_W, S5_L * PAIR_W
    out_specs = [tokspec(d), tokspec(w), tokspec(w), tokspec(w), tokspec(w), tokspec(w),
                 pl.BlockSpec((1, NSUB, w, TM), lambda b, j: (b, 0, 0, j)), tokspec(w),
                 pl.BlockSpec((npair, 1, NSUB, 1, CROWS, crow_w), lambda b, j: (0, b, 0, j, 0, 0)), cvspec, cvspec]
    sds = jax.ShapeDtypeStruct
    out_shape = [sds((nbp, NSUB, nt, d), F32)] + [sds((nbp, NSUB, nt, w), BF16)] * 5 + [sds((nbp, NSUB, w, nt), BF16)] + \
                [sds((nbp, NSUB, nt, w), BF16), sds((npair, nbp, NSUB, nblk, CROWS, crow_w), BF16)] + \
                [sds((nbp, NSUB, nblk, 3 * SUBLANES, w), F32)] * 2
    outs = pl.pallas_call(
        functools.partial(_inproj0_kernel, nctx_blk, bsz, w, q_scale),
        grid=(nbp, nblk), in_specs=in_specs, out_specs=out_specs, out_shape=out_shape,
        scratch_shapes=[pltpu.VMEM((w // LANES, TM, LANES), F32)],
        compiler_params=_params(2), name="inproj_gla_s5",
    )(_pairs(x), _pairs(ctx), pos, mod, nw, wm, wa, a2, ab, csf, csb)
    ucr = outs[8].reshape(npair, bsz * nblk * CROWS, crow_w)
    return [_unpairs(o) for o in outs[:8]] + [ucr] + [_unpairs(o) for o in outs[9:]]


def _recur_kernel(n_heads, qf_ref, kf_ref, vf_ref, vtf_ref, cvf_ref, qb_ref, kb_ref, vb_ref, vtb_ref, cvb_ref,
                  of_ref, ob_ref, s_ref):
    @pl.when(pl.program_id(1) == 0)
    def _():
        s_ref[...] = jnp.zeros_like(s_ref)

    hrows = STACK_HEADS * CHUNK
    ri = lax.broadcasted_iota(I32, (hrows, hrows), 0)
    ci = lax.broadcasted_iota(I32, (hrows, hrows), 1)
    same_head = (ri // CHUNK) == (ci // CHUNK)
    keep = (same_head & (ci <= ri), same_head & (ci >= ri))
    pw = 2 * HEAD_W
    prow = lax.broadcasted_iota(I32, (2 * CHUNK, pw), 0)
    zero_blk = jnp.zeros((HEAD_W, HEAD_W), BF16)
    stack = lambda ref, r0, h_first: jnp.concatenate(
        [ref[0, r0:r0 + CHUNK, h * HEAD_W:(h + 1) * HEAD_W] for h in range(h_first, h_first + STACK_HEADS)], axis=0)
    dirs = ((qf_ref, kf_ref, vf_ref, vtf_ref, cvf_ref, of_ref), (qb_ref, kb_ref, vb_ref, vtb_ref, cvb_ref, ob_ref))
    for cc in range(CPT):
        for d, (q_ref, k_ref, v_ref, vt_ref, cv_ref, o_ref) in enumerate(dirs):
            c = cc if d == 0 else CPT - 1 - cc
            r0 = c * CHUNK
            p0 = (c // 2) * 2 * CHUNK
            in_chunk = (prow >= r0 - p0) & (prow < r0 - p0 + CHUNK)
            e_mid = cv_ref[0, 0, c:c + 1, :]
            e_rest = cv_ref[0, 0, SUBLANES + c:SUBLANES + c + 1, :]
            e_all = cv_ref[0, 0, 2 * SUBLANES + c:2 * SUBLANES + c + 1, :]
            o_intra = []
            for h_first in range(0, n_heads, STACK_HEADS):
                scores = _dot_nt(stack(q_ref, r0, h_first), stack(k_ref, r0, h_first))
                attn = jnp.where(keep[d], scores, 0.0).astype(BF16)
                o_intra.append(_dot(attn, stack(v_ref, r0, h_first)))
            o_intra = jnp.concatenate(o_intra, axis=0)
            for p in range(n_heads // 2):
                ps = slice(p * pw, (p + 1) * pw)
                h0, h1 = 2 * p, 2 * p + 1
                hs0, hs1 = slice(h0 * HEAD_W, h1 * HEAD_W), slice(h1 * HEAD_W, (h1 + 1) * HEAD_W)
                st0, st1 = s_ref[d, h0], s_ref[d, h1]
                sb0 = (st0 * e_mid[:, hs0]).astype(BF16)
                sb1 = (st1 * e_mid[:, hs1]).astype(BF16)
                carried = jnp.concatenate([jnp.concatenate([sb0, zero_blk], axis=1),
                                           jnp.concatenate([zero_blk, sb1], axis=1)], axis=0)
                o_inter = _dot_nt(q_ref[0, r0:r0 + CHUNK, ps], carried)
                o_ref[0, r0:r0 + CHUNK, hs0] = (o_inter[:, 0:HEAD_W] + o_intra[h0 * CHUNK:h1 * CHUNK]).astype(o_ref.dtype)
                o_ref[0, r0:r0 + CHUNK, hs1] = (o_inter[:, HEAD_W:pw]
                                                + o_intra[h1 * CHUNK:(h1 + 1) * CHUNK]).astype(o_ref.dtype)
                kp = k_ref[0, p0:p0 + 2 * CHUNK, ps]
                kp = jnp.where(in_chunk, kp, jnp.zeros_like(kp))
                ds = _dot(vt_ref[0, ps, p0:p0 + 2 * CHUNK], kp)
                s_ref[d, h0] = st0 * e_all[:, hs0] + ds[0:HEAD_W, 0:HEAD_W] * e_rest[:, hs0]
                s_ref[d, h1] = st1 * e_all[:, hs1] + ds[HEAD_W:pw, HEAD_W:pw] * e_rest[:, hs1]


def _recurrence(qf, kf, qb, kb, v, vt, cvf, cvb, n_heads, nctx_blk):
    bsz, nt, w = qf.shape
    nblk = nt // TM
    fwd = lambda j: j
    bwd = lambda j: jnp.where(j < nctx_blk, nctx_blk - 1 - j, nblk - 1 - (j - nctx_blk))
    specs = []
    for order in (fwd, bwd):
        tok = pl.BlockSpec((1, TM, w), lambda b, j, o=order: (b, o(j), 0))
        specs += [tok, tok, tok,
                  pl.BlockSpec((1, w, TM), lambda b, j, o=order: (b, 0, o(j))),
                  pl.BlockSpec((1, 1, 3 * SUBLANES, w), lambda b, j, o=order: (b, o(j), 0, 0))]
    out_specs = [pl.BlockSpec((1, TM, w), lambda b, j: (b, j, 0)),
                 pl.BlockSpec((1, TM, w), lambda b, j: (b, bwd(j), 0))]
    return pl.pallas_call(
        functools.partial(_recur_kernel, n_heads),
        grid=(bsz, nblk), in_specs=specs, out_specs=out_specs,
        out_shape=[jax.ShapeDtypeStruct((bsz, nt, w), BF16)] * 2,
        scratch_shapes=[pltpu.VMEM((2, n_heads, HEAD_W, HEAD_W), F32)],
        compiler_params=_params(2), name=f"recurrence_h{n_heads}",
    )(qf, kf, v, vt, cvf, qb, kb, v, vt, cvb)


def _s5_operators(lam_re, lam_im, log_dt, b_re, b_im, c_re, c_im, d_skip):
    L = S5_L
    ndir, ng, p = lam_re.shape
    hs = b_re.shape[-1]
    dt = jnp.exp(log_dt)[..., None]
    lr, li = lam_re * dt, lam_im * dt
    mag = jnp.exp(lr)
    a_re, a_im = mag * jnp.cos(li), mag * jnp.sin(li)
    den = lam_re * lam_re + lam_im * lam_im
    f_re = ((a_re - 1.0) * lam_re + a_im * lam_im) / den
    f_im = (a_im * lam_re - (a_re - 1.0) * lam_im) / den
    bb_re = f_re[..., None] * b_re - f_im[..., None] * b_im
    bb_im = f_re[..., None] * b_im + f_im[..., None] * b_re
    m = jnp.arange(L + 1, dtype=F32)[:, None, None, None]
    pw_mag = jnp.exp(m * lr[None])
    pw_re, pw_im = pw_mag * jnp.cos(m * li[None]), pw_mag * jnp.sin(m * li[None])
    eye = jnp.eye(2, dtype=F32)
    npair, pw, cw = ng // 2, 2 * hs, 2 * L * hs

    toep = jnp.zeros((npair, cw, cw), F32)
    wz_p, vo_p = [], []
    for di in range(ndir):
        pr, pi = pw_re[:L, di, :, :, None], pw_im[:L, di, :, :, None]
        ab_re = pr * bb_re[di][None] - pi * bb_im[di][None]
        ab_im = pr * bb_im[di][None] + pi * bb_re[di][None]
        kf = jnp.einsum('mgnk,ghn->gkmh', ab_re, c_re[di]) - jnp.einsum('mgnk,ghn->gkmh', ab_im, c_im[di])
        if di == 1:
            kf = kf[:, :, ::-1, :]
        kp = jnp.einsum('pskmh,st->pskmth', kf.reshape(npair, 2, hs, L, hs), eye).reshape(npair, pw, cw)
        if di == 0:
            kpad = jnp.pad(kp, ((0, 0), (0, 0), (cw, 0)))
            blocks = [kpad[:, :, cw - pw * j:2 * cw - pw * j] for j in range(L)]
        else:
            kpad = jnp.pad(kp, ((0, 0), (0, 0), (0, cw)))
            blocks = [kpad[:, :, pw * (L - 1 - j):pw * (L - 1 - j) + cw] for j in range(L)]
        toep = toep + jnp.stack(blocks, axis=1).reshape(npair, cw, cw)
        sel_i = (lambda a: a[::-1]) if di == 0 else (lambda a: a)
        pack_w = lambda a: jnp.einsum('jpsnk,st->pjsktn', sel_i(a).reshape(L, npair, 2, p, hs), eye).reshape(npair, cw, 2 * p)
        wz_p.append(jnp.concatenate([pack_w(ab_re), pack_w(ab_im)], axis=2))
        qr, qi = pw_re[1:, di, :, None, :], pw_im[1:, di, :, None, :]
        ca_re = c_re[di][None] * qr - c_im[di][None] * qi
        ca_im = c_re[di][None] * qi + c_im[di][None] * qr
        sel_o = (lambda a: a) if di == 0 else (lambda a: a[::-1])
        pack_v = lambda a: jnp.einsum('ipshn,st->psnith', sel_o(a).reshape(L, npair, 2, hs, p), eye).reshape(npair, 2 * p, cw)
        vo_p.append(jnp.concatenate([pack_v(ca_re), -pack_v(ca_im)], axis=1))
    wz_p, vo_p = jnp.stack(wz_p, axis=1), jnp.stack(vo_p, axis=1)
    al = jnp.stack([pw_re[L], pw_im[L]], axis=1)
    al_p = jnp.transpose(al.reshape(ndir, 2, npair, 2 * p), (2, 0, 1, 3))
    d_p = jnp.tile(d_skip.reshape(npair, 1, pw), (1, L, 1)).reshape(npair, 1, cw)
    return wz_p.astype(BF16), vo_p.astype(BF16), toep.astype(BF16), al_p, d_p


def _s5_row_block(rows_total):
    tile = 2 * SUBLANES
    return max(r for r in range(tile, S5_MAX_ROWS + 1, tile) if rows_total % r == 0)


def _s5_kernel(nc, ncc, bsz, u_ref, wz_ref, vo_ref, tp_ref, al_ref, d_ref, y_ref, zf, zb, xf, xb):
    rows_total = nc * bsz
    rb = _s5_row_block(rows_total)
    half = zf.shape[2]

    def z_body(i, carry):
        rows = pl.ds(pl.multiple_of(i * rb, rb), rb)
        ub = u_ref[0, rows, :]
        for z, di in ((zf, 0), (zb, 1)):
            zz = _dot(ub, wz_ref[0, di])
            z[0, rows, :] = zz[:, 0:half]
            z[1, rows, :] = zz[:, half:2 * half]
        return carry

    lax.fori_loop(0, rows_total // rb, z_body, 0)

    coef = [[jnp.broadcast_to(al_ref[0, di, ri:ri + 1, :], (bsz, half)) for ri in range(2)] for di in range(2)]
    orders = (lambda i: i, lambda i: jnp.where(i < ncc, ncc - 1 - i, nc - 1 - (i - ncc)))

    def scan_body(i, carry):
        new = []
        for di, (z, xs) in enumerate(((zf, xf), (zb, xb))):
            x_re, x_im = carry[2 * di], carry[2 * di + 1]
            a_re, a_im = coef[di]
            rows = pl.ds(orders[di](i), bsz, stride=nc)
            xs[0, rows, :] = x_re
            xs[1, rows, :] = x_im
            new += [a_re * x_re - a_im * x_im + z[0, rows, :], a_re * x_im + a_im * x_re + z[1, rows, :]]
        return tuple(new)

    zero = jnp.zeros((bsz, half), F32)
    lax.fori_loop(0, nc, scan_body, (zero,) * 4, unroll=SCAN_UNROLL)

    def y_body(i, carry):
        rows = pl.ds(pl.multiple_of(i * rb, rb), rb)
        ub = u_ref[0, rows, :]
        state = lambda xs: jnp.concatenate([xs[0, rows, :], xs[1, rows, :]], axis=1).astype(BF16)
        y = _dot(state(xf), vo_ref[0, 0]) + _dot(state(xb), vo_ref[0, 1])
        y_ref[0, rows, :] = (y + _dot(ub, tp_ref[0]) + ub.astype(F32) * d_ref[0]).astype(y_ref.dtype)
        return carry

    lax.fori_loop(0, rows_total // rb, y_body, 0)


def _s5(u_cr, ops, nc, ncc, bsz):
    wz_p, vo_p, toep_p, al_p, d_p = ops
    npair, rows, _ = u_cr.shape
    per_pair = lambda a: pl.BlockSpec((1,) + a.shape[1:], lambda p: (p,) + (0,) * (a.ndim - 1))
    return pl.pallas_call(
        functools.partial(_s5_kernel, nc, ncc, bsz),
        grid=(npair,),
        in_specs=[per_pair(u_cr), per_pair(wz_p), per_pair(vo_p), per_pair(toep_p), per_pair(al_p), per_pair(d_p)],
        out_specs=per_pair(u_cr),
        out_shape=jax.ShapeDtypeStruct(u_cr.shape, BF16),
        scratch_shapes=[pltpu.VMEM((2, rows, wz_p.shape[-1] // 2), F32)] * 4,
        compiler_params=_params(1), name="s5",
    )(u_cr, wz_p, vo_p, toep_p, al_p, d_p)


def _route(h2, rwh_ref, rwl_ref, rb_ref, utri_ref, ones_ref, carry_ref):
    hi, lo = _split(h2)
    logits = _dot_nt(rwh_ref[...], hi) + _dot_nt(rwl_ref[...], hi) + _dot_nt(rwh_ref[...], lo)
    aff = _sigmoid(logits)
    sel = aff + rb_ref[...]
    epg = EXPERTS_PER_GROUP
    s = [sel[N_GROUPS * p:N_GROUPS * (p + 1), :] for p in range(epg)]
    a = [aff[N_GROUPS * p:N_GROUPS * (p + 1), :] for p in range(epg)]
    m1, n1 = jnp.maximum(s[0], s[1]), jnp.minimum(s[0], s[1])
    m2, n2 = jnp.maximum(s[2], s[3]), jnp.minimum(s[2], s[3])
    score = jnp.maximum(m1, m2) + jnp.maximum(jnp.minimum(m1, m2), jnp.maximum(n1, n2))
    gi = lax.broadcasted_iota(I32, score.shape, 0)
    best = jnp.max(score, axis=0, keepdims=True)
    gidx = jnp.min(jnp.where(score == best, gi, N_GROUPS), axis=0, keepdims=True)
    onehot = gi == gidx
    gates = []
    for p in range(epg):
        ahead = jnp.zeros_like(score)
        for q in range(epg):
            if q != p:
                beats = (s[q] >= s[p]) if q < p else (s[q] > s[p])
                ahead = ahead + jnp.where(beats, 1.0, 0.0)
        picked = jnp.where(onehot, jnp.where(ahead < float(TOP_K) - 0.5, a[p], 0.0), 0.0)
        gates.append(jnp.sum(picked, axis=0, keepdims=True))
    den = gates[0] + gates[1] + gates[2] + gates[3]
    gates = [g / den for g in gates]
    oh = jnp.where(onehot, 1.0, 0.0)
    oh = jnp.concatenate([oh, jnp.zeros_like(oh)], axis=0).astype(BF16)
    before = _dot(oh, utri_ref[...])[0:N_GROUPS]
    count = _dot(oh, ones_ref[...])[0:N_GROUPS]
    run = jnp.floor((count + (RUN - 1.0)) * (1.0 / RUN)) * RUN
    carry = carry_ref[...]
    carry_ref[...] = carry + run
    starts, nxt = [], jnp.zeros_like(run[0:1])
    for g in range(N_GROUPS):
        starts.append(nxt)
        nxt = nxt + run[g:g + 1]
    lstart = jnp.concatenate(starts, axis=0)
    lpos = jnp.sum(jnp.where(onehot, lstart + before, 0.0), axis=0, keepdims=True)
    return lpos, gates, (lstart, count, carry)


def _mixout_kernel(has_s5, n_heads, nctx_blk, tok_off, nb, *refs):
    if has_s5:
        (x_ref, of_ref, ob_ref, g_ref, ycr_ref, mod_ref, gn_ref, gluw_ref, glub_ref, wo_ref, nf_ref,
         rwh_ref, rwl_ref, rb_ref, utri_ref, ones_ref,
         x1_ref, pay_ref, aux_ref, meta_ref, tab_ref, cnt_ref, carry_ref, ys_ref) = refs
    else:
        (x_ref, of_ref, ob_ref, g_ref, mod_ref, gn_ref, wo_ref, nf_ref,
         rwh_ref, rwl_ref, rb_ref, utri_ref, ones_ref,
         x1_ref, pay_ref, aux_ref, meta_ref, tab_ref, cnt_ref, carry_ref) = refs
    bp, j = pl.program_id(0), pl.program_id(1)

    @pl.when((bp == 0) & (j == 0))
    def _():
        carry_ref[...] = jnp.zeros_like(carry_ref)

    d = x_ref.shape[-1]
    merge = lambda ref: jnp.concatenate([ref[0, s] for s in range(NSUB)], axis=0)
    o = merge(of_ref).astype(F32) + merge(ob_ref).astype(F32)
    mixed = _head_rms(o, gn_ref[...], n_heads) * _silu(merge(g_ref).astype(F32))
    if has_s5:
        y5 = jnp.concatenate([_from_chunk_rows(ycr_ref.at[:, 0, s, 0], ys_ref) for s in range(NSUB)], axis=0)
        act = _gelu_tanh(y5)
        glu = act * _sigmoid(_dot(act.astype(BF16), gluw_ref[...]) + glub_ref[...])
        mixed = jnp.concatenate([mixed, glu], axis=1)
    proj = _dot(mixed.astype(BF16), wo_ref[...])
    h2 = []
    for s in range(NSUB):
        mod = _mod_row(mod_ref, j + tok_off < nctx_blk, nb, bp * NSUB + s)
        x1 = x_ref[0, s] + mod[:, 2 * d:3 * d] * proj[s * TM:(s + 1) * TM]
        x1_ref[0, s] = x1
        h2.append(_rms(x1) * nf_ref[...] * (1.0 + mod[:, 4 * d:5 * d]) + mod[:, 3 * d:4 * d])
    h2 = jnp.concatenate(h2, axis=0)

    lpos, gates, (lstart, count, carry) = _route(h2, rwh_ref, rwl_ref, rb_ref, utri_ref, ones_ref, carry_ref)
    gi = lax.broadcasted_iota(I32, (SUBLANES, TR), 0)
    meta_ref[0] = jnp.where(gi == 0, lpos.astype(I32), 0)
    ti = lax.broadcasted_iota(I32, (N_GROUPS, LANES), 1)
    first = lambda a: a[:, 0:LANES].astype(I32)
    tab_ref[0] = jnp.where(ti == TAB_START, first(lstart),
                           jnp.where(ti == TAB_COUNT, first(count), jnp.where(ti == TAB_CARRY, first(carry), 0)))
    cnt_ref[...] = carry_ref[...]

    li = lax.broadcasted_iota(I32, (LANES, TR), 0)
    gt = jnp.where(li == AUX_LPOS, lpos, 0.0)
    for p, g in enumerate(gates):
        gt = gt + jnp.where(li == p, g, 0.0)
    aux = gt.T
    pay_ref[...] = h2.astype(BF16)
    aux_ref[...] = aux


def _mixout(has_s5, n_heads, nctx_blk, tok_off, x, of, ob, g, y5, mod, gn, gluw, glub, wo, nf, rwh, rwl, rb, utri, ones):
    bsz, _, d = x.shape
    w = of.shape[-1]
    nblk_all = of.shape[1] // TM
    nblk = nblk_all - tok_off
    nbp = bsz // NSUB
    tok = lambda width: pl.BlockSpec((1, NSUB, TM, width), lambda b, j: (b, 0, j + tok_off, 0))
    ins, specs = [_pairs(x), _pairs(of), _pairs(ob), _pairs(g)], [tok(d), tok(w), tok(w), tok(w)]
    scratch = [pltpu.VMEM((N_GROUPS, TR), F32)]
    if has_s5:
        npair, _, crow_w = y5.shape
        ins.append(y5.reshape(npair, nbp, NSUB, nblk_all, CROWS, crow_w))
        specs.append(pl.BlockSpec((npair, 1, NSUB, 1, CROWS, crow_w), lambda b, j: (0, b, 0, j, 0, 0)))
        scratch.append(pltpu.VMEM((w // LANES, TM, LANES), F32))
    consts = [mod, gn] + ([gluw, glub] if has_s5 else []) + [wo, nf, rwh, rwl, rb, utri, ones]
    ins += consts
    specs += [_const_spec(c.shape) for c in consts]
    ntile = nbp * nblk
    lin = lambda b, j: b * nblk + j
    out_specs = [pl.BlockSpec((1, NSUB, TM, d), lambda b, j: (b, 0, j, 0)),
                 pl.BlockSpec((TR, d), lambda b, j: (lin(b, j), 0)),
                 pl.BlockSpec((TR, LANES), lambda b, j: (lin(b, j), 0)),
                 pl.BlockSpec((1, SUBLANES, TR), lambda b, j: (lin(b, j), 0, 0)),
                 pl.BlockSpec((1, N_GROUPS, LANES), lambda b, j: (lin(b, j), 0, 0)),
                 _const_spec((N_GROUPS, TR))]
    sds = jax.ShapeDtypeStruct
    out_shape = [sds((nbp, NSUB, nblk * TM, d), F32), sds((ntile * TR, d), BF16), sds((ntile * TR, LANES), F32),
                 sds((ntile, SUBLANES, TR), I32), sds((ntile, N_GROUPS, LANES), I32), sds((N_GROUPS, TR), F32)]
    outs = pl.pallas_call(
        functools.partial(_mixout_kernel, has_s5, n_heads, nctx_blk, tok_off, bsz),
        grid=(nbp, nblk), in_specs=specs, out_specs=out_specs, out_shape=out_shape,
        scratch_shapes=scratch,
        compiler_params=_params(2), name="mixout_s5" if has_s5 else "mixout",
    )(*ins)
    return [_unpairs(outs[0])] + list(outs[1:])


def _for_each_run(seg_ref, tab_ref, fn):
    for g in range(N_GROUPS):
        lstart = tab_ref[0, g, TAB_START]
        first = seg_ref[g] + tab_ref[0, g, TAB_CARRY]
        n_pieces = lax.shift_right_logical(tab_ref[0, g, TAB_COUNT] + (RUN - 1), RUN_SHIFT)

        def piece(k, carry, lstart=lstart, first=first):
            fn(pl.ds(pl.multiple_of(lstart + k * RUN, RUN), RUN), pl.ds(pl.multiple_of(first + k * RUN, RUN), RUN))
            return carry

        lax.fori_loop(0, n_pieces, piece, 0)


def _dispatch_kernel(seg_ref, cnt_ref, tab_ref, tab_prev_ref, meta_ref, pay_ref, aux_ref, hs_hbm, s_ref, z_ref, sem, zsem):
    i = pl.program_id(0)
    last = pl.num_programs(0) - 1
    slot = lax.rem(i, 2)
    d = pay_ref.shape[1]
    lpos = meta_ref[0, 0:1, :]
    rows = lax.broadcasted_iota(I32, (LROWS, TR), 0)
    pick = jnp.where(rows == lpos, 1.0, 0.0).astype(BF16)
    s_ref[slot, :, 0:d] = _dot(pick, pay_ref[...])
    ghi, glo = _split(aux_ref[...])
    s_ref[slot, :, d:d + LANES] = _dot(pick, ghi) + _dot(pick, glo)

    def runs(t_ref, sl, op):
        copy = lambda src, dst: pltpu.make_async_copy(s_ref.at[sl, src], hs_hbm.at[dst], sem.at[sl])
        _for_each_run(seg_ref, t_ref, lambda src, dst: getattr(copy(src, dst), op)())

    runs(tab_ref, slot, "start")

    @pl.when(i > 0)
    def _():
        runs(tab_prev_ref, 1 - slot, "wait")

    @pl.when(i == last)
    def _():
        runs(tab_ref, slot, "wait")
        z_ref[...] = jnp.zeros_like(z_ref)
        for op in ("start", "wait"):
            for g in range(N_GROUPS + 1):
                begin = seg_ref[g] + cnt_ref[g]

                def piece(k, carry, begin=begin):
                    dst = hs_hbm.at[pl.ds(pl.multiple_of(begin + k * RUN, RUN), RUN)]
                    getattr(pltpu.make_async_copy(z_ref, dst, zsem), op)()
                    return carry

                lax.fori_loop(0, lax.shift_right_logical(seg_ref[g + 1] - begin, RUN_SHIFT), piece, 0)


def _dispatch(seg, cnt, tab, meta, pay, aux, n_sorted):
    ntile = meta.shape[0]
    d = pay.shape[1]
    width = d + LANES
    grid_spec = pltpu.PrefetchScalarGridSpec(
        num_scalar_prefetch=2, grid=(ntile,),
        in_specs=[pl.BlockSpec((1, N_GROUPS, LANES), lambda i, seg, cnt: (i, 0, 0), memory_space=pltpu.SMEM),
                  pl.BlockSpec((1, N_GROUPS, LANES), lambda i, seg, cnt: (jnp.maximum(i - 1, 0), 0, 0),
                               memory_space=pltpu.SMEM),
                  pl.BlockSpec((1, SUBLANES, TR), lambda i, seg, cnt: (i, 0, 0)),
                  pl.BlockSpec((TR, d), lambda i, seg, cnt: (i, 0)),
                  pl.BlockSpec((TR, LANES), lambda i, seg, cnt: (i, 0))],
        out_specs=pl.BlockSpec(memory_space=pl.ANY),
        scratch_shapes=[pltpu.VMEM((2, LROWS, width), F32), pltpu.VMEM((RUN, width), F32),
                        pltpu.SemaphoreType.DMA((2,)), pltpu.SemaphoreType.DMA(())])
    return pl.pallas_call(
        _dispatch_kernel, grid_spec=grid_spec,
        out_shape=jax.ShapeDtypeStruct((n_sorted, width), F32),
        compiler_params=_params(1), name="moe_dispatch",
    )(seg, cnt, tab, tab, meta, pay, aux)


def _moe_kernel(tg_ref, nv_ref, hs_ref, wg_ref, wu_ref, wd_ref, ys_ref, wgb, wub, wdb):
    i = pl.program_id(0)
    valid = i < nv_ref[0]

    @pl.when(valid & ((i == 0) | (tg_ref[i] != tg_ref[jnp.maximum(i - 1, 0)])))
    def _():
        wgb[...] = wg_ref[...].astype(BF16)
        wub[...] = wu_ref[...].astype(BF16)
        wdb[...] = wd_ref[...].astype(BF16)

    @pl.when(valid)
    def _():
        d = wg_ref.shape[1]
        h = hs_ref[:, 0:d].astype(BF16)
        gate = hs_ref[:, d:d + LANES]
        acc = jnp.zeros((TR, d), F32)
        for p in range(EXPERTS_PER_GROUP):
            act = _silu(_dot(h, wgb[p])) * _dot(h, wub[p]) * gate[:, p:p + 1]
            acc = acc + _dot(act.astype(BF16), wdb[p])
        ys_ref[...] = acc

    @pl.when(jnp.logical_not(valid))
    def _():
        ys_ref[...] = jnp.zeros_like(ys_ref)


def _moe(tile_group, n_valid, hs, layer, wg, wu, wd):
    n_sorted, pay_w = hs.shape
    ntile = n_sorted // TR
    d = wg.shape[2]
    epg = EXPERTS_PER_GROUP
    clamp = lambda i, tg, nv: jnp.minimum(i, nv[0] - 1)
    wspec = lambda a: pl.BlockSpec((None, epg) + a.shape[2:], lambda i, tg, nv: (layer, tg[i], 0, 0))
    grid_spec = pltpu.PrefetchScalarGridSpec(
        num_scalar_prefetch=2, grid=(ntile,),
        in_specs=[pl.BlockSpec((TR, pay_w), lambda i, tg, nv: (clamp(i, tg, nv), 0)), wspec(wg), wspec(wu), wspec(wd)],
        out_specs=pl.BlockSpec((TR, d), lambda i, tg, nv: (i, 0)),
        scratch_shapes=[pltpu.VMEM((epg,) + wg.shape[2:], BF16), pltpu.VMEM((epg,) + wu.shape[2:], BF16),
                        pltpu.VMEM((epg,) + wd.shape[2:], BF16)])
    return pl.pallas_call(
        _moe_kernel, grid_spec=grid_spec,
        out_shape=jax.ShapeDtypeStruct((n_sorted, d), F32),
        compiler_params=_params(1), name="moe_experts",
    )(tile_group, n_valid, hs, wg, wu, wd)


def _route_specs(bsz, nblk):
    last = bsz * nblk - 1
    cur = pl.BlockSpec((1, N_GROUPS, LANES), lambda b, j, seg: (b * nblk + j, 0, 0), memory_space=pltpu.SMEM)
    nxt = pl.BlockSpec((1, N_GROUPS, LANES), lambda b, j, seg: (jnp.minimum(b * nblk + j + 1, last), 0, 0),
                       memory_space=pltpu.SMEM)
    aux = pl.BlockSpec((TR, LANES), lambda b, j, seg: (b * nblk + j, 0))
    return [cur, nxt, aux]


def _gather_ffn(step, n_steps, seg_ref, tab_ref, tab_next_ref, aux_ref, ys_hbm, buf, sem):
    def runs(t_ref, slot, op):
        copy = lambda dst, src: pltpu.make_async_copy(ys_hbm.at[src], buf.at[slot, dst], sem.at[slot])
        _for_each_run(seg_ref, t_ref, lambda dst, src: getattr(copy(dst, src), op)())

    slot = lax.rem(step, 2)

    @pl.when(step == 0)
    def _():
        buf[...] = jnp.zeros_like(buf)
        runs(tab_ref, 0, "start")

    @pl.when(step + 1 < n_steps)
    def _():
        runs(tab_next_ref, 1 - slot, "start")

    runs(tab_ref, slot, "wait")
    lpos = aux_ref[:, AUX_LPOS:AUX_LPOS + 1].astype(I32)
    cols = lax.broadcasted_iota(I32, (TR, LROWS), 1)
    pick = jnp.where(cols == lpos, 1.0, 0.0).astype(BF16)
    return _dot(pick, buf[slot].astype(BF16))


def _inproj1_kernel(nctx_blk, nb, seg_ref, tab_ref, tab_next_ref, aux_ref, ys_hbm, x_ref, mod0_ref, mod1_ref, nw_ref,
                    w_ref, lb_ref, csf_ref, csb_ref,
                    x2_ref, qf_ref, kf_ref, qb_ref, kb_ref, v_ref, vt_ref, g_ref, cvf_ref, cvb_ref, buf, sem):
    bp, j = pl.program_id(0), pl.program_id(1)
    nblk = pl.num_programs(1)
    ffn = _gather_ffn(bp * nblk + j, pl.num_programs(0) * nblk, seg_ref, tab_ref, tab_next_ref, aux_ref, ys_hbm,
                      buf, sem)
    d = x_ref.shape[-1]
    hb = []
    for s in range(NSUB):
        mod0 = _mod_row(mod0_ref, j < nctx_blk, nb, bp * NSUB + s)
        mod1 = _mod_row(mod1_ref, j < nctx_blk, nb, bp * NSUB + s)
        x2 = x_ref[0, s] + mod0[:, 5 * d:6 * d] * ffn[s * TM:(s + 1) * TM]
        x2_ref[0, s] = x2
        hb.append((_rms(x2) * nw_ref[...] * (1.0 + mod1[:, d:2 * d]) + mod1[:, 0:d]).astype(BF16))
    hb = jnp.concatenate(hb, axis=0)
    proj = lambda n: _dot(hb, w_ref[:, n * d:(n + 1) * d])
    q = _silu(proj(0))
    lb = lb_ref[...]
    v = proj(3)
    g = proj(4)
    f_f = lb + (1.0 - lb) * _sigmoid(proj(1))
    f_b = lb + (1.0 - lb) * _sigmoid(proj(2))
    for s in range(NSUB):
        rows = slice(s * TM, (s + 1) * TM)
        v_ref[0, s] = v[rows].astype(BF16)
        vt_ref[0, s] = v[rows].T.astype(BF16)
        g_ref[0, s] = g[rows].astype(BF16)
        _gate_outputs(q[rows], 1.0 - f_f[rows], jnp.log(f_f[rows]), csf_ref,
                      qf_ref.at[0, s], kf_ref.at[0, s], cvf_ref.at[0, s, 0])
        _gate_outputs(q[rows], 1.0 - f_b[rows], jnp.log(f_b[rows]), csb_ref,
                      qb_ref.at[0, s], kb_ref.at[0, s], cvb_ref.at[0, s, 0])


def _inproj1(seg, tab, aux, ys, x1, mod0, mod1, nw, w_in, lb, consts, nctx_blk):
    bsz, nt, d = x1.shape
    nblk = nt // TM
    nbp = bsz // NSUB
    csf, csb = consts
    cs = lambda a: pl.BlockSpec(a.shape, lambda b, j, seg: (0,) * a.ndim)
    tok = lambda width: pl.BlockSpec((1, NSUB, TM, width), lambda b, j, seg: (b, 0, j, 0))
    cvspec = pl.BlockSpec((1, NSUB, 1, 3 * SUBLANES, d), lambda b, j, seg: (b, 0, j, 0, 0))
    grid_spec = pltpu.PrefetchScalarGridSpec(
        num_scalar_prefetch=1, grid=(nbp, nblk),
        in_specs=_route_specs(nbp, nblk) + [
                  pl.BlockSpec(memory_space=pl.ANY), tok(d), cs(mod0), cs(mod1), cs(nw),
                  pl.BlockSpec(w_in.shape, lambda b, j, seg: (0, 0), pipeline_mode=pl.Buffered(1)), cs(lb),
                  cs(csf), cs(csb)],
        out_specs=[tok(d)] * 6 + [pl.BlockSpec((1, NSUB, d, TM), lambda b, j, seg: (b, 0, 0, j)), tok(d), cvspec, cvspec],
        scratch_shapes=[pltpu.VMEM((2, LROWS, d), F32), pltpu.SemaphoreType.DMA((2,))])
    sds = jax.ShapeDtypeStruct
    out_shape = [sds((nbp, NSUB, nt, d), F32)] + [sds((nbp, NSUB, nt, d), BF16)] * 5 + [sds((nbp, NSUB, d, nt), BF16)] + \
                [sds((nbp, NSUB, nt, d), BF16)] + [sds((nbp, NSUB, nblk, 3 * SUBLANES, d), F32)] * 2
    outs = pl.pallas_call(
        functools.partial(_inproj1_kernel, nctx_blk, bsz), grid_spec=grid_spec, out_shape=out_shape,
        compiler_params=_params(2), name="combine_inproj_hgrn",
    )(seg, tab, tab, aux, ys, _pairs(x1), mod0, mod1, nw, w_in, lb, csf, csb)
    return [_unpairs(o) for o in outs]


def _final_kernel(seg_ref, tab_ref, tab_next_ref, aux_ref, ys_hbm, x_ref, mod_ref, fw_ref, o_ref, buf, sem):
    nblk = pl.num_programs(1)
    bp = pl.program_id(0)
    step = bp * nblk + pl.program_id(1)
    ffn = _gather_ffn(step, pl.num_programs(0) * nblk, seg_ref, tab_ref, tab_next_ref, aux_ref, ys_hbm, buf, sem)
    d = x_ref.shape[-1]
    for s in range(NSUB):
        mod = mod_ref[pl.ds(bp * NSUB + s, 1), :]
        o_ref[0, s] = _rms(x_ref[0, s] + mod[:, 5 * d:6 * d] * ffn[s * TM:(s + 1) * TM]) * fw_ref[...]


def _final(seg, tab, aux, ys, x3, mod, fw):
    bsz, seq, d = x3.shape
    nblk = seq // TM
    nbp = bsz // NSUB
    cs = lambda a: pl.BlockSpec(a.shape, lambda b, j, seg: (0,) * a.ndim)
    tok = pl.BlockSpec((1, NSUB, TM, d), lambda b, j, seg: (b, 0, j, 0))
    grid_spec = pltpu.PrefetchScalarGridSpec(
        num_scalar_prefetch=1, grid=(nbp, nblk),
        in_specs=_route_specs(nbp, nblk) + [pl.BlockSpec(memory_space=pl.ANY), tok, cs(mod), cs(fw)],
        out_specs=tok,
        scratch_shapes=[pltpu.VMEM((2, LROWS, d), F32), pltpu.SemaphoreType.DMA((2,))])
    out = pl.pallas_call(
        _final_kernel, grid_spec=grid_spec, out_shape=jax.ShapeDtypeStruct((nbp, NSUB, seq, d), F32),
        compiler_params=_params(2), name="combine_final_norm",
    )(seg, tab, tab, aux, ys, _pairs(x3), mod, fw)
    return _unpairs(out)


def _grid_sincos(n_tokens, dim):
    rows = n_tokens // GRID_W
    quarter = dim // 4
    omega = 1.0 / (10000.0 ** (jnp.arange(quarter, dtype=F32) / quarter))

    def emb(n):
        ang = jnp.arange(n, dtype=F32).reshape(-1, 1) * omega
        return jnp.concatenate([jnp.sin(ang), jnp.cos(ang)], axis=-1)

    half = 2 * quarter
    by_row = jnp.broadcast_to(emb(rows)[:, None, :], (rows, GRID_W, half))
    by_col = jnp.broadcast_to(emb(GRID_W)[None, :, :], (rows, GRID_W, half))
    return jnp.concatenate([by_row, by_col], axis=-1).reshape(rows * GRID_W, 2 * half)


def _pad_heads(w, n_heads):
    dk = w.shape[-1] // n_heads
    w = w.reshape(w.shape[:-1] + (n_heads, dk))
    w = jnp.pad(w, [(0, 0)] * (w.ndim - 1) + [(0, HEAD_W - dk)])
    return w.reshape(w.shape[:-2] + (n_heads * HEAD_W,))


def _router_tables(router_w, router_bias):
    n_exp = router_w.shape[1]
    epg = n_exp // N_GROUPS
    assert epg == EXPERTS_PER_GROUP
    perm = np.array([epg * g + p for p in range(epg) for g in range(N_GROUPS)])
    wt = jnp.pad(router_w.T[perm], ((0, LANES - n_exp), (0, 0)))
    hi = wt.astype(BF16)
    lo = (wt - hi.astype(F32)).astype(BF16)
    bias = jnp.pad(router_bias[perm], (0, LANES - n_exp))
    return hi, lo, jnp.broadcast_to(bias[:, None], (LANES, TR)).astype(F32)


def _segments(counts, n_tokens):
    cnt = counts[:, 0].astype(I32)
    tiles = (cnt + TR - 1) // TR
    ends = jnp.cumsum(tiles)
    ntile = (n_tokens + (n_tokens // TR) * N_GROUPS * (RUN - 1) + TR - 1) // TR + N_GROUPS
    n_valid = ends[-1]
    seg = jnp.concatenate([(ends - tiles) * TR, (n_valid * TR).reshape(1), jnp.full((1,), ntile * TR, I32)])
    tidx = jnp.minimum(jnp.arange(ntile, dtype=I32), n_valid - 1)
    tile_group = jnp.sum((tidx[:, None] >= ends[None, :]).astype(I32), axis=1)
    cnt = jnp.concatenate([cnt, jnp.zeros((1,), I32)])
    return seg.astype(I32), cnt, tile_group.astype(I32), n_valid.reshape(1).astype(I32), ntile * TR


def _moe_layer(tab, meta, counts, pay, aux, weights):
    seg, cnt, tile_group, n_valid, n_sorted = _segments(counts, pay.shape[0])
    hs = _dispatch(seg, cnt, tab, meta, pay, aux, n_sorted)
    ys = _moe(tile_group, n_valid, hs, *weights)
    return seg, ys


def kernel(x, c, ctx, c_ctx, ada_w, ada_b, norm_mix, norm_ffn, ab_w_in, ab_w_out, gla_a2, gla_ab, gla_norm, s5_lam_re, s5_lam_im, s5_log_dt, s5_b_re, s5_b_im, s5_c_re, s5_c_im, s5_d, s5_glu_w, s5_glu_b, hg_w_in, hg_w_out, hg_lb_logits, hg_norm, router_w, router_bias, moe_w_gate, moe_w_up, moe_w_down, final_norm):
    bsz, seq, d = x.shape
    nctx = ctx.shape[1]
    depth = ada_w.shape[0]
    assert depth == 2 and seq % TM == 0 and nctx % TM == 0 and bsz % SUBLANES == 0 and bsz < MOD_ROWS
    assert bsz % NSUB == 0
    nctx_blk = nctx // TM
    nt = nctx + seq
    nc = nt // S5_L
    assert (nc * bsz) % (2 * SUBLANES) == 0

    cond = jnp.zeros((MOD_ROWS, d), F32).at[:bsz].set(c).at[bsz].set(c_ctx)
    mod = _modulation(cond, ada_w, ada_b)
    consts = _chunk_matrices()
    row = lambda v: v.reshape(1, -1)

    qk = gla_a2.shape[-1]
    gv = ab_w_out.shape[1] // 2
    rank = gla_a2.shape[2]
    w_in = ab_w_in[0]
    o_v, o_g, o_a, o_u = 2 * qk, 2 * qk + gv, 2 * qk + 2 * gv, 2 * qk + 2 * gv + 2 * rank
    wm = jnp.concatenate([_pad_heads(w_in[:, 0:qk], GLA_HEADS), _pad_heads(w_in[:, qk:o_v], GLA_HEADS),
                          w_in[:, o_v:o_g], w_in[:, o_g:o_a], w_in[:, o_u:]], axis=1).astype(BF16)
    wa = jnp.pad(w_in[:, o_a:o_u], ((0, 0), (0, LANES - 2 * rank))).astype(BF16)
    a2p = _pad_heads(gla_a2[0], GLA_HEADS)
    a2 = jnp.zeros((LANES, 2 * gv), F32).at[0:rank, 0:gv].set(a2p[0]).at[rank:2 * rank, gv:].set(a2p[1]).astype(BF16)
    ab = _pad_heads(gla_ab[0], GLA_HEADS).reshape(1, 2 * gv)
    pos = _grid_sincos(seq, d)
    (xs, qf, kf, qb, kb, v, vt, g, u_cr, cvf, cvb) = _inproj0(x, ctx, pos, mod[0], row(norm_mix[0]), wm, wa, a2, ab, consts,
                                                               float(qk // GLA_HEADS) ** -0.5)
    o_f, o_b = _recurrence(qf, kf, qb, kb, v, vt, cvf, cvb, GLA_HEADS, nctx_blk)

    assert 2 * s5_b_re.shape[-1] == PAIR_W
    ops = _s5_operators(s5_lam_re[0], s5_lam_im[0], s5_log_dt[0], s5_b_re[0], s5_b_im[0], s5_c_re[0], s5_c_im[0], s5_d[0])
    y5 = _s5(u_cr, ops, nc, nctx // S5_L, bsz)

    rwh, rwl, rb = _router_tables(router_w, router_bias)
    utri = jnp.asarray(np.triu(np.ones((TR, TR), np.float32), 1), dtype=BF16)
    ones = jnp.ones((TR, TR), BF16)
    x1, pay, aux, meta, tab, counts = _mixout(True, GLA_HEADS, nctx_blk, 0, xs, o_f, o_b, g, y5, mod[0],
                                              row(gla_norm[0]), s5_glu_w[0].astype(BF16), row(s5_glu_b[0]),
                                              ab_w_out[0].astype(BF16), row(norm_ffn[0]), rwh, rwl, rb, utri, ones)
    assert moe_w_gate.shape[1] == N_GROUPS * EXPERTS_PER_GROUP
    seg, ys = _moe_layer(tab, meta, counts, pay, aux, (0, moe_w_gate, moe_w_up, moe_w_down))

    lb_all = jax.nn.softmax(hg_lb_logits.astype(F32), axis=0)
    lb_all = jnp.cumsum(lb_all, axis=0) - lb_all[0]
    (x2, qf, kf, qb, kb, v, vt, g, cvf, cvb) = _inproj1(seg, tab, aux, ys, x1, mod[0], mod[1], row(norm_mix[1]),
                                                          hg_w_in[0].astype(BF16), row(lb_all[1]), consts, nctx_blk)
    o_f, o_b = _recurrence(qf, kf, qb, kb, v, vt, cvf, cvb, HG_HEADS, nctx_blk)
    x3, pay, aux, meta, tab, counts = _mixout(False, HG_HEADS, nctx_blk, nctx_blk, x2, o_f, o_b, g, None, mod[1],
                                              row(hg_norm[0]), None, None, hg_w_out[0].astype(BF16),
                                              row(norm_ffn[1]), rwh, rwl, rb, utri, ones)
    seg, ys = _moe_layer(tab, meta, counts, pay, aux, (1, moe_w_gate, moe_w_up, moe_w_down))
    return _final(seg, tab, aux, ys, x3, mod[1], row(final_norm))
```

```python
import functools
import math

import numpy as np
import jax
import jax.numpy as jnp
from jax import lax
from jax.experimental import pallas as pl
from jax.experimental.pallas import tpu as pltpu

F32, BF16, I32 = jnp.float32, jnp.bfloat16, jnp.int32

EPS = 1e-6
CHUNK = 64
GRID_W = 64
GLA_HEADS = 4
GLA_GATE_NORM = 16.0
HG_HEADS = 8
N_GROUPS = 8
EXPERTS_PER_GROUP = 4
TOP_K = 2

LANES = 128
SUBLANES = 8
TM = 256
CPT = TM // CHUNK
HEAD_W = 128
S5_L = 16
MOD_ROWS = 16
COL_BLK = 256
CSUM_ROWS = TM + 4 * SUBLANES
S5_MAX_ROWS = 1280
SCAN_UNROLL = 8
STACK_HEADS = 4
RUN = SUBLANES
RUN_SHIFT = 3
NSUB = 2
TR = NSUB * TM
LROWS = TR + N_GROUPS * RUN
TAB_START, TAB_COUNT, TAB_CARRY = 0, 1, 2
AUX_LPOS = 4
VMEM_LIMIT = 56 * 1024 * 1024


def _dot(a, b):
    return jnp.dot(a, b, preferred_element_type=F32)


def _dot_nt(a, b):
    return lax.dot_general(a, b, (((1,), (1,)), ((), ())), preferred_element_type=F32)


def _split(x):
    hi = x.astype(BF16)
    lo = (x - hi.astype(F32)).astype(BF16)
    return hi, lo


def _sigmoid(x):
    return 1.0 / (1.0 + jnp.exp(-x))


def _silu(x):
    return x * _sigmoid(x)


def _gelu_tanh(x):
    return 0.5 * x * (1.0 + jnp.tanh(math.sqrt(2.0 / math.pi) * (x + 0.044715 * (x * x * x))))


def _rms(x):
    return x * lax.rsqrt(jnp.mean(x * x, axis=-1, keepdims=True) + EPS)


def _head_rms(o, gain, n_heads):
    outs = []
    for h in range(n_heads):
        oh = o[:, h * HEAD_W:(h + 1) * HEAD_W]
        outs.append(_rms(oh) * gain)
    return jnp.concatenate(outs, axis=1)


def _pairs(a):
    return a.reshape((a.shape[0] // NSUB, NSUB) + a.shape[1:])


def _unpairs(a):
    return a.reshape((a.shape[0] * NSUB,) + a.shape[2:])


def _mod_row(mod_ref, is_ctx, ctx_row, batch_row):
    return mod_ref[pl.ds(jnp.where(is_ctx, ctx_row, batch_row), 1), :]


def _params(n_grid_dims, vmem=VMEM_LIMIT, flags=None):
    return pltpu.CompilerParams(dimension_semantics=("arbitrary",) * n_grid_dims, vmem_limit_bytes=vmem, flags=flags)


def _const_spec(shape):
    nd = len(shape)
    return pl.BlockSpec(shape, lambda *_: (0,) * nd)


def _mod_kernel(c_ref, w_ref, b_ref, o_ref):
    s = _silu(c_ref[...])
    shi, slo = _split(s)
    whi, wlo = _split(w_ref[0])
    o_ref[0] = _dot(shi, whi) + _dot(shi, wlo) + _dot(slo, whi) + b_ref[0]


def _modulation(cond, ada_w, ada_b):
    depth, d, six_d = ada_w.shape
    nchunk = six_d // d
    return pl.pallas_call(
        _mod_kernel,
        grid=(depth, nchunk),
        in_specs=[
            _const_spec((MOD_ROWS, d)),
            pl.BlockSpec((1, d, d), lambda l, n: (l, 0, n)),
            pl.BlockSpec((1, 1, d), lambda l, n: (l, 0, n)),
        ],
        out_specs=pl.BlockSpec((1, MOD_ROWS, d), lambda l, n: (l, 0, n)),
        out_shape=jax.ShapeDtypeStruct((depth, MOD_ROWS, six_d), F32),
        compiler_params=_params(2),
        name="modulation",
    )(cond, ada_w, ada_b.reshape(depth, 1, six_d))


def _chunk_matrices():
    t = np.arange(TM)
    c, l = t // CHUNK, t % CHUNK
    same = (c[:, None] == c[None, :]).astype(np.float32)
    li, ls = l[:, None], l[None, :]
    mid_f = CHUNK // 2 - 1
    mid_b = CHUNK // 2
    d1f = same * ((ls <= li).astype(np.float32) - (ls <= mid_f).astype(np.float32))
    d1b = same * ((ls >= li).astype(np.float32) - (ls >= mid_b).astype(np.float32))
    inchunk = (np.arange(SUBLANES)[:, None] == c[None, :]).astype(np.float32)
    mf = np.concatenate([inchunk * (l <= mid_f), inchunk * (l > mid_f), inchunk])
    mb = np.concatenate([inchunk * (l >= mid_b), inchunk * (l < mid_b), inchunk])
    pad = np.zeros((CSUM_ROWS - TM - 3 * SUBLANES, TM), np.float32)
    as_bf16 = lambda a: jnp.asarray(a, dtype=BF16)
    return as_bf16(np.concatenate([d1f, mf, pad])), as_bf16(np.concatenate([d1b, mb, pad]))


def _gate_outputs(q, k, la, csum_ref, q_out, k_out, cv_out):
    hi, lo = _split(la)
    op = csum_ref[...]
    sums = _dot(op, hi) + _dot(op, lo)
    d1 = sums[0:TM]
    q_out[...] = (q * jnp.exp(d1)).astype(BF16)
    k_out[...] = (k * jnp.exp(-d1)).astype(BF16)
    cv_out[...] = jnp.exp(sums[TM:TM + 3 * SUBLANES])


PAIR_W = 32
PAIRS_PER_TILE = LANES // PAIR_W
CROWS = TM // S5_L


def _to_chunk_rows(u, us_ref, ucr_ref):
    for q in range(us_ref.shape[0]):
        us_ref[q] = u[:, q * LANES:(q + 1) * LANES]
    lane = lax.broadcasted_iota(I32, (CROWS, LANES), 1)
    for q in range(us_ref.shape[0]):
        tiles = [[] for _ in range(PAIRS_PER_TILE)]
        for t in range(S5_L // PAIRS_PER_TILE):
            acc = [None] * PAIRS_PER_TILE
            for jj in range(PAIRS_PER_TILE):
                piece = us_ref[q, pl.ds(t * PAIRS_PER_TILE + jj, CROWS, stride=S5_L), :]
                here = (lane >= PAIR_W * jj) & (lane < PAIR_W * (jj + 1))
                for pp in range(PAIRS_PER_TILE):
                    shift = (PAIR_W * (jj - pp)) % LANES
                    moved = pltpu.roll(piece, shift, axis=1) if shift else piece
                    acc[pp] = moved if acc[pp] is None else jnp.where(here, moved, acc[pp])
            for pp in range(PAIRS_PER_TILE):
                tiles[pp].append(acc[pp])
        for pp in range(PAIRS_PER_TILE):
            ucr_ref[q * PAIRS_PER_TILE + pp] = jnp.concatenate(tiles[pp], axis=1).astype(BF16)


def _from_chunk_rows(ycr_ref, ys_ref):
    lane = lax.broadcasted_iota(I32, (CROWS, LANES), 1)
    for q in range(ycr_ref.shape[0] // PAIRS_PER_TILE):
        for i in range(S5_L):
            t, ii = divmod(i, PAIRS_PER_TILE)
            acc = None
            for pp in range(PAIRS_PER_TILE):
                piece = ycr_ref[q * PAIRS_PER_TILE + pp, :, t * LANES:(t + 1) * LANES].astype(F32)
                shift = (PAIR_W * (pp - ii)) % LANES
                if shift:
                    piece = pltpu.roll(piece, shift, axis=1)
                here = (lane >= PAIR_W * pp) & (lane < PAIR_W * (pp + 1))
                acc = piece if acc is None else jnp.where(here, piece, acc)
            ys_ref[q, pl.ds(i, CROWS, stride=S5_L), :] = acc
    return jnp.concatenate([ys_ref[q] for q in range(ys_ref.shape[0])], axis=1)


def _inproj0_kernel(nctx_blk, nb, gla_w, q_scale, x_ref, ctx_ref, pos_ref, mod_ref, nw_ref, wm_ref, wa_ref, a2_ref,
                    ab_ref, csf_ref, csb_ref,
                    xs_ref, qf_ref, kf_ref, qb_ref, kb_ref, v_ref, vt_ref, g_ref, ucr_ref, cvf_ref, cvb_ref, us_ref):
    bp, j = pl.program_id(0), pl.program_id(1)
    is_ctx = j < nctx_blk
    xin = jnp.where(is_ctx, ctx_ref[0], x_ref[0] + pos_ref[...])
    xs_ref[0] = xin
    d = xin.shape[-1]
    hb = []
    for s in range(NSUB):
        mod = _mod_row(mod_ref, is_ctx, nb, bp * NSUB + s)
        hb.append((_rms(xin[s]) * nw_ref[...] * (1.0 + mod[:, d:2 * d]) + mod[:, 0:d]).astype(BF16))
    hb = jnp.concatenate(hb, axis=0)
    w = gla_w
    u = _dot(hb, wm_ref[:, 4 * w:5 * w])
    for s in range(NSUB):
        _to_chunk_rows(u[s * TM:(s + 1) * TM], us_ref, ucr_ref.at[:, 0, s, 0])
    a = _dot(hb, wa_ref[...]).astype(BF16)
    for n in range(w // COL_BLK):
        cs = slice(n * COL_BLK, (n + 1) * COL_BLK)
        proj = lambda i: _dot(hb, wm_ref[:, i * w + n * COL_BLK:i * w + (n + 1) * COL_BLK])
        q = proj(0) * q_scale
        k = proj(1)
        v = proj(2)
        g = proj(3)
        las = []
        for di in range(2):
            zs = slice(di * w + n * COL_BLK, di * w + (n + 1) * COL_BLK)
            z = _dot(a, a2_ref[:, zs]) + ab_ref[:, zs]
            las.append((jnp.minimum(z, 0.0) - jnp.log(1.0 + jnp.exp(-jnp.abs(z)))) * (1.0 / GLA_GATE_NORM))
        for s in range(NSUB):
            rows = slice(s * TM, (s + 1) * TM)
            g_ref[0, s, :, cs] = g[rows].astype(BF16)
            v_ref[0, s, :, cs] = v[rows].astype(BF16)
            vt_ref[0, s, cs, :] = v[rows].T.astype(BF16)
            _gate_outputs(q[rows], k[rows], las[0][rows], csf_ref,
                          qf_ref.at[0, s, :, cs], kf_ref.at[0, s, :, cs], cvf_ref.at[0, s, 0, :, cs])
            _gate_outputs(q[rows], k[rows], las[1][rows], csb_ref,
                          qb_ref.at[0, s, :, cs], kb_ref.at[0, s, :, cs], cvb_ref.at[0, s, 0, :, cs])


def _inproj0(x, ctx, pos, mod, nw, wm, wa, a2, ab, consts, q_scale):
    bsz, seq, d = x.shape
    nctx = ctx.shape[1]
    nctx_blk, nlat_blk = nctx // TM, seq // TM
    nblk = nctx_blk + nlat_blk
    nt = nblk * TM
    nbp = bsz // NSUB
    w = wm.shape[1] // 5
    csf, csb = consts
    in_specs = [
        pl.BlockSpec((1, NSUB, TM, d), lambda b, j: (b, 0, jnp.maximum(j - nctx_blk, 0), 0)),
        pl.BlockSpec((1, NSUB, TM, d), lambda b, j: (b, 0, jnp.minimum(j, nctx_blk - 1), 0)),
        pl.BlockSpec((TM, d), lambda b, j: (jnp.maximum(j - nctx_blk, 0), 0)),
        _const_spec(mod.shape), _const_spec(nw.shape), _const_spec(wm.shape), _const_spec(wa.shape),
        _const_spec(a2.shape), _const_spec(ab.shape), _const_spec(csf.shape), _const_spec(csb.shape),
    ]
    tokspec = lambda width: pl.BlockSpec((1, NSUB, TM, width), lambda b, j: (b, 0, j, 0))
    cvspec = pl.BlockSpec((1, NSUB, 1, 3 * SUBLANES, w), lambda b, j: (b, 0, j, 0, 0))
    npair, crow_w = w // PAIR_W, S5_L * PAIR_W
    out_specs = [tokspec(d), tokspec(w), tokspec(w), tokspec(w), tokspec(w), tokspec(w),
                 pl.BlockSpec((1, NSUB, w, TM), lambda b, j: (b, 0, 0, j)), tokspec(w),
                 pl.BlockSpec((npair, 1, NSUB, 1, CROWS, crow_w), lambda b, j: (0, b, 0, j, 0, 0)), cvspec, cvspec]
    sds = jax.ShapeDtypeStruct
    out_shape = [sds((nbp, NSUB, nt, d), F32)] + [sds((nbp, NSUB, nt, w), BF16)] * 5 + [sds((nbp, NSUB, w, nt), BF16)] + \
                [sds((nbp, NSUB, nt, w), BF16), sds((npair, nbp, NSUB, nblk, CROWS, crow_w), BF16)] + \
                [sds((nbp, NSUB, nblk, 3 * SUBLANES, w), F32)] * 2
    outs = pl.pallas_call(
        functools.partial(_inproj0_kernel, nctx_blk, bsz, w, q_scale),
        grid=(nbp, nblk), in_specs=in_specs, out_specs=out_specs, out_shape=out_shape,
        scratch_shapes=[pltpu.VMEM((w // LANES, TM, LANES), F32)],
        compiler_params=_params(2), name="inproj_gla_s5",
    )(_pairs(x), _pairs(ctx), pos, mod, nw, wm, wa, a2, ab, csf, csb)
    ucr = outs[8].reshape(npair, bsz * nblk * CROWS, crow_w)
    return [_unpairs(o) for o in outs[:8]] + [ucr] + [_unpairs(o) for o in outs[9:]]


def _recur_kernel(n_heads, qf_ref, kf_ref, vf_ref, vtf_ref, cvf_ref, qb_ref, kb_ref, vb_ref, vtb_ref, cvb_ref,
                  of_ref, ob_ref, s_ref):
    @pl.when(pl.program_id(1) == 0)
    def _():
        s_ref[...] = jnp.zeros_like(s_ref)

    hrows = STACK_HEADS * CHUNK
    ri = lax.broadcasted_iota(I32, (hrows, hrows), 0)
    ci = lax.broadcasted_iota(I32, (hrows, hrows), 1)
    same_head = (ri // CHUNK) == (ci // CHUNK)
    keep = (same_head & (ci <= ri), same_head & (ci >= ri))
    pw = 2 * HEAD_W
    prow = lax.broadcasted_iota(I32, (2 * CHUNK, pw), 0)
    zero_blk = jnp.zeros((HEAD_W, HEAD_W), BF16)
    stack = lambda ref, r0, h_first: jnp.concatenate(
        [ref[0, r0:r0 + CHUNK, h * HEAD_W:(h + 1) * HEAD_W] for h in range(h_first, h_first + STACK_HEADS)], axis=0)
    dirs = ((qf_ref, kf_ref, vf_ref, vtf_ref, cvf_ref, of_ref), (qb_ref, kb_ref, vb_ref, vtb_ref, cvb_ref, ob_ref))
    for cc in range(CPT):
        for d, (q_ref, k_ref, v_ref, vt_ref, cv_ref, o_ref) in enumerate(dirs):
            c = cc if d == 0 else CPT - 1 - cc
            r0 = c * CHUNK
            p0 = (c // 2) * 2 * CHUNK
            in_chunk = (prow >= r0 - p0) & (prow < r0 - p0 + CHUNK)
            e_mid = cv_ref[0, 0, c:c + 1, :]
            e_rest = cv_ref[0, 0, SUBLANES + c:SUBLANES + c + 1, :]
            e_all = cv_ref[0, 0, 2 * SUBLANES + c:2 * SUBLANES + c + 1, :]
            o_intra = []
            for h_first in range(0, n_heads, STACK_HEADS):
                scores = _dot_nt(stack(q_ref, r0, h_first), stack(k_ref, r0, h_first))
                attn = jnp.where(keep[d], scores, 0.0).astype(BF16)
                o_intra.append(_dot(attn, stack(v_ref, r0, h_first)))
            o_intra = jnp.concatenate(o_intra, axis=0)
            for p in range(n_heads // 2):
                ps = slice(p * pw, (p + 1) * pw)
                h0, h1 = 2 * p, 2 * p + 1
                hs0, hs1 = slice(h0 * HEAD_W, h1 * HEAD_W), slice(h1 * HEAD_W, (h1 + 1) * HEAD_W)
                st0, st1 = s_ref[d, h0], s_ref[d, h1]
                sb0 = (st0 * e_mid[:, hs0]).astype(BF16)
                sb1 = (st1 * e_mid[:, hs1]).astype(BF16)
                carried = jnp.concatenate([jnp.concatenate([sb0, zero_blk], axis=1),
                                           jnp.concatenate([zero_blk, sb1], axis=1)], axis=0)
                o_inter = _dot_nt(q_ref[0, r0:r0 + CHUNK, ps], carried)
                o_ref[0, r0:r0 + CHUNK, hs0] = (o_inter[:, 0:HEAD_W] + o_intra[h0 * CHUNK:h1 * CHUNK]).astype(o_ref.dtype)
                o_ref[0, r0:r0 + CHUNK, hs1] = (o_inter[:, HEAD_W:pw]
                                                + o_intra[h1 * CHUNK:(h1 + 1) * CHUNK]).astype(o_ref.dtype)
                kp = k_ref[0, p0:p0 + 2 * CHUNK, ps]
                kp = jnp.where(in_chunk, kp, jnp.zeros_like(kp))
                ds = _dot(vt_ref[0, ps, p0:p0 + 2 * CHUNK], kp)
                s_ref[d, h0] = st0 * e_all[:, hs0] + ds[0:HEAD_W, 0:HEAD_W] * e_rest[:, hs0]
                s_ref[d, h1] = st1 * e_all[:, hs1] + ds[HEAD_W:pw, HEAD_W:pw] * e_rest[:, hs1]


def _recurrence(qf, kf, qb, kb, v, vt, cvf, cvb, n_heads, nctx_blk):
    bsz, nt, w = qf.shape
    nblk = nt // TM
    fwd = lambda j: j
    bwd = lambda j: jnp.where(j < nctx_blk, nctx_blk - 1 - j, nblk - 1 - (j - nctx_blk))
    specs = []
    for order in (fwd, bwd):
        tok = pl.BlockSpec((1, TM, w), lambda b, j, o=order: (b, o(j), 0))
        specs += [tok, tok, tok,
                  pl.BlockSpec((1, w, TM), lambda b, j, o=order: (b, 0, o(j))),
                  pl.BlockSpec((1, 1, 3 * SUBLANES, w), lambda b, j, o=order: (b, o(j), 0, 0))]
    out_specs = [pl.BlockSpec((1, TM, w), lambda b, j: (b, j, 0)),
                 pl.BlockSpec((1, TM, w), lambda b, j: (b, bwd(j), 0))]
    return pl.pallas_call(
        functools.partial(_recur_kernel, n_heads),
        grid=(bsz, nblk), in_specs=specs, out_specs=out_specs,
        out_shape=[jax.ShapeDtypeStruct((bsz, nt, w), BF16)] * 2,
        scratch_shapes=[pltpu.VMEM((2, n_heads, HEAD_W, HEAD_W), F32)],
        compiler_params=_params(2), name=f"recurrence_h{n_heads}",
    )(qf, kf, v, vt, cvf, qb, kb, v, vt, cvb)


def _s5_operators(lam_re, lam_im, log_dt, b_re, b_im, c_re, c_im, d_skip):
    L = S5_L
    ndir, ng, p = lam_re.shape
    hs = b_re.shape[-1]
    dt = jnp.exp(log_dt)[..., None]
    lr, li = lam_re * dt, lam_im * dt
    mag = jnp.exp(lr)
    a_re, a_im = mag * jnp.cos(li), mag * jnp.sin(li)
    den = lam_re * lam_re + lam_im * lam_im
    f_re = ((a_re - 1.0) * lam_re + a_im * lam_im) / den
    f_im = (a_im * lam_re - (a_re - 1.0) * lam_im) / den
    bb_re = f_re[..., None] * b_re - f_im[..., None] * b_im
    bb_im = f_re[..., None] * b_im + f_im[..., None] * b_re
    m = jnp.arange(L + 1, dtype=F32)[:, None, None, None]
    pw_mag = jnp.exp(m * lr[None])
    pw_re, pw_im = pw_mag * jnp.cos(m * li[None]), pw_mag * jnp.sin(m * li[None])
    eye = jnp.eye(2, dtype=F32)
    npair, pw, cw = ng // 2, 2 * hs, 2 * L * hs

    toep = jnp.zeros((npair, cw, cw), F32)
    wz_p, vo_p = [], []
    for di in range(ndir):
        pr, pi = pw_re[:L, di, :, :, None], pw_im[:L, di, :, :, None]
        ab_re = pr * bb_re[di][None] - pi * bb_im[di][None]
        ab_im = pr * bb_im[di][None] + pi * bb_re[di][None]
        kf = jnp.einsum('mgnk,ghn->gkmh', ab_re, c_re[di]) - jnp.einsum('mgnk,ghn->gkmh', ab_im, c_im[di])
        if di == 1:
            kf = kf[:, :, ::-1, :]
        kp = jnp.einsum('pskmh,st->pskmth', kf.reshape(npair, 2, hs, L, hs), eye).reshape(npair, pw, cw)
        if di == 0:
            kpad = jnp.pad(kp, ((0, 0), (0, 0), (cw, 0)))
            blocks = [kpad[:, :, cw - pw * j:2 * cw - pw * j] for j in range(L)]
        else:
            kpad = jnp.pad(kp, ((0, 0), (0, 0), (0, cw)))
            blocks = [kpad[:, :, pw * (L - 1 - j):pw * (L - 1 - j) + cw] for j in range(L)]
        toep = toep + jnp.stack(blocks, axis=1).reshape(npair, cw, cw)
        sel_i = (lambda a: a[::-1]) if di == 0 else (lambda a: a)
        pack_w = lambda a: jnp.einsum('jpsnk,st->pjsktn', sel_i(a).reshape(L, npair, 2, p, hs), eye).reshape(npair, cw, 2 * p)
        wz_p.append(jnp.concatenate([pack_w(ab_re), pack_w(ab_im)], axis=2))
        qr, qi = pw_re[1:, di, :, None, :], pw_im[1:, di, :, None, :]
        ca_re = c_re[di][None] * qr - c_im[di][None] * qi
        ca_im = c_re[di][None] * qi + c_im[di][None] * qr
        sel_o = (lambda a: a) if di == 0 else (lambda a: a[::-1])
        pack_v = lambda a: jnp.einsum('ipshn,st->psnith', sel_o(a).reshape(L, npair, 2, hs, p), eye).reshape(npair, 2 * p, cw)
        vo_p.append(jnp.concatenate([pack_v(ca_re), -pack_v(ca_im)], axis=1))
    wz_p, vo_p = jnp.stack(wz_p, axis=1), jnp.stack(vo_p, axis=1)
    al = jnp.stack([pw_re[L], pw_im[L]], axis=1)
    al_p = jnp.transpose(al.reshape(ndir, 2, npair, 2 * p), (2, 0, 1, 3))
    d_p = jnp.tile(d_skip.reshape(npair, 1, pw), (1, L, 1)).reshape(npair, 1, cw)
    return wz_p.astype(BF16), vo_p.astype(BF16), toep.astype(BF16), al_p, d_p


def _s5_row_block(rows_total):
    tile = 2 * SUBLANES
    return max(r for r in range(tile, S5_MAX_ROWS + 1, tile) if rows_total % r == 0)


def _s5_kernel(nc, ncc, bsz, u_ref, wz_ref, vo_ref, tp_ref, al_ref, d_ref, y_ref, zf, zb, xf, xb):
    rows_total = nc * bsz
    rb = _s5_row_block(rows_total)
    half = zf.shape[2]

    def z_body(i, carry):
        rows = pl.ds(pl.multiple_of(i * rb, rb), rb)
        ub = u_ref[0, rows, :]
        for z, di in ((zf, 0), (zb, 1)):
            zz = _dot(ub, wz_ref[0, di])
            z[0, rows, :] = zz[:, 0:half]
            z[1, rows, :] = zz[:, half:2 * half]
        return carry

    lax.fori_loop(0, rows_total // rb, z_body, 0)

    coef = [[jnp.broadcast_to(al_ref[0, di, ri:ri + 1, :], (bsz, half)) for ri in range(2)] for di in range(2)]
    orders = (lambda i: i, lambda i: jnp.where(i < ncc, ncc - 1 - i, nc - 1 - (i - ncc)))

    def scan_body(i, carry):
        new = []
        for di, (z, xs) in enumerate(((zf, xf), (zb, xb))):
            x_re, x_im = carry[2 * di], carry[2 * di + 1]
            a_re, a_im = coef[di]
            rows = pl.ds(orders[di](i), bsz, stride=nc)
            xs[0, rows, :] = x_re
            xs[1, rows, :] = x_im
            new += [a_re * x_re - a_im * x_im + z[0, rows, :], a_re * x_im + a_im * x_re + z[1, rows, :]]
        return tuple(new)

    zero = jnp.zeros((bsz, half), F32)
    lax.fori_loop(0, nc, scan_body, (zero,) * 4, unroll=SCAN_UNROLL)

    def y_body(i, carry):
        rows = pl.ds(pl.multiple_of(i * rb, rb), rb)
        ub = u_ref[0, rows, :]
        state = lambda xs: jnp.concatenate([xs[0, rows, :], xs[1, rows, :]], axis=1).astype(BF16)
        y = _dot(state(xf), vo_ref[0, 0]) + _dot(state(xb), vo_ref[0, 1])
        y_ref[0, rows, :] = (y + _dot(ub, tp_ref[0]) + ub.astype(F32) * d_ref[0]).astype(y_ref.dtype)
        return carry

    lax.fori_loop(0, rows_total // rb, y_body, 0)


def _s5(u_cr, ops, nc, ncc, bsz):
    wz_p, vo_p, toep_p, al_p, d_p = ops
    npair, rows, _ = u_cr.shape
    per_pair = lambda a: pl.BlockSpec((1,) + a.shape[1:], lambda p: (p,) + (0,) * (a.ndim - 1))
    return pl.pallas_call(
        functools.partial(_s5_kernel, nc, ncc, bsz),
        grid=(npair,),
        in_specs=[per_pair(u_cr), per_pair(wz_p), per_pair(vo_p), per_pair(toep_p), per_pair(al_p), per_pair(d_p)],
        out_specs=per_pair(u_cr),
        out_shape=jax.ShapeDtypeStruct(u_cr.shape, BF16),
        scratch_shapes=[pltpu.VMEM((2, rows, wz_p.shape[-1] // 2), F32)] * 4,
        compiler_params=_params(1), name="s5",
    )(u_cr, wz_p, vo_p, toep_p, al_p, d_p)


def _route(h2, rwh_ref, rwl_ref, rb_ref, utri_ref, ones_ref, carry_ref):
    hi, lo = _split(h2)
    logits = _dot_nt(rwh_ref[...], hi) + _dot_nt(rwl_ref[...], hi) + _dot_nt(rwh_ref[...], lo)
    aff = _sigmoid(logits)
    sel = aff + rb_ref[...]
    epg = EXPERTS_PER_GROUP
    s = [sel[N_GROUPS * p:N_GROUPS * (p + 1), :] for p in range(epg)]
    a = [aff[N_GROUPS * p:N_GROUPS * (p + 1), :] for p in range(epg)]
    m1, n1 = jnp.maximum(s[0], s[1]), jnp.minimum(s[0], s[1])
    m2, n2 = jnp.maximum(s[2], s[3]), jnp.minimum(s[2], s[3])
    score = jnp.maximum(m1, m2) + jnp.maximum(jnp.minimum(m1, m2), jnp.maximum(n1, n2))
    gi = lax.broadcasted_iota(I32, score.shape, 0)
    best = jnp.max(score, axis=0, keepdims=True)
    gidx = jnp.min(jnp.where(score == best, gi, N_GROUPS), axis=0, keepdims=True)
    onehot = gi == gidx
    gates = []
    for p in range(epg):
        ahead = jnp.zeros_like(score)
        for q in range(epg):
            if q != p:
                beats = (s[q] >= s[p]) if q < p else (s[q] > s[p])
                ahead = ahead + jnp.where(beats, 1.0, 0.0)
        picked = jnp.where(onehot, jnp.where(ahead < float(TOP_K) - 0.5, a[p], 0.0), 0.0)
        gates.append(jnp.sum(picked, axis=0, keepdims=True))
    den = gates[0] + gates[1] + gates[2] + gates[3]
    gates = [g / den for g in gates]
    oh = jnp.where(onehot, 1.0, 0.0)
    oh = jnp.concatenate([oh, jnp.zeros_like(oh)], axis=0).astype(BF16)
    before = _dot(oh, utri_ref[...])[0:N_GROUPS]
    count = _dot(oh, ones_ref[...])[0:N_GROUPS]
    run = jnp.floor((count + (RUN - 1.0)) * (1.0 / RUN)) * RUN
    carry = carry_ref[...]
    carry_ref[...] = carry + run
    starts, nxt = [], jnp.zeros_like(run[0:1])
    for g in range(N_GROUPS):
        starts.append(nxt)
        nxt = nxt + run[g:g + 1]
    lstart = jnp.concatenate(starts, axis=0)
    lpos = jnp.sum(jnp.where(onehot, lstart + before, 0.0), axis=0, keepdims=True)
    return lpos, gates, (lstart, count, carry)


def _mixout_kernel(has_s5, n_heads, nctx_blk, tok_off, nb, *refs):
    if has_s5:
        (x_ref, of_ref, ob_ref, g_ref, ycr_ref, mod_ref, gn_ref, gluw_ref, glub_ref, wo_ref, nf_ref,
         rwh_ref, rwl_ref, rb_ref, utri_ref, ones_ref,
         x1_ref, pay_ref, aux_ref, meta_ref, tab_ref, cnt_ref, carry_ref, ys_ref) = refs
    else:
        (x_ref, of_ref, ob_ref, g_ref, mod_ref, gn_ref, wo_ref, nf_ref,
         rwh_ref, rwl_ref, rb_ref, utri_ref, ones_ref,
         x1_ref, pay_ref, aux_ref, meta_ref, tab_ref, cnt_ref, carry_ref) = refs
    bp, j = pl.program_id(0), pl.program_id(1)

    @pl.when((bp == 0) & (j == 0))
    def _():
        carry_ref[...] = jnp.zeros_like(carry_ref)

    d = x_ref.shape[-1]
    merge = lambda ref: jnp.concatenate([ref[0, s] for s in range(NSUB)], axis=0)
    o = merge(of_ref).astype(F32) + merge(ob_ref).astype(F32)
    mixed = _head_rms(o, gn_ref[...], n_heads) * _silu(merge(g_ref).astype(F32))
    if has_s5:
        y5 = jnp.concatenate([_from_chunk_rows(ycr_ref.at[:, 0, s, 0], ys_ref) for s in range(NSUB)], axis=0)
        act = _gelu_tanh(y5)
        glu = act * _sigmoid(_dot(act.astype(BF16), gluw_ref[...]) + glub_ref[...])
        mixed = jnp.concatenate([mixed, glu], axis=1)
    proj = _dot(mixed.astype(BF16), wo_ref[...])
    h2 = []
    for s in range(NSUB):
        mod = _mod_row(mod_ref, j + tok_off < nctx_blk, nb, bp * NSUB + s)
        x1 = x_ref[0, s] + mod[:, 2 * d:3 * d] * proj[s * TM:(s + 1) * TM]
        x1_ref[0, s] = x1
        h2.append(_rms(x1) * nf_ref[...] * (1.0 + mod[:, 4 * d:5 * d]) + mod[:, 3 * d:4 * d])
    h2 = jnp.concatenate(h2, axis=0)

    lpos, gates, (lstart, count, carry) = _route(h2, rwh_ref, rwl_ref, rb_ref, utri_ref, ones_ref, carry_ref)
    gi = lax.broadcasted_iota(I32, (SUBLANES, TR), 0)
    meta_ref[0] = jnp.where(gi == 0, lpos.astype(I32), 0)
    ti = lax.broadcasted_iota(I32, (N_GROUPS, LANES), 1)
    first = lambda a: a[:, 0:LANES].astype(I32)
    tab_ref[0] = jnp.where(ti == TAB_START, first(lstart),
                           jnp.where(ti == TAB_COUNT, first(count), jnp.where(ti == TAB_CARRY, first(carry), 0)))
    cnt_ref[...] = carry_ref[...]

    li = lax.broadcasted_iota(I32, (LANES, TR), 0)
    gt = jnp.where(li == AUX_LPOS, lpos, 0.0)
    for p, g in enumerate(gates):
        gt = gt + jnp.where(li == p, g, 0.0)
    aux = gt.T
    pay_ref[...] = h2.astype(BF16)
    aux_ref[...] = aux


def _mixout(has_s5, n_heads, nctx_blk, tok_off, x, of, ob, g, y5, mod, gn, gluw, glub, wo, nf, rwh, rwl, rb, utri, ones):
    bsz, _, d = x.shape
    w = of.shape[-1]
    nblk_all = of.shape[1] // TM
    nblk = nblk_all - tok_off
    nbp = bsz // NSUB
    tok = lambda width: pl.BlockSpec((1, NSUB, TM, width), lambda b, j: (b, 0, j + tok_off, 0))
    ins, specs = [_pairs(x), _pairs(of), _pairs(ob), _pairs(g)], [tok(d), tok(w), tok(w), tok(w)]
    scratch = [pltpu.VMEM((N_GROUPS, TR), F32)]
    if has_s5:
        npair, _, crow_w = y5.shape
        ins.append(y5.reshape(npair, nbp, NSUB, nblk_all, CROWS, crow_w))
        specs.append(pl.BlockSpec((npair, 1, NSUB, 1, CROWS, crow_w), lambda b, j: (0, b, 0, j, 0, 0)))
        scratch.append(pltpu.VMEM((w // LANES, TM, LANES), F32))
    consts = [mod, gn] + ([gluw, glub] if has_s5 else []) + [wo, nf, rwh, rwl, rb, utri, ones]
    ins += consts
    specs += [_const_spec(c.shape) for c in consts]
    ntile = nbp * nblk
    lin = lambda b, j: b * nblk + j
    out_specs = [pl.BlockSpec((1, NSUB, TM, d), lambda b, j: (b, 0, j, 0)),
                 pl.BlockSpec((TR, d), lambda b, j: (lin(b, j), 0)),
                 pl.BlockSpec((TR, LANES), lambda b, j: (lin(b, j), 0)),
                 pl.BlockSpec((1, SUBLANES, TR), lambda b, j: (lin(b, j), 0, 0)),
                 pl.BlockSpec((1, N_GROUPS, LANES), lambda b, j: (lin(b, j), 0, 0)),
                 _const_spec((N_GROUPS, TR))]
    sds = jax.ShapeDtypeStruct
    out_shape = [sds((nbp, NSUB, nblk * TM, d), F32), sds((ntile * TR, d), BF16), sds((ntile * TR, LANES), F32),
                 sds((ntile, SUBLANES, TR), I32), sds((ntile, N_GROUPS, LANES), I32), sds((N_GROUPS, TR), F32)]
    outs = pl.pallas_call(
        functools.partial(_mixout_kernel, has_s5, n_heads, nctx_blk, tok_off, bsz),
        grid=(nbp, nblk), in_specs=specs, out_specs=out_specs, out_shape=out_shape,
        scratch_shapes=scratch,
        compiler_params=_params(2), name="mixout_s5" if has_s5 else "mixout",
    )(*ins)
    return [_unpairs(outs[0])] + list(outs[1:])


def _for_each_run(seg_ref, tab_ref, fn):
    for g in range(N_GROUPS):
        lstart = tab_ref[0, g, TAB_START]
        first = seg_ref[g] + tab_ref[0, g, TAB_CARRY]
        n_pieces = lax.shift_right_logical(tab_ref[0, g, TAB_COUNT] + (RUN - 1), RUN_SHIFT)

        def piece(k, carry, lstart=lstart, first=first):
            fn(pl.ds(pl.multiple_of(lstart + k * RUN, RUN), RUN), pl.ds(pl.multiple_of(first + k * RUN, RUN), RUN))
            return carry

        lax.fori_loop(0, n_pieces, piece, 0)


def _dispatch_kernel(seg_ref, cnt_ref, tab_ref, tab_prev_ref, meta_ref, pay_ref, aux_ref, hs_hbm, s_ref, z_ref, sem, zsem):
    i = pl.program_id(0)
    last = pl.num_programs(0) - 1
    slot = lax.rem(i, 2)
    d = pay_ref.shape[1]
    lpos = meta_ref[0, 0:1, :]
    rows = lax.broadcasted_iota(I32, (LROWS, TR), 0)
    pick = jnp.where(rows == lpos, 1.0, 0.0).astype(BF16)
    s_ref[slot, :, 0:d] = _dot(pick, pay_ref[...])
    ghi, glo = _split(aux_ref[...])
    s_ref[slot, :, d:d + LANES] = _dot(pick, ghi) + _dot(pick, glo)

    def runs(t_ref, sl, op):
        copy = lambda src, dst: pltpu.make_async_copy(s_ref.at[sl, src], hs_hbm.at[dst], sem.at[sl])
        _for_each_run(seg_ref, t_ref, lambda src, dst: getattr(copy(src, dst), op)())

    runs(tab_ref, slot, "start")

    @pl.when(i > 0)
    def _():
        runs(tab_prev_ref, 1 - slot, "wait")

    @pl.when(i == last)
    def _():
        runs(tab_ref, slot, "wait")
        z_ref[...] = jnp.zeros_like(z_ref)
        for op in ("start", "wait"):
            for g in range(N_GROUPS + 1):
                begin = seg_ref[g] + cnt_ref[g]

                def piece(k, carry, begin=begin):
                    dst = hs_hbm.at[pl.ds(pl.multiple_of(begin + k * RUN, RUN), RUN)]
                    getattr(pltpu.make_async_copy(z_ref, dst, zsem), op)()
                    return carry

                lax.fori_loop(0, lax.shift_right_logical(seg_ref[g + 1] - begin, RUN_SHIFT), piece, 0)


def _dispatch(seg, cnt, tab, meta, pay, aux, n_sorted):
    ntile = meta.shape[0]
    d = pay.shape[1]
    width = d + LANES
    grid_spec = pltpu.PrefetchScalarGridSpec(
        num_scalar_prefetch=2, grid=(ntile,),
        in_specs=[pl.BlockSpec((1, N_GROUPS, LANES), lambda i, seg, cnt: (i, 0, 0), memory_space=pltpu.SMEM),
                  pl.BlockSpec((1, N_GROUPS, LANES), lambda i, seg, cnt: (jnp.maximum(i - 1, 0), 0, 0),
                               memory_space=pltpu.SMEM),
                  pl.BlockSpec((1, SUBLANES, TR), lambda i, seg, cnt: (i, 0, 0)),
                  pl.BlockSpec((TR, d), lambda i, seg, cnt: (i, 0)),
                  pl.BlockSpec((TR, LANES), lambda i, seg, cnt: (i, 0))],
        out_specs=pl.BlockSpec(memory_space=pl.ANY),
        scratch_shapes=[pltpu.VMEM((2, LROWS, width), F32), pltpu.VMEM((RUN, width), F32),
                        pltpu.SemaphoreType.DMA((2,)), pltpu.SemaphoreType.DMA(())])
    return pl.pallas_call(
        _dispatch_kernel, grid_spec=grid_spec,
        out_shape=jax.ShapeDtypeStruct((n_sorted, width), F32),
        compiler_params=_params(1), name="moe_dispatch",
    )(seg, cnt, tab, tab, meta, pay, aux)


def _moe_kernel(tg_ref, nv_ref, hs_ref, wg_ref, wu_ref, wd_ref, ys_ref, wgb, wub, wdb):
    i = pl.program_id(0)
    valid = i < nv_ref[0]

    @pl.when(valid & ((i == 0) | (tg_ref[i] != tg_ref[jnp.maximum(i - 1, 0)])))
    def _():
        wgb[...] = wg_ref[...].astype(BF16)
        wub[...] = wu_ref[...].astype(BF16)
        wdb[...] = wd_ref[...].astype(BF16)

    @pl.when(valid)
    def _():
        d = wg_ref.shape[1]
        h = hs_ref[:, 0:d].astype(BF16)
        gate = hs_ref[:, d:d + LANES]
        acc = jnp.zeros((TR, d), F32)
        for p in range(EXPERTS_PER_GROUP):
            act = _silu(_dot(h, wgb[p])) * _dot(h, wub[p]) * gate[:, p:p + 1]
            acc = acc + _dot(act.astype(BF16), wdb[p])
        ys_ref[...] = acc

    @pl.when(jnp.logical_not(valid))
    def _():
        ys_ref[...] = jnp.zeros_like(ys_ref)


def _moe(tile_group, n_valid, hs, layer, wg, wu, wd):
    n_sorted, pay_w = hs.shape
    ntile = n_sorted // TR
    d = wg.shape[2]
    epg = EXPERTS_PER_GROUP
    clamp = lambda i, tg, nv: jnp.minimum(i, nv[0] - 1)
    wspec = lambda a: pl.BlockSpec((None, epg) + a.shape[2:], lambda i, tg, nv: (layer, tg[i], 0, 0))
    grid_spec = pltpu.PrefetchScalarGridSpec(
        num_scalar_prefetch=2, grid=(ntile,),
        in_specs=[pl.BlockSpec((TR, pay_w), lambda i, tg, nv: (clamp(i, tg, nv), 0)), wspec(wg), wspec(wu), wspec(wd)],
        out_specs=pl.BlockSpec((TR, d), lambda i, tg, nv: (i, 0)),
        scratch_shapes=[pltpu.VMEM((epg,) + wg.shape[2:], BF16), pltpu.VMEM((epg,) + wu.shape[2:], BF16),
                        pltpu.VMEM((epg,) + wd.shape[2:], BF16)])
    return pl.pallas_call(
        _moe_kernel, grid_spec=grid_spec,
        out_shape=jax.ShapeDtypeStruct((n_sorted, d), F32),
        compiler_params=_params(1), name="moe_experts",
    )(tile_group, n_valid, hs, wg, wu, wd)


def _route_specs(bsz, nblk):
    last = bsz * nblk - 1
    cur = pl.BlockSpec((1, N_GROUPS, LANES), lambda b, j, seg: (b * nblk + j, 0, 0), memory_space=pltpu.SMEM)
    nxt = pl.BlockSpec((1, N_GROUPS, LANES), lambda b, j, seg: (jnp.minimum(b * nblk + j + 1, last), 0, 0),
                       memory_space=pltpu.SMEM)
    aux = pl.BlockSpec((TR, LANES), lambda b, j, seg: (b * nblk + j, 0))
    return [cur, nxt, aux]


def _gather_ffn(step, n_steps, seg_ref, tab_ref, tab_next_ref, aux_ref, ys_hbm, buf, sem):
    def runs(t_ref, slot, op):
        copy = lambda dst, src: pltpu.make_async_copy(ys_hbm.at[src], buf.at[slot, dst], sem.at[slot])
        _for_each_run(seg_ref, t_ref, lambda dst, src: getattr(copy(dst, src), op)())

    slot = lax.rem(step, 2)

    @pl.when(step == 0)
    def _():
        buf[...] = jnp.zeros_like(buf)
        runs(tab_ref, 0, "start")

    @pl.when(step + 1 < n_steps)
    def _():
        runs(tab_next_ref, 1 - slot, "start")

    runs(tab_ref, slot, "wait")
    lpos = aux_ref[:, AUX_LPOS:AUX_LPOS + 1].astype(I32)
    cols = lax.broadcasted_iota(I32, (TR, LROWS), 1)
    pick = jnp.where(cols == lpos, 1.0, 0.0).astype(BF16)
    return _dot(pick, buf[slot].astype(BF16))


def _inproj1_kernel(nctx_blk, nb, seg_ref, tab_ref, tab_next_ref, aux_ref, ys_hbm, x_ref, mod0_ref, mod1_ref, nw_ref,
                    w_ref, lb_ref, csf_ref, csb_ref,
                    x2_ref, qf_ref, kf_ref, qb_ref, kb_ref, v_ref, vt_ref, g_ref, cvf_ref, cvb_ref, buf, sem):
    bp, j = pl.program_id(0), pl.program_id(1)
    nblk = pl.num_programs(1)
    ffn = _gather_ffn(bp * nblk + j, pl.num_programs(0) * nblk, seg_ref, tab_ref, tab_next_ref, aux_ref, ys_hbm,
                      buf, sem)
    d = x_ref.shape[-1]
    hb = []
    for s in range(NSUB):
        mod0 = _mod_row(mod0_ref, j < nctx_blk, nb, bp * NSUB + s)
        mod1 = _mod_row(mod1_ref, j < nctx_blk, nb, bp * NSUB + s)
        x2 = x_ref[0, s] + mod0[:, 5 * d:6 * d] * ffn[s * TM:(s + 1) * TM]
        x2_ref[0, s] = x2
        hb.append((_rms(x2) * nw_ref[...] * (1.0 + mod1[:, d:2 * d]) + mod1[:, 0:d]).astype(BF16))
    hb = jnp.concatenate(hb, axis=0)
    y = _dot(hb, w_ref[...])
    proj = lambda n: y[:, n * d:(n + 1) * d]
    q = _silu(proj(0))
    lb = lb_ref[...]
    v = proj(3)
    g = proj(4)
    f_f = lb + (1.0 - lb) * _sigmoid(proj(1))
    f_b = lb + (1.0 - lb) * _sigmoid(proj(2))
    for s in range(NSUB):
        rows = slice(s * TM, (s + 1) * TM)
        v_ref[0, s] = v[rows].astype(BF16)
        vt_ref[0, s] = v[rows].T.astype(BF16)
        g_ref[0, s] = g[rows].astype(BF16)
        _gate_outputs(q[rows], 1.0 - f_f[rows], jnp.log(f_f[rows]), csf_ref,
                      qf_ref.at[0, s], kf_ref.at[0, s], cvf_ref.at[0, s, 0])
        _gate_outputs(q[rows], 1.0 - f_b[rows], jnp.log(f_b[rows]), csb_ref,
                      qb_ref.at[0, s], kb_ref.at[0, s], cvb_ref.at[0, s, 0])


def _inproj1(seg, tab, aux, ys, x1, mod0, mod1, nw, w_in, lb, consts, nctx_blk):
    bsz, nt, d = x1.shape
    nblk = nt // TM
    nbp = bsz // NSUB
    csf, csb = consts
    cs = lambda a: pl.BlockSpec(a.shape, lambda b, j, seg: (0,) * a.ndim)
    tok = lambda width: pl.BlockSpec((1, NSUB, TM, width), lambda b, j, seg: (b, 0, j, 0))
    cvspec = pl.BlockSpec((1, NSUB, 1, 3 * SUBLANES, d), lambda b, j, seg: (b, 0, j, 0, 0))
    grid_spec = pltpu.PrefetchScalarGridSpec(
        num_scalar_prefetch=1, grid=(nbp, nblk),
        in_specs=_route_specs(nbp, nblk) + [
                  pl.BlockSpec(memory_space=pl.ANY), tok(d), cs(mod0), cs(mod1), cs(nw),
                  pl.BlockSpec(w_in.shape, lambda b, j, seg: (0, 0), pipeline_mode=pl.Buffered(1)), cs(lb),
                  cs(csf), cs(csb)],
        out_specs=[tok(d)] * 6 + [pl.BlockSpec((1, NSUB, d, TM), lambda b, j, seg: (b, 0, 0, j)), tok(d), cvspec, cvspec],
        scratch_shapes=[pltpu.VMEM((2, LROWS, d), F32), pltpu.SemaphoreType.DMA((2,))])
    sds = jax.ShapeDtypeStruct
    out_shape = [sds((nbp, NSUB, nt, d), F32)] + [sds((nbp, NSUB, nt, d), BF16)] * 5 + [sds((nbp, NSUB, d, nt), BF16)] + \
                [sds((nbp, NSUB, nt, d), BF16)] + [sds((nbp, NSUB, nblk, 3 * SUBLANES, d), F32)] * 2
    outs = pl.pallas_call(
        functools.partial(_inproj1_kernel, nctx_blk, bsz), grid_spec=grid_spec, out_shape=out_shape,
        compiler_params=_params(2), name="combine_inproj_hgrn",
    )(seg, tab, tab, aux, ys, _pairs(x1), mod0, mod1, nw, w_in, lb, csf, csb)
    return [_unpairs(o) for o in outs]


def _final_kernel(seg_ref, tab_ref, tab_next_ref, aux_ref, ys_hbm, x_ref, mod_ref, fw_ref, o_ref, buf, sem):
    nblk = pl.num_programs(1)
    bp = pl.program_id(0)
    step = bp * nblk + pl.program_id(1)
    ffn = _gather_ffn(step, pl.num_programs(0) * nblk, seg_ref, tab_ref, tab_next_ref, aux_ref, ys_hbm, buf, sem)
    d = x_ref.shape[-1]
    for s in range(NSUB):
        mod = mod_ref[pl.ds(bp * NSUB + s, 1), :]
        o_ref[0, s] = _rms(x_ref[0, s] + mod[:, 5 * d:6 * d] * ffn[s * TM:(s + 1) * TM]) * fw_ref[...]


def _final(seg, tab, aux, ys, x3, mod, fw):
    bsz, seq, d = x3.shape
    nblk = seq // TM
    nbp = bsz // NSUB
    cs = lambda a: pl.BlockSpec(a.shape, lambda b, j, seg: (0,) * a.ndim)
    tok = pl.BlockSpec((1, NSUB, TM, d), lambda b, j, seg: (b, 0, j, 0))
    grid_spec = pltpu.PrefetchScalarGridSpec(
        num_scalar_prefetch=1, grid=(nbp, nblk),
        in_specs=_route_specs(nbp, nblk) + [pl.BlockSpec(memory_space=pl.ANY), tok, cs(mod), cs(fw)],
        out_specs=tok,
        scratch_shapes=[pltpu.VMEM((2, LROWS, d), F32), pltpu.SemaphoreType.DMA((2,))])
    out = pl.pallas_call(
        _final_kernel, grid_spec=grid_spec, out_shape=jax.ShapeDtypeStruct((nbp, NSUB, seq, d), F32),
        compiler_params=_params(2), name="combine_final_norm",
    )(seg, tab, tab, aux, ys, _pairs(x3), mod, fw)
    return _unpairs(out)


def _grid_sincos(n_tokens, dim):
    rows = n_tokens // GRID_W
    quarter = dim // 4
    omega = 1.0 / (10000.0 ** (jnp.arange(quarter, dtype=F32) / quarter))

    def emb(n):
        ang = jnp.arange(n, dtype=F32).reshape(-1, 1) * omega
        return jnp.concatenate([jnp.sin(ang), jnp.cos(ang)], axis=-1)

    half = 2 * quarter
    by_row = jnp.broadcast_to(emb(rows)[:, None, :], (rows, GRID_W, half))
    by_col = jnp.broadcast_to(emb(GRID_W)[None, :, :], (rows, GRID_W, half))
    return jnp.concatenate([by_row, by_col], axis=-1).reshape(rows * GRID_W, 2 * half)


def _pad_heads(w, n_heads):
    dk = w.shape[-1] // n_heads
    w = w.reshape(w.shape[:-1] + (n_heads, dk))
    w = jnp.pad(w, [(0, 0)] * (w.ndim - 1) + [(0, HEAD_W - dk)])
    return w.reshape(w.shape[:-2] + (n_heads * HEAD_W,))


def _router_tables(router_w, router_bias):
    n_exp = router_w.shape[1]
    epg = n_exp // N_GROUPS
    assert epg == EXPERTS_PER_GROUP
    perm = np.array([epg * g + p for p in range(epg) for g in range(N_GROUPS)])
    wt = jnp.pad(router_w.T[perm], ((0, LANES - n_exp), (0, 0)))
    hi = wt.astype(BF16)
    lo = (wt - hi.astype(F32)).astype(BF16)
    bias = jnp.pad(router_bias[perm], (0, LANES - n_exp))
    return hi, lo, jnp.broadcast_to(bias[:, None], (LANES, TR)).astype(F32)


def _segments(counts, n_tokens):
    cnt = counts[:, 0].astype(I32)
    tiles = (cnt + TR - 1) // TR
    ends = jnp.cumsum(tiles)
    ntile = (n_tokens + (n_tokens // TR) * N_GROUPS * (RUN - 1) + TR - 1) // TR + N_GROUPS
    n_valid = ends[-1]
    seg = jnp.concatenate([(ends - tiles) * TR, (n_valid * TR).reshape(1), jnp.full((1,), ntile * TR, I32)])
    tidx = jnp.minimum(jnp.arange(ntile, dtype=I32), n_valid - 1)
    tile_group = jnp.sum((tidx[:, None] >= ends[None, :]).astype(I32), axis=1)
    cnt = jnp.concatenate([cnt, jnp.zeros((1,), I32)])
    return seg.astype(I32), cnt, tile_group.astype(I32), n_valid.reshape(1).astype(I32), ntile * TR


def _moe_layer(tab, meta, counts, pay, aux, weights):
    seg, cnt, tile_group, n_valid, n_sorted = _segments(counts, pay.shape[0])
    hs = _dispatch(seg, cnt, tab, meta, pay, aux, n_sorted)
    ys = _moe(tile_group, n_valid, hs, *weights)
    return seg, ys


def kernel(x, c, ctx, c_ctx, ada_w, ada_b, norm_mix, norm_ffn, ab_w_in, ab_w_out, gla_a2, gla_ab, gla_norm, s5_lam_re, s5_lam_im, s5_log_dt, s5_b_re, s5_b_im, s5_c_re, s5_c_im, s5_d, s5_glu_w, s5_glu_b, hg_w_in, hg_w_out, hg_lb_logits, hg_norm, router_w, router_bias, moe_w_gate, moe_w_up, moe_w_down, final_norm):
    bsz, seq, d = x.shape
    nctx = ctx.shape[1]
    depth = ada_w.shape[0]
    assert depth == 2 and seq % TM == 0 and nctx % TM == 0 and bsz % SUBLANES == 0 and bsz < MOD_ROWS
    assert bsz % NSUB == 0
    nctx_blk = nctx // TM
    nt = nctx + seq
    nc = nt // S5_L
    assert (nc * bsz) % (2 * SUBLANES) == 0

    cond = jnp.zeros((MOD_ROWS, d), F32).at[:bsz].set(c).at[bsz].set(c_ctx)
    mod = _modulation(cond, ada_w, ada_b)
    consts = _chunk_matrices()
    row = lambda v: v.reshape(1, -1)

    qk = gla_a2.shape[-1]
    gv = ab_w_out.shape[1] // 2
    rank = gla_a2.shape[2]
    w_in = ab_w_in[0]
    o_v, o_g, o_a, o_u = 2 * qk, 2 * qk + gv, 2 * qk + 2 * gv, 2 * qk + 2 * gv + 2 * rank
    wm = jnp.concatenate([_pad_heads(w_in[:, 0:qk], GLA_HEADS), _pad_heads(w_in[:, qk:o_v], GLA_HEADS),
                          w_in[:, o_v:o_g], w_in[:, o_g:o_a], w_in[:, o_u:]], axis=1).astype(BF16)
    wa = jnp.pad(w_in[:, o_a:o_u], ((0, 0), (0, LANES - 2 * rank))).astype(BF16)
    a2p = _pad_heads(gla_a2[0], GLA_HEADS)
    a2 = jnp.zeros((LANES, 2 * gv), F32).at[0:rank, 0:gv].set(a2p[0]).at[rank:2 * rank, gv:].set(a2p[1]).astype(BF16)
    ab = _pad_heads(gla_ab[0], GLA_HEADS).reshape(1, 2 * gv)
    pos = _grid_sincos(seq, d)
    (xs, qf, kf, qb, kb, v, vt, g, u_cr, cvf, cvb) = _inproj0(x, ctx, pos, mod[0], row(norm_mix[0]), wm, wa, a2, ab, consts,
                                                               float(qk // GLA_HEADS) ** -0.5)
    o_f, o_b = _recurrence(qf, kf, qb, kb, v, vt, cvf, cvb, GLA_HEADS, nctx_blk)

    assert 2 * s5_b_re.shape[-1] == PAIR_W
    ops = _s5_operators(s5_lam_re[0], s5_lam_im[0], s5_log_dt[0], s5_b_re[0], s5_b_im[0], s5_c_re[0], s5_c_im[0], s5_d[0])
    y5 = _s5(u_cr, ops, nc, nctx // S5_L, bsz)

    rwh, rwl, rb = _router_tables(router_w, router_bias)
    utri = jnp.asarray(np.triu(np.ones((TR, TR), np.float32), 1), dtype=BF16)
    ones = jnp.ones((TR, TR), BF16)
    x1, pay, aux, meta, tab, counts = _mixout(True, GLA_HEADS, nctx_blk, 0, xs, o_f, o_b, g, y5, mod[0],
                                              row(gla_norm[0]), s5_glu_w[0].astype(BF16), row(s5_glu_b[0]),
                                              ab_w_out[0].astype(BF16), row(norm_ffn[0]), rwh, rwl, rb, utri, ones)
    assert moe_w_gate.shape[1] == N_GROUPS * EXPERTS_PER_GROUP
    seg, ys = _moe_layer(tab, meta, counts, pay, aux, (0, moe_w_gate, moe_w_up, moe_w_down))

    lb_all = jax.nn.softmax(hg_lb_logits.astype(F32), axis=0)
    lb_all = jnp.cumsum(lb_all, axis=0) - lb_all[0]
    (x2, qf, kf, qb, kb, v, vt, g, cvf, cvb) = _inproj1(seg, tab, aux, ys, x1, mod[0], mod[1], row(norm_mix[1]),
                                                          hg_w_in[0].astype(BF16), row(lb_all[1]), consts, nctx_blk)
    o_f, o_b = _recurrence(qf, kf, qb, kb, v, vt, cvf, cvb, HG_HEADS, nctx_blk)
    x3, pay, aux, meta, tab, counts = _mixout(False, HG_HEADS, nctx_blk, nctx_blk, x2, o_f, o_b, g, None, mod[1],
                                              row(hg_norm[0]), None, None, hg_w_out[0].astype(BF16),
                                              row(norm_ffn[1]), rwh, rwl, rb, utri, ones)
    seg, ys = _moe_layer(tab, meta, counts, pay, aux, (1, moe_w_gate, moe_w_up, moe_w_down))
    return _final(seg, tab, aux, ys, x3, mod[1], row(final_norm))
```
